```python
import jax, jax.numpy as jnp
from jax import lax
import numpy as np

D_MODEL = 1024
BATCH = 2
SEQ = 16384
DEPTH = 4

GRID_W = 64
CTX_LEN = 256
N_MIXERS = 4
QBLK = 128
EPS = 1e-6
ROPE_THETA = 10000.0
NEG_INF = -1e30

MLA_HEADS = 8
MLA_Q_RANK = 512
MLA_KV_RANK = 256
MLA_NOPE = 128
MLA_ROPE = 64
MLA_V = 128
FNET_GROUPS = 4
FNET_GC = D_MODEL // FNET_GROUPS
NA_HEADS = 16
NA_HD = D_MODEL // NA_HEADS
NA_KR_MAX = 8
NA_KC = 16
SW_HEADS = 16
SW_KV_HEADS = 4
SW_HD = 64
SW_WINDOW = 128
MOE_GROUPS = 4
MOE_PER_GROUP = 8
MOE_EXPERTS = MOE_GROUPS * MOE_PER_GROUP
MOE_TOPK = 2
MOE_FF = 512
MOE_BLK = 128

kernel_name = 'hybrid_mla_fnet_na_swa_hmoe'


def _n_uses(m):
    return len(range(m, DEPTH, N_MIXERS))


def rmsnorm(x, g):
    xf = x.astype(jnp.float32)
    y = xf * lax.rsqrt(jnp.mean(xf * xf, axis=-1, keepdims=True) + EPS)
    return (y * g.astype(jnp.float32)).astype(x.dtype)


def softmax_f32(s, dtype):
    return jax.nn.softmax(s.astype(jnp.float32), axis=-1).astype(dtype)


def axial_rope_angles(row, col, d_rot):
    n = d_rot // 4
    inv = ROPE_THETA ** (-jnp.arange(n, dtype=jnp.float32) / n)
    return jnp.concatenate([row.astype(jnp.float32)[:, None] * inv,
                            col.astype(jnp.float32)[:, None] * inv], axis=-1)


def apply_rope(x, ang):
    cos = jnp.cos(ang)[None, :, None, :]
    sin = jnp.sin(ang)[None, :, None, :]
    x1, x2 = jnp.split(x.astype(jnp.float32), 2, axis=-1)
    return jnp.concatenate([x1 * cos - x2 * sin, x1 * sin + x2 * cos], axis=-1).astype(x.dtype)


def mla_project(h, w_dq, g_q, w_uq, w_dkv, g_kv, w_ukv, ang):
    B, L, _ = h.shape
    q = (rmsnorm(h @ w_dq, g_q) @ w_uq).reshape(B, L, MLA_HEADS, MLA_NOPE + MLA_ROPE)
    q_nope, q_rope = q[..., :MLA_NOPE], q[..., MLA_NOPE:]
    kv_a = h @ w_dkv
    c_kv = rmsnorm(kv_a[..., :MLA_KV_RANK], g_kv)
    k_rope = kv_a[..., None, MLA_KV_RANK:]
    kv = (c_kv @ w_ukv).reshape(B, L, MLA_HEADS, MLA_NOPE + MLA_V)
    k_nope, v = kv[..., :MLA_NOPE], kv[..., MLA_NOPE:]
    if ang is not None:
        q_rope = apply_rope(q_rope, ang)
        k_rope = apply_rope(k_rope, ang)
    return q_nope, q_rope, k_nope, k_rope[:, :, 0], v


def mla_mixer(h_c, h_l, ang, w_dq, g_q, w_uq, w_dkv, g_kv, w_ukv, w_o, with_ctx):
    B, S, _ = h_l.shape
    qn_c, qr_c, kn_c, kr_c, v_c = mla_project(h_c, w_dq, g_q, w_uq, w_dkv, g_kv, w_ukv, None)
    qn_l, qr_l, kn_l, kr_l, v_l = mla_project(h_l, w_dq, g_q, w_uq, w_dkv, g_kv, w_ukv, ang)
    kn = jnp.concatenate([kn_c, kn_l], axis=1)
    kr = jnp.concatenate([kr_c, kr_l], axis=1)
    v = jnp.concatenate([v_c, v_l], axis=1)
    scale = (MLA_NOPE + MLA_ROPE) ** -0.5

    def attend(qn, qr, kn_, kr_, v_):
        s = (jnp.einsum('bqhd,bkhd->bhqk', qn, kn_) + jnp.einsum('bqhd,bkd->bhqk', qr, kr_)) * scale
        return jnp.einsum('bhqk,bkhd->bqhd', softmax_f32(s, v_.dtype), v_)

    def block(i):
        sl = lambda a: lax.dynamic_slice_in_dim(a, i * QBLK, QBLK, 1)
        return attend(sl(qn_l), sl(qr_l), kn, kr, v)

    o = lax.map(block, jnp.arange(S // QBLK))
    y_l = o.transpose(1, 0, 2, 3, 4).reshape(B, S, MLA_HEADS * MLA_V) @ w_o
    y_c = None
    if with_ctx:
        y_c = attend(qn_c, qr_c, kn_c, kr_c, v_c).reshape(B, h_c.shape[1], MLA_HEADS * MLA_V) @ w_o
    return y_c, y_l


def grouped_fourier(h):
    B, L, D = h.shape
    hg = h.astype(jnp.float32).reshape(B, L, FNET_GROUPS, FNET_GC)
    f = jnp.fft.fft2(hg, axes=(1, 3), norm='ortho').real
    return f.reshape(B, L, D).astype(h.dtype)


def fnet_mixer(h_c, h_l, w_o, b_o, with_ctx):
    y_l = grouped_fourier(h_l) @ w_o + b_o
    y_c = grouped_fourier(h_c) @ w_o + b_o if with_ctx else None
    return y_c, y_l


def na_mixer(h_c, h_l, w_qkv, rpb, w_o, with_ctx):
    B, S, D = h_l.shape
    C = h_c.shape[1]
    rows = S // GRID_W
    kr = min(NA_KR_MAX, rows)
    nloc = kr * NA_KC
    scale = NA_HD ** -0.5
    qkv_l = (h_l @ w_qkv).reshape(B, S, 3, NA_HEADS, NA_HD)
    qkv_c = (h_c @ w_qkv).reshape(B, C, 3, NA_HEADS, NA_HD)
    q_c, k_c, v_c = qkv_c[:, :, 0], qkv_c[:, :, 1], qkv_c[:, :, 2]
    qg = qkv_l[:, :, 0].reshape(B, rows, GRID_W, NA_HEADS, NA_HD)
    kg = qkv_l[:, :, 1].reshape(B, rows, GRID_W, NA_HEADS, NA_HD)
    vg = qkv_l[:, :, 2].reshape(B, rows, GRID_W, NA_HEADS, NA_HD)
    colq = jnp.arange(GRID_W)
    cidx = jnp.clip(colq - NA_KC // 2, 0, GRID_W - NA_KC)[:, None] + jnp.arange(NA_KC)
    dc = cidx - colq[:, None]

    def row_block(r):
        rs = jnp.clip(r - kr // 2, 0, rows - kr)
        kb = lax.dynamic_slice_in_dim(kg, rs, kr, 1)[:, :, cidx]
        vb = lax.dynamic_slice_in_dim(vg, rs, kr, 1)[:, :, cidx]
        qb = lax.dynamic_index_in_dim(qg, r, 1, keepdims=False)
        dr = rs + jnp.arange(kr) - r
        bias = rpb[:, dr[None, :, None] + NA_KR_MAX - 1, dc[:, None, :] + NA_KC - 1]
        s_loc = jnp.einsum('bqhd,brqchd->bhqrc', qb, kb) * scale + bias[None]
        s_ctx = jnp.einsum('bqhd,bkhd->bhqk', qb, k_c) * scale
        p = softmax_f32(jnp.concatenate([s_loc.reshape(B, NA_HEADS, GRID_W, nloc), s_ctx], -1), vb.dtype)
        p_loc = p[..., :nloc].reshape(B, NA_HEADS, GRID_W, kr, NA_KC)
        return (jnp.einsum('bhqrc,brqchd->bqhd', p_loc, vb)
                + jnp.einsum('bhqk,bkhd->bqhd', p[..., nloc:], v_c))

    o = lax.map(row_block, jnp.arange(rows))
    y_l = o.transpose(1, 0, 2, 3, 4).reshape(B, S, NA_HEADS * NA_HD) @ w_o
    y_c = None
    if with_ctx:
        s = jnp.einsum('bqhd,bkhd->bhqk', q_c, k_c) * scale
        o_c = jnp.einsum('bhqk,bkhd->bqhd', softmax_f32(s, v_c.dtype), v_c)
        y_c = o_c.reshape(B, C, NA_HEADS * NA_HD) @ w_o
    return y_c, y_l


def swa_mixer(h_c, h_l, ang, w_qkv, sinks, w_o, with_ctx):
    B, S, _ = h_l.shape
    G = SW_HEADS // SW_KV_HEADS
    nq, nk = SW_HEADS * SW_HD, SW_KV_HEADS * SW_HD
    scale = SW_HD ** -0.5
    sink = sinks.reshape(SW_KV_HEADS, G).astype(jnp.float32)

    def proj(h):
        L = h.shape[1]
        qkv = h @ w_qkv
        q = qkv[..., :nq].reshape(B, L, SW_HEADS, SW_HD)
        k = qkv[..., nq:nq + nk].reshape(B, L, SW_KV_HEADS, SW_HD)
        v = qkv[..., nq + nk:].reshape(B, L, SW_KV_HEADS, SW_HD)
        return q, k, v

    def scores(q, k):
        qg = q.reshape(q.shape[0], q.shape[1], SW_KV_HEADS, G, SW_HD)
        return (jnp.einsum('bqkgd,bnkd->bkgqn', qg, k) * scale).astype(jnp.float32)

    def out(p, v):
        o = jnp.einsum('bkgqn,bnkd->bqkgd', p, v)
        return o.reshape(o.shape[0], o.shape[1], nq)

    def with_sink(s):
        return jnp.concatenate([s, jnp.broadcast_to(sink[None, :, :, None, None], s.shape[:-1] + (1,))], -1)

    q_l, k_l, v_l = proj(h_l)
    q_l, k_l = apply_rope(q_l, ang), apply_rope(k_l, ang)
    q_c, k_c, v_c = proj(h_c)
    band = QBLK + 2 * SW_WINDOW
    pad = ((0, 0), (SW_WINDOW, SW_WINDOW), (0, 0), (0, 0))
    kp, vp = jnp.pad(k_l, pad), jnp.pad(v_l, pad)
    rel = (jnp.arange(band) - SW_WINDOW)[None, :] - jnp.arange(QBLK)[:, None]

    def block(i):
        qb = lax.dynamic_slice_in_dim(q_l, i * QBLK, QBLK, 1)
        kb = lax.dynamic_slice_in_dim(kp, i * QBLK, band, 1)
        vb = lax.dynamic_slice_in_dim(vp, i * QBLK, band, 1)
        kpos = i * QBLK - SW_WINDOW + jnp.arange(band)
        ok = (jnp.abs(rel) <= SW_WINDOW) & (kpos >= 0)[None, :] & (kpos < S)[None, :]
        s_loc = jnp.where(ok, scores(qb, kb), NEG_INF)
        s = with_sink(jnp.concatenate([s_loc, scores(qb, k_c)], -1))
        p = jax.nn.softmax(s, axis=-1)[..., :-1].astype(vb.dtype)
        return out(p[..., :band], vb) + out(p[..., band:], v_c)

    o = lax.map(block, jnp.arange(S // QBLK))
    y_l = o.transpose(1, 0, 2, 3).reshape(B, S, nq) @ w_o
    y_c = None
    if with_ctx:
        p = jax.nn.softmax(with_sink(scores(q_c, k_c)), axis=-1)[..., :-1].astype(v_c.dtype)
        y_c = out(p, v_c) @ w_o
    return y_c, y_l


def hier_moe(h, w_grp, b_grp, w_rt, b_rt, w_gate, w_up, w_down):
    N, D = h.shape
    hf = h.astype(jnp.float32)
    p_grp = jax.nn.softmax(hf @ w_grp.astype(jnp.float32) + b_grp.astype(jnp.float32), axis=-1)
    w_g, g_idx = lax.top_k(p_grp, 1)
    logit_e = (hf @ w_rt.astype(jnp.float32) + b_rt.astype(jnp.float32)).reshape(N, MOE_GROUPS, MOE_PER_GROUP)
    logit_e = jnp.take_along_axis(logit_e, g_idx[:, :, None], axis=1)[:, 0]
    w_e, e_idx = lax.top_k(jax.nn.softmax(logit_e, axis=-1), MOE_TOPK)
    gate = (w_g * w_e / jnp.sum(w_e, axis=-1, keepdims=True)).reshape(-1)
    eid = (g_idx * MOE_PER_GROUP + e_idx).reshape(-1)
    tok = jnp.repeat(jnp.arange(N, dtype=jnp.int32), MOE_TOPK)
    T = N * MOE_TOPK
    order = jnp.argsort(eid)
    se = eid[order]
    counts = jnp.bincount(eid, length=MOE_EXPERTS)
    start = jnp.cumsum(counts) - counts
    pcounts = (counts + MOE_BLK - 1) // MOE_BLK * MOE_BLK
    pend = jnp.cumsum(pcounts)
    dest = pend[se] - pcounts[se] + jnp.arange(T) - start[se]
    cap = -(-T // MOE_BLK) * MOE_BLK + MOE_EXPERTS * MOE_BLK
    nblk = cap // MOE_BLK
    buf_tok = jnp.zeros((cap,), jnp.int32).at[dest].set(tok[order])
    buf_gate = jnp.zeros((cap,), jnp.float32).at[dest].set(gate[order])
    blk_e = jnp.minimum(jnp.searchsorted(pend, jnp.arange(nblk) * MOE_BLK, side='right'), MOE_EXPERTS - 1)
    xb = h[buf_tok].reshape(nblk, MOE_BLK, D)

    def expert_block(args):
        xe, e = args
        return (jax.nn.silu(xe @ w_gate[e]) * (xe @ w_up[e])) @ w_down[e]

    yb = lax.map(expert_block, (xb, blk_e)).reshape(cap, D)
    return jnp.zeros_like(h).at[buf_tok].add(yb * buf_gate[:, None].astype(h.dtype))


def setup_inputs(seed: int = 0) -> dict:
    key = jax.random.key(seed)
    keys = iter(jax.random.split(key, 40))
    D = D_MODEL

    def nrm(shape, scale):
        return jax.random.normal(next(keys), shape, jnp.float32) * scale

    def gain(shape):
        return 1.0 + nrm(shape, 0.05)

    nA, nB, nC, nD = (_n_uses(m) for m in range(N_MIXERS))
    mla_qk = MLA_NOPE + MLA_ROPE
    sw_qkv = (SW_HEADS + 2 * SW_KV_HEADS) * SW_HD
    return {
        'x': nrm((BATCH, SEQ, D), 1.0),
        'c': nrm((BATCH, D), 1.0),
        'ctx': nrm((BATCH, CTX_LEN, D), 1.0),
        'c_ctx': nrm((D,), 1.0),
        'mod_w': nrm((DEPTH, D, 6 * D), 0.5 * D ** -0.5),
        'mod_b': nrm((DEPTH, 6 * D), 0.02),
        'norm1_g': gain((DEPTH, D)),
        'norm2_g': gain((DEPTH, D)),
        'mla_w_dq': nrm((nA, D, MLA_Q_RANK), D ** -0.5),
        'mla_g_q': gain((nA, MLA_Q_RANK)),
        'mla_w_uq': nrm((nA, MLA_Q_RANK, MLA_HEADS * mla_qk), MLA_Q_RANK ** -0.5),
        'mla_w_dkv': nrm((nA, D, MLA_KV_RANK + MLA_ROPE), D ** -0.5),
        'mla_g_kv': gain((nA, MLA_KV_RANK)),
        'mla_w_ukv': nrm((nA, MLA_KV_RANK, MLA_HEADS * (MLA_NOPE + MLA_V)), MLA_KV_RANK ** -0.5),
        'mla_w_o': nrm((nA, MLA_HEADS * MLA_V, D), (MLA_HEADS * MLA_V) ** -0.5),
        'fnet_w_o': nrm((nB, D, D), D ** -0.5),
        'fnet_b_o': nrm((nB, D), 0.02),
        'na_w_qkv': nrm((nC, D, 3 * NA_HEADS * NA_HD), D ** -0.5),
        'na_rpb': nrm((nC, NA_HEADS, 2 * NA_KR_MAX - 1, 2 * NA_KC - 1), 0.5),
        'na_w_o': nrm((nC, NA_HEADS * NA_HD, D), (NA_HEADS * NA_HD) ** -0.5),
        'swa_w_qkv': nrm((nD, D, sw_qkv), D ** -0.5),
        'swa_sinks': nrm((nD, SW_HEADS), 0.5),
        'swa_w_o': nrm((nD, SW_HEADS * SW_HD, D), (SW_HEADS * SW_HD) ** -0.5),
        'moe_w_grp': nrm((DEPTH, D, MOE_GROUPS), D ** -0.5),
        'moe_b_grp': nrm((DEPTH, MOE_GROUPS), 0.01),
        'moe_w_rt': nrm((DEPTH, D, MOE_EXPERTS), D ** -0.5),
        'moe_b_rt': nrm((DEPTH, MOE_EXPERTS), 0.01),
        'moe_w_gate': nrm((DEPTH, MOE_EXPERTS, D, MOE_FF), D ** -0.5),
        'moe_w_up': nrm((DEPTH, MOE_EXPERTS, D, MOE_FF), D ** -0.5),
        'moe_w_down': nrm((DEPTH, MOE_EXPERTS, MOE_FF, D), MOE_FF ** -0.5),
        'final_g': gain((D,)),
    }


def reference(x, c, ctx, c_ctx, mod_w, mod_b, norm1_g, norm2_g,
              mla_w_dq, mla_g_q, mla_w_uq, mla_w_dkv, mla_g_kv, mla_w_ukv, mla_w_o,
              fnet_w_o, fnet_b_o,
              na_w_qkv, na_rpb, na_w_o,
              swa_w_qkv, swa_sinks, swa_w_o,
              moe_w_grp, moe_b_grp, moe_w_rt, moe_b_rt, moe_w_gate, moe_w_up, moe_w_down,
              final_g):
    B, S, D = x.shape
    t = jnp.arange(S)
    ang = axial_rope_angles(t // GRID_W, t % GRID_W, MLA_ROPE)
    c_act = jax.nn.silu(c)
    cc_act = jax.nn.silu(c_ctx)
    x_lat, x_ctx = x, ctx
    for i in range(DEPTH):
        m, j = i % N_MIXERS, i // N_MIXERS
        with_ctx = i < DEPTH - 1
        mod_l = (c_act @ mod_w[i] + mod_b[i])[:, None, :]
        mod_c = cc_act @ mod_w[i] + mod_b[i]
        sh1_l, sc1_l, g1_l, sh2_l, sc2_l, g2_l = jnp.split(mod_l, 6, axis=-1)
        sh1_c, sc1_c, g1_c, sh2_c, sc2_c, g2_c = jnp.split(mod_c, 6, axis=-1)
        h_l = rmsnorm(x_lat, norm1_g[i]) * (1 + sc1_l) + sh1_l
        h_c = rmsnorm(x_ctx, norm1_g[i]) * (1 + sc1_c) + sh1_c
        if m == 0:
            y_c, y_l = mla_mixer(h_c, h_l, ang, mla_w_dq[j], mla_g_q[j], mla_w_uq[j], mla_w_dkv[j],
                                 mla_g_kv[j], mla_w_ukv[j], mla_w_o[j], with_ctx)
        elif m == 1:
            y_c, y_l = fnet_mixer(h_c, h_l, fnet_w_o[j], fnet_b_o[j], with_ctx)
        elif m == 2:
            y_c, y_l = na_mixer(h_c, h_l, na_w_qkv[j], na_rpb[j], na_w_o[j], with_ctx)
        else:
            y_c, y_l = swa_mixer(h_c, h_l, ang, swa_w_qkv[j], swa_sinks[j], swa_w_o[j], with_ctx)
        x_lat = x_lat + g1_l * y_l
        h2_l = rmsnorm(x_lat, norm2_g[i]) * (1 + sc2_l) + sh2_l
        moe_args = (moe_w_grp[i], moe_b_grp[i], moe_w_rt[i], moe_b_rt[i],
                    moe_w_gate[i], moe_w_up[i], moe_w_down[i])
        if with_ctx:
            x_ctx = x_ctx + g1_c * y_c
            h2_c = rmsnorm(x_ctx, norm2_g[i]) * (1 + sc2_c) + sh2_c
            n_c = h2_c.shape[0] * h2_c.shape[1]
            f = hier_moe(jnp.concatenate([h2_c.reshape(-1, D), h2_l.reshape(-1, D)], axis=0), *moe_args)
            x_ctx = x_ctx + g2_c * f[:n_c].reshape(h2_c.shape)
            x_lat = x_lat + g2_l * f[n_c:].reshape(h2_l.shape)
        else:
            x_lat = x_lat + g2_l * hier_moe(h2_l.reshape(-1, D), *moe_args).reshape(h2_l.shape)
    return rmsnorm(x_lat, final_g)
```

```python
import functools
import math

import jax
import jax.numpy as jnp
import numpy as np
from jax import lax
from jax.experimental import pallas as pl
from jax.experimental.pallas import tpu as pltpu

F32 = jnp.float32
BF16 = jnp.bfloat16
I32 = jnp.int32
HIGHEST = lax.Precision.HIGHEST

GRID_W = 64
EPS = 1e-6
ROPE_THETA = 10000.0
NEG_INF = -1e30
MLA_HEADS, MLA_Q_RANK, MLA_KV_RANK, MLA_NOPE, MLA_ROPE, MLA_V = 8, 512, 256, 128, 64, 128
FNET_GROUPS = 4
NA_HEADS, NA_HD, NA_KR, NA_KC = 16, 64, 8, 16
SW_HEADS, SW_KV_HEADS, SW_HD, SW_WINDOW = 16, 4, 64, 128
MOE_GROUPS, MOE_PER_GROUP, MOE_FF = 4, 8, 512
MOE_EXPERTS = MOE_GROUPS * MOE_PER_GROUP

LANES = 128
SUBLANES = 8
TM = 512
MOE_BM = 512
VMEM_LIMIT = 56 * 1024 * 1024
ROUTE_LANE_EID, ROUTE_LANE_RANK, ROUTE_LANE_GATE = 0, 2, 4


def _cp(sem, vmem=VMEM_LIMIT):
    return pltpu.CompilerParams(dimension_semantics=sem, vmem_limit_bytes=vmem)


def _lane_iota(shape):
    return lax.broadcasted_iota(I32, shape, len(shape) - 1)


def _normmod(x, g, sc, sh):
    ms = jnp.mean(x * x, axis=-1, keepdims=True)
    return (x * lax.rsqrt(ms + EPS) * g) * (1.0 + sc) + sh


def _rms(x, g):
    ms = jnp.mean(x * x, axis=-1, keepdims=True)
    return x * lax.rsqrt(ms + EPS) * g


def _swap_halves(t, period):
    n = t.shape[-1]
    half = period // 2
    lane = _lane_iota(t.shape)
    return jnp.where((lane % period) < half, pltpu.roll(t, n - half, 1), pltpu.roll(t, half, 1))


def _dot(a, b):
    return jnp.dot(a, b, preferred_element_type=F32)


def _dot_nt(a, b):
    return lax.dot_general(a, b, (((1,), (1,)), ((), ())), preferred_element_type=F32)


def _mod_kernel(a_ref, w_ref, b_ref, o_ref):
    a = a_ref[...]
    a = a * jax.nn.sigmoid(a)
    o_ref[0] = jnp.dot(a, w_ref[0], precision=HIGHEST, preferred_element_type=F32) + b_ref[0]


def _modulation(cond, mod_w, mod_b):
    depth, d, n = mod_w.shape
    tn = n // 4
    return pl.pallas_call(
        _mod_kernel,
        out_shape=jax.ShapeDtypeStruct((depth, SUBLANES, n), F32),
        grid=(depth, n // tn),
        in_specs=[pl.BlockSpec((SUBLANES, d), lambda l, j: (0, 0)),
                  pl.BlockSpec((1, d, tn), lambda l, j: (l, 0, j)),
                  pl.BlockSpec((1, 1, tn), lambda l, j: (l, 0, j))],
        out_specs=pl.BlockSpec((1, SUBLANES, tn), lambda l, j: (l, 0, j)),
        compiler_params=_cp(("arbitrary", "arbitrary")),
        name="modulation",
    )(cond, mod_w, mod_b.reshape(depth, 1, n))


class _Layout:
    def __init__(self, B, S, C, D):
        self.B, self.S, self.C, self.D = B, S, C, D
        self.NL, self.NC = B * S, B * C
        self.NT = self.NL + self.NC
        assert S % TM == 0 and self.NC % TM == 0 and TM % C == 0
        self.nl_tiles = self.NL // TM
        self.n_tiles = self.NT // TM
        self.tiles_per_batch = S // TM

    def mod_row(self, i):
        return jnp.where(i < self.nl_tiles, i // self.tiles_per_batch, self.B)

    def mod_spec(self, chunk):
        return pl.BlockSpec((1, 1, self.D), lambda i: (chunk * SUBLANES + self.mod_row(i), 0, 0))


def _mla_proj_kernel(x_ref, g_ref, sc_ref, sh_ref, w1_ref, gq_ref, gkv_ref, wq_ref, wkv_ref, cos_ref, sin_ref,
                     q_ref, k_ref, v_ref):
    h = _normmod(x_ref[...], g_ref[...], sc_ref[0], sh_ref[0]).astype(BF16)
    a = _dot(h, w1_ref[...])
    qa = _rms(a[:, :MLA_Q_RANK], gq_ref[...]).astype(BF16)
    ckv = _rms(a[:, MLA_Q_RANK:MLA_Q_RANK + MLA_KV_RANK], gkv_ref[...]).astype(BF16)
    cos, sin = cos_ref[...], sin_ref[...]

    def rope(t):
        return t * cos + _swap_halves(t, MLA_ROPE) * sin

    kr = rope(a[:, MLA_Q_RANK + MLA_KV_RANK:]).astype(BF16)
    scale = (MLA_NOPE + MLA_ROPE) ** -0.5
    q = _dot(qa, wq_ref[...])
    kv = _dot(ckv, wkv_ref[...])
    for hd in range(MLA_HEADS):
        c = hd * 2 * LANES
        q_ref[:, c:c + LANES] = (q[:, c:c + LANES] * scale).astype(BF16)
        q_ref[:, c + LANES:c + 2 * LANES] = (rope(q[:, c + LANES:c + 2 * LANES]) * scale).astype(BF16)
        k_ref[:, c:c + LANES] = kv[:, c:c + LANES].astype(BF16)
        k_ref[:, c + LANES:c + 2 * LANES] = kr
        v_ref[:, hd * LANES:(hd + 1) * LANES] = kv[:, c + LANES:c + 2 * LANES].astype(BF16)


def _mla_proj(lay, x, g, modv, w1, gq, gkv, wq, wkv, cos, sin):
    D = lay.D
    full = lambda a: pl.BlockSpec(a.shape, lambda i: (0,) * a.ndim)
    row = lambda n: pl.BlockSpec((TM, n), lambda i: (i, 0))
    hq = MLA_HEADS * 2 * LANES
    return pl.pallas_call(
        _mla_proj_kernel,
        out_shape=(jax.ShapeDtypeStruct((lay.NT, hq), BF16), jax.ShapeDtypeStruct((lay.NT, hq), BF16),
                   jax.ShapeDtypeStruct((lay.NT, MLA_HEADS * LANES), BF16)),
        grid=(lay.n_tiles,),
        in_specs=[row(D), full(g), lay.mod_spec(1), lay.mod_spec(0), full(w1), full(gq), full(gkv), full(wq),
                  full(wkv), row(LANES), row(LANES)],
        out_specs=(row(hq), row(hq), row(MLA_HEADS * LANES)),
        compiler_params=_cp(("arbitrary",)),
        name="mla_proj",
    )(x, g, modv, modv, w1, gq, gkv, wq, wkv, cos, sin)


def _qkv_proj_kernel(x_ref, g_ref, sc_ref, sh_ref, w_ref, cos_ref, sin_ref, o_ref, *, n_q, n_rope, q_scale, chunk):
    h = _normmod(x_ref[...], g_ref[...], sc_ref[0], sh_ref[0]).astype(BF16)
    n = w_ref.shape[1]
    for c0 in range(0, n, chunk):
        a = _dot(h, w_ref[:, c0:c0 + chunk])
        if c0 < n_rope:
            reps = chunk // LANES
            cos = jnp.concatenate([cos_ref[...]] * reps, axis=1)
            sin = jnp.concatenate([sin_ref[...]] * reps, axis=1)
            a = a * cos + _swap_halves(a, SW_HD) * sin
        if c0 < n_q:
            a = a * q_scale
        o_ref[:, c0:c0 + chunk] = a.astype(BF16)


def _qkv_proj(lay, x, g, modv, w, cos, sin, *, n_q, n_rope, q_scale, chunk=256):
    D, n = lay.D, w.shape[1]
    assert n % chunk == 0 and n_q % chunk == 0 and n_rope % chunk == 0
    full = lambda a: pl.BlockSpec(a.shape, lambda i: (0,) * a.ndim)
    row = lambda m: pl.BlockSpec((TM, m), lambda i: (i, 0))
    return pl.pallas_call(
        functools.partial(_qkv_proj_kernel, n_q=n_q, n_rope=n_rope, q_scale=q_scale, chunk=chunk),
        out_shape=jax.ShapeDtypeStruct((lay.NT, n), BF16),
        grid=(lay.n_tiles,),
        in_specs=[row(D), full(g), lay.mod_spec(1), lay.mod_spec(0), full(w), row(LANES), row(LANES)],
        out_specs=row(n),
        compiler_params=_cp(("arbitrary",)),
        name="qkv_proj",
    )(x, g, modv, modv, w, cos, sin)


def _mla_attn_kernel(*refs, tk, n_lat):
    if n_lat:
        q_ref, kc_ref, vc_ref, kl_ref, vl_ref, o_ref, m_ref, l_ref, acc_ref = refs
    else:
        q_ref, kc_ref, vc_ref, o_ref, m_ref, l_ref, acc_ref = refs
    q = q_ref[...]

    def update(k, v, first):
        s = _dot_nt(q, k)
        if first:
            m_new = jnp.max(s, axis=-1, keepdims=True)
            p = jnp.exp(s - m_new)
            l_ref[...] = jnp.sum(p, axis=-1, keepdims=True)
            acc_ref[...] = _dot(p.astype(BF16), v)
        else:
            m_old = m_ref[...]
            m_new = jnp.maximum(m_old, jnp.max(s, axis=-1, keepdims=True))
            alpha = jnp.exp(m_old - m_new)
            p = jnp.exp(s - m_new)
            l_ref[...] = alpha * l_ref[...] + jnp.sum(p, axis=-1, keepdims=True)
            acc_ref[...] = alpha * acc_ref[...] + _dot(p.astype(BF16), v)
        m_ref[...] = m_new

    update(kc_ref[...], vc_ref[...], True)
    if n_lat:
        def body(j, c):
            r0 = pl.multiple_of(j * tk, tk)
            update(kl_ref[pl.ds(r0, tk), :], vl_ref[pl.ds(r0, tk), :], False)
            return c
        lax.fori_loop(0, n_lat // tk, body, 0)
    o_ref[...] = (acc_ref[...] / l_ref[...]).astype(o_ref.dtype)


def _mla_attention(lay, q, k, v, *, tq=512, tk=512):
    B, S, C = lay.B, lay.S, lay.C
    H = MLA_HEADS
    nq = S // tq
    cblk0 = lay.NL // C
    scratch = lambda t: [pltpu.VMEM((t, 1), F32), pltpu.VMEM((t, 1), F32), pltpu.VMEM((t, LANES), F32)]
    o_lat = pl.pallas_call(
        functools.partial(_mla_attn_kernel, tk=tk, n_lat=S),
        out_shape=jax.ShapeDtypeStruct((lay.NL, H * LANES), BF16),
        grid=(B, H, nq),
        in_specs=[pl.BlockSpec((tq, 2 * LANES), lambda b, h, i: (b * nq + i, h)),
                  pl.BlockSpec((C, 2 * LANES), lambda b, h, i: (cblk0 + b, h)),
                  pl.BlockSpec((C, LANES), lambda b, h, i: (cblk0 + b, h)),
                  pl.BlockSpec((S, 2 * LANES), lambda b, h, i: (b, h)),
                  pl.BlockSpec((S, LANES), lambda b, h, i: (b, h))],
        out_specs=pl.BlockSpec((tq, LANES), lambda b, h, i: (b * nq + i, h)),
        scratch_shapes=scratch(tq),
        compiler_params=_cp(("arbitrary", "arbitrary", "arbitrary")),
        name="mla_attn_latent",
    )(q, k, v, k, v)
    o_ctx = pl.pallas_call(
        functools.partial(_mla_attn_kernel, tk=tk, n_lat=0),
        out_shape=jax.ShapeDtypeStruct((lay.NC, H * LANES), BF16),
        grid=(B, H),
        in_specs=[pl.BlockSpec((C, 2 * LANES), lambda b, h: (cblk0 + b, h)),
                  pl.BlockSpec((C, 2 * LANES), lambda b, h: (cblk0 + b, h)),
                  pl.BlockSpec((C, LANES), lambda b, h: (cblk0 + b, h))],
        out_specs=pl.BlockSpec((C, LANES), lambda b, h: (b, h)),
        scratch_shapes=scratch(C),
        compiler_params=_cp(("arbitrary", "arbitrary")),
        name="mla_attn_ctx",
    )(q, k, v)
    return o_lat, o_ctx


def _dft_tables(S, C, gc):
    P = math.isqrt(S)
    assert P * P == S and (P & (P - 1)) == 0 and (gc & (gc - 1)) == 0 and (C & (C - 1)) == 0

    def cs(idx, n):
        ang = (idx % n).astype(F32) * (2.0 * math.pi / n)
        return jnp.cos(ang), jnp.sin(ang)

    k1 = jnp.arange(P, dtype=I32)
    idx = k1[None, :, None] * (P * k1[None, None, :] + k1[:, None, None])
    c, s = cs(idx, S)
    m1 = jnp.concatenate([c, -s], axis=1) * (1.0 / P)
    c, s = cs(k1[:, None] * k1[None, :], P)
    m2 = jnp.concatenate([jnp.concatenate([c, s], axis=1), jnp.concatenate([-s, c], axis=1)], axis=0)
    kc = jnp.arange(gc, dtype=I32)
    c, s = cs(kc[:, None] * kc[None, :], gc)
    mc = jnp.concatenate([c, s], axis=0) * (gc ** -0.5)
    kq = jnp.arange(C, dtype=I32)
    c, s = cs(kq[:, None] * kq[None, :], C)
    mctx = jnp.concatenate([c, s], axis=0) * (C ** -0.5)
    return m1.astype(BF16), m2.astype(BF16), mc.astype(BF16), mctx.astype(BF16)


def _fnet_stage1_kernel(x_ref, g_ref, sc_ref, sh_ref, m1_ref, z_ref, *, n2c, P):
    g, sc, sh = g_ref[...], sc_ref[0], sh_ref[0]
    for j in range(n2c):
        h = _normmod(x_ref[:, j, :], g, sc, sh).astype(BF16)
        z = _dot(m1_ref[j], h)
        z_ref[0, :, 0, j, :] = z[:P]
        z_ref[0, :, 1, j, :] = z[P:]


def _fnet_stage2_kernel(z_ref, m2_ref, mc_ref, f_ref, *, k1c, P, gc):
    D = f_ref.shape[-1]
    for j in range(k1c):
        z = z_ref[0, j].reshape(2 * P, D).astype(BF16)
        y = _dot(m2_ref[...], z)
        yr, yi = y[:P].astype(BF16), y[P:].astype(BF16)
        outs = []
        for gi in range(D // gc):
            sl = slice(gi * gc, (gi + 1) * gc)
            outs.append(_dot(yr[:, sl], mc_ref[:gc, :]) + _dot(yi[:, sl], mc_ref[gc:, :]))
        f_ref[:, j, :] = jnp.concatenate(outs, axis=1)


def _fnet_ctx_kernel(x_ref, g_ref, sc_ref, sh_ref, ml_ref, mc_ref, f_ref, *, C, gc):
    D = x_ref.shape[-1]
    h = _normmod(x_ref[...], g_ref[...], sc_ref[0], sh_ref[0]).astype(BF16)
    y = _dot(ml_ref[...], h)
    yc, ys = y[:C].astype(BF16), y[C:].astype(BF16)
    outs = []
    for gi in range(D // gc):
        sl = slice(gi * gc, (gi + 1) * gc)
        outs.append(_dot(yc[:, sl], mc_ref[:gc, :]) - _dot(ys[:, sl], mc_ref[gc:, :]))
    f_ref[...] = jnp.concatenate(outs, axis=1)


def _fnet_mix(lay, x, g, modv, tables):
    B, S, C, D = lay.B, lay.S, lay.C, lay.D
    m1, m2, mc, mctx = tables
    P = math.isqrt(S)
    gc = D // FNET_GROUPS
    n2c = SUBLANES
    k1c = SUBLANES
    full = lambda a: pl.BlockSpec(a.shape, lambda *i: (0,) * a.ndim)
    modspec = lambda chunk: pl.BlockSpec((1, 1, D), lambda b, j: (chunk * SUBLANES + b, 0, 0))
    assert C % P == 0 and P % n2c == 0 and P % k1c == 0
    x3 = x.reshape(lay.NT // P, P, D)
    z = pl.pallas_call(
        functools.partial(_fnet_stage1_kernel, n2c=n2c, P=P),
        out_shape=jax.ShapeDtypeStruct((B, P, 2, P, D), F32),
        grid=(B, P // n2c),
        in_specs=[pl.BlockSpec((P, n2c, D), lambda b, j: (b, j, 0)), full(g), modspec(1), modspec(0),
                  pl.BlockSpec((n2c, 2 * P, P), lambda b, j: (j, 0, 0))],
        out_specs=pl.BlockSpec((1, P, 2, n2c, D), lambda b, j: (b, 0, 0, j, 0)),
        compiler_params=_cp(("arbitrary", "arbitrary")),
        name="fnet_stage1",
    )(x3, g, modv, modv, m1)
    f_lat = pl.pallas_call(
        functools.partial(_fnet_stage2_kernel, k1c=k1c, P=P, gc=gc),
        out_shape=jax.ShapeDtypeStruct((lay.NL // P, P, D), F32),
        grid=(B, P // k1c),
        in_specs=[pl.BlockSpec((1, k1c, 2, P, D), lambda b, j: (b, j, 0, 0, 0)), full(m2), full(mc)],
        out_specs=pl.BlockSpec((P, k1c, D), lambda b, j: (b, j, 0)),
        compiler_params=_cp(("arbitrary", "arbitrary")),
        name="fnet_stage2",
    )(z, m2, mc)
    cblk0 = lay.NL // C
    ctx_mod = lambda chunk: pl.BlockSpec((1, 1, D), lambda b: (chunk * SUBLANES + B, 0, 0))
    f_ctx = pl.pallas_call(
        functools.partial(_fnet_ctx_kernel, C=C, gc=gc),
        out_shape=jax.ShapeDtypeStruct((lay.NC, D), F32),
        grid=(B,),
        in_specs=[pl.BlockSpec((C, D), lambda b: (cblk0 + b, 0)), full(g), ctx_mod(1), ctx_mod(0), full(mctx),
                  full(mc)],
        out_specs=pl.BlockSpec((C, D), lambda b: (b, 0)),
        compiler_params=_cp(("arbitrary",)),
        name="fnet_ctx",
    )(x, g, modv, modv, mctx, mc)
    return f_lat.reshape(lay.NL, D), f_ctx


def _attend(qm, loc, kc, vc, sink=None):
    sc = _dot_nt(qm, kc)
    m = jnp.max(sc, axis=-1, keepdims=True)
    if loc is not None:
        kw, vw, bias = loc
        s = _dot_nt(qm, kw) + bias
        m = jnp.maximum(m, jnp.max(s, axis=-1, keepdims=True))
    if sink is not None:
        m = jnp.maximum(m, sink)
    pc = jnp.exp(sc - m)
    l = jnp.sum(pc, axis=-1, keepdims=True)
    o = _dot(pc.astype(BF16), vc)
    if loc is not None:
        p = jnp.exp(s - m)
        l = l + jnp.sum(p, axis=-1, keepdims=True)
        o = o + _dot(p.astype(BF16), vw)
    if sink is not None:
        l = l + jnp.exp(sink - m)
    return o / l


NA_QROWS = 2
NA_KROWS = NA_KR + NA_QROWS - 1
NA_VARIANTS = 5


def _na_bias_tables(rpb, rows):
    assert rows >= 16 and rows % NA_QROWS == 0
    W = GRID_W
    tabs = []
    for r in (0, 2, 6, rows - 4, rows - 2):
        w0 = min(max(r - NA_KR // 2, 0), rows - NA_KROWS)
        a = np.arange(NA_QROWS)[:, None, None, None]
        cq = np.arange(W)[None, :, None, None]
        j = np.arange(NA_KROWS)[None, None, :, None]
        ck = np.arange(W)[None, None, None, :]
        rq, rk = r + a, w0 + j
        rs = np.clip(rq - NA_KR // 2, 0, rows - NA_KR)
        cs = np.clip(cq - NA_KC // 2, 0, W - NA_KC)
        valid = (rk >= rs) & (rk < rs + NA_KR) & (ck >= cs) & (ck < cs + NA_KC)
        dr = np.clip(rk - rq + NA_KR - 1, 0, 2 * NA_KR - 2)
        dc = np.clip(ck - cq + NA_KC - 1, 0, 2 * NA_KC - 2)
        shape = (NA_QROWS, W, NA_KROWS, W)
        dr, dc, valid = (np.broadcast_to(t, shape).reshape(NA_QROWS * W, NA_KROWS * W) for t in (dr, dc, valid))
        tabs.append(jnp.where(valid[None], rpb[:, dr, dc], NEG_INF))
    return jnp.stack(tabs).astype(F32)


def _na_kernel(*refs, rows, local):
    if local:
        q_ref, kc_ref, vc_ref, kl_ref, vl_ref, bias_ref, o_ref = refs
    else:
        q_ref, kc_ref, vc_ref, o_ref = refs
    lane = _lane_iota((1, LANES))
    masks = (lane < NA_HD, lane >= NA_HD)
    kc, vc = kc_ref[...], vc_ref[...]
    vcm = [jnp.where(mk, vc, jnp.zeros_like(vc)) for mk in masks]
    nq = NA_QROWS * GRID_W
    nk = NA_KROWS * GRID_W

    def block(r0, loc_of):
        q2 = q_ref[pl.ds(r0, nq), :]
        o = jnp.zeros((nq, LANES), F32)
        for hl in range(2):
            qm = jnp.where(masks[hl], q2, jnp.zeros_like(q2))
            o = o + _attend(qm, loc_of(hl), kc, vcm[hl])
        o_ref[pl.ds(r0, nq), :] = o.astype(o_ref.dtype)

    if not local:
        for t in range(q_ref.shape[0] // nq):
            block(t * nq, lambda hl: None)
        return

    qi = pl.program_id(2)
    pairs = q_ref.shape[0] // nq

    def body(t, c):
        r = (qi * pairs + t) * NA_QROWS
        w0 = jnp.clip(r - NA_KR // 2, 0, rows - NA_KROWS)
        k0 = pl.multiple_of(w0 * GRID_W, GRID_W)
        kw, vw = kl_ref[pl.ds(k0, nk), :], vl_ref[pl.ds(k0, nk), :]
        var = jnp.where(r == 0, 0, jnp.where(r == 2, 1, jnp.where(r == rows - 4, 3, jnp.where(r == rows - 2, 4, 2))))
        vwm = [jnp.where(mk, vw, jnp.zeros_like(vw)) for mk in masks]
        block(pl.multiple_of(t * nq, nq), lambda hl: (kw, vwm[hl], bias_ref[var, hl]))
        return c

    lax.fori_loop(0, pairs, body, 0)


def _na_attention(lay, qkv, bias, *, tq=512):
    B, S, C = lay.B, lay.S, lay.C
    rows = S // GRID_W
    HP = NA_HEADS // 2
    nq = S // tq
    cblk0 = lay.NL // C
    o_lat = pl.pallas_call(
        functools.partial(_na_kernel, rows=rows, local=True),
        out_shape=jax.ShapeDtypeStruct((lay.NL, NA_HEADS * NA_HD), BF16),
        grid=(B, HP, nq),
        in_specs=[pl.BlockSpec((tq, LANES), lambda b, h, i: (b * nq + i, h)),
                  pl.BlockSpec((C, LANES), lambda b, h, i: (cblk0 + b, HP + h)),
                  pl.BlockSpec((C, LANES), lambda b, h, i: (cblk0 + b, 2 * HP + h)),
                  pl.BlockSpec((S, LANES), lambda b, h, i: (b, HP + h)),
                  pl.BlockSpec((S, LANES), lambda b, h, i: (b, 2 * HP + h)),
                  pl.BlockSpec((NA_VARIANTS, 2) + bias.shape[2:], lambda b, h, i: (0, h, 0, 0))],
        out_specs=pl.BlockSpec((tq, LANES), lambda b, h, i: (b * nq + i, h)),
        compiler_params=_cp(("arbitrary", "arbitrary", "arbitrary")),
        name="na_attn_latent",
    )(qkv, qkv, qkv, qkv, qkv, bias)
    o_ctx = pl.pallas_call(
        functools.partial(_na_kernel, rows=rows, local=False),
        out_shape=jax.ShapeDtypeStruct((lay.NC, NA_HEADS * NA_HD), BF16),
        grid=(B, HP),
        in_specs=[pl.BlockSpec((C, LANES), lambda b, h: (cblk0 + b, h)),
                  pl.BlockSpec((C, LANES), lambda b, h: (cblk0 + b, HP + h)),
                  pl.BlockSpec((C, LANES), lambda b, h: (cblk0 + b, 2 * HP + h))],
        out_specs=pl.BlockSpec((C, LANES), lambda b, h: (b, h)),
        compiler_params=_cp(("arbitrary", "arbitrary")),
        name="na_attn_ctx",
    )(qkv, qkv, qkv)
    return o_lat, o_ctx


SW_SUB = 128
SW_BAND = SW_SUB + 2 * SW_WINDOW


def _swa_kernel(*refs, S, local):
    if local:
        (sink_ref, q_ref, kc_ref, vc_ref, kp_ref, kcur_ref, kn_ref, vp_ref, vcur_ref, vn_ref, o_ref,
         kbuf, vbuf) = refs
        tq = q_ref.shape[0]
        W = SW_WINDOW
        kbuf[0:W] = kp_ref[...]
        kbuf[W:W + tq] = kcur_ref[...]
        kbuf[W + tq:] = kn_ref[...]
        vbuf[0:W] = vp_ref[...]
        vbuf[W:W + tq] = vcur_ref[...]
        vbuf[W + tq:] = vn_ref[...]
        i = pl.program_id(1)
    else:
        sink_ref, q_ref, kc_ref, vc_ref, o_ref = refs
        tq = q_ref.shape[0]
    lane = _lane_iota((1, LANES))
    masks = (lane < SW_HD, lane >= SW_HD)
    G = SW_HEADS // SW_KV_HEADS
    kc, vc = kc_ref[...], vc_ref[...]

    def sub(sb, c):
        r0 = pl.multiple_of(sb * SW_SUB, SW_SUB)
        if local:
            kw, vw = kbuf[pl.ds(r0, SW_BAND), :], vbuf[pl.ds(r0, SW_BAND), :]
            row = lax.broadcasted_iota(I32, (SW_SUB, SW_BAND), 0)
            col = lax.broadcasted_iota(I32, (SW_SUB, SW_BAND), 1)
            rel = col - SW_WINDOW - row
            kpos = i * tq + sb * SW_SUB - SW_WINDOW + col
            ok = (jnp.abs(rel) <= SW_WINDOW) & (kpos >= 0) & (kpos < S)
            bias = jnp.where(ok, 0.0, NEG_INF).astype(F32)
        for t in range(SW_HEADS // 2):
            kvh = (2 * t) // G
            tile, half = kvh // 2, kvh % 2
            sl = slice(tile * LANES, (tile + 1) * LANES)
            q2 = q_ref[pl.ds(r0, SW_SUB), t * LANES:(t + 1) * LANES].astype(F32)
            q2r = pltpu.roll(q2, SW_HD, 1)
            acc = jnp.zeros((SW_SUB, LANES), F32)
            for qh in range(2):
                src = q2 if qh == half else q2r
                qm = jnp.where(masks[half], src, 0.0).astype(BF16)
                loc = (kw[:, sl], vw[:, sl], bias) if local else None
                oh = _attend(qm, loc, kc[:, sl], vc[:, sl], sink_ref[2 * t + qh])
                if qh != half:
                    oh = pltpu.roll(oh, SW_HD, 1)
                acc = acc + jnp.where(masks[qh], oh, 0.0)
            o_ref[pl.ds(r0, SW_SUB), t * LANES:(t + 1) * LANES] = acc.astype(o_ref.dtype)
        return c

    lax.fori_loop(0, tq // SW_SUB, sub, 0)


def _swa_attention(lay, qkv, sinks, *, tq=512):
    B, S, C = lay.B, lay.S, lay.C
    nq_cols = SW_HEADS * SW_HD
    nkv = SW_KV_HEADS * SW_HD
    kcol, vcol = nq_cols // nkv, nq_cols // nkv + 1
    nq = S // tq
    per = tq // SW_WINDOW
    last = lay.NT // SW_WINDOW - 1
    cblk0 = lay.NL // C
    prev = lambda b, i: jnp.maximum((b * nq + i) * per - 1, 0)
    nxt = lambda b, i: jnp.minimum((b * nq + i + 1) * per, last)
    smem = pl.BlockSpec(memory_space=pltpu.SMEM)
    o_lat = pl.pallas_call(
        functools.partial(_swa_kernel, S=S, local=True),
        out_shape=jax.ShapeDtypeStruct((lay.NL, nq_cols), BF16),
        grid=(B, nq),
        in_specs=[smem,
                  pl.BlockSpec((tq, nq_cols), lambda b, i: (b * nq + i, 0)),
                  pl.BlockSpec((C, nkv), lambda b, i: (cblk0 + b, kcol)),
                  pl.BlockSpec((C, nkv), lambda b, i: (cblk0 + b, vcol)),
                  pl.BlockSpec((SW_WINDOW, nkv), lambda b, i: (prev(b, i), kcol)),
                  pl.BlockSpec((tq, nkv), lambda b, i: (b * nq + i, kcol)),
                  pl.BlockSpec((SW_WINDOW, nkv), lambda b, i: (nxt(b, i), kcol)),
                  pl.BlockSpec((SW_WINDOW, nkv), lambda b, i: (prev(b, i), vcol)),
                  pl.BlockSpec((tq, nkv), lambda b, i: (b * nq + i, vcol)),
                  pl.BlockSpec((SW_WINDOW, nkv), lambda b, i: (nxt(b, i), vcol))],
        out_specs=pl.BlockSpec((tq, nq_cols), lambda b, i: (b * nq + i, 0)),
        scratch_shapes=[pltpu.VMEM((tq + 2 * SW_WINDOW, nkv), BF16), pltpu.VMEM((tq + 2 * SW_WINDOW, nkv), BF16)],
        compiler_params=_cp(("arbitrary", "arbitrary")),
        name="swa_attn_latent",
    )(sinks, qkv, qkv, qkv, qkv, qkv, qkv, qkv, qkv, qkv)
    o_ctx = pl.pallas_call(
        functools.partial(_swa_kernel, S=S, local=False),
        out_shape=jax.ShapeDtypeStruct((lay.NC, nq_cols), BF16),
        grid=(B,),
        in_specs=[smem,
                  pl.BlockSpec((C, nq_cols), lambda b: (cblk0 + b, 0)),
                  pl.BlockSpec((C, nkv), lambda b: (cblk0 + b, kcol)),
                  pl.BlockSpec((C, nkv), lambda b: (cblk0 + b, vcol))],
        out_specs=pl.BlockSpec((C, nq_cols), lambda b: (b, 0)),
        compiler_params=_cp(("arbitrary",)),
        name="swa_attn_ctx",
    )(sinks, qkv, qkv, qkv)
    return o_lat, o_ctx


def _route(logits, tri, carry):
    lane = _lane_iota(logits.shape)
    lanef = lane.astype(F32)
    big = float(LANES)
    rowmax = lambda t: jnp.max(t, axis=-1, keepdims=True)
    rowmin = lambda t: jnp.min(t, axis=-1, keepdims=True)
    rowsum = lambda t: jnp.sum(t, axis=-1, keepdims=True)
    is_g = lane < MOE_GROUPS
    mg = rowmax(jnp.where(is_g, logits, -jnp.inf))
    w_g = 1.0 / rowsum(jnp.where(is_g, jnp.exp(logits - mg), 0.0))
    gidx = rowmin(jnp.where(is_g & (logits == mg), lanef, big))
    g0 = MOE_GROUPS + MOE_PER_GROUP * gidx
    in_grp = (lanef >= g0) & (lanef < g0 + MOE_PER_GROUP)
    le = jnp.where(in_grp, logits, -jnp.inf)
    m1 = rowmax(le)
    i1 = rowmin(jnp.where(in_grp & (le == m1), lanef, big))
    le2 = jnp.where(lanef == i1, -jnp.inf, le)
    m2 = rowmax(le2)
    i2 = rowmin(jnp.where(in_grp & (lanef != i1) & (le2 == m2), lanef, big))
    r = jnp.exp(m2 - m1)
    gate1 = w_g / (1.0 + r)
    gate2 = w_g * r / (1.0 + r)
    sel1, sel2 = lanef == i1, lanef == i2
    member = (sel1 | sel2)
    cum = _dot(tri, member.astype(BF16)) + carry
    rank1 = rowsum(jnp.where(sel1, cum, 0.0))
    rank2 = rowsum(jnp.where(sel2, cum, 0.0))
    new_carry = carry + jnp.sum(member.astype(F32), axis=0, keepdims=True)
    rec = jnp.zeros_like(logits)
    for ln, val in ((0, i1 - MOE_GROUPS), (1, i2 - MOE_GROUPS), (2, rank1), (3, rank2), (4, gate1), (5, gate2)):
        rec = jnp.where(lane == ln, val, rec)
    return rec, new_carry


def _out_proj_kernel(*refs, has_bias, n_lat_tiles):
    if has_bias:
        (al_ref, ac_ref, x_ref, w_ref, b_ref, g1_ref, g2n_ref, sc_ref, sh_ref, wr_ref, br_ref, tri_ref,
         xo_ref, h2_ref, rec_ref, cnt_ref, carry_ref) = refs
    else:
        (al_ref, ac_ref, x_ref, w_ref, g1_ref, g2n_ref, sc_ref, sh_ref, wr_ref, br_ref, tri_ref,
         xo_ref, h2_ref, rec_ref, cnt_ref, carry_ref) = refs

    @pl.when(pl.program_id(0) == 0)
    def _():
        carry_ref[...] = jnp.zeros_like(carry_ref)

    a = jnp.where(pl.program_id(0) < n_lat_tiles, al_ref[...], ac_ref[...])
    y = _dot(a.astype(BF16), w_ref[...])
    if has_bias:
        y = y + b_ref[...]
    xn = x_ref[...] + g1_ref[0] * y
    xo_ref[...] = xn
    h2 = _normmod(xn, g2n_ref[...], sc_ref[0], sh_ref[0])
    h2_ref[...] = h2
    logits = jnp.dot(h2, wr_ref[...], precision=HIGHEST, preferred_element_type=F32) + br_ref[...]
    rec, carry = _route(logits, tri_ref[...], carry_ref[...])
    rec_ref[...] = rec
    carry_ref[...] = carry
    cnt_ref[...] = jnp.broadcast_to(carry, cnt_ref.shape)


def _out_proj(lay, a, x, w, b, g2n, modv, wr, br, tri):
    D = lay.D
    a_lat, a_ctx = a
    nl = lay.nl_tiles
    full = lambda t: pl.BlockSpec(t.shape, lambda i: (0,) * t.ndim)
    row = lambda n: pl.BlockSpec((TM, n), lambda i: (i, 0))
    ins = [a_lat, a_ctx, x, w] + ([b] if b is not None else []) + [modv, g2n, modv, modv, wr, br, tri]
    specs = ([pl.BlockSpec((TM, a_lat.shape[1]), lambda i: (jnp.minimum(i, nl - 1), 0)),
              pl.BlockSpec((TM, a_ctx.shape[1]), lambda i: (jnp.maximum(i - nl, 0), 0)), row(D), full(w)]
             + ([full(b)] if b is not None else [])
             + [lay.mod_spec(2), full(g2n), lay.mod_spec(4), lay.mod_spec(3), full(wr), full(br), full(tri)])
    return pl.pallas_call(
        functools.partial(_out_proj_kernel, has_bias=b is not None, n_lat_tiles=nl),
        out_shape=(jax.ShapeDtypeStruct((lay.NT, D), F32), jax.ShapeDtypeStruct((lay.NT, D), F32),
                   jax.ShapeDtypeStruct((lay.NT, LANES), F32), jax.ShapeDtypeStruct((SUBLANES, LANES), F32)),
        grid=(lay.n_tiles,),
        in_specs=specs,
        out_specs=(row(D), row(D), row(LANES), pl.BlockSpec((SUBLANES, LANES), lambda i: (0, 0))),
        scratch_shapes=[pltpu.VMEM((1, LANES), F32)],
        input_output_aliases={2: 0},
        compiler_params=_cp(("arbitrary",)),
        name="out_proj_router",
    )(*ins)


def _row_copy(src, s, dst, d, sem):
    return pltpu.make_async_copy(src.at[pl.ds(s, 1), :], dst.at[pl.ds(d, 1), :], sem)


def _dispatch_kernel(pos_ref, h_ref, xb_in_ref, xb_ref, sem):
    del xb_in_ref
    base = pl.program_id(0) * TM

    def issue(r, c):
        for k in range(2):
            _row_copy(h_ref, r, xb_ref, pos_ref[(base + r) * 2 + k], sem).start()
        return c

    def drain(r, c):
        for k in range(2):
            _row_copy(h_ref, r, xb_ref, pos_ref[(base + r) * 2 + k], sem).wait()
        return c

    lax.fori_loop(0, TM, issue, 0)
    lax.fori_loop(0, TM, drain, 0)


def _dispatch(lay, pos, h2, cap):
    D = lay.D
    return pl.pallas_call(
        _dispatch_kernel,
        out_shape=jax.ShapeDtypeStruct((cap, D), F32),
        grid_spec=pltpu.PrefetchScalarGridSpec(
            num_scalar_prefetch=1, grid=(lay.n_tiles,),
            in_specs=[pl.BlockSpec((TM, D), lambda i, p: (i, 0)), pl.BlockSpec(memory_space=pl.ANY)],
            out_specs=pl.BlockSpec(memory_space=pl.ANY),
            scratch_shapes=[pltpu.SemaphoreType.DMA]),
        input_output_aliases={2: 0},
        compiler_params=_cp(("arbitrary",)),
        name="moe_dispatch",
    )(pos, h2, jnp.zeros((cap, D), F32))


def _expert_kernel(be_ref, nu_ref, xb_ref, wg_ref, wu_ref, wd_ref, yb_ref, wgb, wub, wdb):
    j = pl.program_id(0)
    prev = be_ref[jnp.maximum(j - 1, 0)]

    @pl.when((j == 0) | (be_ref[j] != prev))
    def _():
        wgb[...] = wg_ref[0].astype(BF16)
        wub[...] = wu_ref[0].astype(BF16)
        wdb[...] = wd_ref[0].astype(BF16)

    @pl.when(j < nu_ref[0])
    def _():
        xe = xb_ref[...].astype(BF16)
        g = _dot(xe, wgb[...])
        u = _dot(xe, wub[...])
        act = (g * jax.nn.sigmoid(g) * u).astype(BF16)
        yb_ref[...] = _dot(act, wdb[...])

    @pl.when(j >= nu_ref[0])
    def _():
        yb_ref[...] = jnp.zeros_like(yb_ref)


def _experts(xb, blk_e, n_used, w_gate, w_up, w_down):
    cap, D = xb.shape
    FF = w_gate.shape[-1]
    nblk = cap // MOE_BM
    blk = lambda j, be, nu: (jnp.minimum(j, nu[0] - 1), 0)
    return pl.pallas_call(
        _expert_kernel,
        out_shape=jax.ShapeDtypeStruct((cap, D), F32),
        grid_spec=pltpu.PrefetchScalarGridSpec(
            num_scalar_prefetch=2, grid=(nblk,),
            in_specs=[pl.BlockSpec((MOE_BM, D), blk),
                      pl.BlockSpec((1, D, FF), lambda j, be, nu: (be[j], 0, 0)),
                      pl.BlockSpec((1, D, FF), lambda j, be, nu: (be[j], 0, 0)),
                      pl.BlockSpec((1, FF, D), lambda j, be, nu: (be[j], 0, 0))],
            out_specs=pl.BlockSpec((MOE_BM, D), lambda j, be, nu: (j, 0)),
            scratch_shapes=[pltpu.VMEM((D, FF), BF16), pltpu.VMEM((D, FF), BF16), pltpu.VMEM((FF, D), BF16)]),
        compiler_params=_cp(("arbitrary",)),
        name="moe_experts",
    )(blk_e, n_used, xb, w_gate, w_up, w_down)


def _combine_kernel(*refs, final):
    if final:
        pos_ref, x_ref, rec_ref, g2_ref, yb_ref, fg_ref, o_ref, buf, sem = refs
    else:
        pos_ref, x_ref, rec_ref, g2_ref, yb_ref, o_ref, buf, sem = refs
    base = pl.program_id(0) * TM

    def issue(r, c):
        for k in range(2):
            _row_copy(yb_ref, pos_ref[(base + r) * 2 + k], buf.at[k], r, sem).start()
        return c

    def drain(r, c):
        for k in range(2):
            _row_copy(yb_ref, pos_ref[(base + r) * 2 + k], buf.at[k], r, sem).wait()
        return c

    lax.fori_loop(0, TM, issue, 0)
    lax.fori_loop(0, TM, drain, 0)
    rec = rec_ref[...]
    f = rec[:, ROUTE_LANE_GATE:ROUTE_LANE_GATE + 1] * buf[0] + rec[:, ROUTE_LANE_GATE + 1:ROUTE_LANE_GATE + 2] * buf[1]
    xn = x_ref[...] + g2_ref[0] * f
    if final:
        xn = _rms(xn, fg_ref[...])
    o_ref[...] = xn


def _combine(lay, pos, x, rec, modv, yb, final_g):
    D = lay.D
    final = final_g is not None
    row = lambda n: pl.BlockSpec((TM, n), lambda i, p: (i, 0))
    specs = [row(D), row(LANES),
             pl.BlockSpec((1, 1, D), lambda i, p: (5 * SUBLANES + lay.mod_row(i), 0, 0)),
             pl.BlockSpec(memory_space=pl.ANY)]
    ins = [x, rec, modv, yb]
    if final:
        specs.append(pl.BlockSpec(final_g.shape, lambda i, p: (0, 0)))
        ins.append(final_g)
    return pl.pallas_call(
        functools.partial(_combine_kernel, final=final),
        out_shape=jax.ShapeDtypeStruct((lay.NT, D), F32),
        grid_spec=pltpu.PrefetchScalarGridSpec(
            num_scalar_prefetch=1, grid=(lay.n_tiles,),
            in_specs=specs,
            out_specs=row(D),
            scratch_shapes=[pltpu.VMEM((2, TM, D), F32), pltpu.SemaphoreType.DMA]),
        input_output_aliases={1: 0},
        compiler_params=_cp(("arbitrary",)),
        name="moe_combine",
    )(pos, *ins)


def _moe(lay, x, h2, rec, counts, modv, w_gate, w_up, w_down, final_g):
    T = 2 * lay.NT
    cap = -(-T // MOE_BM) * MOE_BM + MOE_EXPERTS * MOE_BM
    cnt = counts[0, MOE_GROUPS:MOE_GROUPS + MOE_EXPERTS].astype(I32)
    pcnt = (cnt + MOE_BM - 1) // MOE_BM * MOE_BM
    pend = jnp.cumsum(pcnt)
    start = pend - pcnt
    eid = rec[:, ROUTE_LANE_EID:ROUTE_LANE_EID + 2].astype(I32)
    rank = rec[:, ROUTE_LANE_RANK:ROUTE_LANE_RANK + 2].astype(I32)
    pos = (start[eid] + rank).reshape(-1)
    blk_e = jnp.minimum(jnp.searchsorted(pend, jnp.arange(cap // MOE_BM, dtype=I32) * MOE_BM, side='right'),
                        MOE_EXPERTS - 1).astype(I32)
    n_used = (pend[-1:] // MOE_BM).astype(I32)
    xb = _dispatch(lay, pos, h2, cap)
    yb = _experts(xb, blk_e, n_used, w_gate, w_up, w_down)
    return _combine(lay, pos, x, rec, modv, yb, final_g)


def _rope_tables(lay):
    S = lay.S
    t = jnp.arange(S)
    n = MLA_ROPE // 4
    inv = ROPE_THETA ** (-jnp.arange(n, dtype=F32) / n)
    ang = jnp.concatenate([(t // GRID_W).astype(F32)[:, None] * inv, (t % GRID_W).astype(F32)[:, None] * inv], axis=-1)
    cos, sin = jnp.cos(ang), jnp.sin(ang)
    cos64 = jnp.concatenate([cos, cos], axis=-1)
    sin64 = jnp.concatenate([-sin, sin], axis=-1)
    rows = lambda lat, ctx_val: jnp.concatenate([jnp.tile(lat, (lay.B, 1)), jnp.full((lay.NC, 64), ctx_val, F32)], axis=0)
    cos64, sin64 = rows(cos64, 1.0), rows(sin64, 0.0)
    zero = jnp.zeros_like(cos64)
    return ((jnp.concatenate([cos64, zero], axis=1), jnp.concatenate([sin64, zero], axis=1)),
            (jnp.concatenate([cos64, cos64], axis=1), jnp.concatenate([sin64, sin64], axis=1)))


def kernel(x, c, ctx, c_ctx, mod_w, mod_b, norm1_g, norm2_g, mla_w_dq, mla_g_q, mla_w_uq, mla_w_dkv, mla_g_kv, mla_w_ukv, mla_w_o, fnet_w_o, fnet_b_o, na_w_qkv, na_rpb, na_w_o, swa_w_qkv, swa_sinks, swa_w_o, moe_w_grp, moe_b_grp, moe_w_rt, moe_b_rt, moe_w_gate, moe_w_up, moe_w_down, final_g):
    B, S, D = x.shape
    C = ctx.shape[1]
    depth = mod_w.shape[0]
    lay = _Layout(B, S, C, D)
    X = jnp.concatenate([x.reshape(B * S, D), ctx.reshape(B * C, D)], axis=0)
    cond = jnp.concatenate([c, c_ctx[None], jnp.zeros((SUBLANES - B - 1, D), F32)], axis=0)
    mod = _modulation(cond, mod_w, mod_b)
    (mla_cos, mla_sin), (swa_cos, swa_sin) = _rope_tables(lay)
    tri = (jnp.arange(TM)[:, None] > jnp.arange(TM)[None, :]).astype(BF16)
    n_mix = 4
    for i in range(depth):
        m, j = i % n_mix, i // n_mix
        modv = mod[i].reshape(SUBLANES, 6, D).transpose(1, 0, 2).reshape(6 * SUBLANES, 1, D)
        g1n, g2n = norm1_g[i][None], norm2_g[i][None]
        bias = None
        if m == 0:
            w1 = jnp.concatenate([mla_w_dq[j], mla_w_dkv[j], jnp.zeros((D, LANES - MLA_ROPE), F32)], axis=1).astype(BF16)
            wq = mla_w_uq[j].reshape(MLA_Q_RANK, MLA_HEADS, MLA_NOPE + MLA_ROPE)
            wq = jnp.concatenate([wq, jnp.zeros((MLA_Q_RANK, MLA_HEADS, LANES - MLA_ROPE), F32)], axis=-1)
            wq = wq.reshape(MLA_Q_RANK, MLA_HEADS * 2 * LANES).astype(BF16)
            q, k, v = _mla_proj(lay, X, g1n, modv, w1, mla_g_q[j][None], mla_g_kv[j][None], wq,
                                mla_w_ukv[j].astype(BF16), mla_cos, mla_sin)
            a = _mla_attention(lay, q, k, v)
            w_o = mla_w_o[j]
        elif m == 1:
            a = _fnet_mix(lay, X, g1n, modv, _dft_tables(S, C, D // FNET_GROUPS))
            w_o, bias = fnet_w_o[j], fnet_b_o[j][None]
        elif m == 2:
            qkv = _qkv_proj(lay, X, g1n, modv, na_w_qkv[j].astype(BF16), swa_cos, swa_sin,
                            n_q=NA_HEADS * NA_HD, n_rope=0, q_scale=NA_HD ** -0.5)
            a = _na_attention(lay, qkv, _na_bias_tables(na_rpb[j], S // GRID_W))
            w_o = na_w_o[j]
        else:
            qkv = _qkv_proj(lay, X, g1n, modv, swa_w_qkv[j].astype(BF16), swa_cos, swa_sin,
                            n_q=SW_HEADS * SW_HD, n_rope=(SW_HEADS + SW_KV_HEADS) * SW_HD, q_scale=SW_HD ** -0.5)
            a = _swa_attention(lay, qkv, swa_sinks[j])
            w_o = swa_w_o[j]
        wr = jnp.concatenate([moe_w_grp[i], moe_w_rt[i], jnp.zeros((D, LANES - MOE_GROUPS - MOE_EXPERTS), F32)], axis=1)
        br = jnp.concatenate([moe_b_grp[i], moe_b_rt[i], jnp.zeros((LANES - MOE_GROUPS - MOE_EXPERTS,), F32)])[None]
        X, h2, rec, counts = _out_proj(lay, a, X, w_o.astype(BF16), bias, g2n, modv, wr, br, tri)
        X = _moe(lay, X, h2, rec, counts, modv, moe_w_gate[i], moe_w_up[i], moe_w_down[i],
                 final_g[None] if i == depth - 1 else None)
    return X[:lay.NL].reshape(B, S, D)
```

```python
import functools
import math

import jax
import jax.numpy as jnp
import numpy as np
from jax import lax
from jax.experimental import pallas as pl
from jax.experimental.pallas import tpu as pltpu

F32 = jnp.float32
BF16 = jnp.bfloat16
I32 = jnp.int32
HIGHEST = lax.Precision.HIGHEST

GRID_W = 64
EPS = 1e-6
ROPE_THETA = 10000.0
NEG_INF = -1e30
MLA_HEADS, MLA_Q_RANK, MLA_KV_RANK, MLA_NOPE, MLA_ROPE, MLA_V = 8, 512, 256, 128, 64, 128
MLA_VT_ROWS = MLA_V + 16
FNET_GROUPS = 4
NA_HEADS, NA_HD, NA_KR, NA_KC = 16, 64, 8, 16
SW_HEADS, SW_KV_HEADS, SW_HD, SW_WINDOW = 16, 4, 64, 128
MOE_GROUPS, MOE_PER_GROUP, MOE_FF = 4, 8, 512
MOE_EXPERTS = MOE_GROUPS * MOE_PER_GROUP

LANES = 128
SUBLANES = 8
TM = 512
MOE_BM = 512
VMEM_LIMIT = 56 * 1024 * 1024
ROUTE_LANE_EID, ROUTE_LANE_RANK, ROUTE_LANE_GATE = 0, 2, 4


def _cp(sem, vmem=VMEM_LIMIT):
    return pltpu.CompilerParams(dimension_semantics=sem, vmem_limit_bytes=vmem)


def _lane_iota(shape):
    return lax.broadcasted_iota(I32, shape, len(shape) - 1)


def _normmod(x, g, sc, sh):
    ms = jnp.mean(x * x, axis=-1, keepdims=True)
    return (x * lax.rsqrt(ms + EPS) * g) * (1.0 + sc) + sh


def _rms(x, g):
    ms = jnp.mean(x * x, axis=-1, keepdims=True)
    return x * lax.rsqrt(ms + EPS) * g


def _swap_halves(t, period):
    n = t.shape[-1]
    half = period // 2
    lane = _lane_iota(t.shape)
    return jnp.where((lane % period) < half, pltpu.roll(t, n - half, 1), pltpu.roll(t, half, 1))


def _dot(a, b):
    return jnp.dot(a, b, preferred_element_type=F32)


def _dot_nt(a, b):
    return lax.dot_general(a, b, (((1,), (1,)), ((), ())), preferred_element_type=F32)


def _mod_kernel(a_ref, w_ref, b_ref, o_ref):
    a = a_ref[...]
    a = a * jax.nn.sigmoid(a)
    o_ref[0] = jnp.dot(a, w_ref[0], precision=HIGHEST, preferred_element_type=F32) + b_ref[0]


def _modulation(cond, mod_w, mod_b):
    depth, d, n = mod_w.shape
    tn = n // 4
    return pl.pallas_call(
        _mod_kernel,
        out_shape=jax.ShapeDtypeStruct((depth, SUBLANES, n), F32),
        grid=(depth, n // tn),
        in_specs=[pl.BlockSpec((SUBLANES, d), lambda l, j: (0, 0)),
                  pl.BlockSpec((1, d, tn), lambda l, j: (l, 0, j)),
                  pl.BlockSpec((1, 1, tn), lambda l, j: (l, 0, j))],
        out_specs=pl.BlockSpec((1, SUBLANES, tn), lambda l, j: (l, 0, j)),
        compiler_params=_cp(("arbitrary", "arbitrary")),
        name="modulation",
    )(cond, mod_w, mod_b.reshape(depth, 1, n))


class _Layout:
    def __init__(self, B, S, C, D):
        self.B, self.S, self.C, self.D = B, S, C, D
        self.NL, self.NC = B * S, B * C
        self.NT = self.NL + self.NC
        assert S % TM == 0 and self.NC % TM == 0 and TM % C == 0
        self.nl_tiles = self.NL // TM
        self.n_tiles = self.NT // TM
        self.tiles_per_batch = S // TM

    def mod_row(self, i):
        return jnp.where(i < self.nl_tiles, i // self.tiles_per_batch, self.B)

    def mod_spec(self, chunk):
        return pl.BlockSpec((1, 1, self.D), lambda i: (chunk * SUBLANES + self.mod_row(i), 0, 0))


def _mla_proj_kernel(x_ref, g_ref, sc_ref, sh_ref, w1_ref, gq_ref, gkv_ref, wq_ref, wkv_ref, cos_ref, sin_ref,
                     qt_ref, k_ref, vt_ref):
    h = _normmod(x_ref[...], g_ref[...], sc_ref[0], sh_ref[0]).astype(BF16)
    a = _dot(h, w1_ref[...])
    qa = _rms(a[:, :MLA_Q_RANK], gq_ref[...]).astype(BF16)
    ckv = _rms(a[:, MLA_Q_RANK:MLA_Q_RANK + MLA_KV_RANK], gkv_ref[...]).astype(BF16)
    cos, sin = cos_ref[...], sin_ref[...]

    def rope(t):
        return t * cos + _swap_halves(t, MLA_ROPE) * sin

    kr = rope(a[:, MLA_Q_RANK + MLA_KV_RANK:]).astype(BF16)
    scale = (MLA_NOPE + MLA_ROPE) ** -0.5 * math.log2(math.e)
    q = _dot(qa, wq_ref[...])
    kv = _dot(ckv, wkv_ref[...])
    ones = jnp.ones((MLA_VT_ROWS - MLA_V, x_ref.shape[0]), BF16)
    for hd in range(MLA_HEADS):
        c = hd * 2 * LANES
        qh = jnp.concatenate([q[:, c:c + LANES], rope(q[:, c + LANES:c + 2 * LANES])], axis=1) * scale
        qt_ref[hd] = qh.T.astype(BF16)
        k_ref[:, c:c + LANES] = kv[:, c:c + LANES].astype(BF16)
        k_ref[:, c + LANES:c + 2 * LANES] = kr
        vt_ref[hd, :MLA_V, :] = kv[:, c + LANES:c + 2 * LANES].T.astype(BF16)
        vt_ref[hd, MLA_V:, :] = ones


def _mla_proj(lay, x, g, modv, w1, gq, gkv, wq, wkv, cos, sin):
    D = lay.D
    full = lambda a: pl.BlockSpec(a.shape, lambda i: (0,) * a.ndim)
    row = lambda n: pl.BlockSpec((TM, n), lambda i: (i, 0))
    col = lambda r: pl.BlockSpec((MLA_HEADS, r, TM), lambda i: (0, 0, i))
    hq = MLA_HEADS * 2 * LANES
    return pl.pallas_call(
        _mla_proj_kernel,
        out_shape=(jax.ShapeDtypeStruct((MLA_HEADS, 2 * LANES, lay.NT), BF16),
                   jax.ShapeDtypeStruct((lay.NT, hq), BF16),
                   jax.ShapeDtypeStruct((MLA_HEADS, MLA_VT_ROWS, lay.NT), BF16)),
        grid=(lay.n_tiles,),
        in_specs=[row(D), full(g), lay.mod_spec(1), lay.mod_spec(0), full(w1), full(gq), full(gkv), full(wq),
                  full(wkv), row(LANES), row(LANES)],
        out_specs=(col(2 * LANES), row(hq), col(MLA_VT_ROWS)),
        compiler_params=_cp(("arbitrary",)),
        name="mla_proj",
    )(x, g, modv, modv, w1, gq, gkv, wq, wkv, cos, sin)


def _qkv_proj_kernel(x_ref, g_ref, sc_ref, sh_ref, w_ref, cos_ref, sin_ref, o_ref, *, n_q, n_rope, q_scale, chunk):
    h = _normmod(x_ref[...], g_ref[...], sc_ref[0], sh_ref[0]).astype(BF16)
    n = w_ref.shape[1]
    for c0 in range(0, n, chunk):
        a = _dot(h, w_ref[:, c0:c0 + chunk])
        if c0 < n_rope:
            reps = chunk // LANES
            cos = jnp.concatenate([cos_ref[...]] * reps, axis=1)
            sin = jnp.concatenate([sin_ref[...]] * reps, axis=1)
            a = a * cos + _swap_halves(a, SW_HD) * sin
        if c0 < n_q:
            a = a * q_scale
        o_ref[:, c0:c0 + chunk] = a.astype(BF16)


def _qkv_proj(lay, x, g, modv, w, cos, sin, *, n_q, n_rope, q_scale, chunk=256):
    D, n = lay.D, w.shape[1]
    assert n % chunk == 0 and n_q % chunk == 0 and n_rope % chunk == 0
    full = lambda a: pl.BlockSpec(a.shape, lambda i: (0,) * a.ndim)
    row = lambda m: pl.BlockSpec((TM, m), lambda i: (i, 0))
    return pl.pallas_call(
        functools.partial(_qkv_proj_kernel, n_q=n_q, n_rope=n_rope, q_scale=q_scale, chunk=chunk),
        out_shape=jax.ShapeDtypeStruct((lay.NT, n), BF16),
        grid=(lay.n_tiles,),
        in_specs=[row(D), full(g), lay.mod_spec(1), lay.mod_spec(0), full(w), row(LANES), row(LANES)],
        out_specs=row(n),
        compiler_params=_cp(("arbitrary",)),
        name="qkv_proj",
    )(x, g, modv, modv, w, cos, sin)


def _mla_attn_kernel(*refs, tk, n_lat):
    if n_lat:
        qt_ref, kc_ref, vtc_ref, kl_ref, vtl_ref, o_ref, acc_ref, sa_ref, sb_ref = refs
    else:
        qt_ref, kc_ref, vtc_ref, o_ref, acc_ref = refs
    qt = qt_ref[0]

    st = _dot(kc_ref[...], qt)
    m = jnp.max(st, axis=0, keepdims=True)
    acc_ref[...] = _dot(vtc_ref[0], jnp.exp2(st - m).astype(BF16))

    if n_lat:
        nch = n_lat // tk

        def softmax_pv(st, vt, m):
            m_new = jnp.maximum(m, jnp.max(st, axis=0, keepdims=True))
            acc_ref[...] = jnp.exp2(m - m_new) * acc_ref[...] + _dot(vt, jnp.exp2(st - m_new).astype(BF16))
            return m_new

        sa_ref[...] = _dot(kl_ref[pl.ds(0, tk), :], qt)

        def body(jj, m):
            r0 = pl.multiple_of(2 * jj * tk, tk)
            r1 = pl.multiple_of((2 * jj + 1) * tk, tk)
            r2 = pl.multiple_of(jnp.minimum(2 * jj + 2, nch - 1) * tk, tk)
            sb_ref[...] = _dot(kl_ref[pl.ds(r1, tk), :], qt)
            m = softmax_pv(sa_ref[...], vtl_ref[0, :, pl.ds(r0, tk)], m)
            sa_ref[...] = _dot(kl_ref[pl.ds(r2, tk), :], qt)
            return softmax_pv(sb_ref[...], vtl_ref[0, :, pl.ds(r1, tk)], m)

        lax.fori_loop(0, nch // 2, body, m)
    o_ref[...] = (acc_ref[:MLA_V, :] / acc_ref[MLA_V:MLA_V + 1, :]).T.astype(o_ref.dtype)


def _mla_attention(lay, qt, k, vt, *, tq=512, tk=512):
    B, S, C = lay.B, lay.S, lay.C
    H = MLA_HEADS
    nq = S // tq
    cblk0 = lay.NL // C
    assert S % (2 * tk) == 0
    o_lat = pl.pallas_call(
        functools.partial(_mla_attn_kernel, tk=tk, n_lat=S),
        out_shape=jax.ShapeDtypeStruct((lay.NL, H * LANES), BF16),
        grid=(B, H, nq),
        in_specs=[pl.BlockSpec((1, 2 * LANES, tq), lambda b, h, i: (h, 0, b * nq + i)),
                  pl.BlockSpec((C, 2 * LANES), lambda b, h, i: (cblk0 + b, h)),
                  pl.BlockSpec((1, MLA_VT_ROWS, C), lambda b, h, i: (h, 0, cblk0 + b)),
                  pl.BlockSpec((S, 2 * LANES), lambda b, h, i: (b, h)),
                  pl.BlockSpec((1, MLA_VT_ROWS, S), lambda b, h, i: (h, 0, b))],
        out_specs=pl.BlockSpec((tq, LANES), lambda b, h, i: (b * nq + i, h)),
        scratch_shapes=[pltpu.VMEM((MLA_VT_ROWS, tq), F32), pltpu.VMEM((tk, tq), F32), pltpu.VMEM((tk, tq), F32)],
        compiler_params=_cp(("arbitrary", "arbitrary", "arbitrary")),
        name="mla_attn_latent",
    )(qt, k, vt, k, vt)
    o_ctx = pl.pallas_call(
        functools.partial(_mla_attn_kernel, tk=tk, n_lat=0),
        out_shape=jax.ShapeDtypeStruct((lay.NC, H * LANES), BF16),
        grid=(B, H),
        in_specs=[pl.BlockSpec((1, 2 * LANES, C), lambda b, h: (h, 0, cblk0 + b)),
                  pl.BlockSpec((C, 2 * LANES), lambda b, h: (cblk0 + b, h)),
                  pl.BlockSpec((1, MLA_VT_ROWS, C), lambda b, h: (h, 0, cblk0 + b))],
        out_specs=pl.BlockSpec((C, LANES), lambda b, h: (b, h)),
        scratch_shapes=[pltpu.VMEM((MLA_VT_ROWS, C), F32)],
        compiler_params=_cp(("arbitrary", "arbitrary")),
        name="mla_attn_ctx",
    )(qt, k, vt)
    return o_lat, o_ctx


def _dft_tables(S, C, gc):
    P = math.isqrt(S)
    assert P * P == S and (P & (P - 1)) == 0 and (gc & (gc - 1)) == 0 and (C & (C - 1)) == 0

    def cs(idx, n):
        ang = (idx % n).astype(F32) * (2.0 * math.pi / n)
        return jnp.cos(ang), jnp.sin(ang)

    k1 = jnp.arange(P, dtype=I32)
    idx = k1[None, :, None] * (P * k1[None, None, :] + k1[:, None, None])
    c, s = cs(idx, S)
    m1 = jnp.concatenate([c, -s], axis=1) * (1.0 / P)
    c, s = cs(k1[:, None] * k1[None, :], P)
    m2 = jnp.concatenate([jnp.concatenate([c, s], axis=1), jnp.concatenate([-s, c], axis=1)], axis=0)
    kc = jnp.arange(gc, dtype=I32)
    c, s = cs(kc[:, None] * kc[None, :], gc)
    mc = jnp.concatenate([c, s], axis=0) * (gc ** -0.5)
    kq = jnp.arange(C, dtype=I32)
    c, s = cs(kq[:, None] * kq[None, :], C)
    mctx = jnp.concatenate([c, s], axis=0) * (C ** -0.5)
    return m1.astype(BF16), m2.astype(BF16), mc.astype(BF16), mctx.astype(BF16)


def _fnet_stage1_kernel(x_ref, g_ref, sc_ref, sh_ref, m1_ref, z_ref, *, n2c, P):
    g, sc, sh = g_ref[...], sc_ref[0], sh_ref[0]
    for j in range(n2c):
        h = _normmod(x_ref[:, j, :], g, sc, sh).astype(BF16)
        z = _dot(m1_ref[j], h)
        z_ref[0, :, 0, j, :] = z[:P]
        z_ref[0, :, 1, j, :] = z[P:]


def _fnet_stage2_kernel(z_ref, m2_ref, mc_ref, f_ref, *, k1c, P, gc):
    D = f_ref.shape[-1]
    for j in range(k1c):
        z = z_ref[0, j].reshape(2 * P, D).astype(BF16)
        y = _dot(m2_ref[...], z)
        yr, yi = y[:P].astype(BF16), y[P:].astype(BF16)
        outs = []
        for gi in range(D // gc):
            sl = slice(gi * gc, (gi + 1) * gc)
            outs.append(_dot(yr[:, sl], mc_ref[:gc, :]) + _dot(yi[:, sl], mc_ref[gc:, :]))
        f_ref[:, j, :] = jnp.concatenate(outs, axis=1)


def _fnet_ctx_kernel(x_ref, g_ref, sc_ref, sh_ref, ml_ref, mc_ref, f_ref, *, C, gc):
    D = x_ref.shape[-1]
    h = _normmod(x_ref[...], g_ref[...], sc_ref[0], sh_ref[0]).astype(BF16)
    y = _dot(ml_ref[...], h)
    yc, ys = y[:C].astype(BF16), y[C:].astype(BF16)
    outs = []
    for gi in range(D // gc):
        sl = slice(gi * gc, (gi + 1) * gc)
        outs.append(_dot(yc[:, sl], mc_ref[:gc, :]) - _dot(ys[:, sl], mc_ref[gc:, :]))
    f_ref[...] = jnp.concatenate(outs, axis=1)


def _fnet_mix(lay, x, g, modv, tables):
    B, S, C, D = lay.B, lay.S, lay.C, lay.D
    m1, m2, mc, mctx = tables
    P = math.isqrt(S)
    gc = D // FNET_GROUPS
    n2c = SUBLANES
    k1c = SUBLANES
    full = lambda a: pl.BlockSpec(a.shape, lambda *i: (0,) * a.ndim)
    modspec = lambda chunk: pl.BlockSpec((1, 1, D), lambda b, j: (chunk * SUBLANES + b, 0, 0))
    assert C % P == 0 and P % n2c == 0 and P % k1c == 0
    x3 = x.reshape(lay.NT // P, P, D)
    z = pl.pallas_call(
        functools.partial(_fnet_stage1_kernel, n2c=n2c, P=P),
        out_shape=jax.ShapeDtypeStruct((B, P, 2, P, D), F32),
        grid=(B, P // n2c),
        in_specs=[pl.BlockSpec((P, n2c, D), lambda b, j: (b, j, 0)), full(g), modspec(1), modspec(0),
                  pl.BlockSpec((n2c, 2 * P, P), lambda b, j: (j, 0, 0))],
        out_specs=pl.BlockSpec((1, P, 2, n2c, D), lambda b, j: (b, 0, 0, j, 0)),
        compiler_params=_cp(("arbitrary", "arbitrary")),
        name="fnet_stage1",
    )(x3, g, modv, modv, m1)
    f_lat = pl.pallas_call(
        functools.partial(_fnet_stage2_kernel, k1c=k1c, P=P, gc=gc),
        out_shape=jax.ShapeDtypeStruct((lay.NL // P, P, D), F32),
        grid=(B, P // k1c),
        in_specs=[pl.BlockSpec((1, k1c, 2, P, D), lambda b, j: (b, j, 0, 0, 0)), full(m2), full(mc)],
        out_specs=pl.BlockSpec((P, k1c, D), lambda b, j: (b, j, 0)),
        compiler_params=_cp(("arbitrary", "arbitrary")),
        name="fnet_stage2",
    )(z, m2, mc)
    cblk0 = lay.NL // C
    ctx_mod = lambda chunk: pl.BlockSpec((1, 1, D), lambda b: (chunk * SUBLANES + B, 0, 0))
    f_ctx = pl.pallas_call(
        functools.partial(_fnet_ctx_kernel, C=C, gc=gc),
        out_shape=jax.ShapeDtypeStruct((lay.NC, D), F32),
        grid=(B,),
        in_specs=[pl.BlockSpec((C, D), lambda b: (cblk0 + b, 0)), full(g), ctx_mod(1), ctx_mod(0), full(mctx),
                  full(mc)],
        out_specs=pl.BlockSpec((C, D), lambda b: (b, 0)),
        compiler_params=_cp(("arbitrary",)),
        name="fnet_ctx",
    )(x, g, modv, modv, mctx, mc)
    return f_lat.reshape(lay.NL, D), f_ctx


def _attend(qm, loc, kc, vc, sink=None):
    sc = _dot_nt(qm, kc)
    m = jnp.max(sc, axis=-1, keepdims=True)
    if loc is not None:
        kw, vw, bias = loc
        s = _dot_nt(qm, kw) + bias
        m = jnp.maximum(m, jnp.max(s, axis=-1, keepdims=True))
    if sink is not None:
        m = jnp.maximum(m, sink)
    pc = jnp.exp(sc - m)
    l = jnp.sum(pc, axis=-1, keepdims=True)
    o = _dot(pc.astype(BF16), vc)
    if loc is not None:
        p = jnp.exp(s - m)
        l = l + jnp.sum(p, axis=-1, keepdims=True)
        o = o + _dot(p.astype(BF16), vw)
    if sink is not None:
        l = l + jnp.exp(sink - m)
    return o / l


NA_QROWS = 2
NA_KROWS = NA_KR + NA_QROWS - 1
NA_VARIANTS = 5


def _na_bias_kernel(rpb_ref, sel_ref, toe_ref, o_ref):
    g = jnp.dot(rpb_ref[0], toe_ref[...], precision=HIGHEST, preferred_element_type=F32)
    for t in range(NA_VARIANTS):
        o_ref[t, 0] = jnp.dot(sel_ref[t], g, precision=HIGHEST, preferred_element_type=F32)


def _na_bias_tables(rpb, rows):
    assert rows >= 16 and rows % NA_QROWS == 0
    W = GRID_W
    nh, nu, nv = rpb.shape
    up, vp, ajp = 2 * SUBLANES, LANES, 3 * SUBLANES
    assert nu <= up and nv <= vp and NA_QROWS * NA_KROWS <= ajp
    cq, ck = np.arange(W)[:, None], np.arange(W)[None, :]
    dc = np.clip(ck - cq + NA_KC - 1, 0, nv - 1).reshape(-1)
    toe = (np.arange(vp)[:, None] == dc[None, :]).astype(np.float32)
    cs = np.clip(cq - NA_KC // 2, 0, W - NA_KC)
    col_ok = (ck >= cs) & (ck < cs + NA_KC)
    sel = np.zeros((NA_VARIANTS, ajp, up), np.float32)
    valid = np.zeros((NA_VARIANTS, NA_QROWS, W, NA_KROWS, W), bool)
    for t, r in enumerate((0, 2, 6, rows - 4, rows - 2)):
        w0 = min(max(r - NA_KR // 2, 0), rows - NA_KROWS)
        for a in range(NA_QROWS):
            rs = min(max(r + a - NA_KR // 2, 0), rows - NA_KR)
            for j in range(NA_KROWS):
                rk = w0 + j
                sel[t, a * NA_KROWS + j, min(max(rk - (r + a) + NA_KR - 1, 0), nu - 1)] = 1.0
                if rs <= rk < rs + NA_KR:
                    valid[t, a, :, j, :] = col_ok
    rpb_p = jnp.pad(rpb, ((0, 0), (0, up - nu), (0, vp - nv)))
    tab = pl.pallas_call(
        _na_bias_kernel,
        out_shape=jax.ShapeDtypeStruct((NA_VARIANTS, nh, ajp, W * W), F32),
        grid=(nh,),
        in_specs=[pl.BlockSpec((1, up, vp), lambda h: (h, 0, 0)),
                  pl.BlockSpec(sel.shape, lambda h: (0, 0, 0)),
                  pl.BlockSpec(toe.shape, lambda h: (0, 0))],
        out_specs=pl.BlockSpec((NA_VARIANTS, 1, ajp, W * W), lambda h: (0, h, 0, 0)),
        compiler_params=_cp(("arbitrary",)),
        name="na_bias",
    )(rpb_p, jnp.asarray(sel), jnp.asarray(toe))
    tab = tab[:, :, :NA_QROWS * NA_KROWS].reshape(NA_VARIANTS, nh, NA_QROWS, NA_KROWS, W, W)
    tab = tab.transpose(0, 1, 2, 4, 3, 5).reshape(NA_VARIANTS, nh, NA_QROWS * W, NA_KROWS * W)
    valid = valid.reshape(NA_VARIANTS, 1, NA_QROWS * W, NA_KROWS * W)
    return jnp.where(jnp.asarray(valid), tab, NEG_INF)


def _na_kernel(*refs, rows, local):
    if local:
        q_ref, kc_ref, vc_ref, kl_ref, vl_ref, bias_ref, o_ref = refs
    else:
        q_ref, kc_ref, vc_ref, o_ref = refs
    lane = _lane_iota((1, LANES))
    masks = (lane < NA_HD, lane >= NA_HD)
    kc, vc = kc_ref[...], vc_ref[...]
    vcm = [jnp.where(mk, vc, jnp.zeros_like(vc)) for mk in masks]
    nq = NA_QROWS * GRID_W
    nk = NA_KROWS * GRID_W

    def block(r0, loc_of):
        q2 = q_ref[pl.ds(r0, nq), :]
        o = jnp.zeros((nq, LANES), F32)
        for hl in range(2):
            qm = jnp.where(masks[hl], q2, jnp.zeros_like(q2))
            o = o + _attend(qm, loc_of(hl), kc, vcm[hl])
        o_ref[pl.ds(r0, nq), :] = o.astype(o_ref.dtype)

    if not local:
        for t in range(q_ref.shape[0] // nq):
            block(t * nq, lambda hl: None)
        return

    qi = pl.program_id(2)
    pairs = q_ref.shape[0] // nq

    def body(t, c):
        r = (qi * pairs + t) * NA_QROWS
        w0 = jnp.clip(r - NA_KR // 2, 0, rows - NA_KROWS)
        k0 = pl.multiple_of(w0 * GRID_W, GRID_W)
        kw, vw = kl_ref[pl.ds(k0, nk), :], vl_ref[pl.ds(k0, nk), :]
        var = jnp.where(r == 0, 0, jnp.where(r == 2, 1, jnp.where(r == rows - 4, 3, jnp.where(r == rows - 2, 4, 2))))
        vwm = [jnp.where(mk, vw, jnp.zeros_like(vw)) for mk in masks]
        block(pl.multiple_of(t * nq, nq), lambda hl: (kw, vwm[hl], bias_ref[var, hl]))
        return c

    lax.fori_loop(0, pairs, body, 0)


def _na_attention(lay, qkv, bias, *, tq=512):
    B, S, C = lay.B, lay.S, lay.C
    rows = S // GRID_W
    HP = NA_HEADS // 2
    nq = S // tq
    cblk0 = lay.NL // C
    o_lat = pl.pallas_call(
        functools.partial(_na_kernel, rows=rows, local=True),
        out_shape=jax.ShapeDtypeStruct((lay.NL, NA_HEADS * NA_HD), BF16),
        grid=(B, HP, nq),
        in_specs=[pl.BlockSpec((tq, LANES), lambda b, h, i: (b * nq + i, h)),
                  pl.BlockSpec((C, LANES), lambda b, h, i: (cblk0 + b, HP + h)),
                  pl.BlockSpec((C, LANES), lambda b, h, i: (cblk0 + b, 2 * HP + h)),
                  pl.BlockSpec((S, LANES), lambda b, h, i: (b, HP + h)),
                  pl.BlockSpec((S, LANES), lambda b, h, i: (b, 2 * HP + h)),
                  pl.BlockSpec((NA_VARIANTS, 2) + bias.shape[2:], lambda b, h, i: (0, h, 0, 0))],
        out_specs=pl.BlockSpec((tq, LANES), lambda b, h, i: (b * nq + i, h)),
        compiler_params=_cp(("arbitrary", "arbitrary", "arbitrary")),
        name="na_attn_latent",
    )(qkv, qkv, qkv, qkv, qkv, bias)
    o_ctx = pl.pallas_call(
        functools.partial(_na_kernel, rows=rows, local=False),
        out_shape=jax.ShapeDtypeStruct((lay.NC, NA_HEADS * NA_HD), BF16),
        grid=(B, HP),
        in_specs=[pl.BlockSpec((C, LANES), lambda b, h: (cblk0 + b, h)),
                  pl.BlockSpec((C, LANES), lambda b, h: (cblk0 + b, HP + h)),
                  pl.BlockSpec((C, LANES), lambda b, h: (cblk0 + b, 2 * HP + h))],
        out_specs=pl.BlockSpec((C, LANES), lambda b, h: (b, h)),
        compiler_params=_cp(("arbitrary", "arbitrary")),
        name="na_attn_ctx",
    )(qkv, qkv, qkv)
    return o_lat, o_ctx


SW_SUB = 128
SW_BAND = SW_SUB + 2 * SW_WINDOW


def _swa_kernel(*refs, S, local):
    if local:
        (sink_ref, q_ref, kc_ref, vc_ref, kp_ref, kcur_ref, kn_ref, vp_ref, vcur_ref, vn_ref, o_ref,
         kbuf, vbuf) = refs
        tq = q_ref.shape[0]
        W = SW_WINDOW
        kbuf[0:W] = kp_ref[...]
        kbuf[W:W + tq] = kcur_ref[...]
        kbuf[W + tq:] = kn_ref[...]
        vbuf[0:W] = vp_ref[...]
        vbuf[W:W + tq] = vcur_ref[...]
        vbuf[W + tq:] = vn_ref[...]
        i = pl.program_id(1)
    else:
        sink_ref, q_ref, kc_ref, vc_ref, o_ref = refs
        tq = q_ref.shape[0]
    lane = _lane_iota((1, LANES))
    masks = (lane < SW_HD, lane >= SW_HD)
    G = SW_HEADS // SW_KV_HEADS
    kc, vc = kc_ref[...], vc_ref[...]

    def sub(sb, c):
        r0 = pl.multiple_of(sb * SW_SUB, SW_SUB)
        if local:
            kw, vw = kbuf[pl.ds(r0, SW_BAND), :], vbuf[pl.ds(r0, SW_BAND), :]
            row = lax.broadcasted_iota(I32, (SW_SUB, SW_BAND), 0)
            col = lax.broadcasted_iota(I32, (SW_SUB, SW_BAND), 1)
            rel = col - SW_WINDOW - row
            kpos = i * tq + sb * SW_SUB - SW_WINDOW + col
            ok = (jnp.abs(rel) <= SW_WINDOW) & (kpos >= 0) & (kpos < S)
            bias = jnp.where(ok, 0.0, NEG_INF).astype(F32)
        for t in range(SW_HEADS // 2):
            kvh = (2 * t) // G
            tile, half = kvh // 2, kvh % 2
            sl = slice(tile * LANES, (tile + 1) * LANES)
            q2 = q_ref[pl.ds(r0, SW_SUB), t * LANES:(t + 1) * LANES].astype(F32)
            q2r = pltpu.roll(q2, SW_HD, 1)
            acc = jnp.zeros((SW_SUB, LANES), F32)
            for qh in range(2):
                src = q2 if qh == half else q2r
                qm = jnp.where(masks[half], src, 0.0).astype(BF16)
                loc = (kw[:, sl], vw[:, sl], bias) if local else None
                oh = _attend(qm, loc, kc[:, sl], vc[:, sl], sink_ref[2 * t + qh])
                if qh != half:
                    oh = pltpu.roll(oh, SW_HD, 1)
                acc = acc + jnp.where(masks[qh], oh, 0.0)
            o_ref[pl.ds(r0, SW_SUB), t * LANES:(t + 1) * LANES] = acc.astype(o_ref.dtype)
        return c

    lax.fori_loop(0, tq // SW_SUB, sub, 0)


def _swa_attention(lay, qkv, sinks, *, tq=512):
    B, S, C = lay.B, lay.S, lay.C
    nq_cols = SW_HEADS * SW_HD
    nkv = SW_KV_HEADS * SW_HD
    kcol, vcol = nq_cols // nkv, nq_cols // nkv + 1
    nq = S // tq
    per = tq // SW_WINDOW
    last = lay.NT // SW_WINDOW - 1
    cblk0 = lay.NL // C
    prev = lambda b, i: jnp.maximum((b * nq + i) * per - 1, 0)
    nxt = lambda b, i: jnp.minimum((b * nq + i + 1) * per, last)
    smem = pl.BlockSpec(memory_space=pltpu.SMEM)
    o_lat = pl.pallas_call(
        functools.partial(_swa_kernel, S=S, local=True),
        out_shape=jax.ShapeDtypeStruct((lay.NL, nq_cols), BF16),
        grid=(B, nq),
        in_specs=[smem,
                  pl.BlockSpec((tq, nq_cols), lambda b, i: (b * nq + i, 0)),
                  pl.BlockSpec((C, nkv), lambda b, i: (cblk0 + b, kcol)),
                  pl.BlockSpec((C, nkv), lambda b, i: (cblk0 + b, vcol)),
                  pl.BlockSpec((SW_WINDOW, nkv), lambda b, i: (prev(b, i), kcol)),
                  pl.BlockSpec((tq, nkv), lambda b, i: (b * nq + i, kcol)),
                  pl.BlockSpec((SW_WINDOW, nkv), lambda b, i: (nxt(b, i), kcol)),
                  pl.BlockSpec((SW_WINDOW, nkv), lambda b, i: (prev(b, i), vcol)),
                  pl.BlockSpec((tq, nkv), lambda b, i: (b * nq + i, vcol)),
                  pl.BlockSpec((SW_WINDOW, nkv), lambda b, i: (nxt(b, i), vcol))],
        out_specs=pl.BlockSpec((tq, nq_cols), lambda b, i: (b * nq + i, 0)),
        scratch_shapes=[pltpu.VMEM((tq + 2 * SW_WINDOW, nkv), BF16), pltpu.VMEM((tq + 2 * SW_WINDOW, nkv), BF16)],
        compiler_params=_cp(("arbitrary", "arbitrary")),
        name="swa_attn_latent",
    )(sinks, qkv, qkv, qkv, qkv, qkv, qkv, qkv, qkv, qkv)
    o_ctx = pl.pallas_call(
        functools.partial(_swa_kernel, S=S, local=False),
        out_shape=jax.ShapeDtypeStruct((lay.NC, nq_cols), BF16),
        grid=(B,),
        in_specs=[smem,
                  pl.BlockSpec((C, nq_cols), lambda b: (cblk0 + b, 0)),
                  pl.BlockSpec((C, nkv), lambda b: (cblk0 + b, kcol)),
                  pl.BlockSpec((C, nkv), lambda b: (cblk0 + b, vcol))],
        out_specs=pl.BlockSpec((C, nq_cols), lambda b: (b, 0)),
        compiler_params=_cp(("arbitrary",)),
        name="swa_attn_ctx",
    )(sinks, qkv, qkv, qkv)
    return o_lat, o_ctx


def _route(logits, tri, carry):
    lane = _lane_iota(logits.shape)
    lanef = lane.astype(F32)
    big = float(LANES)
    rowmax = lambda t: jnp.max(t, axis=-1, keepdims=True)
    rowmin = lambda t: jnp.min(t, axis=-1, keepdims=True)
    rowsum = lambda t: jnp.sum(t, axis=-1, keepdims=True)
    is_g = lane < MOE_GROUPS
    mg = rowmax(jnp.where(is_g, logits, -jnp.inf))
    w_g = 1.0 / rowsum(jnp.where(is_g, jnp.exp(logits - mg), 0.0))
    gidx = rowmin(jnp.where(is_g & (logits == mg), lanef, big))
    g0 = MOE_GROUPS + MOE_PER_GROUP * gidx
    in_grp = (lanef >= g0) & (lanef < g0 + MOE_PER_GROUP)
    le = jnp.where(in_grp, logits, -jnp.inf)
    m1 = rowmax(le)
    i1 = rowmin(jnp.where(in_grp & (le == m1), lanef, big))
    le2 = jnp.where(lanef == i1, -jnp.inf, le)
    m2 = rowmax(le2)
    i2 = rowmin(jnp.where(in_grp & (lanef != i1) & (le2 == m2), lanef, big))
    r = jnp.exp(m2 - m1)
    gate1 = w_g / (1.0 + r)
    gate2 = w_g * r / (1.0 + r)
    sel1, sel2 = lanef == i1, lanef == i2
    member = (sel1 | sel2)
    cum = _dot(tri, member.astype(BF16)) + carry
    rank1 = rowsum(jnp.where(sel1, cum, 0.0))
    rank2 = rowsum(jnp.where(sel2, cum, 0.0))
    new_carry = carry + jnp.sum(member.astype(F32), axis=0, keepdims=True)
    rec = jnp.zeros_like(logits)
    for ln, val in ((0, i1 - MOE_GROUPS), (1, i2 - MOE_GROUPS), (2, rank1), (3, rank2), (4, gate1), (5, gate2)):
        rec = jnp.where(lane == ln, val, rec)
    return rec, new_carry


def _out_proj_kernel(*refs, has_bias, n_lat_tiles):
    if has_bias:
        (al_ref, ac_ref, x_ref, w_ref, b_ref, g1_ref, g2n_ref, sc_ref, sh_ref, wr_ref, br_ref, tri_ref,
         xo_ref, h2_ref, rec_ref, cnt_ref, carry_ref) = refs
    else:
        (al_ref, ac_ref, x_ref, w_ref, g1_ref, g2n_ref, sc_ref, sh_ref, wr_ref, br_ref, tri_ref,
         xo_ref, h2_ref, rec_ref, cnt_ref, carry_ref) = refs

    @pl.when(pl.program_id(0) == 0)
    def _():
        carry_ref[...] = jnp.zeros_like(carry_ref)

    a = jnp.where(pl.program_id(0) < n_lat_tiles, al_ref[...], ac_ref[...])
    y = _dot(a.astype(BF16), w_ref[...])
    if has_bias:
        y = y + b_ref[...]
    xn = x_ref[...] + g1_ref[0] * y
    xo_ref[...] = xn
    h2 = _normmod(xn, g2n_ref[...], sc_ref[0], sh_ref[0])
    h2_ref[...] = h2
    logits = jnp.dot(h2, wr_ref[...], precision=HIGHEST, preferred_element_type=F32) + br_ref[...]
    rec, carry = _route(logits, tri_ref[...], carry_ref[...])
    rec_ref[...] = rec
    carry_ref[...] = carry
    cnt_ref[...] = jnp.broadcast_to(carry, cnt_ref.shape)


def _out_proj(lay, a, x, w, b, g2n, modv, wr, br, tri):
    D = lay.D
    a_lat, a_ctx = a
    nl = lay.nl_tiles
    full = lambda t: pl.BlockSpec(t.shape, lambda i: (0,) * t.ndim)
    row = lambda n: pl.BlockSpec((TM, n), lambda i: (i, 0))
    ins = [a_lat, a_ctx, x, w] + ([b] if b is not None else []) + [modv, g2n, modv, modv, wr, br, tri]
    specs = ([pl.BlockSpec((TM, a_lat.shape[1]), lambda i: (jnp.minimum(i, nl - 1), 0)),
              pl.BlockSpec((TM, a_ctx.shape[1]), lambda i: (jnp.maximum(i - nl, 0), 0)), row(D), full(w)]
             + ([full(b)] if b is not None else [])
             + [lay.mod_spec(2), full(g2n), lay.mod_spec(4), lay.mod_spec(3), full(wr), full(br), full(tri)])
    return pl.pallas_call(
        functools.partial(_out_proj_kernel, has_bias=b is not None, n_lat_tiles=nl),
        out_shape=(jax.ShapeDtypeStruct((lay.NT, D), F32), jax.ShapeDtypeStruct((lay.NT, D), F32),
                   jax.ShapeDtypeStruct((lay.NT, LANES), F32), jax.ShapeDtypeStruct((SUBLANES, LANES), F32)),
        grid=(lay.n_tiles,),
        in_specs=specs,
        out_specs=(row(D), row(D), row(LANES), pl.BlockSpec((SUBLANES, LANES), lambda i: (0, 0))),
        scratch_shapes=[pltpu.VMEM((1, LANES), F32)],
        input_output_aliases={2: 0},
        compiler_params=_cp(("arbitrary",)),
        name="out_proj_router",
    )(*ins)


def _row_copy(src, s, dst, d, sem):
    return pltpu.make_async_copy(src.at[pl.ds(s, 1), :], dst.at[pl.ds(d, 1), :], sem)


def _dispatch_kernel(pos_ref, h_ref, xb_in_ref, xb_ref, sem):
    del xb_in_ref
    base = pl.program_id(0) * TM

    def issue(r, c):
        for k in range(2):
            _row_copy(h_ref, r, xb_ref, pos_ref[(base + r) * 2 + k], sem).start()
        return c

    lax.fori_loop(0, TM, issue, 0, unroll=8)
    for k in range(2):
        pltpu.make_async_copy(h_ref, xb_ref.at[pl.ds(0, TM), :], sem).wait()


def _dispatch(lay, pos, h2, cap):
    D = lay.D
    return pl.pallas_call(
        _dispatch_kernel,
        out_shape=jax.ShapeDtypeStruct((cap, D), F32),
        grid_spec=pltpu.PrefetchScalarGridSpec(
            num_scalar_prefetch=1, grid=(lay.n_tiles,),
            in_specs=[pl.BlockSpec((TM, D), lambda i, p: (i, 0)), pl.BlockSpec(memory_space=pl.ANY)],
            out_specs=pl.BlockSpec(memory_space=pl.ANY),
            scratch_shapes=[pltpu.SemaphoreType.DMA]),
        input_output_aliases={2: 0},
        compiler_params=_cp(("arbitrary",)),
        name="moe_dispatch",
    )(pos, h2, jnp.zeros((cap, D), F32))


def _expert_kernel(be_ref, nu_ref, xb_ref, wg_ref, wu_ref, wd_ref, yb_ref, wgb, wub, wdb):
    j = pl.program_id(0)
    prev = be_ref[jnp.maximum(j - 1, 0)]

    @pl.when((j == 0) | (be_ref[j] != prev))
    def _():
        wgb[...] = wg_ref[0].astype(BF16)
        wub[...] = wu_ref[0].astype(BF16)
        wdb[...] = wd_ref[0].astype(BF16)

    @pl.when(j < nu_ref[0])
    def _():
        xe = xb_ref[...].astype(BF16)
        g = _dot(xe, wgb[...])
        u = _dot(xe, wub[...])
        act = (g * jax.nn.sigmoid(g) * u).astype(BF16)
        yb_ref[...] = _dot(act, wdb[...])

    @pl.when(j >= nu_ref[0])
    def _():
        yb_ref[...] = jnp.zeros_like(yb_ref)


def _experts(xb, blk_e, n_used, w_gate, w_up, w_down):
    cap, D = xb.shape
    FF = w_gate.shape[-1]
    nblk = cap // MOE_BM
    blk = lambda j, be, nu: (jnp.minimum(j, nu[0] - 1), 0)
    return pl.pallas_call(
        _expert_kernel,
        out_shape=jax.ShapeDtypeStruct((cap, D), F32),
        grid_spec=pltpu.PrefetchScalarGridSpec(
            num_scalar_prefetch=2, grid=(nblk,),
            in_specs=[pl.BlockSpec((MOE_BM, D), blk),
                      pl.BlockSpec((1, D, FF), lambda j, be, nu: (be[j], 0, 0)),
                      pl.BlockSpec((1, D, FF), lambda j, be, nu: (be[j], 0, 0)),
                      pl.BlockSpec((1, FF, D), lambda j, be, nu: (be[j], 0, 0))],
            out_specs=pl.BlockSpec((MOE_BM, D), lambda j, be, nu: (j, 0)),
            scratch_shapes=[pltpu.VMEM((D, FF), BF16), pltpu.VMEM((D, FF), BF16), pltpu.VMEM((FF, D), BF16)]),
        compiler_params=_cp(("arbitrary",)),
        name="moe_experts",
    )(blk_e, n_used, xb, w_gate, w_up, w_down)


def _combine_kernel(*refs, final):
    if final:
        pos_ref, x_ref, rec_ref, g2_ref, yb_ref, fg_ref, o_ref, buf, sem = refs
    else:
        pos_ref, x_ref, rec_ref, g2_ref, yb_ref, o_ref, buf, sem = refs
    base = pl.program_id(0) * TM

    def issue(r, c):
        for k in range(2):
            _row_copy(yb_ref, pos_ref[(base + r) * 2 + k], buf.at[k], r, sem).start()
        return c

    lax.fori_loop(0, TM, issue, 0, unroll=8)
    for k in range(2):
        pltpu.make_async_copy(yb_ref.at[pl.ds(0, TM), :], buf.at[k], sem).wait()
    rec = rec_ref[...]
    f = rec[:, ROUTE_LANE_GATE:ROUTE_LANE_GATE + 1] * buf[0] + rec[:, ROUTE_LANE_GATE + 1:ROUTE_LANE_GATE + 2] * buf[1]
    xn = x_ref[...] + g2_ref[0] * f
    if final:
        xn = _rms(xn, fg_ref[...])
    o_ref[...] = xn


def _combine(lay, pos, x, rec, modv, yb, final_g):
    D = lay.D
    final = final_g is not None
    row = lambda n: pl.BlockSpec((TM, n), lambda i, p: (i, 0))
    specs = [row(D), row(LANES),
             pl.BlockSpec((1, 1, D), lambda i, p: (5 * SUBLANES + lay.mod_row(i), 0, 0)),
             pl.BlockSpec(memory_space=pl.ANY)]
    ins = [x, rec, modv, yb]
    if final:
        specs.append(pl.BlockSpec(final_g.shape, lambda i, p: (0, 0)))
        ins.append(final_g)
    return pl.pallas_call(
        functools.partial(_combine_kernel, final=final),
        out_shape=jax.ShapeDtypeStruct((lay.NT, D), F32),
        grid_spec=pltpu.PrefetchScalarGridSpec(
            num_scalar_prefetch=1, grid=(lay.n_tiles,),
            in_specs=specs,
            out_specs=row(D),
            scratch_shapes=[pltpu.VMEM((2, TM, D), F32), pltpu.SemaphoreType.DMA]),
        input_output_aliases={1: 0},
        compiler_params=_cp(("arbitrary",)),
        name="moe_combine",
    )(pos, *ins)


def _moe(lay, x, h2, rec, counts, modv, w_gate, w_up, w_down, final_g):
    T = 2 * lay.NT
    cap = -(-T // MOE_BM) * MOE_BM + MOE_EXPERTS * MOE_BM
    cnt = counts[0, MOE_GROUPS:MOE_GROUPS + MOE_EXPERTS].astype(I32)
    pcnt = (cnt + MOE_BM - 1) // MOE_BM * MOE_BM
    pend = jnp.cumsum(pcnt)
    start = pend - pcnt
    eid = rec[:, ROUTE_LANE_EID:ROUTE_LANE_EID + 2].astype(I32)
    rank = rec[:, ROUTE_LANE_RANK:ROUTE_LANE_RANK + 2].astype(I32)
    pos = (start[eid] + rank).reshape(-1)
    blk_e = jnp.minimum(jnp.searchsorted(pend, jnp.arange(cap // MOE_BM, dtype=I32) * MOE_BM, side='right'),
                        MOE_EXPERTS - 1).astype(I32)
    n_used = (pend[-1:] // MOE_BM).astype(I32)
    xb = _dispatch(lay, pos, h2, cap)
    yb = _experts(xb, blk_e, n_used, w_gate, w_up, w_down)
    return _combine(lay, pos, x, rec, modv, yb, final_g)


def _rope_tables(lay):
    S = lay.S
    t = jnp.arange(S)
    n = MLA_ROPE // 4
    inv = ROPE_THETA ** (-jnp.arange(n, dtype=F32) / n)
    ang = jnp.concatenate([(t // GRID_W).astype(F32)[:, None] * inv, (t % GRID_W).astype(F32)[:, None] * inv], axis=-1)
    cos, sin = jnp.cos(ang), jnp.sin(ang)
    cos64 = jnp.concatenate([cos, cos], axis=-1)
    sin64 = jnp.concatenate([-sin, sin], axis=-1)
    rows = lambda lat, ctx_val: jnp.concatenate([jnp.tile(lat, (lay.B, 1)), jnp.full((lay.NC, 64), ctx_val, F32)], axis=0)
    cos64, sin64 = rows(cos64, 1.0), rows(sin64, 0.0)
    zero = jnp.zeros_like(cos64)
    return ((jnp.concatenate([cos64, zero], axis=1), jnp.concatenate([sin64, zero], axis=1)),
            (jnp.concatenate([cos64, cos64], axis=1), jnp.concatenate([sin64, sin64], axis=1)))


def kernel(x, c, ctx, c_ctx, mod_w, mod_b, norm1_g, norm2_g, mla_w_dq, mla_g_q, mla_w_uq, mla_w_dkv, mla_g_kv, mla_w_ukv, mla_w_o, fnet_w_o, fnet_b_o, na_w_qkv, na_rpb, na_w_o, swa_w_qkv, swa_sinks, swa_w_o, moe_w_grp, moe_b_grp, moe_w_rt, moe_b_rt, moe_w_gate, moe_w_up, moe_w_down, final_g):
    B, S, D = x.shape
    C = ctx.shape[1]
    depth = mod_w.shape[0]
    lay = _Layout(B, S, C, D)
    X = jnp.concatenate([x.reshape(B * S, D), ctx.reshape(B * C, D)], axis=0)
    cond = jnp.concatenate([c, c_ctx[None], jnp.zeros((SUBLANES - B - 1, D), F32)], axis=0)
    mod = _modulation(cond, mod_w, mod_b)
    (mla_cos, mla_sin), (swa_cos, swa_sin) = _rope_tables(lay)
    tri = (jnp.arange(TM)[:, None] > jnp.arange(TM)[None, :]).astype(BF16)
    n_mix = 4
    for i in range(depth):
        m, j = i % n_mix, i // n_mix
        modv = mod[i].reshape(SUBLANES, 6, D).transpose(1, 0, 2).reshape(6 * SUBLANES, 1, D)
        g1n, g2n = norm1_g[i][None], norm2_g[i][None]
        bias = None
        if m == 0:
            w1 = jnp.concatenate([mla_w_dq[j], mla_w_dkv[j], jnp.zeros((D, LANES - MLA_ROPE), F32)], axis=1).astype(BF16)
            wq = mla_w_uq[j].reshape(MLA_Q_RANK, MLA_HEADS, MLA_NOPE + MLA_ROPE)
            wq = jnp.concatenate([wq, jnp.zeros((MLA_Q_RANK, MLA_HEADS, LANES - MLA_ROPE), F32)], axis=-1)
            wq = wq.reshape(MLA_Q_RANK, MLA_HEADS * 2 * LANES).astype(BF16)
            q, k, v = _mla_proj(lay, X, g1n, modv, w1, mla_g_q[j][None], mla_g_kv[j][None], wq,
                                mla_w_ukv[j].astype(BF16), mla_cos, mla_sin)
            a = _mla_attention(lay, q, k, v)
            w_o = mla_w_o[j]
        elif m == 1:
            a = _fnet_mix(lay, X, g1n, modv, _dft_tables(S, C, D // FNET_GROUPS))
            w_o, bias = fnet_w_o[j], fnet_b_o[j][None]
        elif m == 2:
            qkv = _qkv_proj(lay, X, g1n, modv, na_w_qkv[j].astype(BF16), swa_cos, swa_sin,
                            n_q=NA_HEADS * NA_HD, n_rope=0, q_scale=NA_HD ** -0.5)
            a = _na_attention(lay, qkv, _na_bias_tables(na_rpb[j], S // GRID_W))
            w_o = na_w_o[j]
        else:
            qkv = _qkv_proj(lay, X, g1n, modv, swa_w_qkv[j].astype(BF16), swa_cos, swa_sin,
                            n_q=SW_HEADS * SW_HD, n_rope=(SW_HEADS + SW_KV_HEADS) * SW_HD, q_scale=SW_HD ** -0.5)
            a = _swa_attention(lay, qkv, swa_sinks[j])
            w_o = swa_w_o[j]
        wr = jnp.concatenate([moe_w_grp[i], moe_w_rt[i], jnp.zeros((D, LANES - MOE_GROUPS - MOE_EXPERTS), F32)], axis=1)
        br = jnp.concatenate([moe_b_grp[i], moe_b_rt[i], jnp.zeros((LANES - MOE_GROUPS - MOE_EXPERTS,), F32)])[None]
        X, h2, rec, counts = _out_proj(lay, a, X, w_o.astype(BF16), bias, g2n, modv, wr, br, tri)
        X = _moe(lay, X, h2, rec, counts, modv, moe_w_gate[i], moe_w_up[i], moe_w_down[i],
                 final_g[None] if i == depth - 1 else None)
    return X[:lay.NL].reshape(B, S, D)
```

```python
import functools
import math

import jax
import jax.numpy as jnp
import numpy as np
from jax import lax
from jax.experimental import pallas as pl
from jax.experimental.pallas import tpu as pltpu

F32 = jnp.float32
BF16 = jnp.bfloat16
I32 = jnp.int32
HIGHEST = lax.Precision.HIGHEST

GRID_W = 64
EPS = 1e-6
ROPE_THETA = 10000.0
NEG_INF = -1e30
MLA_HEADS, MLA_Q_RANK, MLA_KV_RANK, MLA_NOPE, MLA_ROPE, MLA_V = 8, 512, 256, 128, 64, 128
MLA_VT_ROWS = MLA_V + 16
FNET_GROUPS = 4
NA_HEADS, NA_HD, NA_KR, NA_KC = 16, 64, 8, 16
SW_HEADS, SW_KV_HEADS, SW_HD, SW_WINDOW = 16, 4, 64, 128
MOE_GROUPS, MOE_PER_GROUP, MOE_FF = 4, 8, 512
MOE_EXPERTS = MOE_GROUPS * MOE_PER_GROUP

LANES = 128
SUBLANES = 8
TM = 512
MOE_BM = 512
VMEM_LIMIT = 56 * 1024 * 1024
ROUTE_LANE_EID, ROUTE_LANE_RANK, ROUTE_LANE_GATE = 0, 2, 4


def _cp(sem, vmem=VMEM_LIMIT):
    return pltpu.CompilerParams(dimension_semantics=sem, vmem_limit_bytes=vmem)


def _lane_iota(shape):
    return lax.broadcasted_iota(I32, shape, len(shape) - 1)


def _normmod(x, g, sc, sh):
    ms = jnp.mean(x * x, axis=-1, keepdims=True)
    return (x * lax.rsqrt(ms + EPS) * g) * (1.0 + sc) + sh


def _rms(x, g):
    ms = jnp.mean(x * x, axis=-1, keepdims=True)
    return x * lax.rsqrt(ms + EPS) * g


def _swap_halves(t, period):
    n = t.shape[-1]
    half = period // 2
    lane = _lane_iota(t.shape)
    return jnp.where((lane % period) < half, pltpu.roll(t, n - half, 1), pltpu.roll(t, half, 1))


def _dot(a, b):
    return jnp.dot(a, b, preferred_element_type=F32)


def _dot_nt(a, b):
    return lax.dot_general(a, b, (((1,), (1,)), ((), ())), preferred_element_type=F32)


def _mod_kernel(a_ref, w_ref, b_ref, o_ref):
    a = a_ref[...]
    a = a * jax.nn.sigmoid(a)
    o_ref[0] = jnp.dot(a, w_ref[0], precision=HIGHEST, preferred_element_type=F32) + b_ref[0]


def _modulation(cond, mod_w, mod_b):
    depth, d, n = mod_w.shape
    tn = n // 4
    return pl.pallas_call(
        _mod_kernel,
        out_shape=jax.ShapeDtypeStruct((depth, SUBLANES, n), F32),
        grid=(depth, n // tn),
        in_specs=[pl.BlockSpec((SUBLANES, d), lambda l, j: (0, 0)),
                  pl.BlockSpec((1, d, tn), lambda l, j: (l, 0, j)),
                  pl.BlockSpec((1, 1, tn), lambda l, j: (l, 0, j))],
        out_specs=pl.BlockSpec((1, SUBLANES, tn), lambda l, j: (l, 0, j)),
        compiler_params=_cp(("arbitrary", "arbitrary")),
        name="modulation",
    )(cond, mod_w, mod_b.reshape(depth, 1, n))


class _Layout:
    def __init__(self, B, S, C, D):
        self.B, self.S, self.C, self.D = B, S, C, D
        self.NL, self.NC = B * S, B * C
        self.NT = self.NL + self.NC
        assert S % TM == 0 and self.NC % TM == 0 and TM % C == 0
        self.nl_tiles = self.NL // TM
        self.n_tiles = self.NT // TM
        self.tiles_per_batch = S // TM

    def mod_row(self, i):
        return jnp.where(i < self.nl_tiles, i // self.tiles_per_batch, self.B)

    def mod_spec(self, chunk):
        return pl.BlockSpec((1, 1, self.D), lambda i: (chunk * SUBLANES + self.mod_row(i), 0, 0))


def _mla_proj_kernel(x_ref, g_ref, sc_ref, sh_ref, w1_ref, gq_ref, gkv_ref, wq_ref, wkv_ref, cos_ref, sin_ref,
                     qt_ref, k_ref, vt_ref):
    h = _normmod(x_ref[...], g_ref[...], sc_ref[0], sh_ref[0]).astype(BF16)
    a = _dot(h, w1_ref[...])
    qa = _rms(a[:, :MLA_Q_RANK], gq_ref[...]).astype(BF16)
    ckv = _rms(a[:, MLA_Q_RANK:MLA_Q_RANK + MLA_KV_RANK], gkv_ref[...]).astype(BF16)
    cos, sin = cos_ref[...], sin_ref[...]

    def rope(t):
        return t * cos + _swap_halves(t, MLA_ROPE) * sin

    kr = rope(a[:, MLA_Q_RANK + MLA_KV_RANK:]).astype(BF16)
    scale = (MLA_NOPE + MLA_ROPE) ** -0.5 * math.log2(math.e)
    q = _dot(qa, wq_ref[...])
    kv = _dot(ckv, wkv_ref[...])
    ones = jnp.ones((MLA_VT_ROWS - MLA_V, x_ref.shape[0]), BF16)
    for hd in range(MLA_HEADS):
        c = hd * 2 * LANES
        qh = jnp.concatenate([q[:, c:c + LANES], rope(q[:, c + LANES:c + 2 * LANES])], axis=1) * scale
        qt_ref[hd] = qh.T.astype(BF16)
        k_ref[:, c:c + LANES] = kv[:, c:c + LANES].astype(BF16)
        k_ref[:, c + LANES:c + 2 * LANES] = kr
        vt_ref[hd, :MLA_V, :] = kv[:, c + LANES:c + 2 * LANES].T.astype(BF16)
        vt_ref[hd, MLA_V:, :] = ones


def _mla_proj(lay, x, g, modv, w1, gq, gkv, wq, wkv, cos, sin):
    D = lay.D
    full = lambda a: pl.BlockSpec(a.shape, lambda i: (0,) * a.ndim)
    row = lambda n: pl.BlockSpec((TM, n), lambda i: (i, 0))
    col = lambda r: pl.BlockSpec((MLA_HEADS, r, TM), lambda i: (0, 0, i))
    hq = MLA_HEADS * 2 * LANES
    return pl.pallas_call(
        _mla_proj_kernel,
        out_shape=(jax.ShapeDtypeStruct((MLA_HEADS, 2 * LANES, lay.NT), BF16),
                   jax.ShapeDtypeStruct((lay.NT, hq), BF16),
                   jax.ShapeDtypeStruct((MLA_HEADS, MLA_VT_ROWS, lay.NT), BF16)),
        grid=(lay.n_tiles,),
        in_specs=[row(D), full(g), lay.mod_spec(1), lay.mod_spec(0), full(w1), full(gq), full(gkv), full(wq),
                  full(wkv), row(LANES), row(LANES)],
        out_specs=(col(2 * LANES), row(hq), col(MLA_VT_ROWS)),
        compiler_params=_cp(("arbitrary",)),
        name="mla_proj",
    )(x, g, modv, modv, w1, gq, gkv, wq, wkv, cos, sin)


def _qkv_proj_kernel(x_ref, g_ref, sc_ref, sh_ref, w_ref, cos_ref, sin_ref, o_ref, *, n_q, n_rope, q_scale, chunk):
    h = _normmod(x_ref[...], g_ref[...], sc_ref[0], sh_ref[0]).astype(BF16)
    n = w_ref.shape[1]
    for c0 in range(0, n, chunk):
        a = _dot(h, w_ref[:, c0:c0 + chunk])
        if c0 < n_rope:
            reps = chunk // LANES
            cos = jnp.concatenate([cos_ref[...]] * reps, axis=1)
            sin = jnp.concatenate([sin_ref[...]] * reps, axis=1)
            a = a * cos + _swap_halves(a, SW_HD) * sin
        if c0 < n_q:
            a = a * q_scale
        o_ref[:, c0:c0 + chunk] = a.astype(BF16)


def _qkv_proj(lay, x, g, modv, w, cos, sin, *, n_q, n_rope, q_scale, chunk=256):
    D, n = lay.D, w.shape[1]
    assert n % chunk == 0 and n_q % chunk == 0 and n_rope % chunk == 0
    full = lambda a: pl.BlockSpec(a.shape, lambda i: (0,) * a.ndim)
    row = lambda m: pl.BlockSpec((TM, m), lambda i: (i, 0))
    return pl.pallas_call(
        functools.partial(_qkv_proj_kernel, n_q=n_q, n_rope=n_rope, q_scale=q_scale, chunk=chunk),
        out_shape=jax.ShapeDtypeStruct((lay.NT, n), BF16),
        grid=(lay.n_tiles,),
        in_specs=[row(D), full(g), lay.mod_spec(1), lay.mod_spec(0), full(w), row(LANES), row(LANES)],
        out_specs=row(n),
        compiler_params=_cp(("arbitrary",)),
        name="qkv_proj",
    )(x, g, modv, modv, w, cos, sin)


def _mla_attn_kernel(*refs, tk, n_lat):
    if n_lat:
        qt_ref, kc_ref, vtc_ref, kl_ref, vtl_ref, o_ref, acc_ref, sa_ref, sb_ref = refs
    else:
        qt_ref, kc_ref, vtc_ref, o_ref, acc_ref = refs
    qt = qt_ref[0]

    st = _dot(kc_ref[...], qt)
    m = jnp.max(st, axis=0, keepdims=True)
    acc_ref[...] = _dot(vtc_ref[0], jnp.exp2(st - m).astype(BF16))

    if n_lat:
        nch = n_lat // tk

        def softmax_pv(st, vt, m):
            m_new = jnp.maximum(m, jnp.max(st, axis=0, keepdims=True))
            acc_ref[...] = jnp.exp2(m - m_new) * acc_ref[...] + _dot(vt, jnp.exp2(st - m_new).astype(BF16))
            return m_new

        sa_ref[...] = _dot(kl_ref[pl.ds(0, tk), :], qt)

        def body(jj, m):
            r0 = pl.multiple_of(2 * jj * tk, tk)
            r1 = pl.multiple_of((2 * jj + 1) * tk, tk)
            r2 = pl.multiple_of(jnp.minimum(2 * jj + 2, nch - 1) * tk, tk)
            sb_ref[...] = _dot(kl_ref[pl.ds(r1, tk), :], qt)
            m = softmax_pv(sa_ref[...], vtl_ref[0, :, pl.ds(r0, tk)], m)
            sa_ref[...] = _dot(kl_ref[pl.ds(r2, tk), :], qt)
            return softmax_pv(sb_ref[...], vtl_ref[0, :, pl.ds(r1, tk)], m)

        lax.fori_loop(0, nch // 2, body, m)
    o_ref[...] = (acc_ref[:MLA_V, :] / acc_ref[MLA_V:MLA_V + 1, :]).T.astype(o_ref.dtype)


def _mla_attention(lay, qt, k, vt, *, tq=512, tk=512):
    B, S, C = lay.B, lay.S, lay.C
    H = MLA_HEADS
    nq = S // tq
    cblk0 = lay.NL // C
    assert S % (2 * tk) == 0
    o_lat = pl.pallas_call(
        functools.partial(_mla_attn_kernel, tk=tk, n_lat=S),
        out_shape=jax.ShapeDtypeStruct((lay.NL, H * LANES), BF16),
        grid=(B, H, nq),
        in_specs=[pl.BlockSpec((1, 2 * LANES, tq), lambda b, h, i: (h, 0, b * nq + i)),
                  pl.BlockSpec((C, 2 * LANES), lambda b, h, i: (cblk0 + b, h)),
                  pl.BlockSpec((1, MLA_VT_ROWS, C), lambda b, h, i: (h, 0, cblk0 + b)),
                  pl.BlockSpec((S, 2 * LANES), lambda b, h, i: (b, h)),
                  pl.BlockSpec((1, MLA_VT_ROWS, S), lambda b, h, i: (h, 0, b))],
        out_specs=pl.BlockSpec((tq, LANES), lambda b, h, i: (b * nq + i, h)),
        scratch_shapes=[pltpu.VMEM((MLA_VT_ROWS, tq), F32), pltpu.VMEM((tk, tq), F32), pltpu.VMEM((tk, tq), F32)],
        compiler_params=_cp(("arbitrary", "arbitrary", "arbitrary")),
        name="mla_attn_latent",
    )(qt, k, vt, k, vt)
    o_ctx = pl.pallas_call(
        functools.partial(_mla_attn_kernel, tk=tk, n_lat=0),
        out_shape=jax.ShapeDtypeStruct((lay.NC, H * LANES), BF16),
        grid=(B, H),
        in_specs=[pl.BlockSpec((1, 2 * LANES, C), lambda b, h: (h, 0, cblk0 + b)),
                  pl.BlockSpec((C, 2 * LANES), lambda b, h: (cblk0 + b, h)),
                  pl.BlockSpec((1, MLA_VT_ROWS, C), lambda b, h: (h, 0, cblk0 + b))],
        out_specs=pl.BlockSpec((C, LANES), lambda b, h: (b, h)),
        scratch_shapes=[pltpu.VMEM((MLA_VT_ROWS, C), F32)],
        compiler_params=_cp(("arbitrary", "arbitrary")),
        name="mla_attn_ctx",
    )(qt, k, vt)
    return o_lat, o_ctx


def _dft_tables(S, C, gc):
    P = math.isqrt(S)
    assert P * P == S and (P & (P - 1)) == 0 and (gc & (gc - 1)) == 0 and (C & (C - 1)) == 0

    def cs(idx, n):
        ang = (idx % n).astype(F32) * (2.0 * math.pi / n)
        return jnp.cos(ang), jnp.sin(ang)

    k1 = jnp.arange(P, dtype=I32)
    idx = k1[None, :, None] * (P * k1[None, None, :] + k1[:, None, None])
    c, s = cs(idx, S)
    m1 = jnp.concatenate([c, -s], axis=1) * (1.0 / P)
    c, s = cs(k1[:, None] * k1[None, :], P)
    m2 = jnp.concatenate([jnp.concatenate([c, s], axis=1), jnp.concatenate([-s, c], axis=1)], axis=0)
    kc = jnp.arange(gc, dtype=I32)
    c, s = cs(kc[:, None] * kc[None, :], gc)
    mc = jnp.concatenate([c, s], axis=0) * (gc ** -0.5)
    kq = jnp.arange(C, dtype=I32)
    c, s = cs(kq[:, None] * kq[None, :], C)
    mctx = jnp.concatenate([c, s], axis=0) * (C ** -0.5)
    return m1.astype(BF16), m2.astype(BF16), mc.astype(BF16), mctx.astype(BF16)


def _fnet_stage1_kernel(x_ref, g_ref, sc_ref, sh_ref, m1_ref, z_ref, *, n2c, P):
    g, sc, sh = g_ref[...], sc_ref[0], sh_ref[0]
    for j in range(n2c):
        h = _normmod(x_ref[:, j, :], g, sc, sh).astype(BF16)
        z = _dot(m1_ref[j], h)
        z_ref[0, :, 0, j, :] = z[:P]
        z_ref[0, :, 1, j, :] = z[P:]


def _fnet_stage2_kernel(z_ref, m2_ref, mc_ref, f_ref, *, k1c, P, gc):
    D = f_ref.shape[-1]
    for j in range(k1c):
        z = z_ref[0, j].reshape(2 * P, D).astype(BF16)
        y = _dot(m2_ref[...], z)
        yr, yi = y[:P].astype(BF16), y[P:].astype(BF16)
        outs = []
        for gi in range(D // gc):
            sl = slice(gi * gc, (gi + 1) * gc)
            outs.append(_dot(yr[:, sl], mc_ref[:gc, :]) + _dot(yi[:, sl], mc_ref[gc:, :]))
        f_ref[:, j, :] = jnp.concatenate(outs, axis=1)


def _fnet_ctx_kernel(x_ref, g_ref, sc_ref, sh_ref, ml_ref, mc_ref, f_ref, *, C, gc):
    D = x_ref.shape[-1]
    h = _normmod(x_ref[...], g_ref[...], sc_ref[0], sh_ref[0]).astype(BF16)
    y = _dot(ml_ref[...], h)
    yc, ys = y[:C].astype(BF16), y[C:].astype(BF16)
    outs = []
    for gi in range(D // gc):
        sl = slice(gi * gc, (gi + 1) * gc)
        outs.append(_dot(yc[:, sl], mc_ref[:gc, :]) - _dot(ys[:, sl], mc_ref[gc:, :]))
    f_ref[...] = jnp.concatenate(outs, axis=1)


def _fnet_mix(lay, x, g, modv, tables):
    B, S, C, D = lay.B, lay.S, lay.C, lay.D
    m1, m2, mc, mctx = tables
    P = math.isqrt(S)
    gc = D // FNET_GROUPS
    n2c = SUBLANES
    k1c = SUBLANES
    full = lambda a: pl.BlockSpec(a.shape, lambda *i: (0,) * a.ndim)
    modspec = lambda chunk: pl.BlockSpec((1, 1, D), lambda b, j: (chunk * SUBLANES + b, 0, 0))
    assert C % P == 0 and P % n2c == 0 and P % k1c == 0
    x3 = x.reshape(lay.NT // P, P, D)
    z = pl.pallas_call(
        functools.partial(_fnet_stage1_kernel, n2c=n2c, P=P),
        out_shape=jax.ShapeDtypeStruct((B, P, 2, P, D), F32),
        grid=(B, P // n2c),
        in_specs=[pl.BlockSpec((P, n2c, D), lambda b, j: (b, j, 0)), full(g), modspec(1), modspec(0),
                  pl.BlockSpec((n2c, 2 * P, P), lambda b, j: (j, 0, 0))],
        out_specs=pl.BlockSpec((1, P, 2, n2c, D), lambda b, j: (b, 0, 0, j, 0)),
        compiler_params=_cp(("arbitrary", "arbitrary")),
        name="fnet_stage1",
    )(x3, g, modv, modv, m1)
    f_lat = pl.pallas_call(
        functools.partial(_fnet_stage2_kernel, k1c=k1c, P=P, gc=gc),
        out_shape=jax.ShapeDtypeStruct((lay.NL // P, P, D), F32),
        grid=(B, P // k1c),
        in_specs=[pl.BlockSpec((1, k1c, 2, P, D), lambda b, j: (b, j, 0, 0, 0)), full(m2), full(mc)],
        out_specs=pl.BlockSpec((P, k1c, D), lambda b, j: (b, j, 0)),
        compiler_params=_cp(("arbitrary", "arbitrary")),
        name="fnet_stage2",
    )(z, m2, mc)
    cblk0 = lay.NL // C
    ctx_mod = lambda chunk: pl.BlockSpec((1, 1, D), lambda b: (chunk * SUBLANES + B, 0, 0))
    f_ctx = pl.pallas_call(
        functools.partial(_fnet_ctx_kernel, C=C, gc=gc),
        out_shape=jax.ShapeDtypeStruct((lay.NC, D), F32),
        grid=(B,),
        in_specs=[pl.BlockSpec((C, D), lambda b: (cblk0 + b, 0)), full(g), ctx_mod(1), ctx_mod(0), full(mctx),
                  full(mc)],
        out_specs=pl.BlockSpec((C, D), lambda b: (b, 0)),
        compiler_params=_cp(("arbitrary",)),
        name="fnet_ctx",
    )(x, g, modv, modv, mctx, mc)
    return f_lat.reshape(lay.NL, D), f_ctx


def _attend(qm, loc, kc, vc, sink=None):
    sc = _dot_nt(qm, kc)
    m = jnp.max(sc, axis=-1, keepdims=True)
    if loc is not None:
        kw, vw, bias = loc
        s = _dot_nt(qm, kw) + bias
        m = jnp.maximum(m, jnp.max(s, axis=-1, keepdims=True))
    if sink is not None:
        m = jnp.maximum(m, sink)
    pc = jnp.exp(sc - m)
    l = jnp.sum(pc, axis=-1, keepdims=True)
    o = _dot(pc.astype(BF16), vc)
    if loc is not None:
        p = jnp.exp(s - m)
        l = l + jnp.sum(p, axis=-1, keepdims=True)
        o = o + _dot(p.astype(BF16), vw)
    if sink is not None:
        l = l + jnp.exp(sink - m)
    return o / l


NA_QROWS = 2
NA_KROWS = NA_KR + NA_QROWS - 1
NA_VARIANTS = 5


def _na_bias_kernel(rpb_ref, sel_ref, toe_ref, o_ref):
    g = jnp.dot(rpb_ref[0], toe_ref[...], precision=HIGHEST, preferred_element_type=F32)
    for t in range(NA_VARIANTS):
        o_ref[t, 0] = jnp.dot(sel_ref[t], g, precision=HIGHEST, preferred_element_type=F32)


def _na_bias_tables(rpb, rows):
    assert rows >= 16 and rows % NA_QROWS == 0
    W = GRID_W
    nh, nu, nv = rpb.shape
    up, vp, ajp = 2 * SUBLANES, LANES, 3 * SUBLANES
    assert nu <= up and nv <= vp and NA_QROWS * NA_KROWS <= ajp
    cq, ck = np.arange(W)[:, None], np.arange(W)[None, :]
    dc = np.clip(ck - cq + NA_KC - 1, 0, nv - 1).reshape(-1)
    toe = (np.arange(vp)[:, None] == dc[None, :]).astype(np.float32)
    cs = np.clip(cq - NA_KC // 2, 0, W - NA_KC)
    col_ok = (ck >= cs) & (ck < cs + NA_KC)
    sel = np.zeros((NA_VARIANTS, ajp, up), np.float32)
    valid = np.zeros((NA_VARIANTS, NA_QROWS, W, NA_KROWS, W), bool)
    for t, r in enumerate((0, 2, 6, rows - 4, rows - 2)):
        w0 = min(max(r - NA_KR // 2, 0), rows - NA_KROWS)
        for a in range(NA_QROWS):
            rs = min(max(r + a - NA_KR // 2, 0), rows - NA_KR)
            for j in range(NA_KROWS):
                rk = w0 + j
                sel[t, a * NA_KROWS + j, min(max(rk - (r + a) + NA_KR - 1, 0), nu - 1)] = 1.0
                if rs <= rk < rs + NA_KR:
                    valid[t, a, :, j, :] = col_ok
    rpb_p = jnp.pad(rpb, ((0, 0), (0, up - nu), (0, vp - nv)))
    tab = pl.pallas_call(
        _na_bias_kernel,
        out_shape=jax.ShapeDtypeStruct((NA_VARIANTS, nh, ajp, W * W), F32),
        grid=(nh,),
        in_specs=[pl.BlockSpec((1, up, vp), lambda h: (h, 0, 0)),
                  pl.BlockSpec(sel.shape, lambda h: (0, 0, 0)),
                  pl.BlockSpec(toe.shape, lambda h: (0, 0))],
        out_specs=pl.BlockSpec((NA_VARIANTS, 1, ajp, W * W), lambda h: (0, h, 0, 0)),
        compiler_params=_cp(("arbitrary",)),
        name="na_bias",
    )(rpb_p, jnp.asarray(sel), jnp.asarray(toe))
    tab = tab[:, :, :NA_QROWS * NA_KROWS].reshape(NA_VARIANTS, nh, NA_QROWS, NA_KROWS, W, W)
    tab = tab.transpose(0, 1, 2, 4, 3, 5).reshape(NA_VARIANTS, nh, NA_QROWS * W, NA_KROWS * W)
    valid = valid.reshape(NA_VARIANTS, 1, NA_QROWS * W, NA_KROWS * W)
    return jnp.where(jnp.asarray(valid), tab, NEG_INF)


def _na_kernel(*refs, rows, local):
    if local:
        q_ref, kc_ref, vc_ref, kl_ref, vl_ref, bias_ref, o_ref = refs
    else:
        q_ref, kc_ref, vc_ref, o_ref = refs
    lane = _lane_iota((1, LANES))
    masks = (lane < NA_HD, lane >= NA_HD)
    kc, vc = kc_ref[...], vc_ref[...]
    vcm = [jnp.where(mk, vc, jnp.zeros_like(vc)) for mk in masks]
    nq = NA_QROWS * GRID_W
    nk = NA_KROWS * GRID_W

    def block(r0, loc_of):
        q2 = q_ref[pl.ds(r0, nq), :]
        o = jnp.zeros((nq, LANES), F32)
        for hl in range(2):
            qm = jnp.where(masks[hl], q2, jnp.zeros_like(q2))
            o = o + _attend(qm, loc_of(hl), kc, vcm[hl])
        o_ref[pl.ds(r0, nq), :] = o.astype(o_ref.dtype)

    if not local:
        for t in range(q_ref.shape[0] // nq):
            block(t * nq, lambda hl: None)
        return

    qi = pl.program_id(2)
    pairs = q_ref.shape[0] // nq

    def body(t, c):
        r = (qi * pairs + t) * NA_QROWS
        w0 = jnp.clip(r - NA_KR // 2, 0, rows - NA_KROWS)
        k0 = pl.multiple_of(w0 * GRID_W, GRID_W)
        kw, vw = kl_ref[pl.ds(k0, nk), :], vl_ref[pl.ds(k0, nk), :]
        var = jnp.where(r == 0, 0, jnp.where(r == 2, 1, jnp.where(r == rows - 4, 3, jnp.where(r == rows - 2, 4, 2))))
        vwm = [jnp.where(mk, vw, jnp.zeros_like(vw)) for mk in masks]
        block(pl.multiple_of(t * nq, nq), lambda hl: (kw, vwm[hl], bias_ref[var, hl]))
        return c

    lax.fori_loop(0, pairs, body, 0)


def _na_attention(lay, qkv, bias, *, tq=512):
    B, S, C = lay.B, lay.S, lay.C
    rows = S // GRID_W
    HP = NA_HEADS // 2
    nq = S // tq
    cblk0 = lay.NL // C
    o_lat = pl.pallas_call(
        functools.partial(_na_kernel, rows=rows, local=True),
        out_shape=jax.ShapeDtypeStruct((lay.NL, NA_HEADS * NA_HD), BF16),
        grid=(B, HP, nq),
        in_specs=[pl.BlockSpec((tq, LANES), lambda b, h, i: (b * nq + i, h)),
                  pl.BlockSpec((C, LANES), lambda b, h, i: (cblk0 + b, HP + h)),
                  pl.BlockSpec((C, LANES), lambda b, h, i: (cblk0 + b, 2 * HP + h)),
                  pl.BlockSpec((S, LANES), lambda b, h, i: (b, HP + h)),
                  pl.BlockSpec((S, LANES), lambda b, h, i: (b, 2 * HP + h)),
                  pl.BlockSpec((NA_VARIANTS, 2) + bias.shape[2:], lambda b, h, i: (0, h, 0, 0))],
        out_specs=pl.BlockSpec((tq, LANES), lambda b, h, i: (b * nq + i, h)),
        compiler_params=_cp(("arbitrary", "arbitrary", "arbitrary")),
        name="na_attn_latent",
    )(qkv, qkv, qkv, qkv, qkv, bias)
    o_ctx = pl.pallas_call(
        functools.partial(_na_kernel, rows=rows, local=False),
        out_shape=jax.ShapeDtypeStruct((lay.NC, NA_HEADS * NA_HD), BF16),
        grid=(B, HP),
        in_specs=[pl.BlockSpec((C, LANES), lambda b, h: (cblk0 + b, h)),
                  pl.BlockSpec((C, LANES), lambda b, h: (cblk0 + b, HP + h)),
                  pl.BlockSpec((C, LANES), lambda b, h: (cblk0 + b, 2 * HP + h))],
        out_specs=pl.BlockSpec((C, LANES), lambda b, h: (b, h)),
        compiler_params=_cp(("arbitrary", "arbitrary")),
        name="na_attn_ctx",
    )(qkv, qkv, qkv)
    return o_lat, o_ctx


SW_SUB = 128
SW_BAND = SW_SUB + 2 * SW_WINDOW


def _swa_kernel(*refs, S, local):
    if local:
        (sink_ref, q_ref, kc_ref, vc_ref, kp_ref, kcur_ref, kn_ref, vp_ref, vcur_ref, vn_ref, o_ref,
         kbuf, vbuf) = refs
        tq = q_ref.shape[0]
        W = SW_WINDOW
        kbuf[0:W] = kp_ref[...]
        kbuf[W:W + tq] = kcur_ref[...]
        kbuf[W + tq:] = kn_ref[...]
        vbuf[0:W] = vp_ref[...]
        vbuf[W:W + tq] = vcur_ref[...]
        vbuf[W + tq:] = vn_ref[...]
        i = pl.program_id(1)
    else:
        sink_ref, q_ref, kc_ref, vc_ref, o_ref = refs
        tq = q_ref.shape[0]
    lane = _lane_iota((1, LANES))
    masks = (lane < SW_HD, lane >= SW_HD)
    G = SW_HEADS // SW_KV_HEADS
    kc, vc = kc_ref[...], vc_ref[...]

    def sub(sb, c):
        r0 = pl.multiple_of(sb * SW_SUB, SW_SUB)
        if local:
            kw, vw = kbuf[pl.ds(r0, SW_BAND), :], vbuf[pl.ds(r0, SW_BAND), :]
            row = lax.broadcasted_iota(I32, (SW_SUB, SW_BAND), 0)
            col = lax.broadcasted_iota(I32, (SW_SUB, SW_BAND), 1)
            rel = col - SW_WINDOW - row
            kpos = i * tq + sb * SW_SUB - SW_WINDOW + col
            ok = (jnp.abs(rel) <= SW_WINDOW) & (kpos >= 0) & (kpos < S)
            bias = jnp.where(ok, 0.0, NEG_INF).astype(F32)
        for t in range(SW_HEADS // 2):
            kvh = (2 * t) // G
            tile, half = kvh // 2, kvh % 2
            sl = slice(tile * LANES, (tile + 1) * LANES)
            q2 = q_ref[pl.ds(r0, SW_SUB), t * LANES:(t + 1) * LANES].astype(F32)
            q2r = pltpu.roll(q2, SW_HD, 1)
            acc = jnp.zeros((SW_SUB, LANES), F32)
            for qh in range(2):
                src = q2 if qh == half else q2r
                qm = jnp.where(masks[half], src, 0.0).astype(BF16)
                loc = (kw[:, sl], vw[:, sl], bias) if local else None
                oh = _attend(qm, loc, kc[:, sl], vc[:, sl], sink_ref[2 * t + qh])
                if qh != half:
                    oh = pltpu.roll(oh, SW_HD, 1)
                acc = acc + jnp.where(masks[qh], oh, 0.0)
            o_ref[pl.ds(r0, SW_SUB), t * LANES:(t + 1) * LANES] = acc.astype(o_ref.dtype)
        return c

    lax.fori_loop(0, tq // SW_SUB, sub, 0)


def _swa_attention(lay, qkv, sinks, *, tq=512):
    B, S, C = lay.B, lay.S, lay.C
    nq_cols = SW_HEADS * SW_HD
    nkv = SW_KV_HEADS * SW_HD
    kcol, vcol = nq_cols // nkv, nq_cols // nkv + 1
    nq = S // tq
    per = tq // SW_WINDOW
    last = lay.NT // SW_WINDOW - 1
    cblk0 = lay.NL // C
    prev = lambda b, i: jnp.maximum((b * nq + i) * per - 1, 0)
    nxt = lambda b, i: jnp.minimum((b * nq + i + 1) * per, last)
    smem = pl.BlockSpec(memory_space=pltpu.SMEM)
    o_lat = pl.pallas_call(
        functools.partial(_swa_kernel, S=S, local=True),
        out_shape=jax.ShapeDtypeStruct((lay.NL, nq_cols), BF16),
        grid=(B, nq),
        in_specs=[smem,
                  pl.BlockSpec((tq, nq_cols), lambda b, i: (b * nq + i, 0)),
                  pl.BlockSpec((C, nkv), lambda b, i: (cblk0 + b, kcol)),
                  pl.BlockSpec((C, nkv), lambda b, i: (cblk0 + b, vcol)),
                  pl.BlockSpec((SW_WINDOW, nkv), lambda b, i: (prev(b, i), kcol)),
                  pl.BlockSpec((tq, nkv), lambda b, i: (b * nq + i, kcol)),
                  pl.BlockSpec((SW_WINDOW, nkv), lambda b, i: (nxt(b, i), kcol)),
                  pl.BlockSpec((SW_WINDOW, nkv), lambda b, i: (prev(b, i), vcol)),
                  pl.BlockSpec((tq, nkv), lambda b, i: (b * nq + i, vcol)),
                  pl.BlockSpec((SW_WINDOW, nkv), lambda b, i: (nxt(b, i), vcol))],
        out_specs=pl.BlockSpec((tq, nq_cols), lambda b, i: (b * nq + i, 0)),
        scratch_shapes=[pltpu.VMEM((tq + 2 * SW_WINDOW, nkv), BF16), pltpu.VMEM((tq + 2 * SW_WINDOW, nkv), BF16)],
        compiler_params=_cp(("arbitrary", "arbitrary")),
        name="swa_attn_latent",
    )(sinks, qkv, qkv, qkv, qkv, qkv, qkv, qkv, qkv, qkv)
    o_ctx = pl.pallas_call(
        functools.partial(_swa_kernel, S=S, local=False),
        out_shape=jax.ShapeDtypeStruct((lay.NC, nq_cols), BF16),
        grid=(B,),
        in_specs=[smem,
                  pl.BlockSpec((C, nq_cols), lambda b: (cblk0 + b, 0)),
                  pl.BlockSpec((C, nkv), lambda b: (cblk0 + b, kcol)),
                  pl.BlockSpec((C, nkv), lambda b: (cblk0 + b, vcol))],
        out_specs=pl.BlockSpec((C, nq_cols), lambda b: (b, 0)),
        compiler_params=_cp(("arbitrary",)),
        name="swa_attn_ctx",
    )(sinks, qkv, qkv, qkv)
    return o_lat, o_ctx


def _route(logits, tri, carry):
    lane = _lane_iota(logits.shape)
    lanef = lane.astype(F32)
    big = float(LANES)
    rowmax = lambda t: jnp.max(t, axis=-1, keepdims=True)
    rowmin = lambda t: jnp.min(t, axis=-1, keepdims=True)
    rowsum = lambda t: jnp.sum(t, axis=-1, keepdims=True)
    is_g = lane < MOE_GROUPS
    mg = rowmax(jnp.where(is_g, logits, -jnp.inf))
    w_g = 1.0 / rowsum(jnp.where(is_g, jnp.exp(logits - mg), 0.0))
    gidx = rowmin(jnp.where(is_g & (logits == mg), lanef, big))
    g0 = MOE_GROUPS + MOE_PER_GROUP * gidx
    in_grp = (lanef >= g0) & (lanef < g0 + MOE_PER_GROUP)
    le = jnp.where(in_grp, logits, -jnp.inf)
    m1 = rowmax(le)
    i1 = rowmin(jnp.where(in_grp & (le == m1), lanef, big))
    le2 = jnp.where(lanef == i1, -jnp.inf, le)
    m2 = rowmax(le2)
    i2 = rowmin(jnp.where(in_grp & (lanef != i1) & (le2 == m2), lanef, big))
    r = jnp.exp(m2 - m1)
    gate1 = w_g / (1.0 + r)
    gate2 = w_g * r / (1.0 + r)
    sel1, sel2 = lanef == i1, lanef == i2
    member = (sel1 | sel2)
    cum = _dot(tri, member.astype(BF16)) + carry
    rank1 = rowsum(jnp.where(sel1, cum, 0.0))
    rank2 = rowsum(jnp.where(sel2, cum, 0.0))
    new_carry = carry + jnp.sum(member.astype(F32), axis=0, keepdims=True)
    rec = jnp.zeros_like(logits)
    for ln, val in ((0, i1 - MOE_GROUPS), (1, i2 - MOE_GROUPS), (2, rank1), (3, rank2), (4, gate1), (5, gate2)):
        rec = jnp.where(lane == ln, val, rec)
    return rec, new_carry


def _out_proj_kernel(*refs, has_bias, n_lat_tiles):
    if has_bias:
        (al_ref, ac_ref, x_ref, w_ref, b_ref, g1_ref, g2n_ref, sc_ref, sh_ref, wr_ref, br_ref, tri_ref,
         xo_ref, h2_ref, rec_ref, cnt_ref, carry_ref) = refs
    else:
        (al_ref, ac_ref, x_ref, w_ref, g1_ref, g2n_ref, sc_ref, sh_ref, wr_ref, br_ref, tri_ref,
         xo_ref, h2_ref, rec_ref, cnt_ref, carry_ref) = refs

    @pl.when(pl.program_id(0) == 0)
    def _():
        carry_ref[...] = jnp.zeros_like(carry_ref)

    a = jnp.where(pl.program_id(0) < n_lat_tiles, al_ref[...], ac_ref[...])
    y = _dot(a.astype(BF16), w_ref[...])
    if has_bias:
        y = y + b_ref[...]
    xn = x_ref[...] + g1_ref[0] * y
    xo_ref[...] = xn
    h2 = _normmod(xn, g2n_ref[...], sc_ref[0], sh_ref[0])
    h2_ref[...] = h2
    h_hi = h2.astype(BF16)
    h_lo = (h2 - h_hi.astype(F32)).astype(BF16)
    hw = _dot(h_hi, wr_ref[...])
    logits = hw[:, :LANES] + hw[:, LANES:] + _dot(h_lo, wr_ref[:, :LANES]) + br_ref[...]
    rec, carry = _route(logits, tri_ref[...], carry_ref[...])
    rec_ref[...] = rec
    carry_ref[...] = carry
    cnt_ref[...] = jnp.broadcast_to(carry, cnt_ref.shape)


def _out_proj(lay, a, x, w, b, g2n, modv, wr, br, tri):
    D = lay.D
    a_lat, a_ctx = a
    nl = lay.nl_tiles
    full = lambda t: pl.BlockSpec(t.shape, lambda i: (0,) * t.ndim)
    row = lambda n: pl.BlockSpec((TM, n), lambda i: (i, 0))
    ins = [a_lat, a_ctx, x, w] + ([b] if b is not None else []) + [modv, g2n, modv, modv, wr, br, tri]
    specs = ([pl.BlockSpec((TM, a_lat.shape[1]), lambda i: (jnp.minimum(i, nl - 1), 0)),
              pl.BlockSpec((TM, a_ctx.shape[1]), lambda i: (jnp.maximum(i - nl, 0), 0)), row(D), full(w)]
             + ([full(b)] if b is not None else [])
             + [lay.mod_spec(2), full(g2n), lay.mod_spec(4), lay.mod_spec(3), full(wr), full(br), full(tri)])
    return pl.pallas_call(
        functools.partial(_out_proj_kernel, has_bias=b is not None, n_lat_tiles=nl),
        out_shape=(jax.ShapeDtypeStruct((lay.NT, D), F32), jax.ShapeDtypeStruct((lay.NT, D), F32),
                   jax.ShapeDtypeStruct((lay.NT, LANES), F32), jax.ShapeDtypeStruct((SUBLANES, LANES), F32)),
        grid=(lay.n_tiles,),
        in_specs=specs,
        out_specs=(row(D), row(D), row(LANES), pl.BlockSpec((SUBLANES, LANES), lambda i: (0, 0))),
        scratch_shapes=[pltpu.VMEM((1, LANES), F32)],
        input_output_aliases={2: 0},
        compiler_params=_cp(("arbitrary",)),
        name="out_proj_router",
    )(*ins)


def _row_copy(src, s, dst, d, sem):
    return pltpu.make_async_copy(src.at[pl.ds(s, 1), :], dst.at[pl.ds(d, 1), :], sem)


def _dispatch_kernel(pos_ref, pend_ref, nu_ref, h_ref, xb_ref, zbuf, sem, zsem):
    base = pl.program_id(0) * TM
    nblk = xb_ref.shape[0] // MOE_BM

    @pl.when(pl.program_id(0) == 0)
    def _():
        zbuf[...] = jnp.zeros_like(zbuf)

        def zero_block(row0):
            return pltpu.make_async_copy(zbuf, xb_ref.at[pl.ds(pl.multiple_of(row0, MOE_BM), MOE_BM), :], zsem)

        def each(fn):
            def expert(e, c):
                end = pend_ref[e]

                @pl.when(end > jnp.where(e > 0, pend_ref[jnp.maximum(e - 1, 0)], 0))
                def _():
                    fn(zero_block(end - MOE_BM))
                return c

            def tail(j, c):
                @pl.when(j >= nu_ref[0])
                def _():
                    fn(zero_block(j * MOE_BM))
                return c

            lax.fori_loop(0, MOE_EXPERTS, expert, 0)
            lax.fori_loop(0, nblk, tail, 0)

        each(lambda cp: cp.start())
        each(lambda cp: cp.wait())

    def issue(r, c):
        for k in range(2):
            _row_copy(h_ref, r, xb_ref, pos_ref[(base + r) * 2 + k], sem).start()
        return c

    lax.fori_loop(0, TM, issue, 0, unroll=8)
    for k in range(2):
        pltpu.make_async_copy(h_ref, xb_ref.at[pl.ds(0, TM), :], sem).wait()


def _dispatch(lay, pos, pend, n_used, h2, cap):
    D = lay.D
    return pl.pallas_call(
        _dispatch_kernel,
        out_shape=jax.ShapeDtypeStruct((cap, D), F32),
        grid_spec=pltpu.PrefetchScalarGridSpec(
            num_scalar_prefetch=3, grid=(lay.n_tiles,),
            in_specs=[pl.BlockSpec((TM, D), lambda i, p, e, n: (i, 0))],
            out_specs=pl.BlockSpec(memory_space=pl.ANY),
            scratch_shapes=[pltpu.VMEM((MOE_BM, D), F32), pltpu.SemaphoreType.DMA, pltpu.SemaphoreType.DMA]),
        compiler_params=_cp(("arbitrary",)),
        name="moe_dispatch",
    )(pos, pend, n_used, h2)


def _expert_kernel(be_ref, nu_ref, xb_ref, wg_ref, wu_ref, wd_ref, yb_ref, wgb, wub, wdb):
    j = pl.program_id(0)
    prev = be_ref[jnp.maximum(j - 1, 0)]

    @pl.when((j == 0) | (be_ref[j] != prev))
    def _():
        wgb[...] = wg_ref[0, 0].astype(BF16)
        wub[...] = wu_ref[0, 0].astype(BF16)
        wdb[...] = wd_ref[0, 0].astype(BF16)

    @pl.when(j < nu_ref[0])
    def _():
        xe = xb_ref[...].astype(BF16)
        g = _dot(xe, wgb[...])
        u = _dot(xe, wub[...])
        act = (g * jax.nn.sigmoid(g) * u).astype(BF16)
        yb_ref[...] = _dot(act, wdb[...])

    @pl.when(j >= nu_ref[0])
    def _():
        yb_ref[...] = jnp.zeros_like(yb_ref)


def _experts(xb, blk_e, n_used, w_gate, w_up, w_down, layer):
    cap, D = xb.shape
    FF = w_gate.shape[-1]
    nblk = cap // MOE_BM
    blk = lambda j, be, nu: (jnp.maximum(jnp.minimum(j, nu[0] - 1), 0), 0)
    wblk = lambda j, be, nu: (layer, be[j], 0, 0)
    return pl.pallas_call(
        _expert_kernel,
        out_shape=jax.ShapeDtypeStruct((cap, D), F32),
        grid_spec=pltpu.PrefetchScalarGridSpec(
            num_scalar_prefetch=2, grid=(nblk,),
            in_specs=[pl.BlockSpec((MOE_BM, D), blk),
                      pl.BlockSpec((1, 1, D, FF), wblk), pl.BlockSpec((1, 1, D, FF), wblk),
                      pl.BlockSpec((1, 1, FF, D), wblk)],
            out_specs=pl.BlockSpec((MOE_BM, D), lambda j, be, nu: (j, 0)),
            scratch_shapes=[pltpu.VMEM((D, FF), BF16), pltpu.VMEM((D, FF), BF16), pltpu.VMEM((FF, D), BF16)]),
        compiler_params=_cp(("arbitrary",)),
        name="moe_experts",
    )(blk_e, n_used, xb, w_gate, w_up, w_down)


def _combine_kernel(*refs, final):
    if final:
        pos_ref, x_ref, rec_ref, g2_ref, yb_ref, fg_ref, o_ref, buf, sem = refs
    else:
        pos_ref, x_ref, rec_ref, g2_ref, yb_ref, o_ref, buf, sem = refs
    base = pl.program_id(0) * TM

    def issue(r, c):
        for k in range(2):
            _row_copy(yb_ref, pos_ref[(base + r) * 2 + k], buf.at[k], r, sem).start()
        return c

    lax.fori_loop(0, TM, issue, 0, unroll=8)
    for k in range(2):
        pltpu.make_async_copy(yb_ref.at[pl.ds(0, TM), :], buf.at[k], sem).wait()
    rec = rec_ref[...]
    f = rec[:, ROUTE_LANE_GATE:ROUTE_LANE_GATE + 1] * buf[0] + rec[:, ROUTE_LANE_GATE + 1:ROUTE_LANE_GATE + 2] * buf[1]
    xn = x_ref[...] + g2_ref[0] * f
    if final:
        xn = _rms(xn, fg_ref[...])
    o_ref[...] = xn


def _combine(lay, pos, x, rec, modv, yb, final_g):
    D = lay.D
    final = final_g is not None
    row = lambda n: pl.BlockSpec((TM, n), lambda i, p: (i, 0))
    specs = [row(D), row(LANES),
             pl.BlockSpec((1, 1, D), lambda i, p: (5 * SUBLANES + lay.mod_row(i), 0, 0)),
             pl.BlockSpec(memory_space=pl.ANY)]
    ins = [x, rec, modv, yb]
    if final:
        specs.append(pl.BlockSpec(final_g.shape, lambda i, p: (0, 0)))
        ins.append(final_g)
    n_rows, n_tiles = (lay.NL, lay.nl_tiles) if final else (lay.NT, lay.n_tiles)
    return pl.pallas_call(
        functools.partial(_combine_kernel, final=final),
        out_shape=jax.ShapeDtypeStruct((n_rows, D), F32),
        grid_spec=pltpu.PrefetchScalarGridSpec(
            num_scalar_prefetch=1, grid=(n_tiles,),
            in_specs=specs,
            out_specs=row(D),
            scratch_shapes=[pltpu.VMEM((2, TM, D), F32), pltpu.SemaphoreType.DMA]),
        input_output_aliases={} if final else {1: 0},
        compiler_params=_cp(("arbitrary",)),
        name="moe_combine",
    )(pos, *ins)


def _moe(lay, x, h2, rec, counts, modv, w_gate, w_up, w_down, layer, final_g):
    T = 2 * lay.NT
    cap = -(-T // MOE_BM) * MOE_BM + MOE_EXPERTS * MOE_BM
    cnt = counts[0, MOE_GROUPS:MOE_GROUPS + MOE_EXPERTS].astype(I32)
    pcnt = (cnt + MOE_BM - 1) // MOE_BM * MOE_BM
    pend = jnp.cumsum(pcnt)
    start = pend - pcnt
    eid = rec[:, ROUTE_LANE_EID:ROUTE_LANE_EID + 2].astype(I32)
    rank = rec[:, ROUTE_LANE_RANK:ROUTE_LANE_RANK + 2].astype(I32)
    pos = (start[eid] + rank).reshape(-1)
    blk_row = jnp.arange(cap // MOE_BM, dtype=I32) * MOE_BM
    blk_e = jnp.minimum(jnp.sum((pend[None, :] <= blk_row[:, None]).astype(I32), axis=1), MOE_EXPERTS - 1)
    n_used = (pend[-1:] // MOE_BM).astype(I32)
    xb = _dispatch(lay, pos, pend.astype(I32), n_used, h2, cap)
    yb = _experts(xb, blk_e, n_used, w_gate, w_up, w_down, layer)
    return _combine(lay, pos, x, rec, modv, yb, final_g)


def _rope_tables(lay):
    S = lay.S
    t = jnp.arange(S)
    n = MLA_ROPE // 4
    inv = ROPE_THETA ** (-jnp.arange(n, dtype=F32) / n)
    ang = jnp.concatenate([(t // GRID_W).astype(F32)[:, None] * inv, (t % GRID_W).astype(F32)[:, None] * inv], axis=-1)
    cos, sin = jnp.cos(ang), jnp.sin(ang)
    cos64 = jnp.concatenate([cos, cos], axis=-1)
    sin64 = jnp.concatenate([-sin, sin], axis=-1)
    rows = lambda lat, ctx_val: jnp.concatenate([jnp.tile(lat, (lay.B, 1)), jnp.full((lay.NC, 64), ctx_val, F32)], axis=0)
    cos64, sin64 = rows(cos64, 1.0), rows(sin64, 0.0)
    zero = jnp.zeros_like(cos64)
    return ((jnp.concatenate([cos64, zero], axis=1), jnp.concatenate([sin64, zero], axis=1)),
            (jnp.concatenate([cos64, cos64], axis=1), jnp.concatenate([sin64, sin64], axis=1)))


def kernel(x, c, ctx, c_ctx, mod_w, mod_b, norm1_g, norm2_g, mla_w_dq, mla_g_q, mla_w_uq, mla_w_dkv, mla_g_kv, mla_w_ukv, mla_w_o, fnet_w_o, fnet_b_o, na_w_qkv, na_rpb, na_w_o, swa_w_qkv, swa_sinks, swa_w_o, moe_w_grp, moe_b_grp, moe_w_rt, moe_b_rt, moe_w_gate, moe_w_up, moe_w_down, final_g):
    B, S, D = x.shape
    C = ctx.shape[1]
    depth = mod_w.shape[0]
    lay = _Layout(B, S, C, D)
    X = jnp.concatenate([x.reshape(B * S, D), ctx.reshape(B * C, D)], axis=0)
    cond = jnp.concatenate([c, c_ctx[None], jnp.zeros((SUBLANES - B - 1, D), F32)], axis=0)
    mod = _modulation(cond, mod_w, mod_b)
    (mla_cos, mla_sin), (swa_cos, swa_sin) = _rope_tables(lay)
    tri = (jnp.arange(TM)[:, None] > jnp.arange(TM)[None, :]).astype(BF16)
    n_mix = 4
    for i in range(depth):
        m, j = i % n_mix, i // n_mix
        modv = mod[i].reshape(SUBLANES, 6, D).transpose(1, 0, 2).reshape(6 * SUBLANES, 1, D)
        g1n, g2n = norm1_g[i][None], norm2_g[i][None]
        bias = None
        if m == 0:
            w1 = jnp.concatenate([mla_w_dq[j], mla_w_dkv[j], jnp.zeros((D, LANES - MLA_ROPE), F32)], axis=1).astype(BF16)
            wq = mla_w_uq[j].reshape(MLA_Q_RANK, MLA_HEADS, MLA_NOPE + MLA_ROPE)
            wq = jnp.concatenate([wq, jnp.zeros((MLA_Q_RANK, MLA_HEADS, LANES - MLA_ROPE), F32)], axis=-1)
            wq = wq.reshape(MLA_Q_RANK, MLA_HEADS * 2 * LANES).astype(BF16)
            q, k, v = _mla_proj(lay, X, g1n, modv, w1, mla_g_q[j][None], mla_g_kv[j][None], wq,
                                mla_w_ukv[j].astype(BF16), mla_cos, mla_sin)
            a = _mla_attention(lay, q, k, v)
            w_o = mla_w_o[j]
        elif m == 1:
            a = _fnet_mix(lay, X, g1n, modv, _dft_tables(S, C, D // FNET_GROUPS))
            w_o, bias = fnet_w_o[j], fnet_b_o[j][None]
        elif m == 2:
            qkv = _qkv_proj(lay, X, g1n, modv, na_w_qkv[j].astype(BF16), swa_cos, swa_sin,
                            n_q=NA_HEADS * NA_HD, n_rope=0, q_scale=NA_HD ** -0.5)
            a = _na_attention(lay, qkv, _na_bias_tables(na_rpb[j], S // GRID_W))
            w_o = na_w_o[j]
        else:
            qkv = _qkv_proj(lay, X, g1n, modv, swa_w_qkv[j].astype(BF16), swa_cos, swa_sin,
                            n_q=SW_HEADS * SW_HD, n_rope=(SW_HEADS + SW_KV_HEADS) * SW_HD, q_scale=SW_HD ** -0.5)
            a = _swa_attention(lay, qkv, swa_sinks[j])
            w_o = swa_w_o[j]
        wr = jnp.concatenate([moe_w_grp[i], moe_w_rt[i], jnp.zeros((D, LANES - MOE_GROUPS - MOE_EXPERTS), F32)], axis=1)
        br = jnp.concatenate([moe_b_grp[i], moe_b_rt[i], jnp.zeros((LANES - MOE_GROUPS - MOE_EXPERTS,), F32)])[None]
        wr_hi = wr.astype(BF16)
        wr2 = jnp.concatenate([wr_hi, (wr - wr_hi.astype(F32)).astype(BF16)], axis=1)
        X, h2, rec, counts = _out_proj(lay, a, X, w_o.astype(BF16), bias, g2n, modv, wr2, br, tri)
        X = _moe(lay, X, h2, rec, counts, modv, moe_w_gate, moe_w_up, moe_w_down, i,
                 final_g[None] if i == depth - 1 else None)
    return X.reshape(B, S, D)
```

```python
import functools
import math

import jax
import jax.numpy as jnp
import numpy as np
from jax import lax
from jax.experimental import pallas as pl
from jax.experimental.pallas import tpu as pltpu

F32 = jnp.float32
BF16 = jnp.bfloat16
I32 = jnp.int32
HIGHEST = lax.Precision.HIGHEST

GRID_W = 64
EPS = 1e-6
ROPE_THETA = 10000.0
NEG_INF = -1e30
MLA_HEADS, MLA_Q_RANK, MLA_KV_RANK, MLA_NOPE, MLA_ROPE, MLA_V = 8, 512, 256, 128, 64, 128
MLA_VT_ROWS = MLA_V + 16
FNET_GROUPS = 4
NA_HEADS, NA_HD, NA_KR, NA_KC = 16, 64, 8, 16
SW_HEADS, SW_KV_HEADS, SW_HD, SW_WINDOW = 16, 4, 64, 128
MOE_GROUPS, MOE_PER_GROUP, MOE_FF = 4, 8, 512
MOE_EXPERTS = MOE_GROUPS * MOE_PER_GROUP

LANES = 128
SUBLANES = 8
TM = 512
MOE_BM = 512
VMEM_LIMIT = 56 * 1024 * 1024
ROUTE_LANE_EID, ROUTE_LANE_RANK, ROUTE_LANE_GATE = 0, 2, 4


def _cp(sem, vmem=VMEM_LIMIT):
    return pltpu.CompilerParams(dimension_semantics=sem, vmem_limit_bytes=vmem)


def _lane_iota(shape):
    return lax.broadcasted_iota(I32, shape, len(shape) - 1)


def _normmod(x, g, sc, sh):
    ms = jnp.mean(x * x, axis=-1, keepdims=True)
    return (x * lax.rsqrt(ms + EPS) * g) * (1.0 + sc) + sh


def _rms(x, g):
    ms = jnp.mean(x * x, axis=-1, keepdims=True)
    return x * lax.rsqrt(ms + EPS) * g


def _swap_halves(t, period):
    n = t.shape[-1]
    half = period // 2
    lane = _lane_iota(t.shape)
    return jnp.where((lane % period) < half, pltpu.roll(t, n - half, 1), pltpu.roll(t, half, 1))


def _dot(a, b):
    return jnp.dot(a, b, preferred_element_type=F32)


def _mod_kernel(a_ref, w_ref, b_ref, o_ref):
    a = a_ref[...]
    a = a * jax.nn.sigmoid(a)
    o_ref[0] = jnp.dot(a, w_ref[0], precision=HIGHEST, preferred_element_type=F32) + b_ref[0]


def _modulation(cond, mod_w, mod_b):
    depth, d, n = mod_w.shape
    tn = n // 4
    return pl.pallas_call(
        _mod_kernel,
        out_shape=jax.ShapeDtypeStruct((depth, SUBLANES, n), F32),
        grid=(depth, n // tn),
        in_specs=[pl.BlockSpec((SUBLANES, d), lambda l, j: (0, 0)),
                  pl.BlockSpec((1, d, tn), lambda l, j: (l, 0, j)),
                  pl.BlockSpec((1, 1, tn), lambda l, j: (l, 0, j))],
        out_specs=pl.BlockSpec((1, SUBLANES, tn), lambda l, j: (l, 0, j)),
        compiler_params=_cp(("arbitrary", "arbitrary")),
        name="modulation",
    )(cond, mod_w, mod_b.reshape(depth, 1, n))


class _Layout:
    def __init__(self, B, S, C, D):
        self.B, self.S, self.C, self.D = B, S, C, D
        self.NL, self.NC = B * S, B * C
        self.NT = self.NL + self.NC
        assert S % TM == 0 and self.NC % TM == 0 and TM % C == 0
        self.nl_tiles = self.NL // TM
        self.n_tiles = self.NT // TM
        self.tiles_per_batch = S // TM

    def mod_row(self, i):
        return jnp.where(i < self.nl_tiles, i // self.tiles_per_batch, self.B)

    def mod_spec(self, chunk):
        return pl.BlockSpec((1, 1, self.D), lambda i: (chunk * SUBLANES + self.mod_row(i), 0, 0))


def _mla_proj_kernel(x_ref, g_ref, sc_ref, sh_ref, w1_ref, gq_ref, gkv_ref, wq_ref, wkv_ref, cos_ref, sin_ref,
                     qt_ref, k_ref, vt_ref):
    h = _normmod(x_ref[...], g_ref[...], sc_ref[0], sh_ref[0]).astype(BF16)
    a = _dot(h, w1_ref[...])
    qa = _rms(a[:, :MLA_Q_RANK], gq_ref[...]).astype(BF16)
    ckv = _rms(a[:, MLA_Q_RANK:MLA_Q_RANK + MLA_KV_RANK], gkv_ref[...]).astype(BF16)
    cos, sin = cos_ref[...], sin_ref[...]

    def rope(t):
        return t * cos + _swap_halves(t, MLA_ROPE) * sin

    kr = rope(a[:, MLA_Q_RANK + MLA_KV_RANK:]).astype(BF16)
    scale = (MLA_NOPE + MLA_ROPE) ** -0.5 * math.log2(math.e)
    q = _dot(qa, wq_ref[...])
    kv = _dot(ckv, wkv_ref[...])
    ones = jnp.ones((MLA_VT_ROWS - MLA_V, x_ref.shape[0]), BF16)
    for hd in range(MLA_HEADS):
        c = hd * 2 * LANES
        qh = jnp.concatenate([q[:, c:c + LANES], rope(q[:, c + LANES:c + 2 * LANES])], axis=1) * scale
        qt_ref[hd] = qh.T.astype(BF16)
        k_ref[:, c:c + LANES] = kv[:, c:c + LANES].astype(BF16)
        k_ref[:, c + LANES:c + 2 * LANES] = kr
        vt_ref[hd, :MLA_V, :] = kv[:, c + LANES:c + 2 * LANES].T.astype(BF16)
        vt_ref[hd, MLA_V:, :] = ones


def _mla_proj(lay, x, g, modv, w1, gq, gkv, wq, wkv, cos, sin):
    D = lay.D
    full = lambda a: pl.BlockSpec(a.shape, lambda i: (0,) * a.ndim)
    row = lambda n: pl.BlockSpec((TM, n), lambda i: (i, 0))
    col = lambda r: pl.BlockSpec((MLA_HEADS, r, TM), lambda i: (0, 0, i))
    hq = MLA_HEADS * 2 * LANES
    return pl.pallas_call(
        _mla_proj_kernel,
        out_shape=(jax.ShapeDtypeStruct((MLA_HEADS, 2 * LANES, lay.NT), BF16),
                   jax.ShapeDtypeStruct((lay.NT, hq), BF16),
                   jax.ShapeDtypeStruct((MLA_HEADS, MLA_VT_ROWS, lay.NT), BF16)),
        grid=(lay.n_tiles,),
        in_specs=[row(D), full(g), lay.mod_spec(1), lay.mod_spec(0), full(w1), full(gq), full(gkv), full(wq),
                  full(wkv), row(LANES), row(LANES)],
        out_specs=(col(2 * LANES), row(hq), col(MLA_VT_ROWS)),
        compiler_params=_cp(("arbitrary",)),
        name="mla_proj",
    )(x, g, modv, modv, w1, gq, gkv, wq, wkv, cos, sin)


def _qkv_proj_kernel(x_ref, g_ref, sc_ref, sh_ref, w_ref, cos_ref, sin_ref, qt_ref, k_ref, vt_ref, *,
                     n_q, n_k, rope, q_scale, chunk):
    h = _normmod(x_ref[...], g_ref[...], sc_ref[0], sh_ref[0]).astype(BF16)
    n = w_ref.shape[1]
    for c0 in range(0, n, chunk):
        a = _dot(h, w_ref[:, c0:c0 + chunk])
        if rope and c0 < n_q + n_k:
            reps = chunk // LANES
            cos = jnp.concatenate([cos_ref[...]] * reps, axis=1)
            sin = jnp.concatenate([sin_ref[...]] * reps, axis=1)
            a = a * cos + _swap_halves(a, SW_HD) * sin
        if c0 < n_q:
            qt_ref[c0:c0 + chunk, :] = (a * q_scale).T.astype(BF16)
        elif c0 < n_q + n_k:
            k_ref[:, c0 - n_q:c0 - n_q + chunk] = a.astype(BF16)
        else:
            c = c0 - n_q - n_k
            vt_ref[c:c + chunk, :] = a.T.astype(BF16)


def _qkv_proj(lay, x, g, modv, w, cos, sin, *, n_q, n_k, rope, q_scale, chunk=256):
    D, n = lay.D, w.shape[1]
    n_v = n - n_q - n_k
    assert n_q % chunk == 0 and n_k % chunk == 0 and n_v % chunk == 0
    full = lambda a: pl.BlockSpec(a.shape, lambda i: (0,) * a.ndim)
    row = lambda m: pl.BlockSpec((TM, m), lambda i: (i, 0))
    col = lambda m: pl.BlockSpec((m, TM), lambda i: (0, i))
    return pl.pallas_call(
        functools.partial(_qkv_proj_kernel, n_q=n_q, n_k=n_k, rope=rope, q_scale=q_scale, chunk=chunk),
        out_shape=(jax.ShapeDtypeStruct((n_q, lay.NT), BF16), jax.ShapeDtypeStruct((lay.NT, n_k), BF16),
                   jax.ShapeDtypeStruct((n_v, lay.NT), BF16)),
        grid=(lay.n_tiles,),
        in_specs=[row(D), full(g), lay.mod_spec(1), lay.mod_spec(0), full(w), row(LANES), row(LANES)],
        out_specs=(col(n_q), row(n_k), col(n_v)),
        compiler_params=_cp(("arbitrary",)),
        name="qkv_proj",
    )(x, g, modv, modv, w, cos, sin)


def _mla_attn_kernel(*refs, tk, n_lat):
    if n_lat:
        qt_ref, kc_ref, vtc_ref, kl_ref, vtl_ref, o_ref, acc_ref, sa_ref, sb_ref = refs
    else:
        qt_ref, kc_ref, vtc_ref, o_ref, acc_ref = refs
    qt = qt_ref[0]

    st = _dot(kc_ref[...], qt)
    m = jnp.max(st, axis=0, keepdims=True)
    acc_ref[...] = _dot(vtc_ref[0], jnp.exp2(st - m).astype(BF16))

    if n_lat:
        nch = n_lat // tk

        def softmax_pv(st, vt, m):
            m_new = jnp.maximum(m, jnp.max(st, axis=0, keepdims=True))
            acc_ref[...] = jnp.exp2(m - m_new) * acc_ref[...] + _dot(vt, jnp.exp2(st - m_new).astype(BF16))
            return m_new

        sa_ref[...] = _dot(kl_ref[pl.ds(0, tk), :], qt)

        def body(jj, m):
            r0 = pl.multiple_of(2 * jj * tk, tk)
            r1 = pl.multiple_of((2 * jj + 1) * tk, tk)
            r2 = pl.multiple_of(jnp.minimum(2 * jj + 2, nch - 1) * tk, tk)
            sb_ref[...] = _dot(kl_ref[pl.ds(r1, tk), :], qt)
            m = softmax_pv(sa_ref[...], vtl_ref[0, :, pl.ds(r0, tk)], m)
            sa_ref[...] = _dot(kl_ref[pl.ds(r2, tk), :], qt)
            return softmax_pv(sb_ref[...], vtl_ref[0, :, pl.ds(r1, tk)], m)

        lax.fori_loop(0, nch // 2, body, m)
    o_ref[...] = (acc_ref[:MLA_V, :] / acc_ref[MLA_V:MLA_V + 1, :]).T.astype(o_ref.dtype)


def _mla_attention(lay, qt, k, vt, *, tq=512, tk=512):
    B, S, C = lay.B, lay.S, lay.C
    H = MLA_HEADS
    nq = S // tq
    cblk0 = lay.NL // C
    assert S % (2 * tk) == 0
    o_lat = pl.pallas_call(
        functools.partial(_mla_attn_kernel, tk=tk, n_lat=S),
        out_shape=jax.ShapeDtypeStruct((lay.NL, H * LANES), BF16),
        grid=(B, H, nq),
        in_specs=[pl.BlockSpec((1, 2 * LANES, tq), lambda b, h, i: (h, 0, b * nq + i)),
                  pl.BlockSpec((C, 2 * LANES), lambda b, h, i: (cblk0 + b, h)),
                  pl.BlockSpec((1, MLA_VT_ROWS, C), lambda b, h, i: (h, 0, cblk0 + b)),
                  pl.BlockSpec((S, 2 * LANES), lambda b, h, i: (b, h)),
                  pl.BlockSpec((1, MLA_VT_ROWS, S), lambda b, h, i: (h, 0, b))],
        out_specs=pl.BlockSpec((tq, LANES), lambda b, h, i: (b * nq + i, h)),
        scratch_shapes=[pltpu.VMEM((MLA_VT_ROWS, tq), F32), pltpu.VMEM((tk, tq), F32), pltpu.VMEM((tk, tq), F32)],
        compiler_params=_cp(("arbitrary", "arbitrary", "arbitrary")),
        name="mla_attn_latent",
    )(qt, k, vt, k, vt)
    o_ctx = pl.pallas_call(
        functools.partial(_mla_attn_kernel, tk=tk, n_lat=0),
        out_shape=jax.ShapeDtypeStruct((lay.NC, H * LANES), BF16),
        grid=(B, H),
        in_specs=[pl.BlockSpec((1, 2 * LANES, C), lambda b, h: (h, 0, cblk0 + b)),
                  pl.BlockSpec((C, 2 * LANES), lambda b, h: (cblk0 + b, h)),
                  pl.BlockSpec((1, MLA_VT_ROWS, C), lambda b, h: (h, 0, cblk0 + b))],
        out_specs=pl.BlockSpec((C, LANES), lambda b, h: (b, h)),
        scratch_shapes=[pltpu.VMEM((MLA_VT_ROWS, C), F32)],
        compiler_params=_cp(("arbitrary", "arbitrary")),
        name="mla_attn_ctx",
    )(qt, k, vt)
    return o_lat, o_ctx


def _dft_tables(S, C, gc):
    P = math.isqrt(S)
    assert P * P == S and (P & (P - 1)) == 0 and (gc & (gc - 1)) == 0 and (C & (C - 1)) == 0

    def cs(idx, n):
        ang = (idx % n).astype(F32) * (2.0 * math.pi / n)
        return jnp.cos(ang), jnp.sin(ang)

    k1 = jnp.arange(P, dtype=I32)
    idx = k1[None, :, None] * (P * k1[None, None, :] + k1[:, None, None])
    c, s = cs(idx, S)
    m1 = jnp.concatenate([c, -s], axis=1) * (1.0 / P)
    c, s = cs(k1[:, None] * k1[None, :], P)
    m2 = jnp.concatenate([jnp.concatenate([c, s], axis=1), jnp.concatenate([-s, c], axis=1)], axis=0)
    kc = jnp.arange(gc, dtype=I32)
    c, s = cs(kc[:, None] * kc[None, :], gc)
    mc = jnp.concatenate([c, s], axis=0) * (gc ** -0.5)
    kq = jnp.arange(C, dtype=I32)
    c, s = cs(kq[:, None] * kq[None, :], C)
    mctx = jnp.concatenate([c, s], axis=0) * (C ** -0.5)
    return m1.astype(BF16), m2.astype(BF16), mc.astype(BF16), mctx.astype(BF16)


def _fnet_stage1_kernel(x_ref, g_ref, sc_ref, sh_ref, m1_ref, z_ref, *, n2c, P):
    g, sc, sh = g_ref[...], sc_ref[0], sh_ref[0]
    for j in range(n2c):
        h = _normmod(x_ref[:, j, :], g, sc, sh).astype(BF16)
        z = _dot(m1_ref[j], h)
        z_ref[0, :, 0, j, :] = z[:P]
        z_ref[0, :, 1, j, :] = z[P:]


def _fnet_stage2_kernel(z_ref, m2_ref, mc_ref, f_ref, *, k1c, P, gc):
    D = f_ref.shape[-1]
    for j in range(k1c):
        z = z_ref[0, j].reshape(2 * P, D).astype(BF16)
        y = _dot(m2_ref[...], z)
        yr, yi = y[:P].astype(BF16), y[P:].astype(BF16)
        outs = []
        for gi in range(D // gc):
            sl = slice(gi * gc, (gi + 1) * gc)
            outs.append(_dot(yr[:, sl], mc_ref[:gc, :]) + _dot(yi[:, sl], mc_ref[gc:, :]))
        f_ref[:, j, :] = jnp.concatenate(outs, axis=1)


def _fnet_ctx_kernel(x_ref, g_ref, sc_ref, sh_ref, ml_ref, mc_ref, f_ref, *, C, gc):
    D = x_ref.shape[-1]
    h = _normmod(x_ref[...], g_ref[...], sc_ref[0], sh_ref[0]).astype(BF16)
    y = _dot(ml_ref[...], h)
    yc, ys = y[:C].astype(BF16), y[C:].astype(BF16)
    outs = []
    for gi in range(D // gc):
        sl = slice(gi * gc, (gi + 1) * gc)
        outs.append(_dot(yc[:, sl], mc_ref[:gc, :]) - _dot(ys[:, sl], mc_ref[gc:, :]))
    f_ref[...] = jnp.concatenate(outs, axis=1)


def _fnet_mix(lay, x, g, modv, tables):
    B, S, C, D = lay.B, lay.S, lay.C, lay.D
    m1, m2, mc, mctx = tables
    P = math.isqrt(S)
    gc = D // FNET_GROUPS
    n2c = SUBLANES
    k1c = SUBLANES
    full = lambda a: pl.BlockSpec(a.shape, lambda *i: (0,) * a.ndim)
    modspec = lambda chunk: pl.BlockSpec((1, 1, D), lambda b, j: (chunk * SUBLANES + b, 0, 0))
    assert C % P == 0 and P % n2c == 0 and P % k1c == 0
    x3 = x.reshape(lay.NT // P, P, D)
    z = pl.pallas_call(
        functools.partial(_fnet_stage1_kernel, n2c=n2c, P=P),
        out_shape=jax.ShapeDtypeStruct((B, P, 2, P, D), F32),
        grid=(B, P // n2c),
        in_specs=[pl.BlockSpec((P, n2c, D), lambda b, j: (b, j, 0)), full(g), modspec(1), modspec(0),
                  pl.BlockSpec((n2c, 2 * P, P), lambda b, j: (j, 0, 0))],
        out_specs=pl.BlockSpec((1, P, 2, n2c, D), lambda b, j: (b, 0, 0, j, 0)),
        compiler_params=_cp(("arbitrary", "arbitrary")),
        name="fnet_stage1",
    )(x3, g, modv, modv, m1)
    f_lat = pl.pallas_call(
        functools.partial(_fnet_stage2_kernel, k1c=k1c, P=P, gc=gc),
        out_shape=jax.ShapeDtypeStruct((lay.NL // P, P, D), F32),
        grid=(B, P // k1c),
        in_specs=[pl.BlockSpec((1, k1c, 2, P, D), lambda b, j: (b, j, 0, 0, 0)), full(m2), full(mc)],
        out_specs=pl.BlockSpec((P, k1c, D), lambda b, j: (b, j, 0)),
        compiler_params=_cp(("arbitrary", "arbitrary")),
        name="fnet_stage2",
    )(z, m2, mc)
    cblk0 = lay.NL // C
    ctx_mod = lambda chunk: pl.BlockSpec((1, 1, D), lambda b: (chunk * SUBLANES + B, 0, 0))
    f_ctx = pl.pallas_call(
        functools.partial(_fnet_ctx_kernel, C=C, gc=gc),
        out_shape=jax.ShapeDtypeStruct((lay.NC, D), F32),
        grid=(B,),
        in_specs=[pl.BlockSpec((C, D), lambda b: (cblk0 + b, 0)), full(g), ctx_mod(1), ctx_mod(0), full(mctx),
                  full(mc)],
        out_specs=pl.BlockSpec((C, D), lambda b: (b, 0)),
        compiler_params=_cp(("arbitrary",)),
        name="fnet_ctx",
    )(x, g, modv, modv, mctx, mc)
    return f_lat.reshape(lay.NL, D), f_ctx


NA_QROWS = 2
NA_KROWS = NA_KR + NA_QROWS
NA_VARIANTS = 5
NA_ONES = 16


def _na_bias_kernel(rpb_ref, sel_ref, toe_ref, o_ref):
    g = jnp.dot(rpb_ref[0], toe_ref[...], precision=HIGHEST, preferred_element_type=F32)
    for t in range(NA_VARIANTS):
        o_ref[t, 0] = jnp.dot(sel_ref[t], g, precision=HIGHEST, preferred_element_type=F32)


def _na_window_start(r, rows):
    return jnp.clip(r - NA_KR // 2, 0, rows - NA_KROWS)


def _na_bias_tables(rpb, rows):
    assert rows >= 16 and rows % NA_QROWS == 0
    W = GRID_W
    nh, nu, nv = rpb.shape
    up, vp, ajp = 2 * SUBLANES, LANES, 3 * SUBLANES
    assert nu <= up and nv <= vp and NA_QROWS * NA_KROWS <= ajp
    cq, ck = np.arange(W)[:, None], np.arange(W)[None, :]
    dc = np.clip(ck - cq + NA_KC - 1, 0, nv - 1).reshape(-1)
    toe = (np.arange(vp)[:, None] == dc[None, :]).astype(np.float32)
    cs = np.clip(cq - NA_KC // 2, 0, W - NA_KC)
    col_ok = (ck >= cs) & (ck < cs + NA_KC)
    sel = np.zeros((NA_VARIANTS, ajp, up), np.float32)
    valid = np.zeros((NA_VARIANTS, NA_QROWS, NA_KROWS, W, W), bool)
    for t, r in enumerate((0, 2, 6, rows - 4, rows - 2)):
        w0 = min(max(r - NA_KR // 2, 0), rows - NA_KROWS)
        for a in range(NA_QROWS):
            rs = min(max(r + a - NA_KR // 2, 0), rows - NA_KR)
            for j in range(NA_KROWS):
                rk = w0 + j
                sel[t, a * NA_KROWS + j, min(max(rk - (r + a) + NA_KR - 1, 0), nu - 1)] = 1.0
                if rs <= rk < rs + NA_KR:
                    valid[t, a, j] = col_ok
    rpb_p = jnp.pad(rpb, ((0, 0), (0, up - nu), (0, vp - nv)))
    tab = pl.pallas_call(
        _na_bias_kernel,
        out_shape=jax.ShapeDtypeStruct((NA_VARIANTS, nh, ajp, W * W), F32),
        grid=(nh,),
        in_specs=[pl.BlockSpec((1, up, vp), lambda h: (h, 0, 0)),
                  pl.BlockSpec(sel.shape, lambda h: (0, 0, 0)),
                  pl.BlockSpec(toe.shape, lambda h: (0, 0))],
        out_specs=pl.BlockSpec((NA_VARIANTS, 1, ajp, W * W), lambda h: (0, h, 0, 0)),
        compiler_params=_cp(("arbitrary",)),
        name="na_bias",
    )(rpb_p, jnp.asarray(sel), jnp.asarray(toe))
    tab = tab[:, :, :NA_QROWS * NA_KROWS].reshape(NA_VARIANTS, nh, NA_QROWS, NA_KROWS, W, W)
    tab = jnp.where(jnp.asarray(valid)[:, None], tab * math.log2(math.e), NEG_INF)
    tab = tab.reshape(NA_VARIANTS, nh // 2, 2, NA_QROWS, NA_KROWS, W, W)
    return tab.transpose(0, 1, 4, 6, 2, 3, 5).reshape(NA_VARIANTS, nh // 2, NA_KROWS * W, 2 * NA_QROWS * W)


def _na_kernel(*refs, rows, local):
    if local:
        qt_ref, kc_ref, vtc_ref, kl_ref, vtl_ref, bias_ref, o_ref, st_a, sc_a, st_b, sc_b = refs
    else:
        qt_ref, kc_ref, vtc_ref, o_ref, sc_a, sc_b = refs
        st_a = st_b = None
    nq = NA_QROWS * GRID_W
    nk = NA_KROWS * GRID_W
    kc, vtc = kc_ref[...], vtc_ref[...]
    row = lax.broadcasted_iota(I32, (LANES, nq), 0)
    pairs = qt_ref.shape[1] // nq
    qi = pl.program_id(2) if local else 0

    def ext(vt):
        return jnp.concatenate([vt, jnp.ones((NA_ONES, vt.shape[1]), BF16)], axis=0)

    def window(t):
        r = (qi * pairs + t) * NA_QROWS
        k0 = pl.multiple_of(_na_window_start(r, rows) * GRID_W, LANES)
        var = jnp.where(r == 0, 0, jnp.where(r == 2, 1, jnp.where(r == rows - 4, 3, jnp.where(r == rows - 2, 4, 2))))
        return k0, var

    def scores(t, st_ref, sc_ref):
        qt = qt_ref[:, t * nq:(t + 1) * nq]
        zero = jnp.zeros_like(qt)
        qbd = jnp.concatenate([jnp.where(row < NA_HD, qt, zero), jnp.where(row >= NA_HD, qt, zero)], axis=1)
        sc_ref[...] = _dot(kc, qbd)
        if local:
            k0, var = window(t)
            st_ref[...] = _dot(kl_ref[pl.ds(k0, nk), :], qbd) + bias_ref[var, 0]

    def finish(t, st_ref, sc_ref):
        sc = sc_ref[...]
        m = jnp.max(sc, axis=0, keepdims=True)
        if local:
            st = st_ref[...]
            m = jnp.maximum(m, jnp.max(st, axis=0, keepdims=True))
            pt = jnp.exp2(st - m).astype(BF16)
            vtw = vtl_ref[:, pl.ds(window(t)[0], nk)]
        pc = jnp.exp2(sc - m).astype(BF16)
        outs = []
        for hl in range(2):
            hs, qs = slice(hl * NA_HD, (hl + 1) * NA_HD), slice(hl * nq, (hl + 1) * nq)
            acc = _dot(ext(vtc[hs, :]), pc[:, qs])
            if local:
                acc = acc + _dot(ext(vtw[hs, :]), pt[:, qs])
            outs.append(acc[:NA_HD] / acc[NA_HD:NA_HD + 1])
        o_ref[t * nq:(t + 1) * nq, :] = jnp.concatenate(outs, axis=0).T.astype(o_ref.dtype)

    bufs = ((st_a, sc_a), (st_b, sc_b))
    scores(0, *bufs[0])
    for t in range(pairs):
        if t + 1 < pairs:
            scores(t + 1, *bufs[(t + 1) % 2])
        finish(t, *bufs[t % 2])


def _na_attention(lay, qt, k, vt, bias, *, tq=512):
    B, S, C = lay.B, lay.S, lay.C
    rows = S // GRID_W
    HP = NA_HEADS // 2
    nq = S // tq
    cblk0 = lay.NL // C
    nk, nqp = NA_KROWS * GRID_W, NA_QROWS * GRID_W
    o_lat = pl.pallas_call(
        functools.partial(_na_kernel, rows=rows, local=True),
        out_shape=jax.ShapeDtypeStruct((lay.NL, NA_HEADS * NA_HD), BF16),
        grid=(B, HP, nq),
        in_specs=[pl.BlockSpec((LANES, tq), lambda b, h, i: (h, b * nq + i)),
                  pl.BlockSpec((C, LANES), lambda b, h, i: (cblk0 + b, h)),
                  pl.BlockSpec((LANES, C), lambda b, h, i: (h, cblk0 + b)),
                  pl.BlockSpec((S, LANES), lambda b, h, i: (b, h)),
                  pl.BlockSpec((LANES, S), lambda b, h, i: (h, b)),
                  pl.BlockSpec((NA_VARIANTS, 1) + bias.shape[2:], lambda b, h, i: (0, h, 0, 0))],
        out_specs=pl.BlockSpec((tq, LANES), lambda b, h, i: (b * nq + i, h)),
        scratch_shapes=[pltpu.VMEM((nk, 2 * nqp), F32), pltpu.VMEM((C, 2 * nqp), F32)] * 2,
        compiler_params=_cp(("arbitrary", "arbitrary", "arbitrary")),
        name="na_attn_latent",
    )(qt, k, vt, k, vt, bias)
    o_ctx = pl.pallas_call(
        functools.partial(_na_kernel, rows=rows, local=False),
        out_shape=jax.ShapeDtypeStruct((lay.NC, NA_HEADS * NA_HD), BF16),
        grid=(B, HP),
        in_specs=[pl.BlockSpec((LANES, C), lambda b, h: (h, cblk0 + b)),
                  pl.BlockSpec((C, LANES), lambda b, h: (cblk0 + b, h)),
                  pl.BlockSpec((LANES, C), lambda b, h: (h, cblk0 + b))],
        out_specs=pl.BlockSpec((C, LANES), lambda b, h: (b, h)),
        scratch_shapes=[pltpu.VMEM((C, 2 * nqp), F32)] * 2,
        compiler_params=_cp(("arbitrary", "arbitrary")),
        name="na_attn_ctx",
    )(qt, k, vt)
    return o_lat, o_ctx


SW_SUB = 128
SW_BAND = SW_SUB + 2 * SW_WINDOW
SW_ONES = 16


def _swa_kernel(*refs, S, local):
    if local:
        (sink_ref, qt_ref, kc_ref, vtc_ref, kp_ref, kcur_ref, kn_ref, vtp_ref, vtcur_ref, vtn_ref, o_ref,
         kbuf, vtbuf, st_a, sc_a, st_b, sc_b) = refs
        tq = qt_ref.shape[1]
        W = SW_WINDOW
        kbuf[0:W] = kp_ref[...]
        kbuf[W:W + tq] = kcur_ref[...]
        kbuf[W + tq:] = kn_ref[...]
        vtbuf[:, 0:W] = vtp_ref[...]
        vtbuf[:, W:W + tq] = vtcur_ref[...]
        vtbuf[:, W + tq:] = vtn_ref[...]
        i = pl.program_id(1)
    else:
        sink_ref, qt_ref, kc_ref, vtc_ref, o_ref, sc_a, sc_b = refs
        st_a = st_b = None
        tq = qt_ref.shape[1]
    G = SW_HEADS // SW_KV_HEADS
    kc, vtc = kc_ref[...], vtc_ref[...]
    zeros_q = jnp.zeros((SW_HD, G * SW_SUB), BF16)

    def ext(vt):
        return jnp.concatenate([vt, jnp.ones((SW_ONES, vt.shape[1]), BF16)], axis=0)

    def band_bias(sb):
        key = lax.broadcasted_iota(I32, (SW_BAND, SW_SUB), 0)
        qry = lax.broadcasted_iota(I32, (SW_BAND, SW_SUB), 1)
        rel = key - SW_WINDOW - qry
        kpos = i * tq + sb * SW_SUB - SW_WINDOW + key
        ok = (jnp.abs(rel) <= SW_WINDOW) & (kpos >= 0) & (kpos < S)
        bias = jnp.where(ok, 0.0, NEG_INF).astype(F32)
        return jnp.concatenate([bias] * G, axis=1)

    def scores(sb, g, bias, st_ref, sc_ref):
        r0 = sb * SW_SUB
        tile, half = g // 2, g % 2
        sl = slice(tile * LANES, (tile + 1) * LANES)
        qg = jnp.concatenate([qt_ref[(G * g + hl) * SW_HD:(G * g + hl + 1) * SW_HD, r0:r0 + SW_SUB]
                              for hl in range(G)], axis=1)
        qpad = jnp.concatenate([qg, zeros_q] if half == 0 else [zeros_q, qg], axis=0)
        sc_ref[...] = _dot(kc[:, sl], qpad)
        if local:
            st_ref[...] = _dot(kbuf[r0:r0 + SW_BAND, sl], qpad) + bias

    def finish(sb, g, st_ref, sc_ref):
        r0 = sb * SW_SUB
        sink = sink_ref[g]
        sc = sc_ref[...]
        m = jnp.maximum(jnp.max(sc, axis=0, keepdims=True), sink)
        if local:
            st = st_ref[...]
            m = jnp.maximum(m, jnp.max(st, axis=0, keepdims=True))
        acc = _dot(ext(vtc[g * SW_HD:(g + 1) * SW_HD, :]), jnp.exp2(sc - m).astype(BF16))
        if local:
            acc = acc + _dot(ext(vtbuf[g * SW_HD:(g + 1) * SW_HD, r0:r0 + SW_BAND]), jnp.exp2(st - m).astype(BF16))
        og = acc[:SW_HD] / (acc[SW_HD:SW_HD + 1] + jnp.exp2(sink - m))
        return [og[:, hl * SW_SUB:(hl + 1) * SW_SUB] for hl in range(G)]

    items = [(sb, g) for sb in range(tq // SW_SUB) for g in range(SW_KV_HEADS)]
    bufs = ((st_a, sc_a), (st_b, sc_b))
    bias = band_bias(0) if local else None
    scores(*items[0], bias, *bufs[0])
    outs = []
    for n, (sb, g) in enumerate(items):
        if n + 1 < len(items):
            nsb, ng = items[n + 1]
            if local and ng == 0:
                bias = band_bias(nsb)
            scores(nsb, ng, bias, *bufs[(n + 1) % 2])
        outs.extend(finish(sb, g, *bufs[n % 2]))
        if g == SW_KV_HEADS - 1:
            o_ref[sb * SW_SUB:(sb + 1) * SW_SUB, :] = jnp.concatenate(outs, axis=0).T.astype(o_ref.dtype)
            outs = []


def _swa_attention(lay, qt, k, vt, sinks, *, tq=512):
    B, S, C = lay.B, lay.S, lay.C
    nq_rows = SW_HEADS * SW_HD
    nkv = SW_KV_HEADS * SW_HD
    G = SW_HEADS // SW_KV_HEADS
    nq = S // tq
    per = tq // SW_WINDOW
    last = lay.NT // SW_WINDOW - 1
    cblk0 = lay.NL // C
    prev = lambda b, i: jnp.maximum((b * nq + i) * per - 1, 0)
    nxt = lambda b, i: jnp.minimum((b * nq + i + 1) * per, last)
    sink_l = jnp.repeat(sinks.reshape(SW_KV_HEADS, 1, G) * math.log2(math.e), SW_SUB, axis=2)
    sink_spec = pl.BlockSpec(sink_l.shape, lambda *a: (0, 0, 0))
    o_lat = pl.pallas_call(
        functools.partial(_swa_kernel, S=S, local=True),
        out_shape=jax.ShapeDtypeStruct((lay.NL, nq_rows), BF16),
        grid=(B, nq),
        in_specs=[sink_spec,
                  pl.BlockSpec((nq_rows, tq), lambda b, i: (0, b * nq + i)),
                  pl.BlockSpec((C, nkv), lambda b, i: (cblk0 + b, 0)),
                  pl.BlockSpec((nkv, C), lambda b, i: (0, cblk0 + b)),
                  pl.BlockSpec((SW_WINDOW, nkv), lambda b, i: (prev(b, i), 0)),
                  pl.BlockSpec((tq, nkv), lambda b, i: (b * nq + i, 0)),
                  pl.BlockSpec((SW_WINDOW, nkv), lambda b, i: (nxt(b, i), 0)),
                  pl.BlockSpec((nkv, SW_WINDOW), lambda b, i: (0, prev(b, i))),
                  pl.BlockSpec((nkv, tq), lambda b, i: (0, b * nq + i)),
                  pl.BlockSpec((nkv, SW_WINDOW), lambda b, i: (0, nxt(b, i)))],
        out_specs=pl.BlockSpec((tq, nq_rows), lambda b, i: (b * nq + i, 0)),
        scratch_shapes=([pltpu.VMEM((tq + 2 * SW_WINDOW, nkv), BF16), pltpu.VMEM((nkv, tq + 2 * SW_WINDOW), BF16)]
                        + [pltpu.VMEM((SW_BAND, G * SW_SUB), F32), pltpu.VMEM((C, G * SW_SUB), F32)] * 2),
        compiler_params=_cp(("arbitrary", "arbitrary")),
        name="swa_attn_latent",
    )(sink_l, qt, k, vt, k, k, k, vt, vt, vt)
    o_ctx = pl.pallas_call(
        functools.partial(_swa_kernel, S=S, local=False),
        out_shape=jax.ShapeDtypeStruct((lay.NC, nq_rows), BF16),
        grid=(B,),
        in_specs=[sink_spec,
                  pl.BlockSpec((nq_rows, C), lambda b: (0, cblk0 + b)),
                  pl.BlockSpec((C, nkv), lambda b: (cblk0 + b, 0)),
                  pl.BlockSpec((nkv, C), lambda b: (0, cblk0 + b))],
        out_specs=pl.BlockSpec((C, nq_rows), lambda b: (b, 0)),
        scratch_shapes=[pltpu.VMEM((C, G * SW_SUB), F32)] * 2,
        compiler_params=_cp(("arbitrary",)),
        name="swa_attn_ctx",
    )(sink_l, qt, k, vt)
    return o_lat, o_ctx


def _route(logits, tri, carry):
    lane = _lane_iota(logits.shape)
    lanef = lane.astype(F32)
    big = float(LANES)
    rowmax = lambda t: jnp.max(t, axis=-1, keepdims=True)
    rowmin = lambda t: jnp.min(t, axis=-1, keepdims=True)
    rowsum = lambda t: jnp.sum(t, axis=-1, keepdims=True)
    is_g = lane < MOE_GROUPS
    mg = rowmax(jnp.where(is_g, logits, -jnp.inf))
    w_g = 1.0 / rowsum(jnp.where(is_g, jnp.exp(logits - mg), 0.0))
    gidx = rowmin(jnp.where(is_g & (logits == mg), lanef, big))
    g0 = MOE_GROUPS + MOE_PER_GROUP * gidx
    in_grp = (lanef >= g0) & (lanef < g0 + MOE_PER_GROUP)
    le = jnp.where(in_grp, logits, -jnp.inf)
    m1 = rowmax(le)
    i1 = rowmin(jnp.where(in_grp & (le == m1), lanef, big))
    le2 = jnp.where(lanef == i1, -jnp.inf, le)
    m2 = rowmax(le2)
    i2 = rowmin(jnp.where(in_grp & (lanef != i1) & (le2 == m2), lanef, big))
    r = jnp.exp(m2 - m1)
    gate1 = w_g / (1.0 + r)
    gate2 = w_g * r / (1.0 + r)
    sel1, sel2 = lanef == i1, lanef == i2
    member = (sel1 | sel2)
    cum = _dot(tri, member.astype(BF16)) + carry
    rank1 = rowsum(jnp.where(sel1, cum, 0.0))
    rank2 = rowsum(jnp.where(sel2, cum, 0.0))
    new_carry = carry + jnp.sum(member.astype(F32), axis=0, keepdims=True)
    rec = jnp.zeros_like(logits)
    for ln, val in ((0, i1 - MOE_GROUPS), (1, i2 - MOE_GROUPS), (2, rank1), (3, rank2), (4, gate1), (5, gate2)):
        rec = jnp.where(lane == ln, val, rec)
    return rec, new_carry


def _out_proj_kernel(*refs, has_bias, n_lat_tiles):
    if has_bias:
        (al_ref, ac_ref, x_ref, w_ref, b_ref, g1_ref, g2n_ref, sc_ref, sh_ref, wr_ref, br_ref, tri_ref,
         xo_ref, h2_ref, rec_ref, cnt_ref, carry_ref) = refs
    else:
        (al_ref, ac_ref, x_ref, w_ref, g1_ref, g2n_ref, sc_ref, sh_ref, wr_ref, br_ref, tri_ref,
         xo_ref, h2_ref, rec_ref, cnt_ref, carry_ref) = refs

    @pl.when(pl.program_id(0) == 0)
    def _():
        carry_ref[...] = jnp.zeros_like(carry_ref)

    a = jnp.where(pl.program_id(0) < n_lat_tiles, al_ref[...], ac_ref[...])
    y = _dot(a.astype(BF16), w_ref[...])
    if has_bias:
        y = y + b_ref[...]
    xn = x_ref[...] + g1_ref[0] * y
    xo_ref[...] = xn
    h2 = _normmod(xn, g2n_ref[...], sc_ref[0], sh_ref[0])
    h2_ref[...] = h2
    h_hi = h2.astype(BF16)
    h_lo = (h2 - h_hi.astype(F32)).astype(BF16)
    hw = _dot(h_hi, wr_ref[...])
    logits = hw[:, :LANES] + hw[:, LANES:] + _dot(h_lo, wr_ref[:, :LANES]) + br_ref[...]
    rec, carry = _route(logits, tri_ref[...], carry_ref[...])
    rec_ref[...] = rec
    carry_ref[...] = carry
    cnt_ref[...] = jnp.broadcast_to(carry, cnt_ref.shape)


def _out_proj(lay, a, x, w, b, g2n, modv, wr, br, tri):
    D = lay.D
    a_lat, a_ctx = a
    nl = lay.nl_tiles
    full = lambda t: pl.BlockSpec(t.shape, lambda i: (0,) * t.ndim)
    row = lambda n: pl.BlockSpec((TM, n), lambda i: (i, 0))
    ins = [a_lat, a_ctx, x, w] + ([b] if b is not None else []) + [modv, g2n, modv, modv, wr, br, tri]
    specs = ([pl.BlockSpec((TM, a_lat.shape[1]), lambda i: (jnp.minimum(i, nl - 1), 0)),
              pl.BlockSpec((TM, a_ctx.shape[1]), lambda i: (jnp.maximum(i - nl, 0), 0)), row(D), full(w)]
             + ([full(b)] if b is not None else [])
             + [lay.mod_spec(2), full(g2n), lay.mod_spec(4), lay.mod_spec(3), full(wr), full(br), full(tri)])
    return pl.pallas_call(
        functools.partial(_out_proj_kernel, has_bias=b is not None, n_lat_tiles=nl),
        out_shape=(jax.ShapeDtypeStruct((lay.NT, D), F32), jax.ShapeDtypeStruct((lay.NT, D), F32),
                   jax.ShapeDtypeStruct((lay.NT, LANES), F32), jax.ShapeDtypeStruct((SUBLANES, LANES), F32)),
        grid=(lay.n_tiles,),
        in_specs=specs,
        out_specs=(row(D), row(D), row(LANES), pl.BlockSpec((SUBLANES, LANES), lambda i: (0, 0))),
        scratch_shapes=[pltpu.VMEM((1, LANES), F32)],
        input_output_aliases={2: 0},
        compiler_params=_cp(("arbitrary",)),
        name="out_proj_router",
    )(*ins)


def _row_copy(src, s, dst, d, sem):
    return pltpu.make_async_copy(src.at[pl.ds(s, 1), :], dst.at[pl.ds(d, 1), :], sem)


def _dispatch_kernel(pos_ref, pend_ref, nu_ref, h_ref, xb_ref, zbuf, sem, zsem):
    base = pl.program_id(0) * TM
    nblk = xb_ref.shape[0] // MOE_BM

    @pl.when(pl.program_id(0) == 0)
    def _():
        zbuf[...] = jnp.zeros_like(zbuf)

        def zero_block(row0):
            return pltpu.make_async_copy(zbuf, xb_ref.at[pl.ds(pl.multiple_of(row0, MOE_BM), MOE_BM), :], zsem)

        def each(fn):
            def expert(e, c):
                end = pend_ref[e]

                @pl.when(end > jnp.where(e > 0, pend_ref[jnp.maximum(e - 1, 0)], 0))
                def _():
                    fn(zero_block(end - MOE_BM))
                return c

            def tail(j, c):
                @pl.when(j >= nu_ref[0])
                def _():
                    fn(zero_block(j * MOE_BM))
                return c

            lax.fori_loop(0, MOE_EXPERTS, expert, 0)
            lax.fori_loop(0, nblk, tail, 0)

        each(lambda cp: cp.start())
        each(lambda cp: cp.wait())

    def issue(r, c):
        for k in range(2):
            _row_copy(h_ref, r, xb_ref, pos_ref[(base + r) * 2 + k], sem).start()
        return c

    lax.fori_loop(0, TM, issue, 0, unroll=8)
    for k in range(2):
        pltpu.make_async_copy(h_ref, xb_ref.at[pl.ds(0, TM), :], sem).wait()


def _dispatch(lay, pos, pend, n_used, h2, cap):
    D = lay.D
    return pl.pallas_call(
        _dispatch_kernel,
        out_shape=jax.ShapeDtypeStruct((cap, D), F32),
        grid_spec=pltpu.PrefetchScalarGridSpec(
            num_scalar_prefetch=3, grid=(lay.n_tiles,),
            in_specs=[pl.BlockSpec((TM, D), lambda i, p, e, n: (i, 0))],
            out_specs=pl.BlockSpec(memory_space=pl.ANY),
            scratch_shapes=[pltpu.VMEM((MOE_BM, D), F32), pltpu.SemaphoreType.DMA, pltpu.SemaphoreType.DMA]),
        compiler_params=_cp(("arbitrary",)),
        name="moe_dispatch",
    )(pos, pend, n_used, h2)


def _expert_kernel(be_ref, nu_ref, xb_ref, wg_ref, wu_ref, wd_ref, yb_ref, wgb, wub, wdb):
    j = pl.program_id(0)
    prev = be_ref[jnp.maximum(j - 1, 0)]

    @pl.when((j == 0) | (be_ref[j] != prev))
    def _():
        wgb[...] = wg_ref[0, 0].astype(BF16)
        wub[...] = wu_ref[0, 0].astype(BF16)
        wdb[...] = wd_ref[0, 0].astype(BF16)

    @pl.when(j < nu_ref[0])
    def _():
        xe = xb_ref[...].astype(BF16)
        g = _dot(xe, wgb[...])
        u = _dot(xe, wub[...])
        act = (g * jax.nn.sigmoid(g) * u).astype(BF16)
        yb_ref[...] = _dot(act, wdb[...])

    @pl.when(j >= nu_ref[0])
    def _():
        yb_ref[...] = jnp.zeros_like(yb_ref)


def _experts(xb, blk_e, n_used, w_gate, w_up, w_down, layer):
    cap, D = xb.shape
    FF = w_gate.shape[-1]
    nblk = cap // MOE_BM
    blk = lambda j, be, nu: (jnp.maximum(jnp.minimum(j, nu[0] - 1), 0), 0)
    wblk = lambda j, be, nu: (layer, be[j], 0, 0)
    return pl.pallas_call(
        _expert_kernel,
        out_shape=jax.ShapeDtypeStruct((cap, D), F32),
        grid_spec=pltpu.PrefetchScalarGridSpec(
            num_scalar_prefetch=2, grid=(nblk,),
            in_specs=[pl.BlockSpec((MOE_BM, D), blk),
                      pl.BlockSpec((1, 1, D, FF), wblk), pl.BlockSpec((1, 1, D, FF), wblk),
                      pl.BlockSpec((1, 1, FF, D), wblk)],
            out_specs=pl.BlockSpec((MOE_BM, D), lambda j, be, nu: (j, 0)),
            scratch_shapes=[pltpu.VMEM((D, FF), BF16), pltpu.VMEM((D, FF), BF16), pltpu.VMEM((FF, D), BF16)]),
        compiler_params=_cp(("arbitrary",)),
        name="moe_experts",
    )(blk_e, n_used, xb, w_gate, w_up, w_down)


def _combine_kernel(*refs, final):
    if final:
        pos_ref, x_ref, rec_ref, g2_ref, yb_ref, fg_ref, o_ref, buf, sem = refs
    else:
        pos_ref, x_ref, rec_ref, g2_ref, yb_ref, o_ref, buf, sem = refs
    base = pl.program_id(0) * TM

    def issue(r, c):
        for k in range(2):
            _row_copy(yb_ref, pos_ref[(base + r) * 2 + k], buf.at[k], r, sem).start()
        return c

    lax.fori_loop(0, TM, issue, 0, unroll=8)
    for k in range(2):
        pltpu.make_async_copy(yb_ref.at[pl.ds(0, TM), :], buf.at[k], sem).wait()
    rec = rec_ref[...]
    f = rec[:, ROUTE_LANE_GATE:ROUTE_LANE_GATE + 1] * buf[0] + rec[:, ROUTE_LANE_GATE + 1:ROUTE_LANE_GATE + 2] * buf[1]
    xn = x_ref[...] + g2_ref[0] * f
    if final:
        xn = _rms(xn, fg_ref[...])
    o_ref[...] = xn


def _combine(lay, pos, x, rec, modv, yb, final_g):
    D = lay.D
    final = final_g is not None
    row = lambda n: pl.BlockSpec((TM, n), lambda i, p: (i, 0))
    specs = [row(D), row(LANES),
             pl.BlockSpec((1, 1, D), lambda i, p: (5 * SUBLANES + lay.mod_row(i), 0, 0)),
             pl.BlockSpec(memory_space=pl.ANY)]
    ins = [x, rec, modv, yb]
    if final:
        specs.append(pl.BlockSpec(final_g.shape, lambda i, p: (0, 0)))
        ins.append(final_g)
    n_rows, n_tiles = (lay.NL, lay.nl_tiles) if final else (lay.NT, lay.n_tiles)
    return pl.pallas_call(
        functools.partial(_combine_kernel, final=final),
        out_shape=jax.ShapeDtypeStruct((n_rows, D), F32),
        grid_spec=pltpu.PrefetchScalarGridSpec(
            num_scalar_prefetch=1, grid=(n_tiles,),
            in_specs=specs,
            out_specs=row(D),
            scratch_shapes=[pltpu.VMEM((2, TM, D), F32), pltpu.SemaphoreType.DMA]),
        input_output_aliases={} if final else {1: 0},
        compiler_params=_cp(("arbitrary",)),
        name="moe_combine",
    )(pos, *ins)


def _moe(lay, x, h2, rec, counts, modv, w_gate, w_up, w_down, layer, final_g):
    T = 2 * lay.NT
    cap = -(-T // MOE_BM) * MOE_BM + MOE_EXPERTS * MOE_BM
    cnt = counts[0, MOE_GROUPS:MOE_GROUPS + MOE_EXPERTS].astype(I32)
    pcnt = (cnt + MOE_BM - 1) // MOE_BM * MOE_BM
    pend = jnp.cumsum(pcnt)
    start = pend - pcnt
    eid = rec[:, ROUTE_LANE_EID:ROUTE_LANE_EID + 2].astype(I32)
    rank = rec[:, ROUTE_LANE_RANK:ROUTE_LANE_RANK + 2].astype(I32)
    pos = (start[eid] + rank).reshape(-1)
    blk_row = jnp.arange(cap // MOE_BM, dtype=I32) * MOE_BM
    blk_e = jnp.minimum(jnp.sum((pend[None, :] <= blk_row[:, None]).astype(I32), axis=1), MOE_EXPERTS - 1)
    n_used = (pend[-1:] // MOE_BM).astype(I32)
    xb = _dispatch(lay, pos, pend.astype(I32), n_used, h2, cap)
    yb = _experts(xb, blk_e, n_used, w_gate, w_up, w_down, layer)
    return _combine(lay, pos, x, rec, modv, yb, final_g)


def _rope_tables(lay):
    S = lay.S
    t = jnp.arange(S)
    n = MLA_ROPE // 4
    inv = ROPE_THETA ** (-jnp.arange(n, dtype=F32) / n)
    ang = jnp.concatenate([(t // GRID_W).astype(F32)[:, None] * inv, (t % GRID_W).astype(F32)[:, None] * inv], axis=-1)
    cos, sin = jnp.cos(ang), jnp.sin(ang)
    cos64 = jnp.concatenate([cos, cos], axis=-1)
    sin64 = jnp.concatenate([-sin, sin], axis=-1)
    rows = lambda lat, ctx_val: jnp.concatenate([jnp.tile(lat, (lay.B, 1)), jnp.full((lay.NC, 64), ctx_val, F32)], axis=0)
    cos64, sin64 = rows(cos64, 1.0), rows(sin64, 0.0)
    zero = jnp.zeros_like(cos64)
    return ((jnp.concatenate([cos64, zero], axis=1), jnp.concatenate([sin64, zero], axis=1)),
            (jnp.concatenate([cos64, cos64], axis=1), jnp.concatenate([sin64, sin64], axis=1)))


def kernel(x, c, ctx, c_ctx, mod_w, mod_b, norm1_g, norm2_g, mla_w_dq, mla_g_q, mla_w_uq, mla_w_dkv, mla_g_kv, mla_w_ukv, mla_w_o, fnet_w_o, fnet_b_o, na_w_qkv, na_rpb, na_w_o, swa_w_qkv, swa_sinks, swa_w_o, moe_w_grp, moe_b_grp, moe_w_rt, moe_b_rt, moe_w_gate, moe_w_up, moe_w_down, final_g):
    B, S, D = x.shape
    C = ctx.shape[1]
    depth = mod_w.shape[0]
    lay = _Layout(B, S, C, D)
    X = jnp.concatenate([x.reshape(B * S, D), ctx.reshape(B * C, D)], axis=0)
    cond = jnp.concatenate([c, c_ctx[None], jnp.zeros((SUBLANES - B - 1, D), F32)], axis=0)
    mod = _modulation(cond, mod_w, mod_b)
    (mla_cos, mla_sin), (swa_cos, swa_sin) = _rope_tables(lay)
    tri = (jnp.arange(TM)[:, None] > jnp.arange(TM)[None, :]).astype(BF16)
    n_mix = 4
    for i in range(depth):
        m, j = i % n_mix, i // n_mix
        modv = mod[i].reshape(SUBLANES, 6, D).transpose(1, 0, 2).reshape(6 * SUBLANES, 1, D)
        g1n, g2n = norm1_g[i][None], norm2_g[i][None]
        bias = None
        if m == 0:
            w1 = jnp.concatenate([mla_w_dq[j], mla_w_dkv[j], jnp.zeros((D, LANES - MLA_ROPE), F32)], axis=1).astype(BF16)
            wq = mla_w_uq[j].reshape(MLA_Q_RANK, MLA_HEADS, MLA_NOPE + MLA_ROPE)
            wq = jnp.concatenate([wq, jnp.zeros((MLA_Q_RANK, MLA_HEADS, LANES - MLA_ROPE), F32)], axis=-1)
            wq = wq.reshape(MLA_Q_RANK, MLA_HEADS * 2 * LANES).astype(BF16)
            q, k, v = _mla_proj(lay, X, g1n, modv, w1, mla_g_q[j][None], mla_g_kv[j][None], wq,
                                mla_w_ukv[j].astype(BF16), mla_cos, mla_sin)
            a = _mla_attention(lay, q, k, v)
            w_o = mla_w_o[j]
        elif m == 1:
            a = _fnet_mix(lay, X, g1n, modv, _dft_tables(S, C, D // FNET_GROUPS))
            w_o, bias = fnet_w_o[j], fnet_b_o[j][None]
        elif m == 2:
            qt, k, vt = _qkv_proj(lay, X, g1n, modv, na_w_qkv[j].astype(BF16), swa_cos, swa_sin,
                                  n_q=NA_HEADS * NA_HD, n_k=NA_HEADS * NA_HD, rope=False,
                                  q_scale=NA_HD ** -0.5 * math.log2(math.e))
            a = _na_attention(lay, qt, k, vt, _na_bias_tables(na_rpb[j], S // GRID_W))
            w_o = na_w_o[j]
        else:
            qt, k, vt = _qkv_proj(lay, X, g1n, modv, swa_w_qkv[j].astype(BF16), swa_cos, swa_sin,
                                  n_q=SW_HEADS * SW_HD, n_k=SW_KV_HEADS * SW_HD, rope=True,
                                  q_scale=SW_HD ** -0.5 * math.log2(math.e))
            a = _swa_attention(lay, qt, k, vt, swa_sinks[j])
            w_o = swa_w_o[j]
        wr = jnp.concatenate([moe_w_grp[i], moe_w_rt[i], jnp.zeros((D, LANES - MOE_GROUPS - MOE_EXPERTS), F32)], axis=1)
        br = jnp.concatenate([moe_b_grp[i], moe_b_rt[i], jnp.zeros((LANES - MOE_GROUPS - MOE_EXPERTS,), F32)])[None]
        wr_hi = wr.astype(BF16)
        wr2 = jnp.concatenate([wr_hi, (wr - wr_hi.astype(F32)).astype(BF16)], axis=1)
        X, h2, rec, counts = _out_proj(lay, a, X, w_o.astype(BF16), bias, g2n, modv, wr2, br, tri)
        X = _moe(lay, X, h2, rec, counts, modv, moe_w_gate, moe_w_up, moe_w_down, i,
                 final_g[None] if i == depth - 1 else None)
    return X.reshape(B, S, D)
```

```python
import functools
import math

import jax
import jax.numpy as jnp
import numpy as np
from jax import lax
from jax.experimental import pallas as pl
from jax.experimental.pallas import tpu as pltpu

F32 = jnp.float32
BF16 = jnp.bfloat16
I32 = jnp.int32
HIGHEST = lax.Precision.HIGHEST

GRID_W = 64
EPS = 1e-6
ROPE_THETA = 10000.0
NEG_INF = -1e30
MLA_HEADS, MLA_Q_RANK, MLA_KV_RANK, MLA_NOPE, MLA_ROPE, MLA_V = 8, 512, 256, 128, 64, 128
MLA_VT_ROWS = MLA_V + 16
MLA_UNROLL = 4
FNET_GROUPS = 4
NA_HEADS, NA_HD, NA_KR, NA_KC = 16, 64, 8, 16
SW_HEADS, SW_KV_HEADS, SW_HD, SW_WINDOW = 16, 4, 64, 128
MOE_GROUPS, MOE_PER_GROUP, MOE_FF = 4, 8, 512
MOE_EXPERTS = MOE_GROUPS * MOE_PER_GROUP

LANES = 128
SUBLANES = 8
TM = 512
MOE_BM = 512
VMEM_LIMIT = 56 * 1024 * 1024
ROUTE_LANE_EID, ROUTE_LANE_RANK, ROUTE_LANE_GATE = 0, 2, 4


def _cp(sem, vmem=VMEM_LIMIT):
    return pltpu.CompilerParams(dimension_semantics=sem, vmem_limit_bytes=vmem)


def _lane_iota(shape):
    return lax.broadcasted_iota(I32, shape, len(shape) - 1)


def _normmod(x, g, sc, sh):
    ms = jnp.mean(x * x, axis=-1, keepdims=True)
    return (x * lax.rsqrt(ms + EPS) * g) * (1.0 + sc) + sh


def _rms(x, g):
    ms = jnp.mean(x * x, axis=-1, keepdims=True)
    return x * lax.rsqrt(ms + EPS) * g


def _swap_halves(t, period):
    n = t.shape[-1]
    half = period // 2
    lane = _lane_iota(t.shape)
    return jnp.where((lane % period) < half, pltpu.roll(t, n - half, 1), pltpu.roll(t, half, 1))


def _dot(a, b):
    return jnp.dot(a, b, preferred_element_type=F32)


def _mod_kernel(a_ref, w_ref, b_ref, o_ref):
    a = a_ref[...]
    a = a * jax.nn.sigmoid(a)
    o_ref[0] = jnp.dot(a, w_ref[0], precision=HIGHEST, preferred_element_type=F32) + b_ref[0]


def _modulation(cond, mod_w, mod_b):
    depth, d, n = mod_w.shape
    tn = n // 4
    return pl.pallas_call(
        _mod_kernel,
        out_shape=jax.ShapeDtypeStruct((depth, SUBLANES, n), F32),
        grid=(depth, n // tn),
        in_specs=[pl.BlockSpec((SUBLANES, d), lambda l, j: (0, 0)),
                  pl.BlockSpec((1, d, tn), lambda l, j: (l, 0, j)),
                  pl.BlockSpec((1, 1, tn), lambda l, j: (l, 0, j))],
        out_specs=pl.BlockSpec((1, SUBLANES, tn), lambda l, j: (l, 0, j)),
        compiler_params=_cp(("arbitrary", "arbitrary")),
        name="modulation",
    )(cond, mod_w, mod_b.reshape(depth, 1, n))


class _Layout:
    def __init__(self, B, S, C, D):
        self.B, self.S, self.C, self.D = B, S, C, D
        self.NL, self.NC = B * S, B * C
        self.NT = self.NL + self.NC
        assert S % TM == 0 and self.NC % TM == 0 and TM % C == 0
        self.nl_tiles = self.NL // TM
        self.n_tiles = self.NT // TM
        self.tiles_per_batch = S // TM

    def mod_row(self, i):
        return jnp.where(i < self.nl_tiles, i // self.tiles_per_batch, self.B)

    def mod_spec(self, chunk):
        return pl.BlockSpec((1, 1, self.D), lambda i: (chunk * SUBLANES + self.mod_row(i), 0, 0))


def _mla_proj_kernel(x_ref, g_ref, sc_ref, sh_ref, w1_ref, gq_ref, gkv_ref, wq_ref, wkv_ref, cos_ref, sin_ref,
                     qt_ref, k_ref, vt_ref):
    h = _normmod(x_ref[...], g_ref[...], sc_ref[0], sh_ref[0]).astype(BF16)
    a = _dot(h, w1_ref[...])
    qa = _rms(a[:, :MLA_Q_RANK], gq_ref[...]).astype(BF16)
    ckv = _rms(a[:, MLA_Q_RANK:MLA_Q_RANK + MLA_KV_RANK], gkv_ref[...]).astype(BF16)
    cos, sin = cos_ref[...], sin_ref[...]

    def rope(t):
        return t * cos + _swap_halves(t, MLA_ROPE) * sin

    kr = rope(a[:, MLA_Q_RANK + MLA_KV_RANK:]).astype(BF16)
    scale = (MLA_NOPE + MLA_ROPE) ** -0.5 * math.log2(math.e)
    q = _dot(qa, wq_ref[...])
    kv = _dot(ckv, wkv_ref[...])
    ones = jnp.ones((MLA_VT_ROWS - MLA_V, x_ref.shape[0]), BF16)
    for hd in range(MLA_HEADS):
        c = hd * 2 * LANES
        qh = jnp.concatenate([q[:, c:c + LANES], rope(q[:, c + LANES:c + 2 * LANES])], axis=1) * scale
        qt_ref[hd] = qh.T.astype(BF16)
        k_ref[:, c:c + LANES] = kv[:, c:c + LANES].astype(BF16)
        k_ref[:, c + LANES:c + 2 * LANES] = kr
        vt_ref[hd, :MLA_V, :] = kv[:, c + LANES:c + 2 * LANES].T.astype(BF16)
        vt_ref[hd, MLA_V:, :] = ones


def _mla_proj(lay, x, g, modv, w1, gq, gkv, wq, wkv, cos, sin):
    D = lay.D
    full = lambda a: pl.BlockSpec(a.shape, lambda i: (0,) * a.ndim)
    row = lambda n: pl.BlockSpec((TM, n), lambda i: (i, 0))
    col = lambda r: pl.BlockSpec((MLA_HEADS, r, TM), lambda i: (0, 0, i))
    hq = MLA_HEADS * 2 * LANES
    return pl.pallas_call(
        _mla_proj_kernel,
        out_shape=(jax.ShapeDtypeStruct((MLA_HEADS, 2 * LANES, lay.NT), BF16),
                   jax.ShapeDtypeStruct((lay.NT, hq), BF16),
                   jax.ShapeDtypeStruct((MLA_HEADS, MLA_VT_ROWS, lay.NT), BF16)),
        grid=(lay.n_tiles,),
        in_specs=[row(D), full(g), lay.mod_spec(1), lay.mod_spec(0), full(w1), full(gq), full(gkv), full(wq),
                  full(wkv), row(LANES), row(LANES)],
        out_specs=(col(2 * LANES), row(hq), col(MLA_VT_ROWS)),
        compiler_params=_cp(("arbitrary",)),
        name="mla_proj",
    )(x, g, modv, modv, w1, gq, gkv, wq, wkv, cos, sin)


def _qkv_proj_kernel(x_ref, g_ref, sc_ref, sh_ref, w_ref, cos_ref, sin_ref, qt_ref, k_ref, vt_ref, *,
                     n_q, n_k, rope, q_scale, chunk):
    h = _normmod(x_ref[...], g_ref[...], sc_ref[0], sh_ref[0]).astype(BF16)
    n = w_ref.shape[1]
    for c0 in range(0, n, chunk):
        a = _dot(h, w_ref[:, c0:c0 + chunk])
        if rope and c0 < n_q + n_k:
            reps = chunk // LANES
            cos = jnp.concatenate([cos_ref[...]] * reps, axis=1)
            sin = jnp.concatenate([sin_ref[...]] * reps, axis=1)
            a = a * cos + _swap_halves(a, SW_HD) * sin
        if c0 < n_q:
            qt_ref[c0:c0 + chunk, :] = (a * q_scale).T.astype(BF16)
        elif c0 < n_q + n_k:
            k_ref[:, c0 - n_q:c0 - n_q + chunk] = a.astype(BF16)
        else:
            c = c0 - n_q - n_k
            vt_ref[c:c + chunk, :] = a.T.astype(BF16)


def _qkv_proj(lay, x, g, modv, w, cos, sin, *, n_q, n_k, rope, q_scale, chunk=256):
    D, n = lay.D, w.shape[1]
    n_v = n - n_q - n_k
    assert n_q % chunk == 0 and n_k % chunk == 0 and n_v % chunk == 0
    full = lambda a: pl.BlockSpec(a.shape, lambda i: (0,) * a.ndim)
    row = lambda m: pl.BlockSpec((TM, m), lambda i: (i, 0))
    col = lambda m: pl.BlockSpec((m, TM), lambda i: (0, i))
    return pl.pallas_call(
        functools.partial(_qkv_proj_kernel, n_q=n_q, n_k=n_k, rope=rope, q_scale=q_scale, chunk=chunk),
        out_shape=(jax.ShapeDtypeStruct((n_q, lay.NT), BF16), jax.ShapeDtypeStruct((lay.NT, n_k), BF16),
                   jax.ShapeDtypeStruct((n_v, lay.NT), BF16)),
        grid=(lay.n_tiles,),
        in_specs=[row(D), full(g), lay.mod_spec(1), lay.mod_spec(0), full(w), row(LANES), row(LANES)],
        out_specs=(col(n_q), row(n_k), col(n_v)),
        compiler_params=_cp(("arbitrary",)),
        name="qkv_proj",
    )(x, g, modv, modv, w, cos, sin)


def _mla_attn_kernel(*refs, tk, n_lat):
    if n_lat:
        qt_ref, kc_ref, vtc_ref, kl_ref, vtl_ref, o_ref, acc_ref, sa_ref, sb_ref = refs
    else:
        qt_ref, kc_ref, vtc_ref, o_ref, acc_ref = refs
    qt = qt_ref[0]

    st = _dot(kc_ref[...], qt)
    m = jnp.max(st, axis=0, keepdims=True)
    acc_ref[...] = _dot(vtc_ref[0], jnp.exp2(st - m).astype(BF16))

    if n_lat:
        nch = n_lat // tk

        def softmax_pv(st, vt, m):
            m_new = jnp.maximum(m, jnp.max(st, axis=0, keepdims=True))
            acc_ref[...] = jnp.exp2(m - m_new) * acc_ref[...] + _dot(vt, jnp.exp2(st - m_new).astype(BF16))
            return m_new

        sa_ref[...] = _dot(kl_ref[pl.ds(0, tk), :], qt)

        def body(jj, m):
            r0 = pl.multiple_of(2 * jj * tk, tk)
            r1 = pl.multiple_of((2 * jj + 1) * tk, tk)
            r2 = pl.multiple_of(jnp.minimum(2 * jj + 2, nch - 1) * tk, tk)
            sb_ref[...] = _dot(kl_ref[pl.ds(r1, tk), :], qt)
            m = softmax_pv(sa_ref[...], vtl_ref[0, :, pl.ds(r0, tk)], m)
            sa_ref[...] = _dot(kl_ref[pl.ds(r2, tk), :], qt)
            return softmax_pv(sb_ref[...], vtl_ref[0, :, pl.ds(r1, tk)], m)

        lax.fori_loop(0, nch // 2, body, m, unroll=math.gcd(nch // 2, MLA_UNROLL))
    o_ref[...] = (acc_ref[:MLA_V, :] / acc_ref[MLA_V:MLA_V + 1, :]).T.astype(o_ref.dtype)


def _mla_attention(lay, qt, k, vt, *, tq=512, tk=512):
    B, S, C = lay.B, lay.S, lay.C
    H = MLA_HEADS
    nq = S // tq
    cblk0 = lay.NL // C
    assert S % (2 * tk) == 0
    o_lat = pl.pallas_call(
        functools.partial(_mla_attn_kernel, tk=tk, n_lat=S),
        out_shape=jax.ShapeDtypeStruct((lay.NL, H * LANES), BF16),
        grid=(B, H, nq),
        in_specs=[pl.BlockSpec((1, 2 * LANES, tq), lambda b, h, i: (h, 0, b * nq + i)),
                  pl.BlockSpec((C, 2 * LANES), lambda b, h, i: (cblk0 + b, h)),
                  pl.BlockSpec((1, MLA_VT_ROWS, C), lambda b, h, i: (h, 0, cblk0 + b)),
                  pl.BlockSpec((S, 2 * LANES), lambda b, h, i: (b, h)),
                  pl.BlockSpec((1, MLA_VT_ROWS, S), lambda b, h, i: (h, 0, b))],
        out_specs=pl.BlockSpec((tq, LANES), lambda b, h, i: (b * nq + i, h)),
        scratch_shapes=[pltpu.VMEM((MLA_VT_ROWS, tq), F32), pltpu.VMEM((tk, tq), F32), pltpu.VMEM((tk, tq), F32)],
        compiler_params=_cp(("arbitrary", "arbitrary", "arbitrary")),
        name="mla_attn_latent",
    )(qt, k, vt, k, vt)
    o_ctx = pl.pallas_call(
        functools.partial(_mla_attn_kernel, tk=tk, n_lat=0),
        out_shape=jax.ShapeDtypeStruct((lay.NC, H * LANES), BF16),
        grid=(B, H),
        in_specs=[pl.BlockSpec((1, 2 * LANES, C), lambda b, h: (h, 0, cblk0 + b)),
                  pl.BlockSpec((C, 2 * LANES), lambda b, h: (cblk0 + b, h)),
                  pl.BlockSpec((1, MLA_VT_ROWS, C), lambda b, h: (h, 0, cblk0 + b))],
        out_specs=pl.BlockSpec((C, LANES), lambda b, h: (b, h)),
        scratch_shapes=[pltpu.VMEM((MLA_VT_ROWS, C), F32)],
        compiler_params=_cp(("arbitrary", "arbitrary")),
        name="mla_attn_ctx",
    )(qt, k, vt)
    return o_lat, o_ctx


def _dft_tables(S, C, gc):
    P = math.isqrt(S)
    assert P * P == S and (P & (P - 1)) == 0 and (gc & (gc - 1)) == 0 and (C & (C - 1)) == 0

    def cs(idx, n):
        ang = (idx % n).astype(F32) * (2.0 * math.pi / n)
        return jnp.cos(ang), jnp.sin(ang)

    k1 = jnp.arange(P, dtype=I32)
    idx = k1[None, :, None] * (P * k1[None, None, :] + k1[:, None, None])
    c, s = cs(idx, S)
    m1 = jnp.concatenate([c, -s], axis=1) * (1.0 / P)
    c, s = cs(k1[:, None] * k1[None, :], P)
    m2 = jnp.concatenate([jnp.concatenate([c, s], axis=1), jnp.concatenate([-s, c], axis=1)], axis=0)
    kc = jnp.arange(gc, dtype=I32)
    c, s = cs(kc[:, None] * kc[None, :], gc)
    mc = jnp.concatenate([c, s], axis=0) * (gc ** -0.5)
    kq = jnp.arange(C, dtype=I32)
    c, s = cs(kq[:, None] * kq[None, :], C)
    mctx = jnp.concatenate([c, s], axis=0) * (C ** -0.5)
    return m1.astype(BF16), m2.astype(BF16), mc.astype(BF16), mctx.astype(BF16)


def _fnet_stage1_kernel(x_ref, g_ref, sc_ref, sh_ref, m1_ref, z_ref, *, n2c, P):
    g, sc, sh = g_ref[...], sc_ref[0], sh_ref[0]
    for j in range(n2c):
        h = _normmod(x_ref[:, j, :], g, sc, sh).astype(BF16)
        z = _dot(m1_ref[j], h)
        z_ref[0, :, 0, j, :] = z[:P]
        z_ref[0, :, 1, j, :] = z[P:]


def _fnet_stage2_kernel(z_ref, m2_ref, mc_ref, f_ref, *, k1c, P, gc):
    D = f_ref.shape[-1]
    for j in range(k1c):
        z = z_ref[0, j].reshape(2 * P, D).astype(BF16)
        y = _dot(m2_ref[...], z)
        yr, yi = y[:P].astype(BF16), y[P:].astype(BF16)
        outs = []
        for gi in range(D // gc):
            sl = slice(gi * gc, (gi + 1) * gc)
            outs.append(_dot(yr[:, sl], mc_ref[:gc, :]) + _dot(yi[:, sl], mc_ref[gc:, :]))
        f_ref[:, j, :] = jnp.concatenate(outs, axis=1)


def _fnet_ctx_kernel(x_ref, g_ref, sc_ref, sh_ref, ml_ref, mc_ref, f_ref, *, C, gc):
    D = x_ref.shape[-1]
    h = _normmod(x_ref[...], g_ref[...], sc_ref[0], sh_ref[0]).astype(BF16)
    y = _dot(ml_ref[...], h)
    yc, ys = y[:C].astype(BF16), y[C:].astype(BF16)
    outs = []
    for gi in range(D // gc):
        sl = slice(gi * gc, (gi + 1) * gc)
        outs.append(_dot(yc[:, sl], mc_ref[:gc, :]) - _dot(ys[:, sl], mc_ref[gc:, :]))
    f_ref[...] = jnp.concatenate(outs, axis=1)


def _fnet_mix(lay, x, g, modv, tables):
    B, S, C, D = lay.B, lay.S, lay.C, lay.D
    m1, m2, mc, mctx = tables
    P = math.isqrt(S)
    gc = D // FNET_GROUPS
    n2c = SUBLANES
    k1c = SUBLANES
    full = lambda a: pl.BlockSpec(a.shape, lambda *i: (0,) * a.ndim)
    modspec = lambda chunk: pl.BlockSpec((1, 1, D), lambda b, j: (chunk * SUBLANES + b, 0, 0))
    assert C % P == 0 and P % n2c == 0 and P % k1c == 0
    x3 = x.reshape(lay.NT // P, P, D)
    z = pl.pallas_call(
        functools.partial(_fnet_stage1_kernel, n2c=n2c, P=P),
        out_shape=jax.ShapeDtypeStruct((B, P, 2, P, D), F32),
        grid=(B, P // n2c),
        in_specs=[pl.BlockSpec((P, n2c, D), lambda b, j: (b, j, 0)), full(g), modspec(1), modspec(0),
                  pl.BlockSpec((n2c, 2 * P, P), lambda b, j: (j, 0, 0))],
        out_specs=pl.BlockSpec((1, P, 2, n2c, D), lambda b, j: (b, 0, 0, j, 0)),
        compiler_params=_cp(("arbitrary", "arbitrary")),
        name="fnet_stage1",
    )(x3, g, modv, modv, m1)
    f_lat = pl.pallas_call(
        functools.partial(_fnet_stage2_kernel, k1c=k1c, P=P, gc=gc),
        out_shape=jax.ShapeDtypeStruct((lay.NL // P, P, D), F32),
        grid=(B, P // k1c),
        in_specs=[pl.BlockSpec((1, k1c, 2, P, D), lambda b, j: (b, j, 0, 0, 0)), full(m2), full(mc)],
        out_specs=pl.BlockSpec((P, k1c, D), lambda b, j: (b, j, 0)),
        compiler_params=_cp(("arbitrary", "arbitrary")),
        name="fnet_stage2",
    )(z, m2, mc)
    cblk0 = lay.NL // C
    ctx_mod = lambda chunk: pl.BlockSpec((1, 1, D), lambda b: (chunk * SUBLANES + B, 0, 0))
    f_ctx = pl.pallas_call(
        functools.partial(_fnet_ctx_kernel, C=C, gc=gc),
        out_shape=jax.ShapeDtypeStruct((lay.NC, D), F32),
        grid=(B,),
        in_specs=[pl.BlockSpec((C, D), lambda b: (cblk0 + b, 0)), full(g), ctx_mod(1), ctx_mod(0), full(mctx),
                  full(mc)],
        out_specs=pl.BlockSpec((C, D), lambda b: (b, 0)),
        compiler_params=_cp(("arbitrary",)),
        name="fnet_ctx",
    )(x, g, modv, modv, mctx, mc)
    return f_lat.reshape(lay.NL, D), f_ctx


NA_QROWS = 2
NA_KROWS = NA_KR + NA_QROWS
NA_VARIANTS = 5
NA_ONES = 16


def _na_bias_kernel(rpb_ref, sel_ref, toe_ref, o_ref):
    g = jnp.dot(rpb_ref[0], toe_ref[...], precision=HIGHEST, preferred_element_type=F32)
    for t in range(NA_VARIANTS):
        o_ref[t, 0] = jnp.dot(sel_ref[t], g, precision=HIGHEST, preferred_element_type=F32)


def _na_window_start(r, rows):
    return jnp.clip(r - NA_KR // 2, 0, rows - NA_KROWS)


def _na_bias_tables(rpb, rows):
    assert rows >= 16 and rows % NA_QROWS == 0
    W = GRID_W
    nh, nu, nv = rpb.shape
    up, vp, ajp = 2 * SUBLANES, LANES, 3 * SUBLANES
    assert nu <= up and nv <= vp and NA_QROWS * NA_KROWS <= ajp
    cq, ck = np.arange(W)[:, None], np.arange(W)[None, :]
    dc = np.clip(ck - cq + NA_KC - 1, 0, nv - 1).reshape(-1)
    toe = (np.arange(vp)[:, None] == dc[None, :]).astype(np.float32)
    cs = np.clip(cq - NA_KC // 2, 0, W - NA_KC)
    col_ok = (ck >= cs) & (ck < cs + NA_KC)
    sel = np.zeros((NA_VARIANTS, ajp, up), np.float32)
    valid = np.zeros((NA_VARIANTS, NA_QROWS, NA_KROWS, W, W), bool)
    for t, r in enumerate((0, 2, 6, rows - 4, rows - 2)):
        w0 = min(max(r - NA_KR // 2, 0), rows - NA_KROWS)
        for a in range(NA_QROWS):
            rs = min(max(r + a - NA_KR // 2, 0), rows - NA_KR)
            for j in range(NA_KROWS):
                rk = w0 + j
                sel[t, a * NA_KROWS + j, min(max(rk - (r + a) + NA_KR - 1, 0), nu - 1)] = 1.0
                if rs <= rk < rs + NA_KR:
                    valid[t, a, j] = col_ok
    rpb_p = jnp.pad(rpb, ((0, 0), (0, up - nu), (0, vp - nv)))
    tab = pl.pallas_call(
        _na_bias_kernel,
        out_shape=jax.ShapeDtypeStruct((NA_VARIANTS, nh, ajp, W * W), F32),
        grid=(nh,),
        in_specs=[pl.BlockSpec((1, up, vp), lambda h: (h, 0, 0)),
                  pl.BlockSpec(sel.shape, lambda h: (0, 0, 0)),
                  pl.BlockSpec(toe.shape, lambda h: (0, 0))],
        out_specs=pl.BlockSpec((NA_VARIANTS, 1, ajp, W * W), lambda h: (0, h, 0, 0)),
        compiler_params=_cp(("arbitrary",)),
        name="na_bias",
    )(rpb_p, jnp.asarray(sel), jnp.asarray(toe))
    tab = tab[:, :, :NA_QROWS * NA_KROWS].reshape(NA_VARIANTS, nh, NA_QROWS, NA_KROWS, W, W)
    tab = jnp.where(jnp.asarray(valid)[:, None], tab * math.log2(math.e), NEG_INF)
    tab = tab.reshape(NA_VARIANTS, nh // 2, 2, NA_QROWS, NA_KROWS, W, W)
    return tab.transpose(0, 1, 4, 6, 2, 3, 5).reshape(NA_VARIANTS, nh // 2, NA_KROWS * W, 2 * NA_QROWS * W)


def _na_kernel(*refs, rows, local):
    if local:
        qt_ref, kc_ref, vtc_ref, kl_ref, vtl_ref, bias_ref, o_ref, st_a, sc_a, st_b, sc_b = refs
    else:
        qt_ref, kc_ref, vtc_ref, o_ref, sc_a, sc_b = refs
        st_a = st_b = None
    nq = NA_QROWS * GRID_W
    nk = NA_KROWS * GRID_W
    kc, vtc = kc_ref[...], vtc_ref[...]
    row = lax.broadcasted_iota(I32, (LANES, nq), 0)
    pairs = qt_ref.shape[1] // nq
    qi = pl.program_id(2) if local else 0

    def ext(vt):
        return jnp.concatenate([vt, jnp.ones((NA_ONES, vt.shape[1]), BF16)], axis=0)

    def window(t):
        r = (qi * pairs + t) * NA_QROWS
        k0 = pl.multiple_of(_na_window_start(r, rows) * GRID_W, LANES)
        var = jnp.where(r == 0, 0, jnp.where(r == 2, 1, jnp.where(r == rows - 4, 3, jnp.where(r == rows - 2, 4, 2))))
        return k0, var

    def scores(t, st_ref, sc_ref):
        qt = qt_ref[:, t * nq:(t + 1) * nq]
        zero = jnp.zeros_like(qt)
        qbd = jnp.concatenate([jnp.where(row < NA_HD, qt, zero), jnp.where(row >= NA_HD, qt, zero)], axis=1)
        sc_ref[...] = _dot(kc, qbd)
        if local:
            k0, var = window(t)
            st_ref[...] = _dot(kl_ref[pl.ds(k0, nk), :], qbd) + bias_ref[var, 0]

    def finish(t, st_ref, sc_ref):
        sc = sc_ref[...]
        m = jnp.max(sc, axis=0, keepdims=True)
        if local:
            st = st_ref[...]
            m = jnp.maximum(m, jnp.max(st, axis=0, keepdims=True))
            pt = jnp.exp2(st - m).astype(BF16)
            vtw = vtl_ref[:, pl.ds(window(t)[0], nk)]
        pc = jnp.exp2(sc - m).astype(BF16)
        outs = []
        for hl in range(2):
            hs, qs = slice(hl * NA_HD, (hl + 1) * NA_HD), slice(hl * nq, (hl + 1) * nq)
            acc = _dot(ext(vtc[hs, :]), pc[:, qs])
            if local:
                acc = acc + _dot(ext(vtw[hs, :]), pt[:, qs])
            outs.append(acc[:NA_HD] / acc[NA_HD:NA_HD + 1])
        o_ref[t * nq:(t + 1) * nq, :] = jnp.concatenate(outs, axis=0).T.astype(o_ref.dtype)

    bufs = ((st_a, sc_a), (st_b, sc_b))
    scores(0, *bufs[0])
    for t in range(pairs):
        if t + 1 < pairs:
            scores(t + 1, *bufs[(t + 1) % 2])
        finish(t, *bufs[t % 2])


def _na_attention(lay, qt, k, vt, bias, *, tq=512):
    B, S, C = lay.B, lay.S, lay.C
    rows = S // GRID_W
    HP = NA_HEADS // 2
    nq = S // tq
    cblk0 = lay.NL // C
    nk, nqp = NA_KROWS * GRID_W, NA_QROWS * GRID_W
    o_lat = pl.pallas_call(
        functools.partial(_na_kernel, rows=rows, local=True),
        out_shape=jax.ShapeDtypeStruct((lay.NL, NA_HEADS * NA_HD), BF16),
        grid=(B, HP, nq),
        in_specs=[pl.BlockSpec((LANES, tq), lambda b, h, i: (h, b * nq + i)),
                  pl.BlockSpec((C, LANES), lambda b, h, i: (cblk0 + b, h)),
                  pl.BlockSpec((LANES, C), lambda b, h, i: (h, cblk0 + b)),
                  pl.BlockSpec((S, LANES), lambda b, h, i: (b, h)),
                  pl.BlockSpec((LANES, S), lambda b, h, i: (h, b)),
                  pl.BlockSpec((NA_VARIANTS, 1) + bias.shape[2:], lambda b, h, i: (0, h, 0, 0))],
        out_specs=pl.BlockSpec((tq, LANES), lambda b, h, i: (b * nq + i, h)),
        scratch_shapes=[pltpu.VMEM((nk, 2 * nqp), F32), pltpu.VMEM((C, 2 * nqp), F32)] * 2,
        compiler_params=_cp(("arbitrary", "arbitrary", "arbitrary")),
        name="na_attn_latent",
    )(qt, k, vt, k, vt, bias)
    o_ctx = pl.pallas_call(
        functools.partial(_na_kernel, rows=rows, local=False),
        out_shape=jax.ShapeDtypeStruct((lay.NC, NA_HEADS * NA_HD), BF16),
        grid=(B, HP),
        in_specs=[pl.BlockSpec((LANES, C), lambda b, h: (h, cblk0 + b)),
                  pl.BlockSpec((C, LANES), lambda b, h: (cblk0 + b, h)),
                  pl.BlockSpec((LANES, C), lambda b, h: (h, cblk0 + b))],
        out_specs=pl.BlockSpec((C, LANES), lambda b, h: (b, h)),
        scratch_shapes=[pltpu.VMEM((C, 2 * nqp), F32)] * 2,
        compiler_params=_cp(("arbitrary", "arbitrary")),
        name="na_attn_ctx",
    )(qt, k, vt)
    return o_lat, o_ctx


SW_SUB = 128
SW_BAND = SW_SUB + 2 * SW_WINDOW
SW_ONES = 16


def _swa_kernel(*refs, S, local):
    if local:
        (sink_ref, qt_ref, kc_ref, vtc_ref, kp_ref, kcur_ref, kn_ref, vtp_ref, vtcur_ref, vtn_ref, o_ref,
         kbuf, vtbuf, st_a, sc_a, st_b, sc_b) = refs
        tq = qt_ref.shape[1]
        W = SW_WINDOW
        kbuf[0:W] = kp_ref[...]
        kbuf[W:W + tq] = kcur_ref[...]
        kbuf[W + tq:] = kn_ref[...]
        vtbuf[:, 0:W] = vtp_ref[...]
        vtbuf[:, W:W + tq] = vtcur_ref[...]
        vtbuf[:, W + tq:] = vtn_ref[...]
        i = pl.program_id(1)
    else:
        sink_ref, qt_ref, kc_ref, vtc_ref, o_ref, sc_a, sc_b = refs
        st_a = st_b = None
        tq = qt_ref.shape[1]
    G = SW_HEADS // SW_KV_HEADS
    kc, vtc = kc_ref[...], vtc_ref[...]
    zeros_q = jnp.zeros((SW_HD, G * SW_SUB), BF16)

    def ext(vt):
        return jnp.concatenate([vt, jnp.ones((SW_ONES, vt.shape[1]), BF16)], axis=0)

    def band_bias(sb):
        key = lax.broadcasted_iota(I32, (SW_BAND, SW_SUB), 0)
        qry = lax.broadcasted_iota(I32, (SW_BAND, SW_SUB), 1)
        rel = key - SW_WINDOW - qry
        kpos = i * tq + sb * SW_SUB - SW_WINDOW + key
        ok = (jnp.abs(rel) <= SW_WINDOW) & (kpos >= 0) & (kpos < S)
        bias = jnp.where(ok, 0.0, NEG_INF).astype(F32)
        return jnp.concatenate([bias] * G, axis=1)

    def scores(sb, g, bias, st_ref, sc_ref):
        r0 = sb * SW_SUB
        tile, half = g // 2, g % 2
        sl = slice(tile * LANES, (tile + 1) * LANES)
        qg = jnp.concatenate([qt_ref[(G * g + hl) * SW_HD:(G * g + hl + 1) * SW_HD, r0:r0 + SW_SUB]
                              for hl in range(G)], axis=1)
        qpad = jnp.concatenate([qg, zeros_q] if half == 0 else [zeros_q, qg], axis=0)
        sc_ref[...] = _dot(kc[:, sl], qpad)
        if local:
            st_ref[...] = _dot(kbuf[r0:r0 + SW_BAND, sl], qpad) + bias

    def finish(sb, g, st_ref, sc_ref):
        r0 = sb * SW_SUB
        sink = sink_ref[g]
        sc = sc_ref[...]
        m = jnp.maximum(jnp.max(sc, axis=0, keepdims=True), sink)
        if local:
            st = st_ref[...]
            m = jnp.maximum(m, jnp.max(st, axis=0, keepdims=True))
        acc = _dot(ext(vtc[g * SW_HD:(g + 1) * SW_HD, :]), jnp.exp2(sc - m).astype(BF16))
        if local:
            acc = acc + _dot(ext(vtbuf[g * SW_HD:(g + 1) * SW_HD, r0:r0 + SW_BAND]), jnp.exp2(st - m).astype(BF16))
        og = acc[:SW_HD] / (acc[SW_HD:SW_HD + 1] + jnp.exp2(sink - m))
        return [og[:, hl * SW_SUB:(hl + 1) * SW_SUB] for hl in range(G)]

    items = [(sb, g) for sb in range(tq // SW_SUB) for g in range(SW_KV_HEADS)]
    bufs = ((st_a, sc_a), (st_b, sc_b))
    bias = band_bias(0) if local else None
    scores(*items[0], bias, *bufs[0])
    outs = []
    for n, (sb, g) in enumerate(items):
        if n + 1 < len(items):
            nsb, ng = items[n + 1]
            if local and ng == 0:
                bias = band_bias(nsb)
            scores(nsb, ng, bias, *bufs[(n + 1) % 2])
        outs.extend(finish(sb, g, *bufs[n % 2]))
        if g == SW_KV_HEADS - 1:
            o_ref[sb * SW_SUB:(sb + 1) * SW_SUB, :] = jnp.concatenate(outs, axis=0).T.astype(o_ref.dtype)
            outs = []


def _swa_attention(lay, qt, k, vt, sinks, *, tq=512):
    B, S, C = lay.B, lay.S, lay.C
    nq_rows = SW_HEADS * SW_HD
    nkv = SW_KV_HEADS * SW_HD
    G = SW_HEADS // SW_KV_HEADS
    nq = S // tq
    per = tq // SW_WINDOW
    last = lay.NT // SW_WINDOW - 1
    cblk0 = lay.NL // C
    prev = lambda b, i: jnp.maximum((b * nq + i) * per - 1, 0)
    nxt = lambda b, i: jnp.minimum((b * nq + i + 1) * per, last)
    sink_l = jnp.repeat(sinks.reshape(SW_KV_HEADS, 1, G) * math.log2(math.e), SW_SUB, axis=2)
    sink_spec = pl.BlockSpec(sink_l.shape, lambda *a: (0, 0, 0))
    o_lat = pl.pallas_call(
        functools.partial(_swa_kernel, S=S, local=True),
        out_shape=jax.ShapeDtypeStruct((lay.NL, nq_rows), BF16),
        grid=(B, nq),
        in_specs=[sink_spec,
                  pl.BlockSpec((nq_rows, tq), lambda b, i: (0, b * nq + i)),
                  pl.BlockSpec((C, nkv), lambda b, i: (cblk0 + b, 0)),
                  pl.BlockSpec((nkv, C), lambda b, i: (0, cblk0 + b)),
                  pl.BlockSpec((SW_WINDOW, nkv), lambda b, i: (prev(b, i), 0)),
                  pl.BlockSpec((tq, nkv), lambda b, i: (b * nq + i, 0)),
                  pl.BlockSpec((SW_WINDOW, nkv), lambda b, i: (nxt(b, i), 0)),
                  pl.BlockSpec((nkv, SW_WINDOW), lambda b, i: (0, prev(b, i))),
                  pl.BlockSpec((nkv, tq), lambda b, i: (0, b * nq + i)),
                  pl.BlockSpec((nkv, SW_WINDOW), lambda b, i: (0, nxt(b, i)))],
        out_specs=pl.BlockSpec((tq, nq_rows), lambda b, i: (b * nq + i, 0)),
        scratch_shapes=([pltpu.VMEM((tq + 2 * SW_WINDOW, nkv), BF16), pltpu.VMEM((nkv, tq + 2 * SW_WINDOW), BF16)]
                        + [pltpu.VMEM((SW_BAND, G * SW_SUB), F32), pltpu.VMEM((C, G * SW_SUB), F32)] * 2),
        compiler_params=_cp(("arbitrary", "arbitrary")),
        name="swa_attn_latent",
    )(sink_l, qt, k, vt, k, k, k, vt, vt, vt)
    o_ctx = pl.pallas_call(
        functools.partial(_swa_kernel, S=S, local=False),
        out_shape=jax.ShapeDtypeStruct((lay.NC, nq_rows), BF16),
        grid=(B,),
        in_specs=[sink_spec,
                  pl.BlockSpec((nq_rows, C), lambda b: (0, cblk0 + b)),
                  pl.BlockSpec((C, nkv), lambda b: (cblk0 + b, 0)),
                  pl.BlockSpec((nkv, C), lambda b: (0, cblk0 + b))],
        out_specs=pl.BlockSpec((C, nq_rows), lambda b: (b, 0)),
        scratch_shapes=[pltpu.VMEM((C, G * SW_SUB), F32)] * 2,
        compiler_params=_cp(("arbitrary",)),
        name="swa_attn_ctx",
    )(sink_l, qt, k, vt)
    return o_lat, o_ctx


def _route(logits, tri, carry):
    lane = _lane_iota(logits.shape)
    lanef = lane.astype(F32)
    big = float(LANES)
    rowmax = lambda t: jnp.max(t, axis=-1, keepdims=True)
    rowmin = lambda t: jnp.min(t, axis=-1, keepdims=True)
    rowsum = lambda t: jnp.sum(t, axis=-1, keepdims=True)
    is_g = lane < MOE_GROUPS
    mg = rowmax(jnp.where(is_g, logits, -jnp.inf))
    w_g = 1.0 / rowsum(jnp.where(is_g, jnp.exp(logits - mg), 0.0))
    gidx = rowmin(jnp.where(is_g & (logits == mg), lanef, big))
    g0 = MOE_GROUPS + MOE_PER_GROUP * gidx
    in_grp = (lanef >= g0) & (lanef < g0 + MOE_PER_GROUP)
    le = jnp.where(in_grp, logits, -jnp.inf)
    m1 = rowmax(le)
    i1 = rowmin(jnp.where(in_grp & (le == m1), lanef, big))
    le2 = jnp.where(lanef == i1, -jnp.inf, le)
    m2 = rowmax(le2)
    i2 = rowmin(jnp.where(in_grp & (lanef != i1) & (le2 == m2), lanef, big))
    r = jnp.exp(m2 - m1)
    gate1 = w_g / (1.0 + r)
    gate2 = w_g * r / (1.0 + r)
    sel1, sel2 = lanef == i1, lanef == i2
    member = (sel1 | sel2)
    cum = _dot(tri, member.astype(BF16)) + carry
    rank1 = rowsum(jnp.where(sel1, cum, 0.0))
    rank2 = rowsum(jnp.where(sel2, cum, 0.0))
    new_carry = carry + jnp.sum(member.astype(F32), axis=0, keepdims=True)
    rec = jnp.zeros_like(logits)
    for ln, val in ((0, i1 - MOE_GROUPS), (1, i2 - MOE_GROUPS), (2, rank1), (3, rank2), (4, gate1), (5, gate2)):
        rec = jnp.where(lane == ln, val, rec)
    return rec, new_carry


def _out_proj_kernel(*refs, has_bias, n_lat_tiles):
    if has_bias:
        (al_ref, ac_ref, x_ref, w_ref, b_ref, g1_ref, g2n_ref, sc_ref, sh_ref, wr_ref, br_ref, tri_ref,
         xo_ref, h2_ref, rec_ref, rect_ref, cnt_ref, carry_ref) = refs
    else:
        (al_ref, ac_ref, x_ref, w_ref, g1_ref, g2n_ref, sc_ref, sh_ref, wr_ref, br_ref, tri_ref,
         xo_ref, h2_ref, rec_ref, rect_ref, cnt_ref, carry_ref) = refs

    @pl.when(pl.program_id(0) == 0)
    def _():
        carry_ref[...] = jnp.zeros_like(carry_ref)

    a = jnp.where(pl.program_id(0) < n_lat_tiles, al_ref[...], ac_ref[...])
    y = _dot(a.astype(BF16), w_ref[...])
    if has_bias:
        y = y + b_ref[...]
    xn = x_ref[...] + g1_ref[0] * y
    xo_ref[...] = xn
    h2 = _normmod(xn, g2n_ref[...], sc_ref[0], sh_ref[0])
    h2_ref[...] = h2
    h_hi = h2.astype(BF16)
    h_lo = (h2 - h_hi.astype(F32)).astype(BF16)
    hw = _dot(h_hi, wr_ref[...])
    logits = hw[:, :LANES] + hw[:, LANES:] + _dot(h_lo, wr_ref[:, :LANES]) + br_ref[...]
    rec, carry = _route(logits, tri_ref[...], carry_ref[...])
    rec_ref[...] = rec
    rect_ref[...] = rec.T[:SUBLANES, :]
    carry_ref[...] = carry
    cnt_ref[...] = jnp.broadcast_to(carry, cnt_ref.shape)


def _out_proj(lay, a, x, w, b, g2n, modv, wr, br, tri):
    D = lay.D
    a_lat, a_ctx = a
    nl = lay.nl_tiles
    full = lambda t: pl.BlockSpec(t.shape, lambda i: (0,) * t.ndim)
    row = lambda n: pl.BlockSpec((TM, n), lambda i: (i, 0))
    ins = [a_lat, a_ctx, x, w] + ([b] if b is not None else []) + [modv, g2n, modv, modv, wr, br, tri]
    specs = ([pl.BlockSpec((TM, a_lat.shape[1]), lambda i: (jnp.minimum(i, nl - 1), 0)),
              pl.BlockSpec((TM, a_ctx.shape[1]), lambda i: (jnp.maximum(i - nl, 0), 0)), row(D), full(w)]
             + ([full(b)] if b is not None else [])
             + [lay.mod_spec(2), full(g2n), lay.mod_spec(4), lay.mod_spec(3), full(wr), full(br), full(tri)])
    return pl.pallas_call(
        functools.partial(_out_proj_kernel, has_bias=b is not None, n_lat_tiles=nl),
        out_shape=(jax.ShapeDtypeStruct((lay.NT, D), F32), jax.ShapeDtypeStruct((lay.NT, D), F32),
                   jax.ShapeDtypeStruct((lay.NT, LANES), F32), jax.ShapeDtypeStruct((SUBLANES, lay.NT), F32),
                   jax.ShapeDtypeStruct((SUBLANES, LANES), F32)),
        grid=(lay.n_tiles,),
        in_specs=specs,
        out_specs=(row(D), row(D), row(LANES), pl.BlockSpec((SUBLANES, TM), lambda i: (0, i)),
                   pl.BlockSpec((SUBLANES, LANES), lambda i: (0, 0))),
        scratch_shapes=[pltpu.VMEM((1, LANES), F32)],
        input_output_aliases={2: 0},
        compiler_params=_cp(("arbitrary",)),
        name="out_proj_router",
    )(*ins)


def _row_copy(src, s, dst, d, sem):
    return pltpu.make_async_copy(src.at[pl.ds(s, 1), :], dst.at[pl.ds(d, 1), :], sem)


def _dispatch_kernel(pos_ref, pend_ref, nu_ref, h_ref, xb_ref, zbuf, sem, zsem):
    base = pl.program_id(0) * TM
    n_tok = pos_ref.shape[0] // 2
    nblk = xb_ref.shape[0] // MOE_BM

    @pl.when(pl.program_id(0) == 0)
    def _():
        zbuf[...] = jnp.zeros_like(zbuf)

        def zero_block(row0):
            return pltpu.make_async_copy(zbuf, xb_ref.at[pl.ds(pl.multiple_of(row0, MOE_BM), MOE_BM), :], zsem)

        def each(fn):
            def expert(e, c):
                end = pend_ref[e]

                @pl.when(end > jnp.where(e > 0, pend_ref[jnp.maximum(e - 1, 0)], 0))
                def _():
                    fn(zero_block(end - MOE_BM))
                return c

            def tail(j, c):
                @pl.when(j >= nu_ref[0])
                def _():
                    fn(zero_block(j * MOE_BM))
                return c

            lax.fori_loop(0, MOE_EXPERTS, expert, 0)
            lax.fori_loop(0, nblk, tail, 0)

        each(lambda cp: cp.start())
        each(lambda cp: cp.wait())

    def issue(r, c):
        for k in range(2):
            _row_copy(h_ref, r, xb_ref, pos_ref[k * n_tok + base + r], sem).start()
        return c

    lax.fori_loop(0, TM, issue, 0, unroll=8)
    for k in range(2):
        pltpu.make_async_copy(h_ref, xb_ref.at[pl.ds(0, TM), :], sem).wait()


def _dispatch(lay, pos, pend, n_used, h2, cap):
    D = lay.D
    return pl.pallas_call(
        _dispatch_kernel,
        out_shape=jax.ShapeDtypeStruct((cap, D), F32),
        grid_spec=pltpu.PrefetchScalarGridSpec(
            num_scalar_prefetch=3, grid=(lay.n_tiles,),
            in_specs=[pl.BlockSpec((TM, D), lambda i, p, e, n: (i, 0))],
            out_specs=pl.BlockSpec(memory_space=pl.ANY),
            scratch_shapes=[pltpu.VMEM((MOE_BM, D), F32), pltpu.SemaphoreType.DMA, pltpu.SemaphoreType.DMA]),
        compiler_params=_cp(("arbitrary",)),
        name="moe_dispatch",
    )(pos, pend, n_used, h2)


def _expert_kernel(be_ref, nu_ref, xb_ref, wg_ref, wu_ref, wd_ref, yb_ref, wgb, wub, wdb):
    j = pl.program_id(0)
    prev = be_ref[jnp.maximum(j - 1, 0)]

    @pl.when((j == 0) | (be_ref[j] != prev))
    def _():
        wgb[...] = wg_ref[0, 0].astype(BF16)
        wub[...] = wu_ref[0, 0].astype(BF16)
        wdb[...] = wd_ref[0, 0].astype(BF16)

    @pl.when(j < nu_ref[0])
    def _():
        xe = xb_ref[...].astype(BF16)
        g = _dot(xe, wgb[...])
        u = _dot(xe, wub[...])
        act = (g * jax.nn.sigmoid(g) * u).astype(BF16)
        yb_ref[...] = _dot(act, wdb[...])

    @pl.when(j >= nu_ref[0])
    def _():
        yb_ref[...] = jnp.zeros_like(yb_ref)


def _experts(xb, blk_e, n_used, w_gate, w_up, w_down, layer):
    cap, D = xb.shape
    FF = w_gate.shape[-1]
    nblk = cap // MOE_BM
    blk = lambda j, be, nu: (jnp.maximum(jnp.minimum(j, nu[0] - 1), 0), 0)
    wblk = lambda j, be, nu: (layer, be[j], 0, 0)
    return pl.pallas_call(
        _expert_kernel,
        out_shape=jax.ShapeDtypeStruct((cap, D), F32),
        grid_spec=pltpu.PrefetchScalarGridSpec(
            num_scalar_prefetch=2, grid=(nblk,),
            in_specs=[pl.BlockSpec((MOE_BM, D), blk),
                      pl.BlockSpec((1, 1, D, FF), wblk), pl.BlockSpec((1, 1, D, FF), wblk),
                      pl.BlockSpec((1, 1, FF, D), wblk)],
            out_specs=pl.BlockSpec((MOE_BM, D), lambda j, be, nu: (j, 0)),
            scratch_shapes=[pltpu.VMEM((D, FF), BF16), pltpu.VMEM((D, FF), BF16), pltpu.VMEM((FF, D), BF16)]),
        compiler_params=_cp(("arbitrary",)),
        name="moe_experts",
    )(blk_e, n_used, xb, w_gate, w_up, w_down)


def _combine_kernel(*refs, final):
    if final:
        pos_ref, x_ref, rec_ref, g2_ref, yb_ref, fg_ref, o_ref, buf, sem = refs
    else:
        pos_ref, x_ref, rec_ref, g2_ref, yb_ref, o_ref, buf, sem = refs
    base = pl.program_id(0) * TM
    n_tok = pos_ref.shape[0] // 2

    def issue(r, c):
        for k in range(2):
            _row_copy(yb_ref, pos_ref[k * n_tok + base + r], buf.at[k], r, sem).start()
        return c

    lax.fori_loop(0, TM, issue, 0, unroll=8)
    for k in range(2):
        pltpu.make_async_copy(yb_ref.at[pl.ds(0, TM), :], buf.at[k], sem).wait()
    rec = rec_ref[...]
    f = rec[:, ROUTE_LANE_GATE:ROUTE_LANE_GATE + 1] * buf[0] + rec[:, ROUTE_LANE_GATE + 1:ROUTE_LANE_GATE + 2] * buf[1]
    xn = x_ref[...] + g2_ref[0] * f
    if final:
        xn = _rms(xn, fg_ref[...])
    o_ref[...] = xn


def _combine(lay, pos, x, rec, modv, yb, final_g):
    D = lay.D
    final = final_g is not None
    row = lambda n: pl.BlockSpec((TM, n), lambda i, p: (i, 0))
    specs = [row(D), row(LANES),
             pl.BlockSpec((1, 1, D), lambda i, p: (5 * SUBLANES + lay.mod_row(i), 0, 0)),
             pl.BlockSpec(memory_space=pl.ANY)]
    ins = [x, rec, modv, yb]
    if final:
        specs.append(pl.BlockSpec(final_g.shape, lambda i, p: (0, 0)))
        ins.append(final_g)
    n_rows, n_tiles = (lay.NL, lay.nl_tiles) if final else (lay.NT, lay.n_tiles)
    return pl.pallas_call(
        functools.partial(_combine_kernel, final=final),
        out_shape=jax.ShapeDtypeStruct((n_rows, D), F32),
        grid_spec=pltpu.PrefetchScalarGridSpec(
            num_scalar_prefetch=1, grid=(n_tiles,),
            in_specs=specs,
            out_specs=row(D),
            scratch_shapes=[pltpu.VMEM((2, TM, D), F32), pltpu.SemaphoreType.DMA]),
        input_output_aliases={} if final else {1: 0},
        compiler_params=_cp(("arbitrary",)),
        name="moe_combine",
    )(pos, *ins)


def _moe(lay, x, h2, rec, rect, counts, modv, w_gate, w_up, w_down, layer, final_g):
    T = 2 * lay.NT
    cap = -(-T // MOE_BM) * MOE_BM + MOE_EXPERTS * MOE_BM
    cnt = counts[0, MOE_GROUPS:MOE_GROUPS + MOE_EXPERTS].astype(I32)
    pcnt = (cnt + MOE_BM - 1) // MOE_BM * MOE_BM
    pend = jnp.cumsum(pcnt)
    start = pend - pcnt
    eid = rect[ROUTE_LANE_EID:ROUTE_LANE_EID + 2].astype(I32)
    rank = rect[ROUTE_LANE_RANK:ROUTE_LANE_RANK + 2].astype(I32)
    pos = (start[eid] + rank).reshape(-1)
    blk_row = jnp.arange(cap // MOE_BM, dtype=I32) * MOE_BM
    blk_e = jnp.minimum(jnp.sum((pend[None, :] <= blk_row[:, None]).astype(I32), axis=1), MOE_EXPERTS - 1)
    n_used = (pend[-1:] // MOE_BM).astype(I32)
    xb = _dispatch(lay, pos, pend.astype(I32), n_used, h2, cap)
    yb = _experts(xb, blk_e, n_used, w_gate, w_up, w_down, layer)
    return _combine(lay, pos, x, rec, modv, yb, final_g)


def _rope_tables(lay):
    S = lay.S
    t = jnp.arange(S)
    n = MLA_ROPE // 4
    inv = ROPE_THETA ** (-jnp.arange(n, dtype=F32) / n)
    ang = jnp.concatenate([(t // GRID_W).astype(F32)[:, None] * inv, (t % GRID_W).astype(F32)[:, None] * inv], axis=-1)
    cos, sin = jnp.cos(ang), jnp.sin(ang)
    cos64 = jnp.concatenate([cos, cos], axis=-1)
    sin64 = jnp.concatenate([-sin, sin], axis=-1)
    rows = lambda lat, ctx_val: jnp.concatenate([jnp.tile(lat, (lay.B, 1)), jnp.full((lay.NC, 64), ctx_val, F32)], axis=0)
    cos64, sin64 = rows(cos64, 1.0), rows(sin64, 0.0)
    zero = jnp.zeros_like(cos64)
    return ((jnp.concatenate([cos64, zero], axis=1), jnp.concatenate([sin64, zero], axis=1)),
            (jnp.concatenate([cos64, cos64], axis=1), jnp.concatenate([sin64, sin64], axis=1)))


def kernel(x, c, ctx, c_ctx, mod_w, mod_b, norm1_g, norm2_g, mla_w_dq, mla_g_q, mla_w_uq, mla_w_dkv, mla_g_kv, mla_w_ukv, mla_w_o, fnet_w_o, fnet_b_o, na_w_qkv, na_rpb, na_w_o, swa_w_qkv, swa_sinks, swa_w_o, moe_w_grp, moe_b_grp, moe_w_rt, moe_b_rt, moe_w_gate, moe_w_up, moe_w_down, final_g):
    B, S, D = x.shape
    C = ctx.shape[1]
    depth = mod_w.shape[0]
    lay = _Layout(B, S, C, D)
    X = jnp.concatenate([x.reshape(B * S, D), ctx.reshape(B * C, D)], axis=0)
    cond = jnp.concatenate([c, c_ctx[None], jnp.zeros((SUBLANES - B - 1, D), F32)], axis=0)
    mod = _modulation(cond, mod_w, mod_b)
    (mla_cos, mla_sin), (swa_cos, swa_sin) = _rope_tables(lay)
    tri = (jnp.arange(TM)[:, None] > jnp.arange(TM)[None, :]).astype(BF16)
    n_mix = 4
    for i in range(depth):
        m, j = i % n_mix, i // n_mix
        modv = mod[i].reshape(SUBLANES, 6, D).transpose(1, 0, 2).reshape(6 * SUBLANES, 1, D)
        g1n, g2n = norm1_g[i][None], norm2_g[i][None]
        bias = None
        if m == 0:
            w1 = jnp.concatenate([mla_w_dq[j], mla_w_dkv[j], jnp.zeros((D, LANES - MLA_ROPE), F32)], axis=1).astype(BF16)
            wq = mla_w_uq[j].reshape(MLA_Q_RANK, MLA_HEADS, MLA_NOPE + MLA_ROPE)
            wq = jnp.concatenate([wq, jnp.zeros((MLA_Q_RANK, MLA_HEADS, LANES - MLA_ROPE), F32)], axis=-1)
            wq = wq.reshape(MLA_Q_RANK, MLA_HEADS * 2 * LANES).astype(BF16)
            q, k, v = _mla_proj(lay, X, g1n, modv, w1, mla_g_q[j][None], mla_g_kv[j][None], wq,
                                mla_w_ukv[j].astype(BF16), mla_cos, mla_sin)
            a = _mla_attention(lay, q, k, v)
            w_o = mla_w_o[j]
        elif m == 1:
            a = _fnet_mix(lay, X, g1n, modv, _dft_tables(S, C, D // FNET_GROUPS))
            w_o, bias = fnet_w_o[j], fnet_b_o[j][None]
        elif m == 2:
            qt, k, vt = _qkv_proj(lay, X, g1n, modv, na_w_qkv[j].astype(BF16), swa_cos, swa_sin,
                                  n_q=NA_HEADS * NA_HD, n_k=NA_HEADS * NA_HD, rope=False,
                                  q_scale=NA_HD ** -0.5 * math.log2(math.e))
            a = _na_attention(lay, qt, k, vt, _na_bias_tables(na_rpb[j], S // GRID_W))
            w_o = na_w_o[j]
        else:
            qt, k, vt = _qkv_proj(lay, X, g1n, modv, swa_w_qkv[j].astype(BF16), swa_cos, swa_sin,
                                  n_q=SW_HEADS * SW_HD, n_k=SW_KV_HEADS * SW_HD, rope=True,
                                  q_scale=SW_HD ** -0.5 * math.log2(math.e))
            a = _swa_attention(lay, qt, k, vt, swa_sinks[j])
            w_o = swa_w_o[j]
        wr = jnp.concatenate([moe_w_grp[i], moe_w_rt[i], jnp.zeros((D, LANES - MOE_GROUPS - MOE_EXPERTS), F32)], axis=1)
        br = jnp.concatenate([moe_b_grp[i], moe_b_rt[i], jnp.zeros((LANES - MOE_GROUPS - MOE_EXPERTS,), F32)])[None]
        wr_hi = wr.astype(BF16)
        wr2 = jnp.concatenate([wr_hi, (wr - wr_hi.astype(F32)).astype(BF16)], axis=1)
        X, h2, rec, rect, counts = _out_proj(lay, a, X, w_o.astype(BF16), bias, g2n, modv, wr2, br, tri)
        X = _moe(lay, X, h2, rec, rect, counts, modv, moe_w_gate, moe_w_up, moe_w_down, i,
                 final_g[None] if i == depth - 1 else None)
    return X.reshape(B, S, D)
```

```python
import functools
import math

import jax
import jax.numpy as jnp
import numpy as np
from jax import lax
from jax.experimental import pallas as pl
from jax.experimental.pallas import tpu as pltpu

F32 = jnp.float32
BF16 = jnp.bfloat16
I32 = jnp.int32
HIGHEST = lax.Precision.HIGHEST

GRID_W = 64
EPS = 1e-6
ROPE_THETA = 10000.0
NEG_INF = -1e30
MLA_HEADS, MLA_Q_RANK, MLA_KV_RANK, MLA_NOPE, MLA_ROPE, MLA_V = 8, 512, 256, 128, 64, 128
MLA_VT_ROWS = MLA_V + 16
MLA_UNROLL = 8
FNET_GROUPS = 4
NA_HEADS, NA_HD, NA_KR, NA_KC = 16, 64, 8, 16
SW_HEADS, SW_KV_HEADS, SW_HD, SW_WINDOW = 16, 4, 64, 128
MOE_GROUPS, MOE_PER_GROUP, MOE_FF = 4, 8, 512
MOE_EXPERTS = MOE_GROUPS * MOE_PER_GROUP

LANES = 128
SUBLANES = 8
TM = 512
MOE_BM = 512
VMEM_LIMIT = 56 * 1024 * 1024
ROUTE_LANE_EID, ROUTE_LANE_RANK, ROUTE_LANE_GATE = 0, 2, 4


def _cp(sem, vmem=VMEM_LIMIT):
    return pltpu.CompilerParams(dimension_semantics=sem, vmem_limit_bytes=vmem)


def _lane_iota(shape):
    return lax.broadcasted_iota(I32, shape, len(shape) - 1)


def _normmod(x, g, sc, sh):
    ms = jnp.mean(x * x, axis=-1, keepdims=True)
    return (x * lax.rsqrt(ms + EPS) * g) * (1.0 + sc) + sh


def _rms(x, g):
    ms = jnp.mean(x * x, axis=-1, keepdims=True)
    return x * lax.rsqrt(ms + EPS) * g


def _swap_halves(t, period):
    n = t.shape[-1]
    half = period // 2
    lane = _lane_iota(t.shape)
    return jnp.where((lane % period) < half, pltpu.roll(t, n - half, 1), pltpu.roll(t, half, 1))


def _dot(a, b):
    return jnp.dot(a, b, preferred_element_type=F32)


def _mod_kernel(a_ref, w_ref, b_ref, o_ref):
    a = a_ref[...]
    a = a * jax.nn.sigmoid(a)
    o_ref[0] = jnp.dot(a, w_ref[0], precision=HIGHEST, preferred_element_type=F32) + b_ref[0]


def _modulation(cond, mod_w, mod_b):
    depth, d, n = mod_w.shape
    tn = n // 4
    return pl.pallas_call(
        _mod_kernel,
        out_shape=jax.ShapeDtypeStruct((depth, SUBLANES, n), F32),
        grid=(depth, n // tn),
        in_specs=[pl.BlockSpec((SUBLANES, d), lambda l, j: (0, 0)),
                  pl.BlockSpec((1, d, tn), lambda l, j: (l, 0, j)),
                  pl.BlockSpec((1, 1, tn), lambda l, j: (l, 0, j))],
        out_specs=pl.BlockSpec((1, SUBLANES, tn), lambda l, j: (l, 0, j)),
        compiler_params=_cp(("arbitrary", "arbitrary")),
        name="modulation",
    )(cond, mod_w, mod_b.reshape(depth, 1, n))


class _Layout:
    def __init__(self, B, S, C, D):
        self.B, self.S, self.C, self.D = B, S, C, D
        self.NL, self.NC = B * S, B * C
        self.NT = self.NL + self.NC
        assert S % TM == 0 and self.NC % TM == 0 and TM % C == 0
        self.nl_tiles = self.NL // TM
        self.n_tiles = self.NT // TM
        self.tiles_per_batch = S // TM

    def mod_row(self, i):
        return jnp.where(i < self.nl_tiles, i // self.tiles_per_batch, self.B)

    def mod_spec(self, chunk):
        return pl.BlockSpec((1, 1, self.D), lambda i: (chunk * SUBLANES + self.mod_row(i), 0, 0))


def _mla_proj_kernel(x_ref, g_ref, sc_ref, sh_ref, w1_ref, gq_ref, gkv_ref, wq_ref, wkv_ref, cos_ref, sin_ref,
                     qt_ref, k_ref, vt_ref):
    h = _normmod(x_ref[...], g_ref[...], sc_ref[0], sh_ref[0]).astype(BF16)
    a = _dot(h, w1_ref[...])
    qa = _rms(a[:, :MLA_Q_RANK], gq_ref[...]).astype(BF16)
    ckv = _rms(a[:, MLA_Q_RANK:MLA_Q_RANK + MLA_KV_RANK], gkv_ref[...]).astype(BF16)
    cos, sin = cos_ref[...], sin_ref[...]

    def rope(t):
        return t * cos + _swap_halves(t, MLA_ROPE) * sin

    kr = rope(a[:, MLA_Q_RANK + MLA_KV_RANK:]).astype(BF16)
    scale = (MLA_NOPE + MLA_ROPE) ** -0.5 * math.log2(math.e)
    q = _dot(qa, wq_ref[...])
    kv = _dot(ckv, wkv_ref[...])
    ones = jnp.ones((MLA_VT_ROWS - MLA_V, x_ref.shape[0]), BF16)
    for hd in range(MLA_HEADS):
        c = hd * 2 * LANES
        qh = jnp.concatenate([q[:, c:c + LANES], rope(q[:, c + LANES:c + 2 * LANES])], axis=1) * scale
        qt_ref[hd] = qh.T.astype(BF16)
        k_ref[:, c:c + LANES] = kv[:, c:c + LANES].astype(BF16)
        k_ref[:, c + LANES:c + 2 * LANES] = kr
        vt_ref[hd, :MLA_V, :] = kv[:, c + LANES:c + 2 * LANES].T.astype(BF16)
        vt_ref[hd, MLA_V:, :] = ones


def _mla_proj(lay, x, g, modv, w1, gq, gkv, wq, wkv, cos, sin):
    D = lay.D
    full = lambda a: pl.BlockSpec(a.shape, lambda i: (0,) * a.ndim)
    row = lambda n: pl.BlockSpec((TM, n), lambda i: (i, 0))
    col = lambda r: pl.BlockSpec((MLA_HEADS, r, TM), lambda i: (0, 0, i))
    hq = MLA_HEADS * 2 * LANES
    return pl.pallas_call(
        _mla_proj_kernel,
        out_shape=(jax.ShapeDtypeStruct((MLA_HEADS, 2 * LANES, lay.NT), BF16),
                   jax.ShapeDtypeStruct((lay.NT, hq), BF16),
                   jax.ShapeDtypeStruct((MLA_HEADS, MLA_VT_ROWS, lay.NT), BF16)),
        grid=(lay.n_tiles,),
        in_specs=[row(D), full(g), lay.mod_spec(1), lay.mod_spec(0), full(w1), full(gq), full(gkv), full(wq),
                  full(wkv), row(LANES), row(LANES)],
        out_specs=(col(2 * LANES), row(hq), col(MLA_VT_ROWS)),
        compiler_params=_cp(("arbitrary",)),
        name="mla_proj",
    )(x, g, modv, modv, w1, gq, gkv, wq, wkv, cos, sin)


def _qkv_proj_kernel(x_ref, g_ref, sc_ref, sh_ref, w_ref, cos_ref, sin_ref, qt_ref, k_ref, vt_ref, *,
                     n_q, n_k, rope, q_scale, chunk):
    h = _normmod(x_ref[...], g_ref[...], sc_ref[0], sh_ref[0]).astype(BF16)
    n = w_ref.shape[1]
    for c0 in range(0, n, chunk):
        a = _dot(h, w_ref[:, c0:c0 + chunk])
        if rope and c0 < n_q + n_k:
            reps = chunk // LANES
            cos = jnp.concatenate([cos_ref[...]] * reps, axis=1)
            sin = jnp.concatenate([sin_ref[...]] * reps, axis=1)
            a = a * cos + _swap_halves(a, SW_HD) * sin
        if c0 < n_q:
            qt_ref[c0:c0 + chunk, :] = (a * q_scale).T.astype(BF16)
        elif c0 < n_q + n_k:
            k_ref[:, c0 - n_q:c0 - n_q + chunk] = a.astype(BF16)
        else:
            c = c0 - n_q - n_k
            vt_ref[c:c + chunk, :] = a.T.astype(BF16)


def _qkv_proj(lay, x, g, modv, w, cos, sin, *, n_q, n_k, rope, q_scale, chunk=256):
    D, n = lay.D, w.shape[1]
    n_v = n - n_q - n_k
    assert n_q % chunk == 0 and n_k % chunk == 0 and n_v % chunk == 0
    full = lambda a: pl.BlockSpec(a.shape, lambda i: (0,) * a.ndim)
    row = lambda m: pl.BlockSpec((TM, m), lambda i: (i, 0))
    col = lambda m: pl.BlockSpec((m, TM), lambda i: (0, i))
    return pl.pallas_call(
        functools.partial(_qkv_proj_kernel, n_q=n_q, n_k=n_k, rope=rope, q_scale=q_scale, chunk=chunk),
        out_shape=(jax.ShapeDtypeStruct((n_q, lay.NT), BF16), jax.ShapeDtypeStruct((lay.NT, n_k), BF16),
                   jax.ShapeDtypeStruct((n_v, lay.NT), BF16)),
        grid=(lay.n_tiles,),
        in_specs=[row(D), full(g), lay.mod_spec(1), lay.mod_spec(0), full(w), row(LANES), row(LANES)],
        out_specs=(col(n_q), row(n_k), col(n_v)),
        compiler_params=_cp(("arbitrary",)),
        name="qkv_proj",
    )(x, g, modv, modv, w, cos, sin)


def _mla_attn_kernel(*refs, tk, n_lat):
    if n_lat:
        qt_ref, kc_ref, vtc_ref, kl_ref, vtl_ref, o_ref, acc_ref, sa_ref, sb_ref = refs
    else:
        qt_ref, kc_ref, vtc_ref, o_ref, acc_ref = refs
    qt = qt_ref[0]

    st = _dot(kc_ref[...], qt)
    m = jnp.max(st, axis=0, keepdims=True)
    acc_ref[...] = _dot(vtc_ref[0], jnp.exp2(st - m).astype(BF16))

    if n_lat:
        nch = n_lat // tk

        def softmax_pv(st, vt, m):
            m_new = jnp.maximum(m, jnp.max(st, axis=0, keepdims=True))
            acc_ref[...] = jnp.exp2(m - m_new) * acc_ref[...] + _dot(vt, jnp.exp2(st - m_new).astype(BF16))
            return m_new

        sa_ref[...] = _dot(kl_ref[pl.ds(0, tk), :], qt)

        def body(jj, m):
            r0 = pl.multiple_of(2 * jj * tk, tk)
            r1 = pl.multiple_of((2 * jj + 1) * tk, tk)
            r2 = pl.multiple_of(jnp.minimum(2 * jj + 2, nch - 1) * tk, tk)
            sb_ref[...] = _dot(kl_ref[pl.ds(r1, tk), :], qt)
            m = softmax_pv(sa_ref[...], vtl_ref[0, :, pl.ds(r0, tk)], m)
            sa_ref[...] = _dot(kl_ref[pl.ds(r2, tk), :], qt)
            return softmax_pv(sb_ref[...], vtl_ref[0, :, pl.ds(r1, tk)], m)

        lax.fori_loop(0, nch // 2, body, m, unroll=math.gcd(nch // 2, MLA_UNROLL))
    o_ref[...] = (acc_ref[:MLA_V, :] / acc_ref[MLA_V:MLA_V + 1, :]).T.astype(o_ref.dtype)


def _mla_attention(lay, qt, k, vt, *, tq=512, tk=512):
    B, S, C = lay.B, lay.S, lay.C
    H = MLA_HEADS
    nq = S // tq
    cblk0 = lay.NL // C
    assert S % (2 * tk) == 0
    o_lat = pl.pallas_call(
        functools.partial(_mla_attn_kernel, tk=tk, n_lat=S),
        out_shape=jax.ShapeDtypeStruct((lay.NL, H * LANES), BF16),
        grid=(B, H, nq),
        in_specs=[pl.BlockSpec((1, 2 * LANES, tq), lambda b, h, i: (h, 0, b * nq + i)),
                  pl.BlockSpec((C, 2 * LANES), lambda b, h, i: (cblk0 + b, h)),
                  pl.BlockSpec((1, MLA_VT_ROWS, C), lambda b, h, i: (h, 0, cblk0 + b)),
                  pl.BlockSpec((S, 2 * LANES), lambda b, h, i: (b, h)),
                  pl.BlockSpec((1, MLA_VT_ROWS, S), lambda b, h, i: (h, 0, b))],
        out_specs=pl.BlockSpec((tq, LANES), lambda b, h, i: (b * nq + i, h)),
        scratch_shapes=[pltpu.VMEM((MLA_VT_ROWS, tq), F32), pltpu.VMEM((tk, tq), F32), pltpu.VMEM((tk, tq), F32)],
        compiler_params=_cp(("arbitrary", "arbitrary", "arbitrary")),
        name="mla_attn_latent",
    )(qt, k, vt, k, vt)
    o_ctx = pl.pallas_call(
        functools.partial(_mla_attn_kernel, tk=tk, n_lat=0),
        out_shape=jax.ShapeDtypeStruct((lay.NC, H * LANES), BF16),
        grid=(B, H),
        in_specs=[pl.BlockSpec((1, 2 * LANES, C), lambda b, h: (h, 0, cblk0 + b)),
                  pl.BlockSpec((C, 2 * LANES), lambda b, h: (cblk0 + b, h)),
                  pl.BlockSpec((1, MLA_VT_ROWS, C), lambda b, h: (h, 0, cblk0 + b))],
        out_specs=pl.BlockSpec((C, LANES), lambda b, h: (b, h)),
        scratch_shapes=[pltpu.VMEM((MLA_VT_ROWS, C), F32)],
        compiler_params=_cp(("arbitrary", "arbitrary")),
        name="mla_attn_ctx",
    )(qt, k, vt)
    return o_lat, o_ctx


def _dft_tables(S, C, gc):
    P = math.isqrt(S)
    assert P * P == S and (P & (P - 1)) == 0 and (gc & (gc - 1)) == 0 and (C & (C - 1)) == 0

    def cs(idx, n):
        ang = (idx % n).astype(F32) * (2.0 * math.pi / n)
        return jnp.cos(ang), jnp.sin(ang)

    k1 = jnp.arange(P, dtype=I32)
    idx = k1[None, :, None] * (P * k1[None, None, :] + k1[:, None, None])
    c, s = cs(idx, S)
    m1 = jnp.concatenate([c, -s], axis=1) * (1.0 / P)
    c, s = cs(k1[:, None] * k1[None, :], P)
    m2 = jnp.concatenate([jnp.concatenate([c, s], axis=1), jnp.concatenate([-s, c], axis=1)], axis=0)
    kc = jnp.arange(gc, dtype=I32)
    c, s = cs(kc[:, None] * kc[None, :], gc)
    mc = jnp.concatenate([c, s], axis=0) * (gc ** -0.5)
    kq = jnp.arange(C, dtype=I32)
    c, s = cs(kq[:, None] * kq[None, :], C)
    mctx = jnp.concatenate([c, s], axis=0) * (C ** -0.5)
    return m1.astype(BF16), m2.astype(BF16), mc.astype(BF16), mctx.astype(BF16)


def _fnet_stage1_kernel(x_ref, g_ref, sc_ref, sh_ref, m1_ref, z_ref, *, n2c, P):
    g, sc, sh = g_ref[...], sc_ref[0], sh_ref[0]
    for j in range(n2c):
        h = _normmod(x_ref[:, j, :], g, sc, sh).astype(BF16)
        z = _dot(m1_ref[j], h)
        z_ref[0, :, 0, j, :] = z[:P]
        z_ref[0, :, 1, j, :] = z[P:]


def _fnet_stage2_kernel(z_ref, m2_ref, mc_ref, f_ref, *, k1c, P, gc):
    D = f_ref.shape[-1]
    for j in range(k1c):
        z = z_ref[0, j].reshape(2 * P, D).astype(BF16)
        y = _dot(m2_ref[...], z)
        yr, yi = y[:P].astype(BF16), y[P:].astype(BF16)
        outs = []
        for gi in range(D // gc):
            sl = slice(gi * gc, (gi + 1) * gc)
            outs.append(_dot(yr[:, sl], mc_ref[:gc, :]) + _dot(yi[:, sl], mc_ref[gc:, :]))
        f_ref[:, j, :] = jnp.concatenate(outs, axis=1)


def _fnet_ctx_kernel(x_ref, g_ref, sc_ref, sh_ref, ml_ref, mc_ref, f_ref, *, C, gc):
    D = x_ref.shape[-1]
    h = _normmod(x_ref[...], g_ref[...], sc_ref[0], sh_ref[0]).astype(BF16)
    y = _dot(ml_ref[...], h)
    yc, ys = y[:C].astype(BF16), y[C:].astype(BF16)
    outs = []
    for gi in range(D // gc):
        sl = slice(gi * gc, (gi + 1) * gc)
        outs.append(_dot(yc[:, sl], mc_ref[:gc, :]) - _dot(ys[:, sl], mc_ref[gc:, :]))
    f_ref[...] = jnp.concatenate(outs, axis=1)


def _fnet_mix(lay, x, g, modv, tables):
    B, S, C, D = lay.B, lay.S, lay.C, lay.D
    m1, m2, mc, mctx = tables
    P = math.isqrt(S)
    gc = D // FNET_GROUPS
    n2c = SUBLANES
    k1c = SUBLANES
    full = lambda a: pl.BlockSpec(a.shape, lambda *i: (0,) * a.ndim)
    modspec = lambda chunk: pl.BlockSpec((1, 1, D), lambda b, j: (chunk * SUBLANES + b, 0, 0))
    assert C % P == 0 and P % n2c == 0 and P % k1c == 0
    x3 = x.reshape(lay.NT // P, P, D)
    z = pl.pallas_call(
        functools.partial(_fnet_stage1_kernel, n2c=n2c, P=P),
        out_shape=jax.ShapeDtypeStruct((B, P, 2, P, D), F32),
        grid=(B, P // n2c),
        in_specs=[pl.BlockSpec((P, n2c, D), lambda b, j: (b, j, 0)), full(g), modspec(1), modspec(0),
                  pl.BlockSpec((n2c, 2 * P, P), lambda b, j: (j, 0, 0))],
        out_specs=pl.BlockSpec((1, P, 2, n2c, D), lambda b, j: (b, 0, 0, j, 0)),
        compiler_params=_cp(("arbitrary", "arbitrary")),
        name="fnet_stage1",
    )(x3, g, modv, modv, m1)
    f_lat = pl.pallas_call(
        functools.partial(_fnet_stage2_kernel, k1c=k1c, P=P, gc=gc),
        out_shape=jax.ShapeDtypeStruct((lay.NL // P, P, D), F32),
        grid=(B, P // k1c),
        in_specs=[pl.BlockSpec((1, k1c, 2, P, D), lambda b, j: (b, j, 0, 0, 0)), full(m2), full(mc)],
        out_specs=pl.BlockSpec((P, k1c, D), lambda b, j: (b, j, 0)),
        compiler_params=_cp(("arbitrary", "arbitrary")),
        name="fnet_stage2",
    )(z, m2, mc)
    cblk0 = lay.NL // C
    ctx_mod = lambda chunk: pl.BlockSpec((1, 1, D), lambda b: (chunk * SUBLANES + B, 0, 0))
    f_ctx = pl.pallas_call(
        functools.partial(_fnet_ctx_kernel, C=C, gc=gc),
        out_shape=jax.ShapeDtypeStruct((lay.NC, D), F32),
        grid=(B,),
        in_specs=[pl.BlockSpec((C, D), lambda b: (cblk0 + b, 0)), full(g), ctx_mod(1), ctx_mod(0), full(mctx),
                  full(mc)],
        out_specs=pl.BlockSpec((C, D), lambda b: (b, 0)),
        compiler_params=_cp(("arbitrary",)),
        name="fnet_ctx",
    )(x, g, modv, modv, mctx, mc)
    return f_lat.reshape(lay.NL, D), f_ctx


NA_QROWS = 2
NA_KROWS = NA_KR + NA_QROWS
NA_VARIANTS = 5
NA_ONES = 16


def _na_bias_kernel(rpb_ref, sel_ref, toe_ref, o_ref):
    g = jnp.dot(rpb_ref[0], toe_ref[...], precision=HIGHEST, preferred_element_type=F32)
    for t in range(NA_VARIANTS):
        o_ref[t, 0] = jnp.dot(sel_ref[t], g, precision=HIGHEST, preferred_element_type=F32)


def _na_window_start(r, rows):
    return jnp.clip(r - NA_KR // 2, 0, rows - NA_KROWS)


def _na_bias_tables(rpb, rows):
    assert rows >= 16 and rows % NA_QROWS == 0
    W = GRID_W
    nh, nu, nv = rpb.shape
    up, vp, ajp = 2 * SUBLANES, LANES, 3 * SUBLANES
    assert nu <= up and nv <= vp and NA_QROWS * NA_KROWS <= ajp
    cq, ck = np.arange(W)[:, None], np.arange(W)[None, :]
    dc = np.clip(ck - cq + NA_KC - 1, 0, nv - 1).reshape(-1)
    toe = (np.arange(vp)[:, None] == dc[None, :]).astype(np.float32)
    cs = np.clip(cq - NA_KC // 2, 0, W - NA_KC)
    col_ok = (ck >= cs) & (ck < cs + NA_KC)
    sel = np.zeros((NA_VARIANTS, ajp, up), np.float32)
    valid = np.zeros((NA_VARIANTS, NA_QROWS, NA_KROWS, W, W), bool)
    for t, r in enumerate((0, 2, 6, rows - 4, rows - 2)):
        w0 = min(max(r - NA_KR // 2, 0), rows - NA_KROWS)
        for a in range(NA_QROWS):
            rs = min(max(r + a - NA_KR // 2, 0), rows - NA_KR)
            for j in range(NA_KROWS):
                rk = w0 + j
                sel[t, a * NA_KROWS + j, min(max(rk - (r + a) + NA_KR - 1, 0), nu - 1)] = 1.0
                if rs <= rk < rs + NA_KR:
                    valid[t, a, j] = col_ok
    rpb_p = jnp.pad(rpb, ((0, 0), (0, up - nu), (0, vp - nv)))
    tab = pl.pallas_call(
        _na_bias_kernel,
        out_shape=jax.ShapeDtypeStruct((NA_VARIANTS, nh, ajp, W * W), F32),
        grid=(nh,),
        in_specs=[pl.BlockSpec((1, up, vp), lambda h: (h, 0, 0)),
                  pl.BlockSpec(sel.shape, lambda h: (0, 0, 0)),
                  pl.BlockSpec(toe.shape, lambda h: (0, 0))],
        out_specs=pl.BlockSpec((NA_VARIANTS, 1, ajp, W * W), lambda h: (0, h, 0, 0)),
        compiler_params=_cp(("arbitrary",)),
        name="na_bias",
    )(rpb_p, jnp.asarray(sel), jnp.asarray(toe))
    tab = tab[:, :, :NA_QROWS * NA_KROWS].reshape(NA_VARIANTS, nh, NA_QROWS, NA_KROWS, W, W)
    tab = jnp.where(jnp.asarray(valid)[:, None], tab * math.log2(math.e), NEG_INF)
    tab = tab.reshape(NA_VARIANTS, nh // 2, 2, NA_QROWS, NA_KROWS, W, W)
    return tab.transpose(0, 1, 4, 6, 2, 3, 5).reshape(NA_VARIANTS, nh // 2, NA_KROWS * W, 2 * NA_QROWS * W)


def _na_kernel(*refs, rows, local):
    if local:
        qt_ref, kc_ref, vtc_ref, kl_ref, vtl_ref, bias_ref, o_ref, st_a, sc_a, st_b, sc_b = refs
    else:
        qt_ref, kc_ref, vtc_ref, o_ref, sc_a, sc_b = refs
        st_a = st_b = None
    nq = NA_QROWS * GRID_W
    nk = NA_KROWS * GRID_W
    kc, vtc = kc_ref[...], vtc_ref[...]
    row = lax.broadcasted_iota(I32, (LANES, nq), 0)
    pairs = qt_ref.shape[1] // nq
    qi = pl.program_id(2) if local else 0

    def ext(vt):
        return jnp.concatenate([vt, jnp.ones((NA_ONES, vt.shape[1]), BF16)], axis=0)

    def window(t):
        r = (qi * pairs + t) * NA_QROWS
        k0 = pl.multiple_of(_na_window_start(r, rows) * GRID_W, LANES)
        var = jnp.where(r == 0, 0, jnp.where(r == 2, 1, jnp.where(r == rows - 4, 3, jnp.where(r == rows - 2, 4, 2))))
        return k0, var

    def scores(t, st_ref, sc_ref):
        qt = qt_ref[:, t * nq:(t + 1) * nq]
        zero = jnp.zeros_like(qt)
        qbd = jnp.concatenate([jnp.where(row < NA_HD, qt, zero), jnp.where(row >= NA_HD, qt, zero)], axis=1)
        sc_ref[...] = _dot(kc, qbd)
        if local:
            k0, var = window(t)
            st_ref[...] = _dot(kl_ref[pl.ds(k0, nk), :], qbd) + bias_ref[var, 0]

    def finish(t, st_ref, sc_ref):
        sc = sc_ref[...]
        m = jnp.max(sc, axis=0, keepdims=True)
        if local:
            st = st_ref[...]
            m = jnp.maximum(m, jnp.max(st, axis=0, keepdims=True))
            pt = jnp.exp2(st - m).astype(BF16)
            vtw = vtl_ref[:, pl.ds(window(t)[0], nk)]
        pc = jnp.exp2(sc - m).astype(BF16)
        outs = []
        for hl in range(2):
            hs, qs = slice(hl * NA_HD, (hl + 1) * NA_HD), slice(hl * nq, (hl + 1) * nq)
            acc = _dot(ext(vtc[hs, :]), pc[:, qs])
            if local:
                acc = acc + _dot(ext(vtw[hs, :]), pt[:, qs])
            outs.append(acc[:NA_HD] / acc[NA_HD:NA_HD + 1])
        o_ref[t * nq:(t + 1) * nq, :] = jnp.concatenate(outs, axis=0).T.astype(o_ref.dtype)

    bufs = ((st_a, sc_a), (st_b, sc_b))
    scores(0, *bufs[0])
    for t in range(pairs):
        if t + 1 < pairs:
            scores(t + 1, *bufs[(t + 1) % 2])
        finish(t, *bufs[t % 2])


def _na_attention(lay, qt, k, vt, bias, *, tq=512):
    B, S, C = lay.B, lay.S, lay.C
    rows = S // GRID_W
    HP = NA_HEADS // 2
    nq = S // tq
    cblk0 = lay.NL // C
    nk, nqp = NA_KROWS * GRID_W, NA_QROWS * GRID_W
    o_lat = pl.pallas_call(
        functools.partial(_na_kernel, rows=rows, local=True),
        out_shape=jax.ShapeDtypeStruct((lay.NL, NA_HEADS * NA_HD), BF16),
        grid=(B, HP, nq),
        in_specs=[pl.BlockSpec((LANES, tq), lambda b, h, i: (h, b * nq + i)),
                  pl.BlockSpec((C, LANES), lambda b, h, i: (cblk0 + b, h)),
                  pl.BlockSpec((LANES, C), lambda b, h, i: (h, cblk0 + b)),
                  pl.BlockSpec((S, LANES), lambda b, h, i: (b, h)),
                  pl.BlockSpec((LANES, S), lambda b, h, i: (h, b)),
                  pl.BlockSpec((NA_VARIANTS, 1) + bias.shape[2:], lambda b, h, i: (0, h, 0, 0))],
        out_specs=pl.BlockSpec((tq, LANES), lambda b, h, i: (b * nq + i, h)),
        scratch_shapes=[pltpu.VMEM((nk, 2 * nqp), F32), pltpu.VMEM((C, 2 * nqp), F32)] * 2,
        compiler_params=_cp(("arbitrary", "arbitrary", "arbitrary")),
        name="na_attn_latent",
    )(qt, k, vt, k, vt, bias)
    o_ctx = pl.pallas_call(
        functools.partial(_na_kernel, rows=rows, local=False),
        out_shape=jax.ShapeDtypeStruct((lay.NC, NA_HEADS * NA_HD), BF16),
        grid=(B, HP),
        in_specs=[pl.BlockSpec((LANES, C), lambda b, h: (h, cblk0 + b)),
                  pl.BlockSpec((C, LANES), lambda b, h: (cblk0 + b, h)),
                  pl.BlockSpec((LANES, C), lambda b, h: (h, cblk0 + b))],
        out_specs=pl.BlockSpec((C, LANES), lambda b, h: (b, h)),
        scratch_shapes=[pltpu.VMEM((C, 2 * nqp), F32)] * 2,
        compiler_params=_cp(("arbitrary", "arbitrary")),
        name="na_attn_ctx",
    )(qt, k, vt)
    return o_lat, o_ctx


SW_SUB = 128
SW_BAND = SW_SUB + 2 * SW_WINDOW
SW_ONES = 16


def _swa_kernel(*refs, S, local):
    if local:
        (sink_ref, qt_ref, kc_ref, vtc_ref, kp_ref, kcur_ref, kn_ref, vtp_ref, vtcur_ref, vtn_ref, o_ref,
         kbuf, vtbuf, st_a, sc_a, st_b, sc_b) = refs
        tq = qt_ref.shape[1]
        W = SW_WINDOW
        kbuf[0:W] = kp_ref[...]
        kbuf[W:W + tq] = kcur_ref[...]
        kbuf[W + tq:] = kn_ref[...]
        vtbuf[:, 0:W] = vtp_ref[...]
        vtbuf[:, W:W + tq] = vtcur_ref[...]
        vtbuf[:, W + tq:] = vtn_ref[...]
        i = pl.program_id(1)
    else:
        sink_ref, qt_ref, kc_ref, vtc_ref, o_ref, sc_a, sc_b = refs
        st_a = st_b = None
        tq = qt_ref.shape[1]
    G = SW_HEADS // SW_KV_HEADS
    kc, vtc = kc_ref[...], vtc_ref[...]
    zeros_q = jnp.zeros((SW_HD, G * SW_SUB), BF16)

    def ext(vt):
        return jnp.concatenate([vt, jnp.ones((SW_ONES, vt.shape[1]), BF16)], axis=0)

    def band_bias(sb):
        key = lax.broadcasted_iota(I32, (SW_BAND, SW_SUB), 0)
        qry = lax.broadcasted_iota(I32, (SW_BAND, SW_SUB), 1)
        rel = key - SW_WINDOW - qry
        kpos = i * tq + sb * SW_SUB - SW_WINDOW + key
        ok = (jnp.abs(rel) <= SW_WINDOW) & (kpos >= 0) & (kpos < S)
        bias = jnp.where(ok, 0.0, NEG_INF).astype(F32)
        return jnp.concatenate([bias] * G, axis=1)

    def scores(sb, g, bias, st_ref, sc_ref):
        r0 = sb * SW_SUB
        tile, half = g // 2, g % 2
        sl = slice(tile * LANES, (tile + 1) * LANES)
        qg = jnp.concatenate([qt_ref[(G * g + hl) * SW_HD:(G * g + hl + 1) * SW_HD, r0:r0 + SW_SUB]
                              for hl in range(G)], axis=1)
        qpad = jnp.concatenate([qg, zeros_q] if half == 0 else [zeros_q, qg], axis=0)
        sc_ref[...] = _dot(kc[:, sl], qpad)
        if local:
            st_ref[...] = _dot(kbuf[r0:r0 + SW_BAND, sl], qpad) + bias

    def finish(sb, g, st_ref, sc_ref):
        r0 = sb * SW_SUB
        sink = sink_ref[g]
        sc = sc_ref[...]
        m = jnp.maximum(jnp.max(sc, axis=0, keepdims=True), sink)
        if local:
            st = st_ref[...]
            m = jnp.maximum(m, jnp.max(st, axis=0, keepdims=True))
        acc = _dot(ext(vtc[g * SW_HD:(g + 1) * SW_HD, :]), jnp.exp2(sc - m).astype(BF16))
        if local:
            acc = acc + _dot(ext(vtbuf[g * SW_HD:(g + 1) * SW_HD, r0:r0 + SW_BAND]), jnp.exp2(st - m).astype(BF16))
        og = acc[:SW_HD] / (acc[SW_HD:SW_HD + 1] + jnp.exp2(sink - m))
        return [og[:, hl * SW_SUB:(hl + 1) * SW_SUB] for hl in range(G)]

    items = [(sb, g) for sb in range(tq // SW_SUB) for g in range(SW_KV_HEADS)]
    bufs = ((st_a, sc_a), (st_b, sc_b))
    bias = band_bias(0) if local else None
    scores(*items[0], bias, *bufs[0])
    outs = []
    for n, (sb, g) in enumerate(items):
        if n + 1 < len(items):
            nsb, ng = items[n + 1]
            if local and ng == 0:
                bias = band_bias(nsb)
            scores(nsb, ng, bias, *bufs[(n + 1) % 2])
        outs.extend(finish(sb, g, *bufs[n % 2]))
        if g == SW_KV_HEADS - 1:
            o_ref[sb * SW_SUB:(sb + 1) * SW_SUB, :] = jnp.concatenate(outs, axis=0).T.astype(o_ref.dtype)
            outs = []


def _swa_attention(lay, qt, k, vt, sinks, *, tq=512):
    B, S, C = lay.B, lay.S, lay.C
    nq_rows = SW_HEADS * SW_HD
    nkv = SW_KV_HEADS * SW_HD
    G = SW_HEADS // SW_KV_HEADS
    nq = S // tq
    per = tq // SW_WINDOW
    last = lay.NT // SW_WINDOW - 1
    cblk0 = lay.NL // C
    prev = lambda b, i: jnp.maximum((b * nq + i) * per - 1, 0)
    nxt = lambda b, i: jnp.minimum((b * nq + i + 1) * per, last)
    sink_l = jnp.repeat(sinks.reshape(SW_KV_HEADS, 1, G) * math.log2(math.e), SW_SUB, axis=2)
    sink_spec = pl.BlockSpec(sink_l.shape, lambda *a: (0, 0, 0))
    o_lat = pl.pallas_call(
        functools.partial(_swa_kernel, S=S, local=True),
        out_shape=jax.ShapeDtypeStruct((lay.NL, nq_rows), BF16),
        grid=(B, nq),
        in_specs=[sink_spec,
                  pl.BlockSpec((nq_rows, tq), lambda b, i: (0, b * nq + i)),
                  pl.BlockSpec((C, nkv), lambda b, i: (cblk0 + b, 0)),
                  pl.BlockSpec((nkv, C), lambda b, i: (0, cblk0 + b)),
                  pl.BlockSpec((SW_WINDOW, nkv), lambda b, i: (prev(b, i), 0)),
                  pl.BlockSpec((tq, nkv), lambda b, i: (b * nq + i, 0)),
                  pl.BlockSpec((SW_WINDOW, nkv), lambda b, i: (nxt(b, i), 0)),
                  pl.BlockSpec((nkv, SW_WINDOW), lambda b, i: (0, prev(b, i))),
                  pl.BlockSpec((nkv, tq), lambda b, i: (0, b * nq + i)),
                  pl.BlockSpec((nkv, SW_WINDOW), lambda b, i: (0, nxt(b, i)))],
        out_specs=pl.BlockSpec((tq, nq_rows), lambda b, i: (b * nq + i, 0)),
        scratch_shapes=([pltpu.VMEM((tq + 2 * SW_WINDOW, nkv), BF16), pltpu.VMEM((nkv, tq + 2 * SW_WINDOW), BF16)]
                        + [pltpu.VMEM((SW_BAND, G * SW_SUB), F32), pltpu.VMEM((C, G * SW_SUB), F32)] * 2),
        compiler_params=_cp(("arbitrary", "arbitrary")),
        name="swa_attn_latent",
    )(sink_l, qt, k, vt, k, k, k, vt, vt, vt)
    o_ctx = pl.pallas_call(
        functools.partial(_swa_kernel, S=S, local=False),
        out_shape=jax.ShapeDtypeStruct((lay.NC, nq_rows), BF16),
        grid=(B,),
        in_specs=[sink_spec,
                  pl.BlockSpec((nq_rows, C), lambda b: (0, cblk0 + b)),
                  pl.BlockSpec((C, nkv), lambda b: (cblk0 + b, 0)),
                  pl.BlockSpec((nkv, C), lambda b: (0, cblk0 + b))],
        out_specs=pl.BlockSpec((C, nq_rows), lambda b: (b, 0)),
        scratch_shapes=[pltpu.VMEM((C, G * SW_SUB), F32)] * 2,
        compiler_params=_cp(("arbitrary",)),
        name="swa_attn_ctx",
    )(sink_l, qt, k, vt)
    return o_lat, o_ctx


def _route(logits, tri, carry):
    lane = _lane_iota(logits.shape)
    lanef = lane.astype(F32)
    big = float(LANES)
    rowmax = lambda t: jnp.max(t, axis=-1, keepdims=True)
    rowmin = lambda t: jnp.min(t, axis=-1, keepdims=True)
    rowsum = lambda t: jnp.sum(t, axis=-1, keepdims=True)
    is_g = lane < MOE_GROUPS
    mg = rowmax(jnp.where(is_g, logits, -jnp.inf))
    w_g = 1.0 / rowsum(jnp.where(is_g, jnp.exp(logits - mg), 0.0))
    gidx = rowmin(jnp.where(is_g & (logits == mg), lanef, big))
    g0 = MOE_GROUPS + MOE_PER_GROUP * gidx
    in_grp = (lanef >= g0) & (lanef < g0 + MOE_PER_GROUP)
    le = jnp.where(in_grp, logits, -jnp.inf)
    m1 = rowmax(le)
    i1 = rowmin(jnp.where(in_grp & (le == m1), lanef, big))
    le2 = jnp.where(lanef == i1, -jnp.inf, le)
    m2 = rowmax(le2)
    i2 = rowmin(jnp.where(in_grp & (lanef != i1) & (le2 == m2), lanef, big))
    r = jnp.exp(m2 - m1)
    gate1 = w_g / (1.0 + r)
    gate2 = w_g * r / (1.0 + r)
    sel1, sel2 = lanef == i1, lanef == i2
    member = (sel1 | sel2)
    cum = _dot(tri, member.astype(BF16)) + carry
    rank1 = rowsum(jnp.where(sel1, cum, 0.0))
    rank2 = rowsum(jnp.where(sel2, cum, 0.0))
    new_carry = carry + jnp.sum(member.astype(F32), axis=0, keepdims=True)
    rec = jnp.zeros_like(logits)
    for ln, val in ((0, i1 - MOE_GROUPS), (1, i2 - MOE_GROUPS), (2, rank1), (3, rank2), (4, gate1), (5, gate2)):
        rec = jnp.where(lane == ln, val, rec)
    return rec, new_carry


def _out_proj_kernel(*refs, has_bias, n_lat_tiles):
    if has_bias:
        (al_ref, ac_ref, x_ref, w_ref, b_ref, g1_ref, g2n_ref, sc_ref, sh_ref, wr_ref, br_ref, tri_ref,
         xo_ref, h2_ref, rec_ref, rect_ref, cnt_ref, carry_ref) = refs
    else:
        (al_ref, ac_ref, x_ref, w_ref, g1_ref, g2n_ref, sc_ref, sh_ref, wr_ref, br_ref, tri_ref,
         xo_ref, h2_ref, rec_ref, rect_ref, cnt_ref, carry_ref) = refs

    @pl.when(pl.program_id(0) == 0)
    def _():
        carry_ref[...] = jnp.zeros_like(carry_ref)

    a = jnp.where(pl.program_id(0) < n_lat_tiles, al_ref[...], ac_ref[...])
    y = _dot(a.astype(BF16), w_ref[...])
    if has_bias:
        y = y + b_ref[...]
    xn = x_ref[...] + g1_ref[0] * y
    xo_ref[...] = xn
    h2 = _normmod(xn, g2n_ref[...], sc_ref[0], sh_ref[0])
    h2_ref[...] = h2
    h_hi = h2.astype(BF16)
    h_lo = (h2 - h_hi.astype(F32)).astype(BF16)
    hw = _dot(h_hi, wr_ref[...])
    logits = hw[:, :LANES] + hw[:, LANES:] + _dot(h_lo, wr_ref[:, :LANES]) + br_ref[...]
    rec, carry = _route(logits, tri_ref[...], carry_ref[...])
    rec_ref[...] = rec
    rect_ref[...] = rec.T[:SUBLANES, :]
    carry_ref[...] = carry
    cnt_ref[...] = jnp.broadcast_to(carry, cnt_ref.shape)


def _out_proj(lay, a, x, w, b, g2n, modv, wr, br, tri):
    D = lay.D
    a_lat, a_ctx = a
    nl = lay.nl_tiles
    full = lambda t: pl.BlockSpec(t.shape, lambda i: (0,) * t.ndim)
    row = lambda n: pl.BlockSpec((TM, n), lambda i: (i, 0))
    ins = [a_lat, a_ctx, x, w] + ([b] if b is not None else []) + [modv, g2n, modv, modv, wr, br, tri]
    specs = ([pl.BlockSpec((TM, a_lat.shape[1]), lambda i: (jnp.minimum(i, nl - 1), 0)),
              pl.BlockSpec((TM, a_ctx.shape[1]), lambda i: (jnp.maximum(i - nl, 0), 0)), row(D), full(w)]
             + ([full(b)] if b is not None else [])
             + [lay.mod_spec(2), full(g2n), lay.mod_spec(4), lay.mod_spec(3), full(wr), full(br), full(tri)])
    return pl.pallas_call(
        functools.partial(_out_proj_kernel, has_bias=b is not None, n_lat_tiles=nl),
        out_shape=(jax.ShapeDtypeStruct((lay.NT, D), F32), jax.ShapeDtypeStruct((lay.NT, D), F32),
                   jax.ShapeDtypeStruct((lay.NT, LANES), F32), jax.ShapeDtypeStruct((SUBLANES, lay.NT), F32),
                   jax.ShapeDtypeStruct((SUBLANES, LANES), F32)),
        grid=(lay.n_tiles,),
        in_specs=specs,
        out_specs=(row(D), row(D), row(LANES), pl.BlockSpec((SUBLANES, TM), lambda i: (0, i)),
                   pl.BlockSpec((SUBLANES, LANES), lambda i: (0, 0))),
        scratch_shapes=[pltpu.VMEM((1, LANES), F32)],
        input_output_aliases={2: 0},
        compiler_params=_cp(("arbitrary",)),
        name="out_proj_router",
    )(*ins)


def _row_copy(src, s, dst, d, sem):
    return pltpu.make_async_copy(src.at[pl.ds(s, 1), :], dst.at[pl.ds(d, 1), :], sem)


def _dispatch_kernel(pos_ref, pend_ref, nu_ref, h_ref, xb_ref, zbuf, sem, zsem):
    base = pl.program_id(0) * TM
    n_tok = pos_ref.shape[0] // 2
    nblk = xb_ref.shape[0] // MOE_BM

    @pl.when(pl.program_id(0) == 0)
    def _():
        zbuf[...] = jnp.zeros_like(zbuf)

        def zero_block(row0):
            return pltpu.make_async_copy(zbuf, xb_ref.at[pl.ds(pl.multiple_of(row0, MOE_BM), MOE_BM), :], zsem)

        def each(fn):
            def expert(e, c):
                end = pend_ref[e]

                @pl.when(end > jnp.where(e > 0, pend_ref[jnp.maximum(e - 1, 0)], 0))
                def _():
                    fn(zero_block(end - MOE_BM))
                return c

            def tail(j, c):
                @pl.when(j >= nu_ref[0])
                def _():
                    fn(zero_block(j * MOE_BM))
                return c

            lax.fori_loop(0, MOE_EXPERTS, expert, 0)
            lax.fori_loop(0, nblk, tail, 0)

        each(lambda cp: cp.start())
        each(lambda cp: cp.wait())

    def issue(r, c):
        for k in range(2):
            _row_copy(h_ref, r, xb_ref, pos_ref[k * n_tok + base + r], sem).start(priority=k)
        return c

    lax.fori_loop(0, TM, issue, 0, unroll=8)
    for k in range(2):
        pltpu.make_async_copy(h_ref, xb_ref.at[pl.ds(0, TM), :], sem).wait()


def _dispatch(lay, pos, pend, n_used, h2, cap):
    D = lay.D
    return pl.pallas_call(
        _dispatch_kernel,
        out_shape=jax.ShapeDtypeStruct((cap, D), F32),
        grid_spec=pltpu.PrefetchScalarGridSpec(
            num_scalar_prefetch=3, grid=(lay.n_tiles,),
            in_specs=[pl.BlockSpec((TM, D), lambda i, p, e, n: (i, 0))],
            out_specs=pl.BlockSpec(memory_space=pl.ANY),
            scratch_shapes=[pltpu.VMEM((MOE_BM, D), F32), pltpu.SemaphoreType.DMA, pltpu.SemaphoreType.DMA]),
        compiler_params=_cp(("arbitrary",)),
        name="moe_dispatch",
    )(pos, pend, n_used, h2)


def _expert_kernel(be_ref, nu_ref, xb_ref, wg_ref, wu_ref, wd_ref, yb_ref, wgb, wub, wdb):
    j = pl.program_id(0)
    prev = be_ref[jnp.maximum(j - 1, 0)]

    @pl.when((j == 0) | (be_ref[j] != prev))
    def _():
        wgb[...] = wg_ref[0, 0].astype(BF16)
        wub[...] = wu_ref[0, 0].astype(BF16)
        wdb[...] = wd_ref[0, 0].astype(BF16)

    @pl.when(j < nu_ref[0])
    def _():
        xe = xb_ref[...].astype(BF16)
        g = _dot(xe, wgb[...])
        u = _dot(xe, wub[...])
        act = (g * jax.nn.sigmoid(g) * u).astype(BF16)
        yb_ref[...] = _dot(act, wdb[...])

    @pl.when(j >= nu_ref[0])
    def _():
        yb_ref[...] = jnp.zeros_like(yb_ref)


def _experts(xb, blk_e, n_used, w_gate, w_up, w_down, layer):
    cap, D = xb.shape
    FF = w_gate.shape[-1]
    nblk = cap // MOE_BM
    blk = lambda j, be, nu: (jnp.maximum(jnp.minimum(j, nu[0] - 1), 0), 0)
    wblk = lambda j, be, nu: (layer, be[j], 0, 0)
    return pl.pallas_call(
        _expert_kernel,
        out_shape=jax.ShapeDtypeStruct((cap, D), F32),
        grid_spec=pltpu.PrefetchScalarGridSpec(
            num_scalar_prefetch=2, grid=(nblk,),
            in_specs=[pl.BlockSpec((MOE_BM, D), blk),
                      pl.BlockSpec((1, 1, D, FF), wblk), pl.BlockSpec((1, 1, D, FF), wblk),
                      pl.BlockSpec((1, 1, FF, D), wblk)],
            out_specs=pl.BlockSpec((MOE_BM, D), lambda j, be, nu: (j, 0)),
            scratch_shapes=[pltpu.VMEM((D, FF), BF16), pltpu.VMEM((D, FF), BF16), pltpu.VMEM((FF, D), BF16)]),
        compiler_params=_cp(("arbitrary",)),
        name="moe_experts",
    )(blk_e, n_used, xb, w_gate, w_up, w_down)


def _combine_kernel(*refs, final):
    if final:
        pos_ref, x_ref, rec_ref, g2_ref, yb_ref, fg_ref, o_ref, buf, sem = refs
    else:
        pos_ref, x_ref, rec_ref, g2_ref, yb_ref, o_ref, buf, sem = refs
    base = pl.program_id(0) * TM
    n_tok = pos_ref.shape[0] // 2

    def issue(r, c):
        for k in range(2):
            _row_copy(yb_ref, pos_ref[k * n_tok + base + r], buf.at[k], r, sem).start(priority=k)
        return c

    lax.fori_loop(0, TM, issue, 0, unroll=8)
    for k in range(2):
        pltpu.make_async_copy(yb_ref.at[pl.ds(0, TM), :], buf.at[k], sem).wait()
    rec = rec_ref[...]
    f = rec[:, ROUTE_LANE_GATE:ROUTE_LANE_GATE + 1] * buf[0] + rec[:, ROUTE_LANE_GATE + 1:ROUTE_LANE_GATE + 2] * buf[1]
    xn = x_ref[...] + g2_ref[0] * f
    if final:
        xn = _rms(xn, fg_ref[...])
    o_ref[...] = xn


def _combine(lay, pos, x, rec, modv, yb, final_g):
    D = lay.D
    final = final_g is not None
    row = lambda n: pl.BlockSpec((TM, n), lambda i, p: (i, 0))
    specs = [row(D), row(LANES),
             pl.BlockSpec((1, 1, D), lambda i, p: (5 * SUBLANES + lay.mod_row(i), 0, 0)),
             pl.BlockSpec(memory_space=pl.ANY)]
    ins = [x, rec, modv, yb]
    if final:
        specs.append(pl.BlockSpec(final_g.shape, lambda i, p: (0, 0)))
        ins.append(final_g)
    n_rows, n_tiles = (lay.NL, lay.nl_tiles) if final else (lay.NT, lay.n_tiles)
    return pl.pallas_call(
        functools.partial(_combine_kernel, final=final),
        out_shape=jax.ShapeDtypeStruct((n_rows, D), F32),
        grid_spec=pltpu.PrefetchScalarGridSpec(
            num_scalar_prefetch=1, grid=(n_tiles,),
            in_specs=specs,
            out_specs=row(D),
            scratch_shapes=[pltpu.VMEM((2, TM, D), F32), pltpu.SemaphoreType.DMA]),
        input_output_aliases={} if final else {1: 0},
        compiler_params=_cp(("arbitrary",)),
        name="moe_combine",
    )(pos, *ins)


def _moe(lay, x, h2, rec, rect, counts, modv, w_gate, w_up, w_down, layer, final_g):
    T = 2 * lay.NT
    cap = -(-T // MOE_BM) * MOE_BM + MOE_EXPERTS * MOE_BM
    cnt = counts[0, MOE_GROUPS:MOE_GROUPS + MOE_EXPERTS].astype(I32)
    pcnt = (cnt + MOE_BM - 1) // MOE_BM * MOE_BM
    pend = jnp.cumsum(pcnt)
    start = pend - pcnt
    eid = rect[ROUTE_LANE_EID:ROUTE_LANE_EID + 2].astype(I32)
    rank = rect[ROUTE_LANE_RANK:ROUTE_LANE_RANK + 2].astype(I32)
    eid, rank = eid.reshape(1, -1), rank.reshape(-1)
    hit = eid == jnp.arange(MOE_EXPERTS, dtype=I32)[:, None]
    pos = jnp.sum(jnp.where(hit, start[:, None], 0), axis=0) + rank
    blk_row = jnp.arange(cap // MOE_BM, dtype=I32) * MOE_BM
    blk_e = jnp.minimum(jnp.sum((pend[None, :] <= blk_row[:, None]).astype(I32), axis=1), MOE_EXPERTS - 1)
    n_used = (pend[-1:] // MOE_BM).astype(I32)
    xb = _dispatch(lay, pos, pend.astype(I32), n_used, h2, cap)
    yb = _experts(xb, blk_e, n_used, w_gate, w_up, w_down, layer)
    return _combine(lay, pos, x, rec, modv, yb, final_g)


def _rope_tables(lay):
    S = lay.S
    t = jnp.arange(S)
    n = MLA_ROPE // 4
    inv = ROPE_THETA ** (-jnp.arange(n, dtype=F32) / n)
    ang = jnp.concatenate([(t // GRID_W).astype(F32)[:, None] * inv, (t % GRID_W).astype(F32)[:, None] * inv], axis=-1)
    cos, sin = jnp.cos(ang), jnp.sin(ang)
    cos64 = jnp.concatenate([cos, cos], axis=-1)
    sin64 = jnp.concatenate([-sin, sin], axis=-1)
    rows = lambda lat, ctx_val: jnp.concatenate([jnp.tile(lat, (lay.B, 1)), jnp.full((lay.NC, 64), ctx_val, F32)], axis=0)
    cos64, sin64 = rows(cos64, 1.0), rows(sin64, 0.0)
    zero = jnp.zeros_like(cos64)
    return ((jnp.concatenate([cos64, zero], axis=1), jnp.concatenate([sin64, zero], axis=1)),
            (jnp.concatenate([cos64, cos64], axis=1), jnp.concatenate([sin64, sin64], axis=1)))


def kernel(x, c, ctx, c_ctx, mod_w, mod_b, norm1_g, norm2_g, mla_w_dq, mla_g_q, mla_w_uq, mla_w_dkv, mla_g_kv, mla_w_ukv, mla_w_o, fnet_w_o, fnet_b_o, na_w_qkv, na_rpb, na_w_o, swa_w_qkv, swa_sinks, swa_w_o, moe_w_grp, moe_b_grp, moe_w_rt, moe_b_rt, moe_w_gate, moe_w_up, moe_w_down, final_g):
    B, S, D = x.shape
    C = ctx.shape[1]
    depth = mod_w.shape[0]
    lay = _Layout(B, S, C, D)
    X = jnp.concatenate([x.reshape(B * S, D), ctx.reshape(B * C, D)], axis=0)
    cond = jnp.concatenate([c, c_ctx[None], jnp.zeros((SUBLANES - B - 1, D), F32)], axis=0)
    mod = _modulation(cond, mod_w, mod_b)
    (mla_cos, mla_sin), (swa_cos, swa_sin) = _rope_tables(lay)
    tri = (jnp.arange(TM)[:, None] > jnp.arange(TM)[None, :]).astype(BF16)
    n_mix = 4
    for i in range(depth):
        m, j = i % n_mix, i // n_mix
        modv = mod[i].reshape(SUBLANES, 6, D).transpose(1, 0, 2).reshape(6 * SUBLANES, 1, D)
        g1n, g2n = norm1_g[i][None], norm2_g[i][None]
        bias = None
        if m == 0:
            w1 = jnp.concatenate([mla_w_dq[j], mla_w_dkv[j], jnp.zeros((D, LANES - MLA_ROPE), F32)], axis=1).astype(BF16)
            wq = mla_w_uq[j].reshape(MLA_Q_RANK, MLA_HEADS, MLA_NOPE + MLA_ROPE)
            wq = jnp.concatenate([wq, jnp.zeros((MLA_Q_RANK, MLA_HEADS, LANES - MLA_ROPE), F32)], axis=-1)
            wq = wq.reshape(MLA_Q_RANK, MLA_HEADS * 2 * LANES).astype(BF16)
            q, k, v = _mla_proj(lay, X, g1n, modv, w1, mla_g_q[j][None], mla_g_kv[j][None], wq,
                                mla_w_ukv[j].astype(BF16), mla_cos, mla_sin)
            a = _mla_attention(lay, q, k, v)
            w_o = mla_w_o[j]
        elif m == 1:
            a = _fnet_mix(lay, X, g1n, modv, _dft_tables(S, C, D // FNET_GROUPS))
            w_o, bias = fnet_w_o[j], fnet_b_o[j][None]
        elif m == 2:
            qt, k, vt = _qkv_proj(lay, X, g1n, modv, na_w_qkv[j].astype(BF16), swa_cos, swa_sin,
                                  n_q=NA_HEADS * NA_HD, n_k=NA_HEADS * NA_HD, rope=False,
                                  q_scale=NA_HD ** -0.5 * math.log2(math.e))
            a = _na_attention(lay, qt, k, vt, _na_bias_tables(na_rpb[j], S // GRID_W))
            w_o = na_w_o[j]
        else:
            qt, k, vt = _qkv_proj(lay, X, g1n, modv, swa_w_qkv[j].astype(BF16), swa_cos, swa_sin,
                                  n_q=SW_HEADS * SW_HD, n_k=SW_KV_HEADS * SW_HD, rope=True,
                                  q_scale=SW_HD ** -0.5 * math.log2(math.e))
            a = _swa_attention(lay, qt, k, vt, swa_sinks[j])
            w_o = swa_w_o[j]
        wr = jnp.concatenate([moe_w_grp[i], moe_w_rt[i], jnp.zeros((D, LANES - MOE_GROUPS - MOE_EXPERTS), F32)], axis=1)
        br = jnp.concatenate([moe_b_grp[i], moe_b_rt[i], jnp.zeros((LANES - MOE_GROUPS - MOE_EXPERTS,), F32)])[None]
        wr_hi = wr.astype(BF16)
        wr2 = jnp.concatenate([wr_hi, (wr - wr_hi.astype(F32)).astype(BF16)], axis=1)
        X, h2, rec, rect, counts = _out_proj(lay, a, X, w_o.astype(BF16), bias, g2n, modv, wr2, br, tri)
        X = _moe(lay, X, h2, rec, rect, counts, modv, moe_w_gate, moe_w_up, moe_w_down, i,
                 final_g[None] if i == depth - 1 else None)
    return X.reshape(B, S, D)
```

```python
import functools
import math

import jax
import jax.numpy as jnp
import numpy as np
from jax import lax
from jax.experimental import pallas as pl
from jax.experimental.pallas import tpu as pltpu

F32 = jnp.float32
BF16 = jnp.bfloat16
I32 = jnp.int32
HIGHEST = lax.Precision.HIGHEST

GRID_W = 64
EPS = 1e-6
ROPE_THETA = 10000.0
NEG_INF = -1e30
MLA_HEADS, MLA_Q_RANK, MLA_KV_RANK, MLA_NOPE, MLA_ROPE, MLA_V = 8, 512, 256, 128, 64, 128
MLA_VT_ROWS = MLA_V + 16
MLA_UNROLL = 8
FNET_GROUPS = 4
NA_HEADS, NA_HD, NA_KR, NA_KC = 16, 64, 8, 16
SW_HEADS, SW_KV_HEADS, SW_HD, SW_WINDOW = 16, 4, 64, 128
MOE_GROUPS, MOE_PER_GROUP, MOE_FF = 4, 8, 512
MOE_EXPERTS = MOE_GROUPS * MOE_PER_GROUP

LANES = 128
SUBLANES = 8
TM = 512
MOE_BM = 512
VMEM_LIMIT = 56 * 1024 * 1024
ROUTE_LANE_EID, ROUTE_LANE_RANK, ROUTE_LANE_GATE = 0, 2, 4


def _cp(sem, vmem=VMEM_LIMIT):
    return pltpu.CompilerParams(dimension_semantics=sem, vmem_limit_bytes=vmem)


def _lane_iota(shape):
    return lax.broadcasted_iota(I32, shape, len(shape) - 1)


def _normmod(x, g, sc, sh):
    ms = jnp.mean(x * x, axis=-1, keepdims=True)
    return (x * lax.rsqrt(ms + EPS) * g) * (1.0 + sc) + sh


def _rms(x, g):
    ms = jnp.mean(x * x, axis=-1, keepdims=True)
    return x * lax.rsqrt(ms + EPS) * g


def _swap_halves(t, period):
    n = t.shape[-1]
    half = period // 2
    lane = _lane_iota(t.shape)
    return jnp.where((lane % period) < half, pltpu.roll(t, n - half, 1), pltpu.roll(t, half, 1))


def _dot(a, b):
    return jnp.dot(a, b, preferred_element_type=F32)


def _mod_kernel(a_ref, w_ref, b_ref, o_ref):
    a = a_ref[...]
    a = a * jax.nn.sigmoid(a)
    o_ref[0] = jnp.dot(a, w_ref[0], precision=HIGHEST, preferred_element_type=F32) + b_ref[0]


def _modulation(cond, mod_w, mod_b):
    depth, d, n = mod_w.shape
    tn = n // 4
    return pl.pallas_call(
        _mod_kernel,
        out_shape=jax.ShapeDtypeStruct((depth, SUBLANES, n), F32),
        grid=(depth, n // tn),
        in_specs=[pl.BlockSpec((SUBLANES, d), lambda l, j: (0, 0)),
                  pl.BlockSpec((1, d, tn), lambda l, j: (l, 0, j)),
                  pl.BlockSpec((1, 1, tn), lambda l, j: (l, 0, j))],
        out_specs=pl.BlockSpec((1, SUBLANES, tn), lambda l, j: (l, 0, j)),
        compiler_params=_cp(("arbitrary", "arbitrary")),
        name="modulation",
    )(cond, mod_w, mod_b.reshape(depth, 1, n))


class _Layout:
    def __init__(self, B, S, C, D):
        self.B, self.S, self.C, self.D = B, S, C, D
        self.NL, self.NC = B * S, B * C
        self.NT = self.NL + self.NC
        assert S % TM == 0 and self.NC % TM == 0 and TM % C == 0
        self.nl_tiles = self.NL // TM
        self.n_tiles = self.NT // TM
        self.tiles_per_batch = S // TM

    def mod_row(self, i):
        return jnp.where(i < self.nl_tiles, i // self.tiles_per_batch, self.B)

    def mod_spec(self, chunk):
        return pl.BlockSpec((1, 1, self.D), lambda i: (chunk * SUBLANES + self.mod_row(i), 0, 0))


def _mla_proj_kernel(x_ref, g_ref, sc_ref, sh_ref, w1_ref, gq_ref, gkv_ref, wq_ref, wkv_ref, cos_ref, sin_ref,
                     qt_ref, k_ref, vt_ref):
    h = _normmod(x_ref[...], g_ref[...], sc_ref[0], sh_ref[0]).astype(BF16)
    a = _dot(h, w1_ref[...])
    qa = _rms(a[:, :MLA_Q_RANK], gq_ref[...]).astype(BF16)
    ckv = _rms(a[:, MLA_Q_RANK:MLA_Q_RANK + MLA_KV_RANK], gkv_ref[...]).astype(BF16)
    cos, sin = cos_ref[...], sin_ref[...]

    def rope(t):
        return t * cos + _swap_halves(t, MLA_ROPE) * sin

    kr = rope(a[:, MLA_Q_RANK + MLA_KV_RANK:]).astype(BF16)
    scale = (MLA_NOPE + MLA_ROPE) ** -0.5 * math.log2(math.e)
    q = _dot(qa, wq_ref[...])
    kv = _dot(ckv, wkv_ref[...])
    ones = jnp.ones((MLA_VT_ROWS - MLA_V, x_ref.shape[0]), BF16)
    for hd in range(MLA_HEADS):
        c = hd * 2 * LANES
        qh = jnp.concatenate([q[:, c:c + LANES], rope(q[:, c + LANES:c + 2 * LANES])], axis=1) * scale
        qt_ref[hd] = qh.T.astype(BF16)
        k_ref[:, c:c + LANES] = kv[:, c:c + LANES].astype(BF16)
        k_ref[:, c + LANES:c + 2 * LANES] = kr
        vt_ref[hd, :MLA_V, :] = kv[:, c + LANES:c + 2 * LANES].T.astype(BF16)
        vt_ref[hd, MLA_V:, :] = ones


def _mla_proj(lay, x, g, modv, w1, gq, gkv, wq, wkv, cos, sin):
    D = lay.D
    full = lambda a: pl.BlockSpec(a.shape, lambda i: (0,) * a.ndim)
    row = lambda n: pl.BlockSpec((TM, n), lambda i: (i, 0))
    col = lambda r: pl.BlockSpec((MLA_HEADS, r, TM), lambda i: (0, 0, i))
    hq = MLA_HEADS * 2 * LANES
    return pl.pallas_call(
        _mla_proj_kernel,
        out_shape=(jax.ShapeDtypeStruct((MLA_HEADS, 2 * LANES, lay.NT), BF16),
                   jax.ShapeDtypeStruct((lay.NT, hq), BF16),
                   jax.ShapeDtypeStruct((MLA_HEADS, MLA_VT_ROWS, lay.NT), BF16)),
        grid=(lay.n_tiles,),
        in_specs=[row(D), full(g), lay.mod_spec(1), lay.mod_spec(0), full(w1), full(gq), full(gkv), full(wq),
                  full(wkv), row(LANES), row(LANES)],
        out_specs=(col(2 * LANES), row(hq), col(MLA_VT_ROWS)),
        compiler_params=_cp(("arbitrary",)),
        name="mla_proj",
    )(x, g, modv, modv, w1, gq, gkv, wq, wkv, cos, sin)


def _qkv_proj_kernel(x_ref, g_ref, sc_ref, sh_ref, w_ref, cos_ref, sin_ref, qt_ref, k_ref, vt_ref, *,
                     n_q, n_k, rope, q_scale, chunk):
    h = _normmod(x_ref[...], g_ref[...], sc_ref[0], sh_ref[0]).astype(BF16)
    n = w_ref.shape[1]
    for c0 in range(0, n, chunk):
        a = _dot(h, w_ref[:, c0:c0 + chunk])
        if rope and c0 < n_q + n_k:
            reps = chunk // LANES
            cos = jnp.concatenate([cos_ref[...]] * reps, axis=1)
            sin = jnp.concatenate([sin_ref[...]] * reps, axis=1)
            a = a * cos + _swap_halves(a, SW_HD) * sin
        if c0 < n_q:
            qt_ref[c0:c0 + chunk, :] = (a * q_scale).T.astype(BF16)
        elif c0 < n_q + n_k:
            k_ref[:, c0 - n_q:c0 - n_q + chunk] = a.astype(BF16)
        else:
            c = c0 - n_q - n_k
            vt_ref[c:c + chunk, :] = a.T.astype(BF16)


def _qkv_proj(lay, x, g, modv, w, cos, sin, *, n_q, n_k, rope, q_scale, chunk=256):
    D, n = lay.D, w.shape[1]
    n_v = n - n_q - n_k
    assert n_q % chunk == 0 and n_k % chunk == 0 and n_v % chunk == 0
    full = lambda a: pl.BlockSpec(a.shape, lambda i: (0,) * a.ndim)
    row = lambda m: pl.BlockSpec((TM, m), lambda i: (i, 0))
    col = lambda m: pl.BlockSpec((m, TM), lambda i: (0, i))
    return pl.pallas_call(
        functools.partial(_qkv_proj_kernel, n_q=n_q, n_k=n_k, rope=rope, q_scale=q_scale, chunk=chunk),
        out_shape=(jax.ShapeDtypeStruct((n_q, lay.NT), BF16), jax.ShapeDtypeStruct((lay.NT, n_k), BF16),
                   jax.ShapeDtypeStruct((n_v, lay.NT), BF16)),
        grid=(lay.n_tiles,),
        in_specs=[row(D), full(g), lay.mod_spec(1), lay.mod_spec(0), full(w), row(LANES), row(LANES)],
        out_specs=(col(n_q), row(n_k), col(n_v)),
        compiler_params=_cp(("arbitrary",)),
        name="qkv_proj",
    )(x, g, modv, modv, w, cos, sin)


def _mla_attn_kernel(*refs, tk, n_lat):
    if n_lat:
        qt_ref, kc_ref, vtc_ref, kl_ref, vtl_ref, o_ref, acc_ref, sa_ref, sb_ref = refs
    else:
        qt_ref, kc_ref, vtc_ref, o_ref, acc_ref = refs
    qt = qt_ref[0]

    st = _dot(kc_ref[...], qt)
    if n_lat:
        sa_ref[...] = _dot(kl_ref[pl.ds(0, tk), :], qt)
    m = jnp.max(st, axis=0, keepdims=True)
    acc_ref[...] = _dot(vtc_ref[0], jnp.exp2(st - m).astype(BF16))

    if n_lat:
        nch = n_lat // tk

        def softmax_pv(st, vt, m):
            m_new = jnp.maximum(m, jnp.max(st, axis=0, keepdims=True))
            acc_ref[...] = jnp.exp2(m - m_new) * acc_ref[...] + _dot(vt, jnp.exp2(st - m_new).astype(BF16))
            return m_new

        def body(jj, m):
            r0 = pl.multiple_of(2 * jj * tk, tk)
            r1 = pl.multiple_of((2 * jj + 1) * tk, tk)
            r2 = pl.multiple_of(jnp.minimum(2 * jj + 2, nch - 1) * tk, tk)
            sb_ref[...] = _dot(kl_ref[pl.ds(r1, tk), :], qt)
            m = softmax_pv(sa_ref[...], vtl_ref[0, :, pl.ds(r0, tk)], m)
            sa_ref[...] = _dot(kl_ref[pl.ds(r2, tk), :], qt)
            return softmax_pv(sb_ref[...], vtl_ref[0, :, pl.ds(r1, tk)], m)

        lax.fori_loop(0, nch // 2, body, m, unroll=math.gcd(nch // 2, MLA_UNROLL))
    o_ref[...] = (acc_ref[:MLA_V, :] / acc_ref[MLA_V:MLA_V + 1, :]).T.astype(o_ref.dtype)


def _mla_attention(lay, qt, k, vt, *, tq=1024, tk=512):
    B, S, C = lay.B, lay.S, lay.C
    H = MLA_HEADS
    nq = S // tq
    cblk0 = lay.NL // C
    assert S % (2 * tk) == 0
    o_lat = pl.pallas_call(
        functools.partial(_mla_attn_kernel, tk=tk, n_lat=S),
        out_shape=jax.ShapeDtypeStruct((lay.NL, H * LANES), BF16),
        grid=(B, H, nq),
        in_specs=[pl.BlockSpec((1, 2 * LANES, tq), lambda b, h, i: (h, 0, b * nq + i)),
                  pl.BlockSpec((C, 2 * LANES), lambda b, h, i: (cblk0 + b, h)),
                  pl.BlockSpec((1, MLA_VT_ROWS, C), lambda b, h, i: (h, 0, cblk0 + b)),
                  pl.BlockSpec((S, 2 * LANES), lambda b, h, i: (b, h)),
                  pl.BlockSpec((1, MLA_VT_ROWS, S), lambda b, h, i: (h, 0, b))],
        out_specs=pl.BlockSpec((tq, LANES), lambda b, h, i: (b * nq + i, h)),
        scratch_shapes=[pltpu.VMEM((MLA_VT_ROWS, tq), F32), pltpu.VMEM((tk, tq), F32), pltpu.VMEM((tk, tq), F32)],
        compiler_params=_cp(("arbitrary", "arbitrary", "arbitrary")),
        name="mla_attn_latent",
    )(qt, k, vt, k, vt)
    o_ctx = pl.pallas_call(
        functools.partial(_mla_attn_kernel, tk=tk, n_lat=0),
        out_shape=jax.ShapeDtypeStruct((lay.NC, H * LANES), BF16),
        grid=(B, H),
        in_specs=[pl.BlockSpec((1, 2 * LANES, C), lambda b, h: (h, 0, cblk0 + b)),
                  pl.BlockSpec((C, 2 * LANES), lambda b, h: (cblk0 + b, h)),
                  pl.BlockSpec((1, MLA_VT_ROWS, C), lambda b, h: (h, 0, cblk0 + b))],
        out_specs=pl.BlockSpec((C, LANES), lambda b, h: (b, h)),
        scratch_shapes=[pltpu.VMEM((MLA_VT_ROWS, C), F32)],
        compiler_params=_cp(("arbitrary", "arbitrary")),
        name="mla_attn_ctx",
    )(qt, k, vt)
    return o_lat, o_ctx


def _dft_tables(S, C, gc):
    P = math.isqrt(S)
    assert P * P == S and (P & (P - 1)) == 0 and (gc & (gc - 1)) == 0 and (C & (C - 1)) == 0

    def cs(idx, n):
        ang = (idx % n).astype(F32) * (2.0 * math.pi / n)
        return jnp.cos(ang), jnp.sin(ang)

    k1 = jnp.arange(P, dtype=I32)
    idx = k1[None, :, None] * (P * k1[None, None, :] + k1[:, None, None])
    c, s = cs(idx, S)
    m1 = jnp.concatenate([c, -s], axis=1) * (1.0 / P)
    c, s = cs(k1[:, None] * k1[None, :], P)
    m2 = jnp.concatenate([jnp.concatenate([c, s], axis=1), jnp.concatenate([-s, c], axis=1)], axis=0)
    kc = jnp.arange(gc, dtype=I32)
    c, s = cs(kc[:, None] * kc[None, :], gc)
    mc = jnp.concatenate([c, s], axis=0) * (gc ** -0.5)
    kq = jnp.arange(C, dtype=I32)
    c, s = cs(kq[:, None] * kq[None, :], C)
    mctx = jnp.concatenate([c, s], axis=0) * (C ** -0.5)
    return m1.astype(BF16), m2.astype(BF16), mc.astype(BF16), mctx.astype(BF16)


def _fnet_stage1_kernel(x_ref, g_ref, sc_ref, sh_ref, m1_ref, z_ref, *, n2c, P):
    g, sc, sh = g_ref[...], sc_ref[0], sh_ref[0]
    for j in range(n2c):
        h = _normmod(x_ref[:, j, :], g, sc, sh).astype(BF16)
        z = _dot(m1_ref[j], h)
        z_ref[0, :, 0, j, :] = z[:P]
        z_ref[0, :, 1, j, :] = z[P:]


def _fnet_stage2_kernel(z_ref, m2_ref, mc_ref, f_ref, *, k1c, P, gc):
    D = f_ref.shape[-1]
    for j in range(k1c):
        z = z_ref[0, j].reshape(2 * P, D).astype(BF16)
        y = _dot(m2_ref[...], z)
        yr, yi = y[:P].astype(BF16), y[P:].astype(BF16)
        outs = []
        for gi in range(D // gc):
            sl = slice(gi * gc, (gi + 1) * gc)
            outs.append(_dot(yr[:, sl], mc_ref[:gc, :]) + _dot(yi[:, sl], mc_ref[gc:, :]))
        f_ref[:, j, :] = jnp.concatenate(outs, axis=1)


def _fnet_ctx_kernel(x_ref, g_ref, sc_ref, sh_ref, ml_ref, mc_ref, f_ref, *, C, gc):
    D = x_ref.shape[-1]
    h = _normmod(x_ref[...], g_ref[...], sc_ref[0], sh_ref[0]).astype(BF16)
    y = _dot(ml_ref[...], h)
    yc, ys = y[:C].astype(BF16), y[C:].astype(BF16)
    outs = []
    for gi in range(D // gc):
        sl = slice(gi * gc, (gi + 1) * gc)
        outs.append(_dot(yc[:, sl], mc_ref[:gc, :]) - _dot(ys[:, sl], mc_ref[gc:, :]))
    f_ref[...] = jnp.concatenate(outs, axis=1)


def _fnet_mix(lay, x, g, modv, tables):
    B, S, C, D = lay.B, lay.S, lay.C, lay.D
    m1, m2, mc, mctx = tables
    P = math.isqrt(S)
    gc = D // FNET_GROUPS
    n2c = SUBLANES
    k1c = SUBLANES
    full = lambda a: pl.BlockSpec(a.shape, lambda *i: (0,) * a.ndim)
    modspec = lambda chunk: pl.BlockSpec((1, 1, D), lambda b, j: (chunk * SUBLANES + b, 0, 0))
    assert C % P == 0 and P % n2c == 0 and P % k1c == 0
    x3 = x.reshape(lay.NT // P, P, D)
    z = pl.pallas_call(
        functools.partial(_fnet_stage1_kernel, n2c=n2c, P=P),
        out_shape=jax.ShapeDtypeStruct((B, P, 2, P, D), F32),
        grid=(B, P // n2c),
        in_specs=[pl.BlockSpec((P, n2c, D), lambda b, j: (b, j, 0)), full(g), modspec(1), modspec(0),
                  pl.BlockSpec((n2c, 2 * P, P), lambda b, j: (j, 0, 0))],
        out_specs=pl.BlockSpec((1, P, 2, n2c, D), lambda b, j: (b, 0, 0, j, 0)),
        compiler_params=_cp(("arbitrary", "arbitrary")),
        name="fnet_stage1",
    )(x3, g, modv, modv, m1)
    f_lat = pl.pallas_call(
        functools.partial(_fnet_stage2_kernel, k1c=k1c, P=P, gc=gc),
        out_shape=jax.ShapeDtypeStruct((lay.NL // P, P, D), F32),
        grid=(B, P // k1c),
        in_specs=[pl.BlockSpec((1, k1c, 2, P, D), lambda b, j: (b, j, 0, 0, 0)), full(m2), full(mc)],
        out_specs=pl.BlockSpec((P, k1c, D), lambda b, j: (b, j, 0)),
        compiler_params=_cp(("arbitrary", "arbitrary")),
        name="fnet_stage2",
    )(z, m2, mc)
    cblk0 = lay.NL // C
    ctx_mod = lambda chunk: pl.BlockSpec((1, 1, D), lambda b: (chunk * SUBLANES + B, 0, 0))
    f_ctx = pl.pallas_call(
        functools.partial(_fnet_ctx_kernel, C=C, gc=gc),
        out_shape=jax.ShapeDtypeStruct((lay.NC, D), F32),
        grid=(B,),
        in_specs=[pl.BlockSpec((C, D), lambda b: (cblk0 + b, 0)), full(g), ctx_mod(1), ctx_mod(0), full(mctx),
                  full(mc)],
        out_specs=pl.BlockSpec((C, D), lambda b: (b, 0)),
        compiler_params=_cp(("arbitrary",)),
        name="fnet_ctx",
    )(x, g, modv, modv, mctx, mc)
    return f_lat.reshape(lay.NL, D), f_ctx


NA_QROWS = 2
NA_KROWS = NA_KR + NA_QROWS
NA_VARIANTS = 5
NA_ONES = 16


def _na_bias_kernel(rpb_ref, sel_ref, toe_ref, o_ref):
    g = jnp.dot(rpb_ref[0], toe_ref[...], precision=HIGHEST, preferred_element_type=F32)
    for t in range(NA_VARIANTS):
        o_ref[t, 0] = jnp.dot(sel_ref[t], g, precision=HIGHEST, preferred_element_type=F32)


def _na_window_start(r, rows):
    return jnp.clip(r - NA_KR // 2, 0, rows - NA_KROWS)


def _na_bias_tables(rpb, rows):
    assert rows >= 16 and rows % NA_QROWS == 0
    W = GRID_W
    nh, nu, nv = rpb.shape
    up, vp, ajp = 2 * SUBLANES, LANES, 3 * SUBLANES
    assert nu <= up and nv <= vp and NA_QROWS * NA_KROWS <= ajp
    cq, ck = np.arange(W)[:, None], np.arange(W)[None, :]
    dc = np.clip(ck - cq + NA_KC - 1, 0, nv - 1).reshape(-1)
    toe = (np.arange(vp)[:, None] == dc[None, :]).astype(np.float32)
    cs = np.clip(cq - NA_KC // 2, 0, W - NA_KC)
    col_ok = (ck >= cs) & (ck < cs + NA_KC)
    sel = np.zeros((NA_VARIANTS, ajp, up), np.float32)
    valid = np.zeros((NA_VARIANTS, NA_QROWS, NA_KROWS, W, W), bool)
    for t, r in enumerate((0, 2, 6, rows - 4, rows - 2)):
        w0 = min(max(r - NA_KR // 2, 0), rows - NA_KROWS)
        for a in range(NA_QROWS):
            rs = min(max(r + a - NA_KR // 2, 0), rows - NA_KR)
            for j in range(NA_KROWS):
                rk = w0 + j
                sel[t, a * NA_KROWS + j, min(max(rk - (r + a) + NA_KR - 1, 0), nu - 1)] = 1.0
                if rs <= rk < rs + NA_KR:
                    valid[t, a, j] = col_ok
    rpb_p = jnp.pad(rpb, ((0, 0), (0, up - nu), (0, vp - nv)))
    tab = pl.pallas_call(
        _na_bias_kernel,
        out_shape=jax.ShapeDtypeStruct((NA_VARIANTS, nh, ajp, W * W), F32),
        grid=(nh,),
        in_specs=[pl.BlockSpec((1, up, vp), lambda h: (h, 0, 0)),
                  pl.BlockSpec(sel.shape, lambda h: (0, 0, 0)),
                  pl.BlockSpec(toe.shape, lambda h: (0, 0))],
        out_specs=pl.BlockSpec((NA_VARIANTS, 1, ajp, W * W), lambda h: (0, h, 0, 0)),
        compiler_params=_cp(("arbitrary",)),
        name="na_bias",
    )(rpb_p, jnp.asarray(sel), jnp.asarray(toe))
    tab = tab[:, :, :NA_QROWS * NA_KROWS].reshape(NA_VARIANTS, nh, NA_QROWS, NA_KROWS, W, W)
    tab = jnp.where(jnp.asarray(valid)[:, None], tab * math.log2(math.e), NEG_INF)
    tab = tab.reshape(NA_VARIANTS, nh // 2, 2, NA_QROWS, NA_KROWS, W, W)
    return tab.transpose(0, 1, 4, 6, 2, 3, 5).reshape(NA_VARIANTS, nh // 2, NA_KROWS * W, 2 * NA_QROWS * W)


def _na_kernel(*refs, rows, local):
    if local:
        qt_ref, kc_ref, vtc_ref, kl_ref, vtl_ref, bias_ref, o_ref, st_a, sc_a, st_b, sc_b = refs
    else:
        qt_ref, kc_ref, vtc_ref, o_ref, sc_a, sc_b = refs
        st_a = st_b = None
    nq = NA_QROWS * GRID_W
    nk = NA_KROWS * GRID_W
    kc, vtc = kc_ref[...], vtc_ref[...]
    row = lax.broadcasted_iota(I32, (LANES, nq), 0)
    pairs = qt_ref.shape[1] // nq
    qi = pl.program_id(2) if local else 0

    def ext(vt):
        return jnp.concatenate([vt, jnp.ones((NA_ONES, vt.shape[1]), BF16)], axis=0)

    def window(t):
        r = (qi * pairs + t) * NA_QROWS
        k0 = pl.multiple_of(_na_window_start(r, rows) * GRID_W, LANES)
        var = jnp.where(r == 0, 0, jnp.where(r == 2, 1, jnp.where(r == rows - 4, 3, jnp.where(r == rows - 2, 4, 2))))
        return k0, var

    def scores(t, st_ref, sc_ref):
        qt = qt_ref[:, t * nq:(t + 1) * nq]
        zero = jnp.zeros_like(qt)
        qbd = jnp.concatenate([jnp.where(row < NA_HD, qt, zero), jnp.where(row >= NA_HD, qt, zero)], axis=1)
        sc_ref[...] = _dot(kc, qbd)
        if local:
            k0, var = window(t)
            st_ref[...] = _dot(kl_ref[pl.ds(k0, nk), :], qbd) + bias_ref[var, 0]

    def finish(t, st_ref, sc_ref):
        sc = sc_ref[...]
        m = jnp.max(sc, axis=0, keepdims=True)
        if local:
            st = st_ref[...]
            m = jnp.maximum(m, jnp.max(st, axis=0, keepdims=True))
            pt = jnp.exp2(st - m).astype(BF16)
            vtw = vtl_ref[:, pl.ds(window(t)[0], nk)]
        pc = jnp.exp2(sc - m).astype(BF16)
        outs = []
        for hl in range(2):
            hs, qs = slice(hl * NA_HD, (hl + 1) * NA_HD), slice(hl * nq, (hl + 1) * nq)
            acc = _dot(ext(vtc[hs, :]), pc[:, qs])
            if local:
                acc = acc + _dot(ext(vtw[hs, :]), pt[:, qs])
            outs.append(acc[:NA_HD] / acc[NA_HD:NA_HD + 1])
        o_ref[t * nq:(t + 1) * nq, :] = jnp.concatenate(outs, axis=0).T.astype(o_ref.dtype)

    bufs = ((st_a, sc_a), (st_b, sc_b))
    scores(0, *bufs[0])
    for t in range(pairs):
        if t + 1 < pairs:
            scores(t + 1, *bufs[(t + 1) % 2])
        finish(t, *bufs[t % 2])


def _na_attention(lay, qt, k, vt, bias, *, tq=1024):
    B, S, C = lay.B, lay.S, lay.C
    rows = S // GRID_W
    HP = NA_HEADS // 2
    nq = S // tq
    cblk0 = lay.NL // C
    nk, nqp = NA_KROWS * GRID_W, NA_QROWS * GRID_W
    o_lat = pl.pallas_call(
        functools.partial(_na_kernel, rows=rows, local=True),
        out_shape=jax.ShapeDtypeStruct((lay.NL, NA_HEADS * NA_HD), BF16),
        grid=(B, HP, nq),
        in_specs=[pl.BlockSpec((LANES, tq), lambda b, h, i: (h, b * nq + i)),
                  pl.BlockSpec((C, LANES), lambda b, h, i: (cblk0 + b, h)),
                  pl.BlockSpec((LANES, C), lambda b, h, i: (h, cblk0 + b)),
                  pl.BlockSpec((S, LANES), lambda b, h, i: (b, h)),
                  pl.BlockSpec((LANES, S), lambda b, h, i: (h, b)),
                  pl.BlockSpec((NA_VARIANTS, 1) + bias.shape[2:], lambda b, h, i: (0, h, 0, 0))],
        out_specs=pl.BlockSpec((tq, LANES), lambda b, h, i: (b * nq + i, h)),
        scratch_shapes=[pltpu.VMEM((nk, 2 * nqp), F32), pltpu.VMEM((C, 2 * nqp), F32)] * 2,
        compiler_params=_cp(("arbitrary", "arbitrary", "arbitrary")),
        name="na_attn_latent",
    )(qt, k, vt, k, vt, bias)
    o_ctx = pl.pallas_call(
        functools.partial(_na_kernel, rows=rows, local=False),
        out_shape=jax.ShapeDtypeStruct((lay.NC, NA_HEADS * NA_HD), BF16),
        grid=(B, HP),
        in_specs=[pl.BlockSpec((LANES, C), lambda b, h: (h, cblk0 + b)),
                  pl.BlockSpec((C, LANES), lambda b, h: (cblk0 + b, h)),
                  pl.BlockSpec((LANES, C), lambda b, h: (h, cblk0 + b))],
        out_specs=pl.BlockSpec((C, LANES), lambda b, h: (b, h)),
        scratch_shapes=[pltpu.VMEM((C, 2 * nqp), F32)] * 2,
        compiler_params=_cp(("arbitrary", "arbitrary")),
        name="na_attn_ctx",
    )(qt, k, vt)
    return o_lat, o_ctx


SW_SUB = 128
SW_BAND = SW_SUB + 2 * SW_WINDOW
SW_ONES = 16


def _swa_kernel(*refs, S, local):
    if local:
        (sink_ref, qt_ref, kc_ref, vtc_ref, kp_ref, kcur_ref, kn_ref, vtp_ref, vtcur_ref, vtn_ref, o_ref,
         kbuf, vtbuf, st_a, sc_a, st_b, sc_b) = refs
        tq = qt_ref.shape[1]
        W = SW_WINDOW
        kbuf[0:W] = kp_ref[...]
        kbuf[W:W + tq] = kcur_ref[...]
        kbuf[W + tq:] = kn_ref[...]
        vtbuf[:, 0:W] = vtp_ref[...]
        vtbuf[:, W:W + tq] = vtcur_ref[...]
        vtbuf[:, W + tq:] = vtn_ref[...]
        i = pl.program_id(1)
    else:
        sink_ref, qt_ref, kc_ref, vtc_ref, o_ref, sc_a, sc_b = refs
        st_a = st_b = None
        tq = qt_ref.shape[1]
    G = SW_HEADS // SW_KV_HEADS
    kc, vtc = kc_ref[...], vtc_ref[...]
    zeros_q = jnp.zeros((SW_HD, G * SW_SUB), BF16)

    def ext(vt):
        return jnp.concatenate([vt, jnp.ones((SW_ONES, vt.shape[1]), BF16)], axis=0)

    def band_bias(sb):
        key = lax.broadcasted_iota(I32, (SW_BAND, SW_SUB), 0)
        qry = lax.broadcasted_iota(I32, (SW_BAND, SW_SUB), 1)
        rel = key - SW_WINDOW - qry
        kpos = i * tq + sb * SW_SUB - SW_WINDOW + key
        ok = (jnp.abs(rel) <= SW_WINDOW) & (kpos >= 0) & (kpos < S)
        bias = jnp.where(ok, 0.0, NEG_INF).astype(F32)
        return jnp.concatenate([bias] * G, axis=1)

    def scores(sb, g, bias, st_ref, sc_ref):
        r0 = sb * SW_SUB
        tile, half = g // 2, g % 2
        sl = slice(tile * LANES, (tile + 1) * LANES)
        qg = jnp.concatenate([qt_ref[(G * g + hl) * SW_HD:(G * g + hl + 1) * SW_HD, r0:r0 + SW_SUB]
                              for hl in range(G)], axis=1)
        qpad = jnp.concatenate([qg, zeros_q] if half == 0 else [zeros_q, qg], axis=0)
        sc_ref[...] = _dot(kc[:, sl], qpad)
        if local:
            st_ref[...] = _dot(kbuf[r0:r0 + SW_BAND, sl], qpad) + bias

    def finish(sb, g, st_ref, sc_ref):
        r0 = sb * SW_SUB
        sink = sink_ref[g]
        sc = sc_ref[...]
        m = jnp.maximum(jnp.max(sc, axis=0, keepdims=True), sink)
        if local:
            st = st_ref[...]
            m = jnp.maximum(m, jnp.max(st, axis=0, keepdims=True))
        acc = _dot(ext(vtc[g * SW_HD:(g + 1) * SW_HD, :]), jnp.exp2(sc - m).astype(BF16))
        if local:
            acc = acc + _dot(ext(vtbuf[g * SW_HD:(g + 1) * SW_HD, r0:r0 + SW_BAND]), jnp.exp2(st - m).astype(BF16))
        og = acc[:SW_HD] / (acc[SW_HD:SW_HD + 1] + jnp.exp2(sink - m))
        return [og[:, hl * SW_SUB:(hl + 1) * SW_SUB] for hl in range(G)]

    items = [(sb, g) for sb in range(tq // SW_SUB) for g in range(SW_KV_HEADS)]
    bufs = ((st_a, sc_a), (st_b, sc_b))
    bias = band_bias(0) if local else None
    scores(*items[0], bias, *bufs[0])
    outs = []
    for n, (sb, g) in enumerate(items):
        if n + 1 < len(items):
            nsb, ng = items[n + 1]
            if local and ng == 0:
                bias = band_bias(nsb)
            scores(nsb, ng, bias, *bufs[(n + 1) % 2])
        outs.extend(finish(sb, g, *bufs[n % 2]))
        if g == SW_KV_HEADS - 1:
            o_ref[sb * SW_SUB:(sb + 1) * SW_SUB, :] = jnp.concatenate(outs, axis=0).T.astype(o_ref.dtype)
            outs = []


def _swa_attention(lay, qt, k, vt, sinks, *, tq=512):
    B, S, C = lay.B, lay.S, lay.C
    nq_rows = SW_HEADS * SW_HD
    nkv = SW_KV_HEADS * SW_HD
    G = SW_HEADS // SW_KV_HEADS
    nq = S // tq
    per = tq // SW_WINDOW
    last = lay.NT // SW_WINDOW - 1
    cblk0 = lay.NL // C
    prev = lambda b, i: jnp.maximum((b * nq + i) * per - 1, 0)
    nxt = lambda b, i: jnp.minimum((b * nq + i + 1) * per, last)
    sink_l = jnp.repeat(sinks.reshape(SW_KV_HEADS, 1, G) * math.log2(math.e), SW_SUB, axis=2)
    sink_spec = pl.BlockSpec(sink_l.shape, lambda *a: (0, 0, 0))
    o_lat = pl.pallas_call(
        functools.partial(_swa_kernel, S=S, local=True),
        out_shape=jax.ShapeDtypeStruct((lay.NL, nq_rows), BF16),
        grid=(B, nq),
        in_specs=[sink_spec,
                  pl.BlockSpec((nq_rows, tq), lambda b, i: (0, b * nq + i)),
                  pl.BlockSpec((C, nkv), lambda b, i: (cblk0 + b, 0)),
                  pl.BlockSpec((nkv, C), lambda b, i: (0, cblk0 + b)),
                  pl.BlockSpec((SW_WINDOW, nkv), lambda b, i: (prev(b, i), 0)),
                  pl.BlockSpec((tq, nkv), lambda b, i: (b * nq + i, 0)),
                  pl.BlockSpec((SW_WINDOW, nkv), lambda b, i: (nxt(b, i), 0)),
                  pl.BlockSpec((nkv, SW_WINDOW), lambda b, i: (0, prev(b, i))),
                  pl.BlockSpec((nkv, tq), lambda b, i: (0, b * nq + i)),
                  pl.BlockSpec((nkv, SW_WINDOW), lambda b, i: (0, nxt(b, i)))],
        out_specs=pl.BlockSpec((tq, nq_rows), lambda b, i: (b * nq + i, 0)),
        scratch_shapes=([pltpu.VMEM((tq + 2 * SW_WINDOW, nkv), BF16), pltpu.VMEM((nkv, tq + 2 * SW_WINDOW), BF16)]
                        + [pltpu.VMEM((SW_BAND, G * SW_SUB), F32), pltpu.VMEM((C, G * SW_SUB), F32)] * 2),
        compiler_params=_cp(("arbitrary", "arbitrary")),
        name="swa_attn_latent",
    )(sink_l, qt, k, vt, k, k, k, vt, vt, vt)
    o_ctx = pl.pallas_call(
        functools.partial(_swa_kernel, S=S, local=False),
        out_shape=jax.ShapeDtypeStruct((lay.NC, nq_rows), BF16),
        grid=(B,),
        in_specs=[sink_spec,
                  pl.BlockSpec((nq_rows, C), lambda b: (0, cblk0 + b)),
                  pl.BlockSpec((C, nkv), lambda b: (cblk0 + b, 0)),
                  pl.BlockSpec((nkv, C), lambda b: (0, cblk0 + b))],
        out_specs=pl.BlockSpec((C, nq_rows), lambda b: (b, 0)),
        scratch_shapes=[pltpu.VMEM((C, G * SW_SUB), F32)] * 2,
        compiler_params=_cp(("arbitrary",)),
        name="swa_attn_ctx",
    )(sink_l, qt, k, vt)
    return o_lat, o_ctx


def _route(logits, tri, carry):
    lane = _lane_iota(logits.shape)
    lanef = lane.astype(F32)
    big = float(LANES)
    rowmax = lambda t: jnp.max(t, axis=-1, keepdims=True)
    rowmin = lambda t: jnp.min(t, axis=-1, keepdims=True)
    rowsum = lambda t: jnp.sum(t, axis=-1, keepdims=True)
    is_g = lane < MOE_GROUPS
    mg = rowmax(jnp.where(is_g, logits, -jnp.inf))
    w_g = 1.0 / rowsum(jnp.where(is_g, jnp.exp(logits - mg), 0.0))
    gidx = rowmin(jnp.where(is_g & (logits == mg), lanef, big))
    g0 = MOE_GROUPS + MOE_PER_GROUP * gidx
    in_grp = (lanef >= g0) & (lanef < g0 + MOE_PER_GROUP)
    le = jnp.where(in_grp, logits, -jnp.inf)
    m1 = rowmax(le)
    i1 = rowmin(jnp.where(in_grp & (le == m1), lanef, big))
    le2 = jnp.where(lanef == i1, -jnp.inf, le)
    m2 = rowmax(le2)
    i2 = rowmin(jnp.where(in_grp & (lanef != i1) & (le2 == m2), lanef, big))
    r = jnp.exp(m2 - m1)
    gate1 = w_g / (1.0 + r)
    gate2 = w_g * r / (1.0 + r)
    sel1, sel2 = lanef == i1, lanef == i2
    member = (sel1 | sel2)
    cum = _dot(tri, member.astype(BF16)) + carry
    rank1 = rowsum(jnp.where(sel1, cum, 0.0))
    rank2 = rowsum(jnp.where(sel2, cum, 0.0))
    new_carry = carry + jnp.sum(member.astype(F32), axis=0, keepdims=True)
    rec = jnp.zeros_like(logits)
    for ln, val in ((0, i1 - MOE_GROUPS), (1, i2 - MOE_GROUPS), (2, rank1), (3, rank2), (4, gate1), (5, gate2)):
        rec = jnp.where(lane == ln, val, rec)
    return rec, new_carry


def _out_proj_kernel(*refs, has_bias, n_lat_tiles):
    if has_bias:
        (al_ref, ac_ref, x_ref, w_ref, b_ref, g1_ref, g2n_ref, sc_ref, sh_ref, wr_ref, br_ref, tri_ref,
         xo_ref, h2_ref, rec_ref, rect_ref, cnt_ref, carry_ref) = refs
    else:
        (al_ref, ac_ref, x_ref, w_ref, g1_ref, g2n_ref, sc_ref, sh_ref, wr_ref, br_ref, tri_ref,
         xo_ref, h2_ref, rec_ref, rect_ref, cnt_ref, carry_ref) = refs

    @pl.when(pl.program_id(0) == 0)
    def _():
        carry_ref[...] = jnp.zeros_like(carry_ref)

    a = jnp.where(pl.program_id(0) < n_lat_tiles, al_ref[...], ac_ref[...])
    y = _dot(a.astype(BF16), w_ref[...])
    if has_bias:
        y = y + b_ref[...]
    xn = x_ref[...] + g1_ref[0] * y
    xo_ref[...] = xn
    h2 = _normmod(xn, g2n_ref[...], sc_ref[0], sh_ref[0])
    h2_ref[...] = h2
    h_hi = h2.astype(BF16)
    h_lo = (h2 - h_hi.astype(F32)).astype(BF16)
    hw = _dot(h_hi, wr_ref[...])
    logits = hw[:, :LANES] + hw[:, LANES:] + _dot(h_lo, wr_ref[:, :LANES]) + br_ref[...]
    rec, carry = _route(logits, tri_ref[...], carry_ref[...])
    rec_ref[...] = rec
    rect_ref[...] = rec.T[:SUBLANES, :]
    carry_ref[...] = carry
    cnt_ref[...] = jnp.broadcast_to(carry, cnt_ref.shape)


def _out_proj(lay, a, x, w, b, g2n, modv, wr, br, tri):
    D = lay.D
    a_lat, a_ctx = a
    nl = lay.nl_tiles
    full = lambda t: pl.BlockSpec(t.shape, lambda i: (0,) * t.ndim)
    row = lambda n: pl.BlockSpec((TM, n), lambda i: (i, 0))
    ins = [a_lat, a_ctx, x, w] + ([b] if b is not None else []) + [modv, g2n, modv, modv, wr, br, tri]
    specs = ([pl.BlockSpec((TM, a_lat.shape[1]), lambda i: (jnp.minimum(i, nl - 1), 0)),
              pl.BlockSpec((TM, a_ctx.shape[1]), lambda i: (jnp.maximum(i - nl, 0), 0)), row(D), full(w)]
             + ([full(b)] if b is not None else [])
             + [lay.mod_spec(2), full(g2n), lay.mod_spec(4), lay.mod_spec(3), full(wr), full(br), full(tri)])
    return pl.pallas_call(
        functools.partial(_out_proj_kernel, has_bias=b is not None, n_lat_tiles=nl),
        out_shape=(jax.ShapeDtypeStruct((lay.NT, D), F32), jax.ShapeDtypeStruct((lay.NT, D), F32),
                   jax.ShapeDtypeStruct((lay.NT, LANES), F32), jax.ShapeDtypeStruct((SUBLANES, lay.NT), F32),
                   jax.ShapeDtypeStruct((SUBLANES, LANES), F32)),
        grid=(lay.n_tiles,),
        in_specs=specs,
        out_specs=(row(D), row(D), row(LANES), pl.BlockSpec((SUBLANES, TM), lambda i: (0, i)),
                   pl.BlockSpec((SUBLANES, LANES), lambda i: (0, 0))),
        scratch_shapes=[pltpu.VMEM((1, LANES), F32)],
        input_output_aliases={2: 0},
        compiler_params=_cp(("arbitrary",)),
        name="out_proj_router",
    )(*ins)


def _row_copy(src, s, dst, d, sem):
    return pltpu.make_async_copy(src.at[pl.ds(s, 1), :], dst.at[pl.ds(d, 1), :], sem)


def _dispatch_kernel(pos_ref, pend_ref, nu_ref, h_ref, xb_ref, zbuf, sem, zsem):
    base = pl.program_id(0) * TM
    n_tok = pos_ref.shape[0] // 2
    nblk = xb_ref.shape[0] // MOE_BM

    @pl.when(pl.program_id(0) == 0)
    def _():
        zbuf[...] = jnp.zeros_like(zbuf)

        def zero_block(row0):
            return pltpu.make_async_copy(zbuf, xb_ref.at[pl.ds(pl.multiple_of(row0, MOE_BM), MOE_BM), :], zsem)

        def each(fn):
            def expert(e, c):
                end = pend_ref[e]

                @pl.when(end > jnp.where(e > 0, pend_ref[jnp.maximum(e - 1, 0)], 0))
                def _():
                    fn(zero_block(end - MOE_BM))
                return c

            def tail(j, c):
                @pl.when(j >= nu_ref[0])
                def _():
                    fn(zero_block(j * MOE_BM))
                return c

            lax.fori_loop(0, MOE_EXPERTS, expert, 0)
            lax.fori_loop(0, nblk, tail, 0)

        each(lambda cp: cp.start())
        each(lambda cp: cp.wait())

    def issue(r, c):
        for k in range(2):
            _row_copy(h_ref, r, xb_ref, pos_ref[k * n_tok + base + r], sem).start(priority=k)
        return c

    lax.fori_loop(0, TM, issue, 0, unroll=8)
    for k in range(2):
        pltpu.make_async_copy(h_ref, xb_ref.at[pl.ds(0, TM), :], sem).wait()


def _dispatch(lay, pos, pend, n_used, h2, cap):
    D = lay.D
    return pl.pallas_call(
        _dispatch_kernel,
        out_shape=jax.ShapeDtypeStruct((cap, D), F32),
        grid_spec=pltpu.PrefetchScalarGridSpec(
            num_scalar_prefetch=3, grid=(lay.n_tiles,),
            in_specs=[pl.BlockSpec((TM, D), lambda i, p, e, n: (i, 0))],
            out_specs=pl.BlockSpec(memory_space=pl.ANY),
            scratch_shapes=[pltpu.VMEM((MOE_BM, D), F32), pltpu.SemaphoreType.DMA, pltpu.SemaphoreType.DMA]),
        compiler_params=_cp(("arbitrary",)),
        name="moe_dispatch",
    )(pos, pend, n_used, h2)


def _expert_kernel(be_ref, nu_ref, xb_ref, wg_ref, wu_ref, wd_ref, yb_ref, wgb, wub, wdb):
    j = pl.program_id(0)
    prev = be_ref[jnp.maximum(j - 1, 0)]

    @pl.when((j == 0) | (be_ref[j] != prev))
    def _():
        wgb[...] = wg_ref[0, 0].astype(BF16)
        wub[...] = wu_ref[0, 0].astype(BF16)
        wdb[...] = wd_ref[0, 0].astype(BF16)

    @pl.when(j < nu_ref[0])
    def _():
        xe = xb_ref[...].astype(BF16)
        g = _dot(xe, wgb[...])
        u = _dot(xe, wub[...])
        act = (g * jax.nn.sigmoid(g) * u).astype(BF16)
        yb_ref[...] = _dot(act, wdb[...])

    @pl.when(j >= nu_ref[0])
    def _():
        yb_ref[...] = jnp.zeros_like(yb_ref)


def _experts(xb, blk_e, n_used, w_gate, w_up, w_down, layer):
    cap, D = xb.shape
    FF = w_gate.shape[-1]
    nblk = cap // MOE_BM
    blk = lambda j, be, nu: (jnp.maximum(jnp.minimum(j, nu[0] - 1), 0), 0)
    wblk = lambda j, be, nu: (layer, be[j], 0, 0)
    return pl.pallas_call(
        _expert_kernel,
        out_shape=jax.ShapeDtypeStruct((cap, D), F32),
        grid_spec=pltpu.PrefetchScalarGridSpec(
            num_scalar_prefetch=2, grid=(nblk,),
            in_specs=[pl.BlockSpec((MOE_BM, D), blk),
                      pl.BlockSpec((1, 1, D, FF), wblk), pl.BlockSpec((1, 1, D, FF), wblk),
                      pl.BlockSpec((1, 1, FF, D), wblk)],
            out_specs=pl.BlockSpec((MOE_BM, D), lambda j, be, nu: (j, 0)),
            scratch_shapes=[pltpu.VMEM((D, FF), BF16), pltpu.VMEM((D, FF), BF16), pltpu.VMEM((FF, D), BF16)]),
        compiler_params=_cp(("arbitrary",)),
        name="moe_experts",
    )(blk_e, n_used, xb, w_gate, w_up, w_down)


def _combine_kernel(*refs, final):
    if final:
        pos_ref, x_ref, rec_ref, g2_ref, yb_ref, fg_ref, o_ref, buf, sem = refs
    else:
        pos_ref, x_ref, rec_ref, g2_ref, yb_ref, o_ref, buf, sem = refs
    base = pl.program_id(0) * TM
    n_tok = pos_ref.shape[0] // 2

    def issue(r, c):
        for k in range(2):
            _row_copy(yb_ref, pos_ref[k * n_tok + base + r], buf.at[k], r, sem).start(priority=k)
        return c

    lax.fori_loop(0, TM, issue, 0, unroll=8)
    for k in range(2):
        pltpu.make_async_copy(yb_ref.at[pl.ds(0, TM), :], buf.at[k], sem).wait()
    rec = rec_ref[...]
    f = rec[:, ROUTE_LANE_GATE:ROUTE_LANE_GATE + 1] * buf[0] + rec[:, ROUTE_LANE_GATE + 1:ROUTE_LANE_GATE + 2] * buf[1]
    xn = x_ref[...] + g2_ref[0] * f
    if final:
        xn = _rms(xn, fg_ref[...])
    o_ref[...] = xn


def _combine(lay, pos, x, rec, modv, yb, final_g):
    D = lay.D
    final = final_g is not None
    row = lambda n: pl.BlockSpec((TM, n), lambda i, p: (i, 0))
    specs = [row(D), row(LANES),
             pl.BlockSpec((1, 1, D), lambda i, p: (5 * SUBLANES + lay.mod_row(i), 0, 0)),
             pl.BlockSpec(memory_space=pl.ANY)]
    ins = [x, rec, modv, yb]
    if final:
        specs.append(pl.BlockSpec(final_g.shape, lambda i, p: (0, 0)))
        ins.append(final_g)
    n_rows, n_tiles = (lay.NL, lay.nl_tiles) if final else (lay.NT, lay.n_tiles)
    return pl.pallas_call(
        functools.partial(_combine_kernel, final=final),
        out_shape=jax.ShapeDtypeStruct((n_rows, D), F32),
        grid_spec=pltpu.PrefetchScalarGridSpec(
            num_scalar_prefetch=1, grid=(n_tiles,),
            in_specs=specs,
            out_specs=row(D),
            scratch_shapes=[pltpu.VMEM((2, TM, D), F32), pltpu.SemaphoreType.DMA]),
        input_output_aliases={} if final else {1: 0},
        compiler_params=_cp(("arbitrary",)),
        name="moe_combine",
    )(pos, *ins)


def _moe(lay, x, h2, rec, rect, counts, modv, w_gate, w_up, w_down, layer, final_g):
    T = 2 * lay.NT
    cap = -(-T // MOE_BM) * MOE_BM + MOE_EXPERTS * MOE_BM
    cnt = counts[0, MOE_GROUPS:MOE_GROUPS + MOE_EXPERTS].astype(I32)
    pcnt = (cnt + MOE_BM - 1) // MOE_BM * MOE_BM
    pend = jnp.cumsum(pcnt)
    start = pend - pcnt
    eid = rect[ROUTE_LANE_EID:ROUTE_LANE_EID + 2].astype(I32)
    rank = rect[ROUTE_LANE_RANK:ROUTE_LANE_RANK + 2].astype(I32)
    eid, rank = eid.reshape(1, -1), rank.reshape(-1)
    hit = eid == jnp.arange(MOE_EXPERTS, dtype=I32)[:, None]
    pos = jnp.sum(jnp.where(hit, start[:, None], 0), axis=0) + rank
    blk_row = jnp.arange(cap // MOE_BM, dtype=I32) * MOE_BM
    blk_e = jnp.minimum(jnp.sum((pend[None, :] <= blk_row[:, None]).astype(I32), axis=1), MOE_EXPERTS - 1)
    n_used = (pend[-1:] // MOE_BM).astype(I32)
    xb = _dispatch(lay, pos, pend.astype(I32), n_used, h2, cap)
    yb = _experts(xb, blk_e, n_used, w_gate, w_up, w_down, layer)
    return _combine(lay, pos, x, rec, modv, yb, final_g)


def _rope_tables(lay):
    S = lay.S
    t = jnp.arange(S)
    n = MLA_ROPE // 4
    inv = ROPE_THETA ** (-jnp.arange(n, dtype=F32) / n)
    ang = jnp.concatenate([(t // GRID_W).astype(F32)[:, None] * inv, (t % GRID_W).astype(F32)[:, None] * inv], axis=-1)
    cos, sin = jnp.cos(ang), jnp.sin(ang)
    cos64 = jnp.concatenate([cos, cos], axis=-1)
    sin64 = jnp.concatenate([-sin, sin], axis=-1)
    rows = lambda lat, ctx_val: jnp.concatenate([jnp.tile(lat, (lay.B, 1)), jnp.full((lay.NC, 64), ctx_val, F32)], axis=0)
    cos64, sin64 = rows(cos64, 1.0), rows(sin64, 0.0)
    zero = jnp.zeros_like(cos64)
    return ((jnp.concatenate([cos64, zero], axis=1), jnp.concatenate([sin64, zero], axis=1)),
            (jnp.concatenate([cos64, cos64], axis=1), jnp.concatenate([sin64, sin64], axis=1)))


def kernel(x, c, ctx, c_ctx, mod_w, mod_b, norm1_g, norm2_g, mla_w_dq, mla_g_q, mla_w_uq, mla_w_dkv, mla_g_kv, mla_w_ukv, mla_w_o, fnet_w_o, fnet_b_o, na_w_qkv, na_rpb, na_w_o, swa_w_qkv, swa_sinks, swa_w_o, moe_w_grp, moe_b_grp, moe_w_rt, moe_b_rt, moe_w_gate, moe_w_up, moe_w_down, final_g):
    B, S, D = x.shape
    C = ctx.shape[1]
    depth = mod_w.shape[0]
    lay = _Layout(B, S, C, D)
    X = jnp.concatenate([x.reshape(B * S, D), ctx.reshape(B * C, D)], axis=0)
    cond = jnp.concatenate([c, c_ctx[None], jnp.zeros((SUBLANES - B - 1, D), F32)], axis=0)
    mod = _modulation(cond, mod_w, mod_b)
    (mla_cos, mla_sin), (swa_cos, swa_sin) = _rope_tables(lay)
    tri = (jnp.arange(TM)[:, None] > jnp.arange(TM)[None, :]).astype(BF16)
    n_mix = 4
    for i in range(depth):
        m, j = i % n_mix, i // n_mix
        modv = mod[i].reshape(SUBLANES, 6, D).transpose(1, 0, 2).reshape(6 * SUBLANES, 1, D)
        g1n, g2n = norm1_g[i][None], norm2_g[i][None]
        bias = None
        if m == 0:
            w1 = jnp.concatenate([mla_w_dq[j], mla_w_dkv[j], jnp.zeros((D, LANES - MLA_ROPE), F32)], axis=1).astype(BF16)
            wq = mla_w_uq[j].reshape(MLA_Q_RANK, MLA_HEADS, MLA_NOPE + MLA_ROPE)
            wq = jnp.concatenate([wq, jnp.zeros((MLA_Q_RANK, MLA_HEADS, LANES - MLA_ROPE), F32)], axis=-1)
            wq = wq.reshape(MLA_Q_RANK, MLA_HEADS * 2 * LANES).astype(BF16)
            q, k, v = _mla_proj(lay, X, g1n, modv, w1, mla_g_q[j][None], mla_g_kv[j][None], wq,
                                mla_w_ukv[j].astype(BF16), mla_cos, mla_sin)
            a = _mla_attention(lay, q, k, v)
            w_o = mla_w_o[j]
        elif m == 1:
            a = _fnet_mix(lay, X, g1n, modv, _dft_tables(S, C, D // FNET_GROUPS))
            w_o, bias = fnet_w_o[j], fnet_b_o[j][None]
        elif m == 2:
            qt, k, vt = _qkv_proj(lay, X, g1n, modv, na_w_qkv[j].astype(BF16), swa_cos, swa_sin,
                                  n_q=NA_HEADS * NA_HD, n_k=NA_HEADS * NA_HD, rope=False,
                                  q_scale=NA_HD ** -0.5 * math.log2(math.e))
            a = _na_attention(lay, qt, k, vt, _na_bias_tables(na_rpb[j], S // GRID_W))
            w_o = na_w_o[j]
        else:
            qt, k, vt = _qkv_proj(lay, X, g1n, modv, swa_w_qkv[j].astype(BF16), swa_cos, swa_sin,
                                  n_q=SW_HEADS * SW_HD, n_k=SW_KV_HEADS * SW_HD, rope=True,
                                  q_scale=SW_HD ** -0.5 * math.log2(math.e))
            a = _swa_attention(lay, qt, k, vt, swa_sinks[j])
            w_o = swa_w_o[j]
        wr = jnp.concatenate([moe_w_grp[i], moe_w_rt[i], jnp.zeros((D, LANES - MOE_GROUPS - MOE_EXPERTS), F32)], axis=1)
        br = jnp.concatenate([moe_b_grp[i], moe_b_rt[i], jnp.zeros((LANES - MOE_GROUPS - MOE_EXPERTS,), F32)])[None]
        wr_hi = wr.astype(BF16)
        wr2 = jnp.concatenate([wr_hi, (wr - wr_hi.astype(F32)).astype(BF16)], axis=1)
        X, h2, rec, rect, counts = _out_proj(lay, a, X, w_o.astype(BF16), bias, g2n, modv, wr2, br, tri)
        X = _moe(lay, X, h2, rec, rect, counts, modv, moe_w_gate, moe_w_up, moe_w_down, i,
                 final_g[None] if i == depth - 1 else None)
    return X.reshape(B, S, D)
```

```python
import functools
import math

import jax
import jax.numpy as jnp
import numpy as np
from jax import lax
from jax.experimental import pallas as pl
from jax.experimental.pallas import tpu as pltpu

F32 = jnp.float32
BF16 = jnp.bfloat16
I32 = jnp.int32
HIGHEST = lax.Precision.HIGHEST

GRID_W = 64
EPS = 1e-6
ROPE_THETA = 10000.0
NEG_INF = -1e30
MLA_HEADS, MLA_Q_RANK, MLA_KV_RANK, MLA_NOPE, MLA_ROPE, MLA_V = 8, 512, 256, 128, 64, 128
MLA_VT_ROWS = MLA_V + 16
MLA_UNROLL = 8
FNET_GROUPS = 4
NA_HEADS, NA_HD, NA_KR, NA_KC = 16, 64, 8, 16
SW_HEADS, SW_KV_HEADS, SW_HD, SW_WINDOW = 16, 4, 64, 128
MOE_GROUPS, MOE_PER_GROUP, MOE_FF = 4, 8, 512
MOE_EXPERTS = MOE_GROUPS * MOE_PER_GROUP

LANES = 128
SUBLANES = 8
TM = 512
MOE_BM = 512
VMEM_LIMIT = 56 * 1024 * 1024
ROUTE_LANE_EID, ROUTE_LANE_RANK, ROUTE_LANE_GATE = 0, 2, 4


def _cp(sem, vmem=VMEM_LIMIT):
    return pltpu.CompilerParams(dimension_semantics=sem, vmem_limit_bytes=vmem)


def _lane_iota(shape):
    return lax.broadcasted_iota(I32, shape, len(shape) - 1)


def _normmod(x, g, sc, sh):
    ms = jnp.mean(x * x, axis=-1, keepdims=True)
    return (x * lax.rsqrt(ms + EPS) * g) * (1.0 + sc) + sh


def _rms(x, g):
    ms = jnp.mean(x * x, axis=-1, keepdims=True)
    return x * lax.rsqrt(ms + EPS) * g


def _swap_halves(t, period):
    n = t.shape[-1]
    half = period // 2
    lane = _lane_iota(t.shape)
    return jnp.where((lane % period) < half, pltpu.roll(t, n - half, 1), pltpu.roll(t, half, 1))


def _dot(a, b):
    return jnp.dot(a, b, preferred_element_type=F32)


def _mod_kernel(a_ref, w_ref, b_ref, o_ref):
    a = a_ref[...]
    a = a * jax.nn.sigmoid(a)
    o_ref[0] = jnp.dot(a, w_ref[0], precision=HIGHEST, preferred_element_type=F32) + b_ref[0]


def _modulation(cond, mod_w, mod_b):
    depth, d, n = mod_w.shape
    tn = n // 4
    return pl.pallas_call(
        _mod_kernel,
        out_shape=jax.ShapeDtypeStruct((depth, SUBLANES, n), F32),
        grid=(depth, n // tn),
        in_specs=[pl.BlockSpec((SUBLANES, d), lambda l, j: (0, 0)),
                  pl.BlockSpec((1, d, tn), lambda l, j: (l, 0, j)),
                  pl.BlockSpec((1, 1, tn), lambda l, j: (l, 0, j))],
        out_specs=pl.BlockSpec((1, SUBLANES, tn), lambda l, j: (l, 0, j)),
        compiler_params=_cp(("arbitrary", "arbitrary")),
        name="modulation",
    )(cond, mod_w, mod_b.reshape(depth, 1, n))


class _Layout:
    def __init__(self, B, S, C, D):
        self.B, self.S, self.C, self.D = B, S, C, D
        self.NL, self.NC = B * S, B * C
        self.NT = self.NL + self.NC
        assert S % TM == 0 and self.NC % TM == 0 and TM % C == 0
        self.nl_tiles = self.NL // TM
        self.n_tiles = self.NT // TM
        self.tiles_per_batch = S // TM

    def mod_row(self, i):
        return jnp.where(i < self.nl_tiles, i // self.tiles_per_batch, self.B)

    def mod_spec(self, chunk):
        return pl.BlockSpec((1, 1, self.D), lambda i: (chunk * SUBLANES + self.mod_row(i), 0, 0))


def _mla_proj_kernel(x_ref, g_ref, sc_ref, sh_ref, w1_ref, gq_ref, gkv_ref, wq_ref, wkv_ref, cos_ref, sin_ref,
                     qt_ref, k_ref, vt_ref):
    h = _normmod(x_ref[...], g_ref[...], sc_ref[0], sh_ref[0]).astype(BF16)
    a = _dot(h, w1_ref[...])
    qa = _rms(a[:, :MLA_Q_RANK], gq_ref[...]).astype(BF16)
    ckv = _rms(a[:, MLA_Q_RANK:MLA_Q_RANK + MLA_KV_RANK], gkv_ref[...]).astype(BF16)
    cos, sin = cos_ref[...], sin_ref[...]

    def rope(t):
        return t * cos + _swap_halves(t, MLA_ROPE) * sin

    kr = rope(a[:, MLA_Q_RANK + MLA_KV_RANK:]).astype(BF16)
    scale = (MLA_NOPE + MLA_ROPE) ** -0.5 * math.log2(math.e)
    q = _dot(qa, wq_ref[...])
    kv = _dot(ckv, wkv_ref[...])
    ones = jnp.ones((MLA_VT_ROWS - MLA_V, x_ref.shape[0]), BF16)
    for hd in range(MLA_HEADS):
        c = hd * 2 * LANES
        qh = jnp.concatenate([q[:, c:c + LANES], rope(q[:, c + LANES:c + 2 * LANES])], axis=1) * scale
        qt_ref[hd] = qh.T.astype(BF16)
        k_ref[:, c:c + LANES] = kv[:, c:c + LANES].astype(BF16)
        k_ref[:, c + LANES:c + 2 * LANES] = kr
        vt_ref[hd, :MLA_V, :] = kv[:, c + LANES:c + 2 * LANES].T.astype(BF16)
        vt_ref[hd, MLA_V:, :] = ones


def _mla_proj(lay, x, g, modv, w1, gq, gkv, wq, wkv, cos, sin):
    D = lay.D
    full = lambda a: pl.BlockSpec(a.shape, lambda i: (0,) * a.ndim)
    row = lambda n: pl.BlockSpec((TM, n), lambda i: (i, 0))
    col = lambda r: pl.BlockSpec((MLA_HEADS, r, TM), lambda i: (0, 0, i))
    hq = MLA_HEADS * 2 * LANES
    return pl.pallas_call(
        _mla_proj_kernel,
        out_shape=(jax.ShapeDtypeStruct((MLA_HEADS, 2 * LANES, lay.NT), BF16),
                   jax.ShapeDtypeStruct((lay.NT, hq), BF16),
                   jax.ShapeDtypeStruct((MLA_HEADS, MLA_VT_ROWS, lay.NT), BF16)),
        grid=(lay.n_tiles,),
        in_specs=[row(D), full(g), lay.mod_spec(1), lay.mod_spec(0), full(w1), full(gq), full(gkv), full(wq),
                  full(wkv), row(LANES), row(LANES)],
        out_specs=(col(2 * LANES), row(hq), col(MLA_VT_ROWS)),
        compiler_params=_cp(("arbitrary",)),
        name="mla_proj",
    )(x, g, modv, modv, w1, gq, gkv, wq, wkv, cos, sin)


def _qkv_proj_kernel(x_ref, g_ref, sc_ref, sh_ref, w_ref, cos_ref, sin_ref, qt_ref, k_ref, vt_ref, *,
                     n_q, n_k, rope, q_scale, chunk):
    h = _normmod(x_ref[...], g_ref[...], sc_ref[0], sh_ref[0]).astype(BF16)
    n = w_ref.shape[1]
    for c0 in range(0, n, chunk):
        a = _dot(h, w_ref[:, c0:c0 + chunk])
        if rope and c0 < n_q + n_k:
            reps = chunk // LANES
            cos = jnp.concatenate([cos_ref[...]] * reps, axis=1)
            sin = jnp.concatenate([sin_ref[...]] * reps, axis=1)
            a = a * cos + _swap_halves(a, SW_HD) * sin
        if c0 < n_q:
            qt_ref[c0:c0 + chunk, :] = (a * q_scale).T.astype(BF16)
        elif c0 < n_q + n_k:
            k_ref[:, c0 - n_q:c0 - n_q + chunk] = a.astype(BF16)
        else:
            c = c0 - n_q - n_k
            vt_ref[c:c + chunk, :] = a.T.astype(BF16)


def _qkv_proj(lay, x, g, modv, w, cos, sin, *, n_q, n_k, rope, q_scale, chunk=256):
    D, n = lay.D, w.shape[1]
    n_v = n - n_q - n_k
    assert n_q % chunk == 0 and n_k % chunk == 0 and n_v % chunk == 0
    full = lambda a: pl.BlockSpec(a.shape, lambda i: (0,) * a.ndim)
    row = lambda m: pl.BlockSpec((TM, m), lambda i: (i, 0))
    col = lambda m: pl.BlockSpec((m, TM), lambda i: (0, i))
    return pl.pallas_call(
        functools.partial(_qkv_proj_kernel, n_q=n_q, n_k=n_k, rope=rope, q_scale=q_scale, chunk=chunk),
        out_shape=(jax.ShapeDtypeStruct((n_q, lay.NT), BF16), jax.ShapeDtypeStruct((lay.NT, n_k), BF16),
                   jax.ShapeDtypeStruct((n_v, lay.NT), BF16)),
        grid=(lay.n_tiles,),
        in_specs=[row(D), full(g), lay.mod_spec(1), lay.mod_spec(0), full(w), row(LANES), row(LANES)],
        out_specs=(col(n_q), row(n_k), col(n_v)),
        compiler_params=_cp(("arbitrary",)),
        name="qkv_proj",
    )(x, g, modv, modv, w, cos, sin)


def _mla_attn_kernel(*refs, tk, n_lat):
    if n_lat:
        qt_ref, kc_ref, vtc_ref, kl_ref, vtl_ref, o_ref, acc_ref, sa_ref, sb_ref = refs
    else:
        qt_ref, kc_ref, vtc_ref, o_ref, acc_ref = refs
    qt = qt_ref[0]

    st = _dot(kc_ref[...], qt)
    if n_lat:
        sa_ref[...] = _dot(kl_ref[pl.ds(0, tk), :], qt)
    m = jnp.max(st, axis=0, keepdims=True)
    acc_ref[...] = _dot(vtc_ref[0], jnp.exp2(st - m).astype(BF16))

    if n_lat:
        nch = n_lat // tk

        def softmax_pv(st, vt, m):
            m_new = jnp.maximum(m, jnp.max(st, axis=0, keepdims=True))
            acc_ref[...] = jnp.exp2(m - m_new) * acc_ref[...] + _dot(vt, jnp.exp2(st - m_new).astype(BF16))
            return m_new

        def body(jj, m):
            r0 = pl.multiple_of(2 * jj * tk, tk)
            r1 = pl.multiple_of((2 * jj + 1) * tk, tk)
            r2 = pl.multiple_of(jnp.minimum(2 * jj + 2, nch - 1) * tk, tk)
            sb_ref[...] = _dot(kl_ref[pl.ds(r1, tk), :], qt)
            m = softmax_pv(sa_ref[...], vtl_ref[0, :, pl.ds(r0, tk)], m)
            sa_ref[...] = _dot(kl_ref[pl.ds(r2, tk), :], qt)
            return softmax_pv(sb_ref[...], vtl_ref[0, :, pl.ds(r1, tk)], m)

        lax.fori_loop(0, nch // 2, body, m, unroll=math.gcd(nch // 2, MLA_UNROLL))
    o_ref[...] = (acc_ref[:MLA_V, :] / acc_ref[MLA_V:MLA_V + 1, :]).T.astype(o_ref.dtype)


def _mla_attention(lay, qt, k, vt, *, tq=1024, tk=512):
    B, S, C = lay.B, lay.S, lay.C
    H = MLA_HEADS
    nq = S // tq
    cblk0 = lay.NL // C
    assert S % (2 * tk) == 0
    o_lat = pl.pallas_call(
        functools.partial(_mla_attn_kernel, tk=tk, n_lat=S),
        out_shape=jax.ShapeDtypeStruct((lay.NL, H * LANES), BF16),
        grid=(B, H, nq),
        in_specs=[pl.BlockSpec((1, 2 * LANES, tq), lambda b, h, i: (h, 0, b * nq + i)),
                  pl.BlockSpec((C, 2 * LANES), lambda b, h, i: (cblk0 + b, h)),
                  pl.BlockSpec((1, MLA_VT_ROWS, C), lambda b, h, i: (h, 0, cblk0 + b)),
                  pl.BlockSpec((S, 2 * LANES), lambda b, h, i: (b, h)),
                  pl.BlockSpec((1, MLA_VT_ROWS, S), lambda b, h, i: (h, 0, b))],
        out_specs=pl.BlockSpec((tq, LANES), lambda b, h, i: (b * nq + i, h)),
        scratch_shapes=[pltpu.VMEM((MLA_VT_ROWS, tq), F32), pltpu.VMEM((tk, tq), F32), pltpu.VMEM((tk, tq), F32)],
        compiler_params=_cp(("arbitrary", "arbitrary", "arbitrary")),
        name="mla_attn_latent",
    )(qt, k, vt, k, vt)
    o_ctx = pl.pallas_call(
        functools.partial(_mla_attn_kernel, tk=tk, n_lat=0),
        out_shape=jax.ShapeDtypeStruct((lay.NC, H * LANES), BF16),
        grid=(B, H),
        in_specs=[pl.BlockSpec((1, 2 * LANES, C), lambda b, h: (h, 0, cblk0 + b)),
                  pl.BlockSpec((C, 2 * LANES), lambda b, h: (cblk0 + b, h)),
                  pl.BlockSpec((1, MLA_VT_ROWS, C), lambda b, h: (h, 0, cblk0 + b))],
        out_specs=pl.BlockSpec((C, LANES), lambda b, h: (b, h)),
        scratch_shapes=[pltpu.VMEM((MLA_VT_ROWS, C), F32)],
        compiler_params=_cp(("arbitrary", "arbitrary")),
        name="mla_attn_ctx",
    )(qt, k, vt)
    return o_lat, o_ctx


def _dft_tables(S, C, gc):
    P = math.isqrt(S)
    assert P * P == S and (P & (P - 1)) == 0 and (gc & (gc - 1)) == 0 and (C & (C - 1)) == 0

    def cs(idx, n):
        ang = (idx % n).astype(F32) * (2.0 * math.pi / n)
        return jnp.cos(ang), jnp.sin(ang)

    k1 = jnp.arange(P, dtype=I32)
    idx = k1[None, :, None] * (P * k1[None, None, :] + k1[:, None, None])
    c, s = cs(idx, S)
    m1 = jnp.concatenate([c, -s], axis=1) * (1.0 / P)
    c, s = cs(k1[:, None] * k1[None, :], P)
    m2 = jnp.concatenate([jnp.concatenate([c, s], axis=1), jnp.concatenate([-s, c], axis=1)], axis=0)
    kc = jnp.arange(gc, dtype=I32)
    c, s = cs(kc[:, None] * kc[None, :], gc)
    mc = jnp.concatenate([c, s], axis=0) * (gc ** -0.5)
    kq = jnp.arange(C, dtype=I32)
    c, s = cs(kq[:, None] * kq[None, :], C)
    mctx = jnp.concatenate([c, s], axis=0) * (C ** -0.5)
    return m1.astype(BF16), m2.astype(BF16), mc.astype(BF16), mctx.astype(BF16)


def _fnet_stage1_kernel(x_ref, g_ref, sc_ref, sh_ref, m1_ref, z_ref, *, n2c, P):
    g, sc, sh = g_ref[...], sc_ref[0], sh_ref[0]
    for j in range(n2c):
        h = _normmod(x_ref[:, j, :], g, sc, sh).astype(BF16)
        z = _dot(m1_ref[j], h)
        z_ref[0, :, 0, j, :] = z[:P]
        z_ref[0, :, 1, j, :] = z[P:]


def _fnet_stage2_kernel(z_ref, m2_ref, mc_ref, f_ref, *, k1c, P, gc):
    D = f_ref.shape[-1]
    for j in range(k1c):
        z = z_ref[0, j].reshape(2 * P, D).astype(BF16)
        y = _dot(m2_ref[...], z)
        yr, yi = y[:P].astype(BF16), y[P:].astype(BF16)
        outs = []
        for gi in range(D // gc):
            sl = slice(gi * gc, (gi + 1) * gc)
            outs.append(_dot(yr[:, sl], mc_ref[:gc, :]) + _dot(yi[:, sl], mc_ref[gc:, :]))
        f_ref[:, j, :] = jnp.concatenate(outs, axis=1)


def _fnet_ctx_kernel(x_ref, g_ref, sc_ref, sh_ref, ml_ref, mc_ref, f_ref, *, C, gc):
    D = x_ref.shape[-1]
    h = _normmod(x_ref[...], g_ref[...], sc_ref[0], sh_ref[0]).astype(BF16)
    y = _dot(ml_ref[...], h)
    yc, ys = y[:C].astype(BF16), y[C:].astype(BF16)
    outs = []
    for gi in range(D // gc):
        sl = slice(gi * gc, (gi + 1) * gc)
        outs.append(_dot(yc[:, sl], mc_ref[:gc, :]) - _dot(ys[:, sl], mc_ref[gc:, :]))
    f_ref[...] = jnp.concatenate(outs, axis=1)


def _fnet_mix(lay, x, g, modv, tables):
    B, S, C, D = lay.B, lay.S, lay.C, lay.D
    m1, m2, mc, mctx = tables
    P = math.isqrt(S)
    gc = D // FNET_GROUPS
    n2c = SUBLANES
    k1c = SUBLANES
    full = lambda a: pl.BlockSpec(a.shape, lambda *i: (0,) * a.ndim)
    modspec = lambda chunk: pl.BlockSpec((1, 1, D), lambda b, j: (chunk * SUBLANES + b, 0, 0))
    assert C % P == 0 and P % n2c == 0 and P % k1c == 0
    x3 = x.reshape(lay.NT // P, P, D)
    z = pl.pallas_call(
        functools.partial(_fnet_stage1_kernel, n2c=n2c, P=P),
        out_shape=jax.ShapeDtypeStruct((B, P, 2, P, D), F32),
        grid=(B, P // n2c),
        in_specs=[pl.BlockSpec((P, n2c, D), lambda b, j: (b, j, 0)), full(g), modspec(1), modspec(0),
                  pl.BlockSpec((n2c, 2 * P, P), lambda b, j: (j, 0, 0))],
        out_specs=pl.BlockSpec((1, P, 2, n2c, D), lambda b, j: (b, 0, 0, j, 0)),
        compiler_params=_cp(("arbitrary", "arbitrary")),
        name="fnet_stage1",
    )(x3, g, modv, modv, m1)
    f_lat = pl.pallas_call(
        functools.partial(_fnet_stage2_kernel, k1c=k1c, P=P, gc=gc),
        out_shape=jax.ShapeDtypeStruct((lay.NL // P, P, D), F32),
        grid=(B, P // k1c),
        in_specs=[pl.BlockSpec((1, k1c, 2, P, D), lambda b, j: (b, j, 0, 0, 0)), full(m2), full(mc)],
        out_specs=pl.BlockSpec((P, k1c, D), lambda b, j: (b, j, 0)),
        compiler_params=_cp(("arbitrary", "arbitrary")),
        name="fnet_stage2",
    )(z, m2, mc)
    cblk0 = lay.NL // C
    ctx_mod = lambda chunk: pl.BlockSpec((1, 1, D), lambda b: (chunk * SUBLANES + B, 0, 0))
    f_ctx = pl.pallas_call(
        functools.partial(_fnet_ctx_kernel, C=C, gc=gc),
        out_shape=jax.ShapeDtypeStruct((lay.NC, D), F32),
        grid=(B,),
        in_specs=[pl.BlockSpec((C, D), lambda b: (cblk0 + b, 0)), full(g), ctx_mod(1), ctx_mod(0), full(mctx),
                  full(mc)],
        out_specs=pl.BlockSpec((C, D), lambda b: (b, 0)),
        compiler_params=_cp(("arbitrary",)),
        name="fnet_ctx",
    )(x, g, modv, modv, mctx, mc)
    return f_lat.reshape(lay.NL, D), f_ctx


NA_QROWS = 2
NA_KROWS = NA_KR + NA_QROWS
NA_VARIANTS = 5
NA_ONES = 16


def _na_bias_kernel(rpb_ref, sel_ref, toe_ref, o_ref):
    g = jnp.dot(rpb_ref[0], toe_ref[...], precision=HIGHEST, preferred_element_type=F32)
    for t in range(NA_VARIANTS):
        o_ref[t, 0] = jnp.dot(sel_ref[t], g, precision=HIGHEST, preferred_element_type=F32)


def _na_window_start(r, rows):
    return jnp.clip(r - NA_KR // 2, 0, rows - NA_KROWS)


def _na_bias_tables(rpb, rows):
    assert rows >= 16 and rows % NA_QROWS == 0
    W = GRID_W
    nh, nu, nv = rpb.shape
    up, vp, ajp = 2 * SUBLANES, LANES, 3 * SUBLANES
    assert nu <= up and nv <= vp and NA_QROWS * NA_KROWS <= ajp
    cq, ck = np.arange(W)[:, None], np.arange(W)[None, :]
    dc = np.clip(ck - cq + NA_KC - 1, 0, nv - 1).reshape(-1)
    toe = (np.arange(vp)[:, None] == dc[None, :]).astype(np.float32)
    cs = np.clip(cq - NA_KC // 2, 0, W - NA_KC)
    col_ok = (ck >= cs) & (ck < cs + NA_KC)
    sel = np.zeros((NA_VARIANTS, ajp, up), np.float32)
    valid = np.zeros((NA_VARIANTS, NA_QROWS, NA_KROWS, W, W), bool)
    for t, r in enumerate((0, 2, 6, rows - 4, rows - 2)):
        w0 = min(max(r - NA_KR // 2, 0), rows - NA_KROWS)
        for a in range(NA_QROWS):
            rs = min(max(r + a - NA_KR // 2, 0), rows - NA_KR)
            for j in range(NA_KROWS):
                rk = w0 + j
                sel[t, a * NA_KROWS + j, min(max(rk - (r + a) + NA_KR - 1, 0), nu - 1)] = 1.0
                if rs <= rk < rs + NA_KR:
                    valid[t, a, j] = col_ok
    rpb_p = jnp.pad(rpb, ((0, 0), (0, up - nu), (0, vp - nv)))
    tab = pl.pallas_call(
        _na_bias_kernel,
        out_shape=jax.ShapeDtypeStruct((NA_VARIANTS, nh, ajp, W * W), F32),
        grid=(nh,),
        in_specs=[pl.BlockSpec((1, up, vp), lambda h: (h, 0, 0)),
                  pl.BlockSpec(sel.shape, lambda h: (0, 0, 0)),
                  pl.BlockSpec(toe.shape, lambda h: (0, 0))],
        out_specs=pl.BlockSpec((NA_VARIANTS, 1, ajp, W * W), lambda h: (0, h, 0, 0)),
        compiler_params=_cp(("arbitrary",)),
        name="na_bias",
    )(rpb_p, jnp.asarray(sel), jnp.asarray(toe))
    tab = tab[:, :, :NA_QROWS * NA_KROWS].reshape(NA_VARIANTS, nh, NA_QROWS, NA_KROWS, W, W)
    tab = jnp.where(jnp.asarray(valid)[:, None], tab * math.log2(math.e), NEG_INF)
    tab = tab.reshape(NA_VARIANTS, nh // 2, 2, NA_QROWS, NA_KROWS, W, W)
    return tab.transpose(0, 1, 4, 6, 2, 3, 5).reshape(NA_VARIANTS, nh // 2, NA_KROWS * W, 2 * NA_QROWS * W)


def _na_kernel(*refs, rows, local):
    if local:
        qt_ref, kc_ref, vtc_ref, kl_ref, vtl_ref, bias_ref, o_ref, st_a, sc_a, st_b, sc_b = refs
    else:
        qt_ref, kc_ref, vtc_ref, o_ref, sc_a, sc_b = refs
        st_a = st_b = None
    nq = NA_QROWS * GRID_W
    nk = NA_KROWS * GRID_W
    kc, vtc = kc_ref[...], vtc_ref[...]
    row = lax.broadcasted_iota(I32, (LANES, nq), 0)
    pairs = qt_ref.shape[1] // nq
    qi = pl.program_id(2) if local else 0

    def ext(vt):
        return jnp.concatenate([vt, jnp.ones((NA_ONES, vt.shape[1]), BF16)], axis=0)

    def window(t):
        r = (qi * pairs + t) * NA_QROWS
        k0 = pl.multiple_of(_na_window_start(r, rows) * GRID_W, LANES)
        var = jnp.where(r == 0, 0, jnp.where(r == 2, 1, jnp.where(r == rows - 4, 3, jnp.where(r == rows - 2, 4, 2))))
        return k0, var

    def scores(t, st_ref, sc_ref):
        qt = qt_ref[:, t * nq:(t + 1) * nq]
        zero = jnp.zeros_like(qt)
        qbd = jnp.concatenate([jnp.where(row < NA_HD, qt, zero), jnp.where(row >= NA_HD, qt, zero)], axis=1)
        sc_ref[...] = _dot(kc, qbd)
        if local:
            k0, var = window(t)
            st_ref[...] = _dot(kl_ref[pl.ds(k0, nk), :], qbd) + bias_ref[var, 0]

    def finish(t, st_ref, sc_ref):
        sc = sc_ref[...]
        m = jnp.max(sc, axis=0, keepdims=True)
        if local:
            st = st_ref[...]
            m = jnp.maximum(m, jnp.max(st, axis=0, keepdims=True))
            pt = jnp.exp2(st - m).astype(BF16)
            vtw = vtl_ref[:, pl.ds(window(t)[0], nk)]
        pc = jnp.exp2(sc - m).astype(BF16)
        outs = []
        for hl in range(2):
            hs, qs = slice(hl * NA_HD, (hl + 1) * NA_HD), slice(hl * nq, (hl + 1) * nq)
            acc = _dot(ext(vtc[hs, :]), pc[:, qs])
            if local:
                acc = acc + _dot(ext(vtw[hs, :]), pt[:, qs])
            outs.append(acc[:NA_HD] / acc[NA_HD:NA_HD + 1])
        o_ref[t * nq:(t + 1) * nq, :] = jnp.concatenate(outs, axis=0).T.astype(o_ref.dtype)

    bufs = ((st_a, sc_a), (st_b, sc_b))
    scores(0, *bufs[0])
    for t in range(pairs):
        if t + 1 < pairs:
            scores(t + 1, *bufs[(t + 1) % 2])
        finish(t, *bufs[t % 2])


def _na_attention(lay, qt, k, vt, bias, *, tq=1024):
    B, S, C = lay.B, lay.S, lay.C
    rows = S // GRID_W
    HP = NA_HEADS // 2
    nq = S // tq
    cblk0 = lay.NL // C
    nk, nqp = NA_KROWS * GRID_W, NA_QROWS * GRID_W
    o_lat = pl.pallas_call(
        functools.partial(_na_kernel, rows=rows, local=True),
        out_shape=jax.ShapeDtypeStruct((lay.NL, NA_HEADS * NA_HD), BF16),
        grid=(B, HP, nq),
        in_specs=[pl.BlockSpec((LANES, tq), lambda b, h, i: (h, b * nq + i)),
                  pl.BlockSpec((C, LANES), lambda b, h, i: (cblk0 + b, h)),
                  pl.BlockSpec((LANES, C), lambda b, h, i: (h, cblk0 + b)),
                  pl.BlockSpec((S, LANES), lambda b, h, i: (b, h)),
                  pl.BlockSpec((LANES, S), lambda b, h, i: (h, b)),
                  pl.BlockSpec((NA_VARIANTS, 1) + bias.shape[2:], lambda b, h, i: (0, h, 0, 0))],
        out_specs=pl.BlockSpec((tq, LANES), lambda b, h, i: (b * nq + i, h)),
        scratch_shapes=[pltpu.VMEM((nk, 2 * nqp), F32), pltpu.VMEM((C, 2 * nqp), F32)] * 2,
        compiler_params=_cp(("arbitrary", "arbitrary", "arbitrary")),
        name="na_attn_latent",
    )(qt, k, vt, k, vt, bias)
    o_ctx = pl.pallas_call(
        functools.partial(_na_kernel, rows=rows, local=False),
        out_shape=jax.ShapeDtypeStruct((lay.NC, NA_HEADS * NA_HD), BF16),
        grid=(B, HP),
        in_specs=[pl.BlockSpec((LANES, C), lambda b, h: (h, cblk0 + b)),
                  pl.BlockSpec((C, LANES), lambda b, h: (cblk0 + b, h)),
                  pl.BlockSpec((LANES, C), lambda b, h: (h, cblk0 + b))],
        out_specs=pl.BlockSpec((C, LANES), lambda b, h: (b, h)),
        scratch_shapes=[pltpu.VMEM((C, 2 * nqp), F32)] * 2,
        compiler_params=_cp(("arbitrary", "arbitrary")),
        name="na_attn_ctx",
    )(qt, k, vt)
    return o_lat, o_ctx


SW_SUB = 128
SW_BAND = SW_SUB + 2 * SW_WINDOW
SW_ONES = 16


def _swa_kernel(*refs, S, local):
    if local:
        (sink_ref, qt_ref, kc_ref, vtc_ref, kp_ref, kcur_ref, kn_ref, vtp_ref, vtcur_ref, vtn_ref, o_ref,
         kbuf, vtbuf, st_a, sc_a, st_b, sc_b) = refs
        tq = qt_ref.shape[1]
        W = SW_WINDOW
        kbuf[0:W] = kp_ref[...]
        kbuf[W:W + tq] = kcur_ref[...]
        kbuf[W + tq:] = kn_ref[...]
        vtbuf[:, 0:W] = vtp_ref[...]
        vtbuf[:, W:W + tq] = vtcur_ref[...]
        vtbuf[:, W + tq:] = vtn_ref[...]
        i = pl.program_id(1)
    else:
        sink_ref, qt_ref, kc_ref, vtc_ref, o_ref, sc_a, sc_b = refs
        st_a = st_b = None
        tq = qt_ref.shape[1]
    G = SW_HEADS // SW_KV_HEADS
    kc, vtc = kc_ref[...], vtc_ref[...]
    zeros_q = jnp.zeros((SW_HD, G * SW_SUB), BF16)

    def ext(vt):
        return jnp.concatenate([vt, jnp.ones((SW_ONES, vt.shape[1]), BF16)], axis=0)

    def band_bias(sb):
        key = lax.broadcasted_iota(I32, (SW_BAND, SW_SUB), 0)
        qry = lax.broadcasted_iota(I32, (SW_BAND, SW_SUB), 1)
        rel = key - SW_WINDOW - qry
        kpos = i * tq + sb * SW_SUB - SW_WINDOW + key
        ok = (jnp.abs(rel) <= SW_WINDOW) & (kpos >= 0) & (kpos < S)
        bias = jnp.where(ok, 0.0, NEG_INF).astype(F32)
        return jnp.concatenate([bias] * G, axis=1)

    def scores(sb, g, bias, st_ref, sc_ref):
        r0 = sb * SW_SUB
        tile, half = g // 2, g % 2
        sl = slice(tile * LANES, (tile + 1) * LANES)
        qg = jnp.concatenate([qt_ref[(G * g + hl) * SW_HD:(G * g + hl + 1) * SW_HD, r0:r0 + SW_SUB]
                              for hl in range(G)], axis=1)
        qpad = jnp.concatenate([qg, zeros_q] if half == 0 else [zeros_q, qg], axis=0)
        sc_ref[...] = _dot(kc[:, sl], qpad)
        if local:
            st_ref[...] = _dot(kbuf[r0:r0 + SW_BAND, sl], qpad) + bias

    def finish(sb, g, st_ref, sc_ref):
        r0 = sb * SW_SUB
        sink = sink_ref[g]
        sc = sc_ref[...]
        m = jnp.maximum(jnp.max(sc, axis=0, keepdims=True), sink)
        if local:
            st = st_ref[...]
            m = jnp.maximum(m, jnp.max(st, axis=0, keepdims=True))
        acc = _dot(ext(vtc[g * SW_HD:(g + 1) * SW_HD, :]), jnp.exp2(sc - m).astype(BF16))
        if local:
            acc = acc + _dot(ext(vtbuf[g * SW_HD:(g + 1) * SW_HD, r0:r0 + SW_BAND]), jnp.exp2(st - m).astype(BF16))
        og = acc[:SW_HD] / (acc[SW_HD:SW_HD + 1] + jnp.exp2(sink - m))
        return [og[:, hl * SW_SUB:(hl + 1) * SW_SUB] for hl in range(G)]

    items = [(sb, g) for sb in range(tq // SW_SUB) for g in range(SW_KV_HEADS)]
    bufs = ((st_a, sc_a), (st_b, sc_b))
    bias = band_bias(0) if local else None
    scores(*items[0], bias, *bufs[0])
    outs = []
    for n, (sb, g) in enumerate(items):
        if n + 1 < len(items):
            nsb, ng = items[n + 1]
            if local and ng == 0:
                bias = band_bias(nsb)
            scores(nsb, ng, bias, *bufs[(n + 1) % 2])
        outs.extend(finish(sb, g, *bufs[n % 2]))
        if g == SW_KV_HEADS - 1:
            o_ref[sb * SW_SUB:(sb + 1) * SW_SUB, :] = jnp.concatenate(outs, axis=0).T.astype(o_ref.dtype)
            outs = []


def _swa_attention(lay, qt, k, vt, sinks, *, tq=512):
    B, S, C = lay.B, lay.S, lay.C
    nq_rows = SW_HEADS * SW_HD
    nkv = SW_KV_HEADS * SW_HD
    G = SW_HEADS // SW_KV_HEADS
    nq = S // tq
    per = tq // SW_WINDOW
    last = lay.NT // SW_WINDOW - 1
    cblk0 = lay.NL // C
    prev = lambda b, i: jnp.maximum((b * nq + i) * per - 1, 0)
    nxt = lambda b, i: jnp.minimum((b * nq + i + 1) * per, last)
    sink_l = jnp.repeat(sinks.reshape(SW_KV_HEADS, 1, G) * math.log2(math.e), SW_SUB, axis=2)
    sink_spec = pl.BlockSpec(sink_l.shape, lambda *a: (0, 0, 0))
    o_lat = pl.pallas_call(
        functools.partial(_swa_kernel, S=S, local=True),
        out_shape=jax.ShapeDtypeStruct((lay.NL, nq_rows), BF16),
        grid=(B, nq),
        in_specs=[sink_spec,
                  pl.BlockSpec((nq_rows, tq), lambda b, i: (0, b * nq + i)),
                  pl.BlockSpec((C, nkv), lambda b, i: (cblk0 + b, 0)),
                  pl.BlockSpec((nkv, C), lambda b, i: (0, cblk0 + b)),
                  pl.BlockSpec((SW_WINDOW, nkv), lambda b, i: (prev(b, i), 0)),
                  pl.BlockSpec((tq, nkv), lambda b, i: (b * nq + i, 0)),
                  pl.BlockSpec((SW_WINDOW, nkv), lambda b, i: (nxt(b, i), 0)),
                  pl.BlockSpec((nkv, SW_WINDOW), lambda b, i: (0, prev(b, i))),
                  pl.BlockSpec((nkv, tq), lambda b, i: (0, b * nq + i)),
                  pl.BlockSpec((nkv, SW_WINDOW), lambda b, i: (0, nxt(b, i)))],
        out_specs=pl.BlockSpec((tq, nq_rows), lambda b, i: (b * nq + i, 0)),
        scratch_shapes=([pltpu.VMEM((tq + 2 * SW_WINDOW, nkv), BF16), pltpu.VMEM((nkv, tq + 2 * SW_WINDOW), BF16)]
                        + [pltpu.VMEM((SW_BAND, G * SW_SUB), F32), pltpu.VMEM((C, G * SW_SUB), F32)] * 2),
        compiler_params=_cp(("arbitrary", "arbitrary")),
        name="swa_attn_latent",
    )(sink_l, qt, k, vt, k, k, k, vt, vt, vt)
    o_ctx = pl.pallas_call(
        functools.partial(_swa_kernel, S=S, local=False),
        out_shape=jax.ShapeDtypeStruct((lay.NC, nq_rows), BF16),
        grid=(B,),
        in_specs=[sink_spec,
                  pl.BlockSpec((nq_rows, C), lambda b: (0, cblk0 + b)),
                  pl.BlockSpec((C, nkv), lambda b: (cblk0 + b, 0)),
                  pl.BlockSpec((nkv, C), lambda b: (0, cblk0 + b))],
        out_specs=pl.BlockSpec((C, nq_rows), lambda b: (b, 0)),
        scratch_shapes=[pltpu.VMEM((C, G * SW_SUB), F32)] * 2,
        compiler_params=_cp(("arbitrary",)),
        name="swa_attn_ctx",
    )(sink_l, qt, k, vt)
    return o_lat, o_ctx


def _route(logits, tri, carry):
    lane = _lane_iota(logits.shape)
    lanef = lane.astype(F32)
    big = float(LANES)
    rowmax = lambda t: jnp.max(t, axis=-1, keepdims=True)
    rowmin = lambda t: jnp.min(t, axis=-1, keepdims=True)
    rowsum = lambda t: jnp.sum(t, axis=-1, keepdims=True)
    is_g = lane < MOE_GROUPS
    mg = rowmax(jnp.where(is_g, logits, -jnp.inf))
    w_g = 1.0 / rowsum(jnp.where(is_g, jnp.exp(logits - mg), 0.0))
    gidx = rowmin(jnp.where(is_g & (logits == mg), lanef, big))
    g0 = MOE_GROUPS + MOE_PER_GROUP * gidx
    in_grp = (lanef >= g0) & (lanef < g0 + MOE_PER_GROUP)
    le = jnp.where(in_grp, logits, -jnp.inf)
    m1 = rowmax(le)
    i1 = rowmin(jnp.where(in_grp & (le == m1), lanef, big))
    le2 = jnp.where(lanef == i1, -jnp.inf, le)
    m2 = rowmax(le2)
    i2 = rowmin(jnp.where(in_grp & (lanef != i1) & (le2 == m2), lanef, big))
    r = jnp.exp(m2 - m1)
    gate1 = w_g / (1.0 + r)
    gate2 = w_g * r / (1.0 + r)
    sel1, sel2 = lanef == i1, lanef == i2
    member = (sel1 | sel2)
    cum = _dot(tri, member.astype(BF16)) + carry
    rank1 = rowsum(jnp.where(sel1, cum, 0.0))
    rank2 = rowsum(jnp.where(sel2, cum, 0.0))
    new_carry = carry + jnp.sum(member.astype(F32), axis=0, keepdims=True)
    rec = jnp.zeros_like(logits)
    for ln, val in ((0, i1 - MOE_GROUPS), (1, i2 - MOE_GROUPS), (2, rank1), (3, rank2), (4, gate1), (5, gate2)):
        rec = jnp.where(lane == ln, val, rec)
    return rec, new_carry


def _out_proj_kernel(*refs, has_bias, n_lat_tiles):
    if has_bias:
        (al_ref, ac_ref, x_ref, w_ref, b_ref, g1_ref, g2n_ref, sc_ref, sh_ref, wr_ref, br_ref, tri_ref,
         xo_ref, h2_ref, rec_ref, rect_ref, cnt_ref, carry_ref) = refs
    else:
        (al_ref, ac_ref, x_ref, w_ref, g1_ref, g2n_ref, sc_ref, sh_ref, wr_ref, br_ref, tri_ref,
         xo_ref, h2_ref, rec_ref, rect_ref, cnt_ref, carry_ref) = refs

    @pl.when(pl.program_id(0) == 0)
    def _():
        carry_ref[...] = jnp.zeros_like(carry_ref)

    a = jnp.where(pl.program_id(0) < n_lat_tiles, al_ref[...], ac_ref[...])
    y = _dot(a.astype(BF16), w_ref[...])
    if has_bias:
        y = y + b_ref[...]
    xn = x_ref[...] + g1_ref[0] * y
    xo_ref[...] = xn
    h2 = _normmod(xn, g2n_ref[...], sc_ref[0], sh_ref[0])
    h2_ref[...] = h2
    h_hi = h2.astype(BF16)
    h_lo = (h2 - h_hi.astype(F32)).astype(BF16)
    hw = _dot(h_hi, wr_ref[...])
    logits = hw[:, :LANES] + hw[:, LANES:] + _dot(h_lo, wr_ref[:, :LANES]) + br_ref[...]
    rec, carry = _route(logits, tri_ref[...], carry_ref[...])
    rec_ref[...] = rec
    rect_ref[...] = rec.T[:SUBLANES, :]
    carry_ref[...] = carry
    cnt_ref[...] = jnp.broadcast_to(carry, cnt_ref.shape)


def _out_proj(lay, a, x, w, b, g2n, modv, wr, br, tri):
    D = lay.D
    a_lat, a_ctx = a
    nl = lay.nl_tiles
    full = lambda t: pl.BlockSpec(t.shape, lambda i: (0,) * t.ndim)
    row = lambda n: pl.BlockSpec((TM, n), lambda i: (i, 0))
    ins = [a_lat, a_ctx, x, w] + ([b] if b is not None else []) + [modv, g2n, modv, modv, wr, br, tri]
    specs = ([pl.BlockSpec((TM, a_lat.shape[1]), lambda i: (jnp.minimum(i, nl - 1), 0)),
              pl.BlockSpec((TM, a_ctx.shape[1]), lambda i: (jnp.maximum(i - nl, 0), 0)), row(D), full(w)]
             + ([full(b)] if b is not None else [])
             + [lay.mod_spec(2), full(g2n), lay.mod_spec(4), lay.mod_spec(3), full(wr), full(br), full(tri)])
    return pl.pallas_call(
        functools.partial(_out_proj_kernel, has_bias=b is not None, n_lat_tiles=nl),
        out_shape=(jax.ShapeDtypeStruct((lay.NT, D), F32), jax.ShapeDtypeStruct((lay.NT, D), F32),
                   jax.ShapeDtypeStruct((lay.NT, LANES), F32), jax.ShapeDtypeStruct((SUBLANES, lay.NT), F32),
                   jax.ShapeDtypeStruct((SUBLANES, LANES), F32)),
        grid=(lay.n_tiles,),
        in_specs=specs,
        out_specs=(row(D), row(D), row(LANES), pl.BlockSpec((SUBLANES, TM), lambda i: (0, i)),
                   pl.BlockSpec((SUBLANES, LANES), lambda i: (0, 0))),
        scratch_shapes=[pltpu.VMEM((1, LANES), F32)],
        input_output_aliases={2: 0},
        compiler_params=_cp(("arbitrary",)),
        name="out_proj_router",
    )(*ins)


def _row_copy(src, s, dst, d, sem):
    return pltpu.make_async_copy(src.at[pl.ds(s, 1), :], dst.at[pl.ds(d, 1), :], sem)


def _expert_kernel(tok_ref, be_ref, nu_ref, h_ref, wg_ref, wu_ref, wd_ref, yb_ref, xbuf, wgb, wub, wdb, sem):
    j = pl.program_id(0)
    nblk = pl.num_programs(0)
    nu = nu_ref[0]
    slot = j % 2

    def gather(blk, dst_slot):
        base = blk * MOE_BM
        for r in range(MOE_BM):
            _row_copy(h_ref, tok_ref[base + r], xbuf.at[dst_slot], r, sem.at[dst_slot]).start(priority=r % 2)

    def wait(s):
        pltpu.make_async_copy(h_ref.at[pl.ds(0, MOE_BM), :], xbuf.at[s], sem.at[s]).wait()

    @pl.when(j == 0)
    def _():
        gather(0, 0)

    @pl.when((j == 0) | (be_ref[j] != be_ref[jnp.maximum(j - 1, 0)]))
    def _():
        wgb[...] = wg_ref[0, 0].astype(BF16)
        wub[...] = wu_ref[0, 0].astype(BF16)
        wdb[...] = wd_ref[0, 0].astype(BF16)

    @pl.when(j < nu)
    def _():
        wait(slot)
        gather(jnp.minimum(j + 1, nblk - 1), 1 - slot)
        xe = xbuf[slot].astype(BF16)
        g = _dot(xe, wgb[...])
        u = _dot(xe, wub[...])
        act = (g * jax.nn.sigmoid(g) * u).astype(BF16)
        yb_ref[...] = _dot(act, wdb[...])

    @pl.when(j >= nu)
    def _():
        yb_ref[...] = jnp.zeros_like(yb_ref)

    @pl.when(j == nu)
    def _():
        wait(slot)

    @pl.when((j == nblk - 1) & (j < nu))
    def _():
        wait(1 - slot)


def _experts(lay, tok_of, h2, blk_e, n_used, w_gate, w_up, w_down, layer):
    cap, D = tok_of.shape[0], lay.D
    FF = w_gate.shape[-1]
    nblk = cap // MOE_BM
    wblk = lambda j, tk, be, nu: (layer, be[j], 0, 0)
    return pl.pallas_call(
        _expert_kernel,
        out_shape=jax.ShapeDtypeStruct((cap, D), F32),
        grid_spec=pltpu.PrefetchScalarGridSpec(
            num_scalar_prefetch=3, grid=(nblk,),
            in_specs=[pl.BlockSpec(memory_space=pl.ANY),
                      pl.BlockSpec((1, 1, D, FF), wblk), pl.BlockSpec((1, 1, D, FF), wblk),
                      pl.BlockSpec((1, 1, FF, D), wblk)],
            out_specs=pl.BlockSpec((MOE_BM, D), lambda j, tk, be, nu: (j, 0)),
            scratch_shapes=[pltpu.VMEM((2, MOE_BM, D), F32), pltpu.VMEM((D, FF), BF16), pltpu.VMEM((D, FF), BF16),
                            pltpu.VMEM((FF, D), BF16), pltpu.SemaphoreType.DMA((2,))]),
        compiler_params=_cp(("arbitrary",)),
        name="moe_experts",
    )(tok_of, blk_e, n_used, h2, w_gate, w_up, w_down)


def _combine_kernel(*refs, final):
    if final:
        pos_ref, x_ref, rec_ref, g2_ref, yb_ref, fg_ref, o_ref, buf, sem = refs
    else:
        pos_ref, x_ref, rec_ref, g2_ref, yb_ref, o_ref, buf, sem = refs
    base = pl.program_id(0) * TM
    n_tok = pos_ref.shape[0] // 2

    def issue(r, c):
        for k in range(2):
            _row_copy(yb_ref, pos_ref[k * n_tok + base + r], buf.at[k], r, sem).start(priority=k)
        return c

    lax.fori_loop(0, TM, issue, 0, unroll=8)
    for k in range(2):
        pltpu.make_async_copy(yb_ref.at[pl.ds(0, TM), :], buf.at[k], sem).wait()
    rec = rec_ref[...]
    f = rec[:, ROUTE_LANE_GATE:ROUTE_LANE_GATE + 1] * buf[0] + rec[:, ROUTE_LANE_GATE + 1:ROUTE_LANE_GATE + 2] * buf[1]
    xn = x_ref[...] + g2_ref[0] * f
    if final:
        xn = _rms(xn, fg_ref[...])
    o_ref[...] = xn


def _combine(lay, pos, x, rec, modv, yb, final_g):
    D = lay.D
    final = final_g is not None
    row = lambda n: pl.BlockSpec((TM, n), lambda i, p: (i, 0))
    specs = [row(D), row(LANES),
             pl.BlockSpec((1, 1, D), lambda i, p: (5 * SUBLANES + lay.mod_row(i), 0, 0)),
             pl.BlockSpec(memory_space=pl.ANY)]
    ins = [x, rec, modv, yb]
    if final:
        specs.append(pl.BlockSpec(final_g.shape, lambda i, p: (0, 0)))
        ins.append(final_g)
    n_rows, n_tiles = (lay.NL, lay.nl_tiles) if final else (lay.NT, lay.n_tiles)
    return pl.pallas_call(
        functools.partial(_combine_kernel, final=final),
        out_shape=jax.ShapeDtypeStruct((n_rows, D), F32),
        grid_spec=pltpu.PrefetchScalarGridSpec(
            num_scalar_prefetch=1, grid=(n_tiles,),
            in_specs=specs,
            out_specs=row(D),
            scratch_shapes=[pltpu.VMEM((2, TM, D), F32), pltpu.SemaphoreType.DMA]),
        input_output_aliases={} if final else {1: 0},
        compiler_params=_cp(("arbitrary",)),
        name="moe_combine",
    )(pos, *ins)


def _moe(lay, x, h2, rec, rect, counts, modv, w_gate, w_up, w_down, layer, final_g):
    T = 2 * lay.NT
    cap = -(-T // MOE_BM) * MOE_BM + MOE_EXPERTS * MOE_BM
    cnt = counts[0, MOE_GROUPS:MOE_GROUPS + MOE_EXPERTS].astype(I32)
    pcnt = (cnt + MOE_BM - 1) // MOE_BM * MOE_BM
    pend = jnp.cumsum(pcnt)
    start = pend - pcnt
    eid = rect[ROUTE_LANE_EID:ROUTE_LANE_EID + 2].astype(I32)
    rank = rect[ROUTE_LANE_RANK:ROUTE_LANE_RANK + 2].astype(I32)
    eid, rank = eid.reshape(1, -1), rank.reshape(-1)
    hit = eid == jnp.arange(MOE_EXPERTS, dtype=I32)[:, None]
    pos = jnp.sum(jnp.where(hit, start[:, None], 0), axis=0) + rank
    blk_row = jnp.arange(cap // MOE_BM, dtype=I32) * MOE_BM
    blk_e = jnp.minimum(jnp.sum((pend[None, :] <= blk_row[:, None]).astype(I32), axis=1), MOE_EXPERTS - 1)
    n_used = (pend[-1:] // MOE_BM).astype(I32)
    tok_of = jnp.zeros((cap,), I32).at[pos].set(jnp.tile(jnp.arange(lay.NT, dtype=I32), 2), unique_indices=True)
    yb = _experts(lay, tok_of, h2, blk_e, n_used, w_gate, w_up, w_down, layer)
    return _combine(lay, pos, x, rec, modv, yb, final_g)


def _rope_tables(lay):
    S = lay.S
    t = jnp.arange(S)
    n = MLA_ROPE // 4
    inv = ROPE_THETA ** (-jnp.arange(n, dtype=F32) / n)
    ang = jnp.concatenate([(t // GRID_W).astype(F32)[:, None] * inv, (t % GRID_W).astype(F32)[:, None] * inv], axis=-1)
    cos, sin = jnp.cos(ang), jnp.sin(ang)
    cos64 = jnp.concatenate([cos, cos], axis=-1)
    sin64 = jnp.concatenate([-sin, sin], axis=-1)
    rows = lambda lat, ctx_val: jnp.concatenate([jnp.tile(lat, (lay.B, 1)), jnp.full((lay.NC, 64), ctx_val, F32)], axis=0)
    cos64, sin64 = rows(cos64, 1.0), rows(sin64, 0.0)
    zero = jnp.zeros_like(cos64)
    return ((jnp.concatenate([cos64, zero], axis=1), jnp.concatenate([sin64, zero], axis=1)),
            (jnp.concatenate([cos64, cos64], axis=1), jnp.concatenate([sin64, sin64], axis=1)))


def kernel(x, c, ctx, c_ctx, mod_w, mod_b, norm1_g, norm2_g, mla_w_dq, mla_g_q, mla_w_uq, mla_w_dkv, mla_g_kv, mla_w_ukv, mla_w_o, fnet_w_o, fnet_b_o, na_w_qkv, na_rpb, na_w_o, swa_w_qkv, swa_sinks, swa_w_o, moe_w_grp, moe_b_grp, moe_w_rt, moe_b_rt, moe_w_gate, moe_w_up, moe_w_down, final_g):
    B, S, D = x.shape
    C = ctx.shape[1]
    depth = mod_w.shape[0]
    lay = _Layout(B, S, C, D)
    X = jnp.concatenate([x.reshape(B * S, D), ctx.reshape(B * C, D)], axis=0)
    cond = jnp.concatenate([c, c_ctx[None], jnp.zeros((SUBLANES - B - 1, D), F32)], axis=0)
    mod = _modulation(cond, mod_w, mod_b)
    (mla_cos, mla_sin), (swa_cos, swa_sin) = _rope_tables(lay)
    tri = (jnp.arange(TM)[:, None] > jnp.arange(TM)[None, :]).astype(BF16)
    n_mix = 4
    for i in range(depth):
        m, j = i % n_mix, i // n_mix
        modv = mod[i].reshape(SUBLANES, 6, D).transpose(1, 0, 2).reshape(6 * SUBLANES, 1, D)
        g1n, g2n = norm1_g[i][None], norm2_g[i][None]
        bias = None
        if m == 0:
            w1 = jnp.concatenate([mla_w_dq[j], mla_w_dkv[j], jnp.zeros((D, LANES - MLA_ROPE), F32)], axis=1).astype(BF16)
            wq = mla_w_uq[j].reshape(MLA_Q_RANK, MLA_HEADS, MLA_NOPE + MLA_ROPE)
            wq = jnp.concatenate([wq, jnp.zeros((MLA_Q_RANK, MLA_HEADS, LANES - MLA_ROPE), F32)], axis=-1)
            wq = wq.reshape(MLA_Q_RANK, MLA_HEADS * 2 * LANES).astype(BF16)
            q, k, v = _mla_proj(lay, X, g1n, modv, w1, mla_g_q[j][None], mla_g_kv[j][None], wq,
                                mla_w_ukv[j].astype(BF16), mla_cos, mla_sin)
            a = _mla_attention(lay, q, k, v)
            w_o = mla_w_o[j]
        elif m == 1:
            a = _fnet_mix(lay, X, g1n, modv, _dft_tables(S, C, D // FNET_GROUPS))
            w_o, bias = fnet_w_o[j], fnet_b_o[j][None]
        elif m == 2:
            qt, k, vt = _qkv_proj(lay, X, g1n, modv, na_w_qkv[j].astype(BF16), swa_cos, swa_sin,
                                  n_q=NA_HEADS * NA_HD, n_k=NA_HEADS * NA_HD, rope=False,
                                  q_scale=NA_HD ** -0.5 * math.log2(math.e))
            a = _na_attention(lay, qt, k, vt, _na_bias_tables(na_rpb[j], S // GRID_W))
            w_o = na_w_o[j]
        else:
            qt, k, vt = _qkv_proj(lay, X, g1n, modv, swa_w_qkv[j].astype(BF16), swa_cos, swa_sin,
                                  n_q=SW_HEADS * SW_HD, n_k=SW_KV_HEADS * SW_HD, rope=True,
                                  q_scale=SW_HD ** -0.5 * math.log2(math.e))
            a = _swa_attention(lay, qt, k, vt, swa_sinks[j])
            w_o = swa_w_o[j]
        wr = jnp.concatenate([moe_w_grp[i], moe_w_rt[i], jnp.zeros((D, LANES - MOE_GROUPS - MOE_EXPERTS), F32)], axis=1)
        br = jnp.concatenate([moe_b_grp[i], moe_b_rt[i], jnp.zeros((LANES - MOE_GROUPS - MOE_EXPERTS,), F32)])[None]
        wr_hi = wr.astype(BF16)
        wr2 = jnp.concatenate([wr_hi, (wr - wr_hi.astype(F32)).astype(BF16)], axis=1)
        X, h2, rec, rect, counts = _out_proj(lay, a, X, w_o.astype(BF16), bias, g2n, modv, wr2, br, tri)
        X = _moe(lay, X, h2, rec, rect, counts, modv, moe_w_gate, moe_w_up, moe_w_down, i,
                 final_g[None] if i == depth - 1 else None)
    return X.reshape(B, S, D)
```

```python
import functools
import math

import jax
import jax.numpy as jnp
import numpy as np
from jax import lax
from jax.experimental import pallas as pl
from jax.experimental.pallas import tpu as pltpu

F32 = jnp.float32
BF16 = jnp.bfloat16
I32 = jnp.int32
HIGHEST = lax.Precision.HIGHEST

GRID_W = 64
EPS = 1e-6
ROPE_THETA = 10000.0
NEG_INF = -1e30
MLA_HEADS, MLA_Q_RANK, MLA_KV_RANK, MLA_NOPE, MLA_ROPE, MLA_V = 8, 512, 256, 128, 64, 128
MLA_VT_ROWS = MLA_V + 16
MLA_UNROLL = 8
FNET_GROUPS = 4
NA_HEADS, NA_HD, NA_KR, NA_KC = 16, 64, 8, 16
SW_HEADS, SW_KV_HEADS, SW_HD, SW_WINDOW = 16, 4, 64, 128
MOE_GROUPS, MOE_PER_GROUP, MOE_FF = 4, 8, 512
MOE_EXPERTS = MOE_GROUPS * MOE_PER_GROUP

LANES = 128
SUBLANES = 8
TM = 512
MOE_BM = 512
VMEM_LIMIT = 56 * 1024 * 1024
ROUTE_LANE_EID, ROUTE_LANE_RANK, ROUTE_LANE_GATE = 0, 2, 4


def _cp(sem, vmem=VMEM_LIMIT):
    return pltpu.CompilerParams(dimension_semantics=sem, vmem_limit_bytes=vmem)


def _lane_iota(shape):
    return lax.broadcasted_iota(I32, shape, len(shape) - 1)


def _normmod(x, g, sc, sh):
    ms = jnp.mean(x * x, axis=-1, keepdims=True)
    return (x * lax.rsqrt(ms + EPS) * g) * (1.0 + sc) + sh


def _rms(x, g):
    ms = jnp.mean(x * x, axis=-1, keepdims=True)
    return x * lax.rsqrt(ms + EPS) * g


def _swap_halves(t, period):
    n = t.shape[-1]
    half = period // 2
    lane = _lane_iota(t.shape)
    return jnp.where((lane % period) < half, pltpu.roll(t, n - half, 1), pltpu.roll(t, half, 1))


def _dot(a, b):
    return jnp.dot(a, b, preferred_element_type=F32)


def _mod_kernel(a_ref, w_ref, b_ref, o_ref):
    a = a_ref[...]
    a = a * jax.nn.sigmoid(a)
    o_ref[0] = jnp.dot(a, w_ref[0], precision=HIGHEST, preferred_element_type=F32) + b_ref[0]


def _modulation(cond, mod_w, mod_b):
    depth, d, n = mod_w.shape
    tn = n // 4
    return pl.pallas_call(
        _mod_kernel,
        out_shape=jax.ShapeDtypeStruct((depth, SUBLANES, n), F32),
        grid=(depth, n // tn),
        in_specs=[pl.BlockSpec((SUBLANES, d), lambda l, j: (0, 0)),
                  pl.BlockSpec((1, d, tn), lambda l, j: (l, 0, j)),
                  pl.BlockSpec((1, 1, tn), lambda l, j: (l, 0, j))],
        out_specs=pl.BlockSpec((1, SUBLANES, tn), lambda l, j: (l, 0, j)),
        compiler_params=_cp(("arbitrary", "arbitrary")),
        name="modulation",
    )(cond, mod_w, mod_b.reshape(depth, 1, n))


class _Layout:
    def __init__(self, B, S, C, D):
        self.B, self.S, self.C, self.D = B, S, C, D
        self.NL, self.NC = B * S, B * C
        self.NT = self.NL + self.NC
        assert S % TM == 0 and self.NC % TM == 0 and TM % C == 0
        self.nl_tiles = self.NL // TM
        self.n_tiles = self.NT // TM
        self.tiles_per_batch = S // TM

    def mod_row(self, i):
        return jnp.where(i < self.nl_tiles, i // self.tiles_per_batch, self.B)

    def mod_spec(self, chunk):
        return pl.BlockSpec((1, 1, self.D), lambda i: (chunk * SUBLANES + self.mod_row(i), 0, 0))


def _mla_proj_kernel(x_ref, g_ref, sc_ref, sh_ref, w1_ref, gq_ref, gkv_ref, wq_ref, wkv_ref, cos_ref, sin_ref,
                     qt_ref, k_ref, vt_ref):
    h = _normmod(x_ref[...], g_ref[...], sc_ref[0], sh_ref[0]).astype(BF16)
    a = _dot(h, w1_ref[...])
    qa = _rms(a[:, :MLA_Q_RANK], gq_ref[...]).astype(BF16)
    ckv = _rms(a[:, MLA_Q_RANK:MLA_Q_RANK + MLA_KV_RANK], gkv_ref[...]).astype(BF16)
    cos, sin = cos_ref[...], sin_ref[...]

    def rope(t):
        return t * cos + _swap_halves(t, MLA_ROPE) * sin

    kr = rope(a[:, MLA_Q_RANK + MLA_KV_RANK:]).astype(BF16)
    scale = (MLA_NOPE + MLA_ROPE) ** -0.5 * math.log2(math.e)
    q = _dot(qa, wq_ref[...])
    kv = _dot(ckv, wkv_ref[...])
    ones = jnp.ones((MLA_VT_ROWS - MLA_V, x_ref.shape[0]), BF16)
    for hd in range(MLA_HEADS):
        c = hd * 2 * LANES
        qh = jnp.concatenate([q[:, c:c + LANES], rope(q[:, c + LANES:c + 2 * LANES])], axis=1) * scale
        qt_ref[hd] = qh.T.astype(BF16)
        k_ref[:, c:c + LANES] = kv[:, c:c + LANES].astype(BF16)
        k_ref[:, c + LANES:c + 2 * LANES] = kr
        vt_ref[hd, :MLA_V, :] = kv[:, c + LANES:c + 2 * LANES].T.astype(BF16)
        vt_ref[hd, MLA_V:, :] = ones


def _mla_proj(lay, x, g, modv, w1, gq, gkv, wq, wkv, cos, sin):
    D = lay.D
    full = lambda a: pl.BlockSpec(a.shape, lambda i: (0,) * a.ndim)
    row = lambda n: pl.BlockSpec((TM, n), lambda i: (i, 0))
    col = lambda r: pl.BlockSpec((MLA_HEADS, r, TM), lambda i: (0, 0, i))
    hq = MLA_HEADS * 2 * LANES
    return pl.pallas_call(
        _mla_proj_kernel,
        out_shape=(jax.ShapeDtypeStruct((MLA_HEADS, 2 * LANES, lay.NT), BF16),
                   jax.ShapeDtypeStruct((lay.NT, hq), BF16),
                   jax.ShapeDtypeStruct((MLA_HEADS, MLA_VT_ROWS, lay.NT), BF16)),
        grid=(lay.n_tiles,),
        in_specs=[row(D), full(g), lay.mod_spec(1), lay.mod_spec(0), full(w1), full(gq), full(gkv), full(wq),
                  full(wkv), row(LANES), row(LANES)],
        out_specs=(col(2 * LANES), row(hq), col(MLA_VT_ROWS)),
        compiler_params=_cp(("arbitrary",)),
        name="mla_proj",
    )(x, g, modv, modv, w1, gq, gkv, wq, wkv, cos, sin)


def _qkv_proj_kernel(x_ref, g_ref, sc_ref, sh_ref, w_ref, cos_ref, sin_ref, qt_ref, k_ref, vt_ref, *,
                     n_q, n_k, rope, q_scale, chunk):
    h = _normmod(x_ref[...], g_ref[...], sc_ref[0], sh_ref[0]).astype(BF16)
    n = w_ref.shape[1]
    for c0 in range(0, n, chunk):
        a = _dot(h, w_ref[:, c0:c0 + chunk])
        if rope and c0 < n_q + n_k:
            reps = chunk // LANES
            cos = jnp.concatenate([cos_ref[...]] * reps, axis=1)
            sin = jnp.concatenate([sin_ref[...]] * reps, axis=1)
            a = a * cos + _swap_halves(a, SW_HD) * sin
        if c0 < n_q:
            qt_ref[c0:c0 + chunk, :] = (a * q_scale).T.astype(BF16)
        elif c0 < n_q + n_k:
            k_ref[:, c0 - n_q:c0 - n_q + chunk] = a.astype(BF16)
        else:
            c = c0 - n_q - n_k
            vt_ref[c:c + chunk, :] = a.T.astype(BF16)


def _qkv_proj(lay, x, g, modv, w, cos, sin, *, n_q, n_k, rope, q_scale, chunk=256):
    D, n = lay.D, w.shape[1]
    n_v = n - n_q - n_k
    assert n_q % chunk == 0 and n_k % chunk == 0 and n_v % chunk == 0
    full = lambda a: pl.BlockSpec(a.shape, lambda i: (0,) * a.ndim)
    row = lambda m: pl.BlockSpec((TM, m), lambda i: (i, 0))
    col = lambda m: pl.BlockSpec((m, TM), lambda i: (0, i))
    return pl.pallas_call(
        functools.partial(_qkv_proj_kernel, n_q=n_q, n_k=n_k, rope=rope, q_scale=q_scale, chunk=chunk),
        out_shape=(jax.ShapeDtypeStruct((n_q, lay.NT), BF16), jax.ShapeDtypeStruct((lay.NT, n_k), BF16),
                   jax.ShapeDtypeStruct((n_v, lay.NT), BF16)),
        grid=(lay.n_tiles,),
        in_specs=[row(D), full(g), lay.mod_spec(1), lay.mod_spec(0), full(w), row(LANES), row(LANES)],
        out_specs=(col(n_q), row(n_k), col(n_v)),
        compiler_params=_cp(("arbitrary",)),
        name="qkv_proj",
    )(x, g, modv, modv, w, cos, sin)


def _mla_attn_kernel(*refs, tk, n_lat):
    if n_lat:
        qt_ref, kc_ref, vtc_ref, kl_ref, vtl_ref, o_ref, acc_ref, sa_ref, sb_ref = refs
    else:
        qt_ref, kc_ref, vtc_ref, o_ref, acc_ref = refs
    qt = qt_ref[0]

    st = _dot(kc_ref[...], qt)
    if n_lat:
        sa_ref[...] = _dot(kl_ref[pl.ds(0, tk), :], qt)
    m = jnp.max(st, axis=0, keepdims=True)
    acc_ref[...] = _dot(vtc_ref[0], jnp.exp2(st - m).astype(BF16))

    if n_lat:
        nch = n_lat // tk

        def softmax_pv(st, vt, m):
            m_new = jnp.maximum(m, jnp.max(st, axis=0, keepdims=True))
            acc_ref[...] = jnp.exp2(m - m_new) * acc_ref[...] + _dot(vt, jnp.exp2(st - m_new).astype(BF16))
            return m_new

        def body(jj, m):
            r0 = pl.multiple_of(2 * jj * tk, tk)
            r1 = pl.multiple_of((2 * jj + 1) * tk, tk)
            r2 = pl.multiple_of(jnp.minimum(2 * jj + 2, nch - 1) * tk, tk)
            sb_ref[...] = _dot(kl_ref[pl.ds(r1, tk), :], qt)
            m = softmax_pv(sa_ref[...], vtl_ref[0, :, pl.ds(r0, tk)], m)
            sa_ref[...] = _dot(kl_ref[pl.ds(r2, tk), :], qt)
            return softmax_pv(sb_ref[...], vtl_ref[0, :, pl.ds(r1, tk)], m)

        lax.fori_loop(0, nch // 2, body, m, unroll=math.gcd(nch // 2, MLA_UNROLL))
    o_ref[...] = (acc_ref[:MLA_V, :] / acc_ref[MLA_V:MLA_V + 1, :]).T.astype(o_ref.dtype)


def _mla_attention(lay, qt, k, vt, *, tq=1024, tk=512):
    B, S, C = lay.B, lay.S, lay.C
    H = MLA_HEADS
    nq = S // tq
    cblk0 = lay.NL // C
    assert S % (2 * tk) == 0
    o_lat = pl.pallas_call(
        functools.partial(_mla_attn_kernel, tk=tk, n_lat=S),
        out_shape=jax.ShapeDtypeStruct((lay.NL, H * LANES), BF16),
        grid=(B, H, nq),
        in_specs=[pl.BlockSpec((1, 2 * LANES, tq), lambda b, h, i: (h, 0, b * nq + i)),
                  pl.BlockSpec((C, 2 * LANES), lambda b, h, i: (cblk0 + b, h)),
                  pl.BlockSpec((1, MLA_VT_ROWS, C), lambda b, h, i: (h, 0, cblk0 + b)),
                  pl.BlockSpec((S, 2 * LANES), lambda b, h, i: (b, h)),
                  pl.BlockSpec((1, MLA_VT_ROWS, S), lambda b, h, i: (h, 0, b))],
        out_specs=pl.BlockSpec((tq, LANES), lambda b, h, i: (b * nq + i, h)),
        scratch_shapes=[pltpu.VMEM((MLA_VT_ROWS, tq), F32), pltpu.VMEM((tk, tq), F32), pltpu.VMEM((tk, tq), F32)],
        compiler_params=_cp(("arbitrary", "arbitrary", "arbitrary")),
        name="mla_attn_latent",
    )(qt, k, vt, k, vt)
    o_ctx = pl.pallas_call(
        functools.partial(_mla_attn_kernel, tk=tk, n_lat=0),
        out_shape=jax.ShapeDtypeStruct((lay.NC, H * LANES), BF16),
        grid=(B, H),
        in_specs=[pl.BlockSpec((1, 2 * LANES, C), lambda b, h: (h, 0, cblk0 + b)),
                  pl.BlockSpec((C, 2 * LANES), lambda b, h: (cblk0 + b, h)),
                  pl.BlockSpec((1, MLA_VT_ROWS, C), lambda b, h: (h, 0, cblk0 + b))],
        out_specs=pl.BlockSpec((C, LANES), lambda b, h: (b, h)),
        scratch_shapes=[pltpu.VMEM((MLA_VT_ROWS, C), F32)],
        compiler_params=_cp(("arbitrary", "arbitrary")),
        name="mla_attn_ctx",
    )(qt, k, vt)
    return o_lat, o_ctx


def _dft_tables(S, C, gc):
    P = math.isqrt(S)
    assert P * P == S and (P & (P - 1)) == 0 and (gc & (gc - 1)) == 0 and (C & (C - 1)) == 0

    def cs(idx, n):
        ang = (idx % n).astype(F32) * (2.0 * math.pi / n)
        return jnp.cos(ang), jnp.sin(ang)

    k1 = jnp.arange(P, dtype=I32)
    idx = k1[None, :, None] * (P * k1[None, None, :] + k1[:, None, None])
    c, s = cs(idx, S)
    m1 = jnp.concatenate([c, -s], axis=1) * (1.0 / P)
    c, s = cs(k1[:, None] * k1[None, :], P)
    m2 = jnp.concatenate([jnp.concatenate([c, s], axis=1), jnp.concatenate([-s, c], axis=1)], axis=0)
    kc = jnp.arange(gc, dtype=I32)
    c, s = cs(kc[:, None] * kc[None, :], gc)
    mc = jnp.concatenate([c, s], axis=0) * (gc ** -0.5)
    kq = jnp.arange(C, dtype=I32)
    c, s = cs(kq[:, None] * kq[None, :], C)
    mctx = jnp.concatenate([c, s], axis=0) * (C ** -0.5)
    return m1.astype(BF16), m2.astype(BF16), mc.astype(BF16), mctx.astype(BF16)


def _fnet_stage1_kernel(x_ref, g_ref, sc_ref, sh_ref, m1_ref, z_ref, *, n2c, P):
    g, sc, sh = g_ref[...], sc_ref[0], sh_ref[0]
    for j in range(n2c):
        h = _normmod(x_ref[:, j, :], g, sc, sh).astype(BF16)
        z = _dot(m1_ref[j], h)
        z_ref[0, :, 0, j, :] = z[:P]
        z_ref[0, :, 1, j, :] = z[P:]


def _fnet_stage2_kernel(z_ref, m2_ref, mc_ref, f_ref, *, k1c, P, gc):
    D = f_ref.shape[-1]
    for j in range(k1c):
        z = z_ref[0, j].reshape(2 * P, D).astype(BF16)
        y = _dot(m2_ref[...], z)
        yr, yi = y[:P].astype(BF16), y[P:].astype(BF16)
        outs = []
        for gi in range(D // gc):
            sl = slice(gi * gc, (gi + 1) * gc)
            outs.append(_dot(yr[:, sl], mc_ref[:gc, :]) + _dot(yi[:, sl], mc_ref[gc:, :]))
        f_ref[:, j, :] = jnp.concatenate(outs, axis=1)


def _fnet_ctx_kernel(x_ref, g_ref, sc_ref, sh_ref, ml_ref, mc_ref, f_ref, *, C, gc):
    D = x_ref.shape[-1]
    h = _normmod(x_ref[...], g_ref[...], sc_ref[0], sh_ref[0]).astype(BF16)
    y = _dot(ml_ref[...], h)
    yc, ys = y[:C].astype(BF16), y[C:].astype(BF16)
    outs = []
    for gi in range(D // gc):
        sl = slice(gi * gc, (gi + 1) * gc)
        outs.append(_dot(yc[:, sl], mc_ref[:gc, :]) - _dot(ys[:, sl], mc_ref[gc:, :]))
    f_ref[...] = jnp.concatenate(outs, axis=1)


def _fnet_mix(lay, x, g, modv, tables):
    B, S, C, D = lay.B, lay.S, lay.C, lay.D
    m1, m2, mc, mctx = tables
    P = math.isqrt(S)
    gc = D // FNET_GROUPS
    n2c = SUBLANES
    k1c = SUBLANES
    full = lambda a: pl.BlockSpec(a.shape, lambda *i: (0,) * a.ndim)
    modspec = lambda chunk: pl.BlockSpec((1, 1, D), lambda b, j: (chunk * SUBLANES + b, 0, 0))
    assert C % P == 0 and P % n2c == 0 and P % k1c == 0
    x3 = x.reshape(lay.NT // P, P, D)
    z = pl.pallas_call(
        functools.partial(_fnet_stage1_kernel, n2c=n2c, P=P),
        out_shape=jax.ShapeDtypeStruct((B, P, 2, P, D), F32),
        grid=(B, P // n2c),
        in_specs=[pl.BlockSpec((P, n2c, D), lambda b, j: (b, j, 0)), full(g), modspec(1), modspec(0),
                  pl.BlockSpec((n2c, 2 * P, P), lambda b, j: (j, 0, 0))],
        out_specs=pl.BlockSpec((1, P, 2, n2c, D), lambda b, j: (b, 0, 0, j, 0)),
        compiler_params=_cp(("arbitrary", "arbitrary")),
        name="fnet_stage1",
    )(x3, g, modv, modv, m1)
    f_lat = pl.pallas_call(
        functools.partial(_fnet_stage2_kernel, k1c=k1c, P=P, gc=gc),
        out_shape=jax.ShapeDtypeStruct((lay.NL // P, P, D), F32),
        grid=(B, P // k1c),
        in_specs=[pl.BlockSpec((1, k1c, 2, P, D), lambda b, j: (b, j, 0, 0, 0)), full(m2), full(mc)],
        out_specs=pl.BlockSpec((P, k1c, D), lambda b, j: (b, j, 0)),
        compiler_params=_cp(("arbitrary", "arbitrary")),
        name="fnet_stage2",
    )(z, m2, mc)
    cblk0 = lay.NL // C
    ctx_mod = lambda chunk: pl.BlockSpec((1, 1, D), lambda b: (chunk * SUBLANES + B, 0, 0))
    f_ctx = pl.pallas_call(
        functools.partial(_fnet_ctx_kernel, C=C, gc=gc),
        out_shape=jax.ShapeDtypeStruct((lay.NC, D), F32),
        grid=(B,),
        in_specs=[pl.BlockSpec((C, D), lambda b: (cblk0 + b, 0)), full(g), ctx_mod(1), ctx_mod(0), full(mctx),
                  full(mc)],
        out_specs=pl.BlockSpec((C, D), lambda b: (b, 0)),
        compiler_params=_cp(("arbitrary",)),
        name="fnet_ctx",
    )(x, g, modv, modv, mctx, mc)
    return f_lat.reshape(lay.NL, D), f_ctx


NA_QROWS = 2
NA_KROWS = NA_KR + NA_QROWS
NA_VARIANTS = 5
NA_ONES = 16


def _na_bias_kernel(rpb_ref, sel_ref, toe_ref, o_ref):
    g = jnp.dot(rpb_ref[0], toe_ref[...], precision=HIGHEST, preferred_element_type=F32)
    for t in range(NA_VARIANTS):
        o_ref[t, 0] = jnp.dot(sel_ref[t], g, precision=HIGHEST, preferred_element_type=F32)


def _na_window_start(r, rows):
    return jnp.clip(r - NA_KR // 2, 0, rows - NA_KROWS)


def _na_bias_tables(rpb, rows):
    assert rows >= 16 and rows % NA_QROWS == 0
    W = GRID_W
    nh, nu, nv = rpb.shape
    up, vp, ajp = 2 * SUBLANES, LANES, 3 * SUBLANES
    assert nu <= up and nv <= vp and NA_QROWS * NA_KROWS <= ajp
    cq, ck = np.arange(W)[:, None], np.arange(W)[None, :]
    dc = np.clip(ck - cq + NA_KC - 1, 0, nv - 1).reshape(-1)
    toe = (np.arange(vp)[:, None] == dc[None, :]).astype(np.float32)
    cs = np.clip(cq - NA_KC // 2, 0, W - NA_KC)
    col_ok = (ck >= cs) & (ck < cs + NA_KC)
    sel = np.zeros((NA_VARIANTS, ajp, up), np.float32)
    valid = np.zeros((NA_VARIANTS, NA_QROWS, NA_KROWS, W, W), bool)
    for t, r in enumerate((0, 2, 6, rows - 4, rows - 2)):
        w0 = min(max(r - NA_KR // 2, 0), rows - NA_KROWS)
        for a in range(NA_QROWS):
            rs = min(max(r + a - NA_KR // 2, 0), rows - NA_KR)
            for j in range(NA_KROWS):
                rk = w0 + j
                sel[t, a * NA_KROWS + j, min(max(rk - (r + a) + NA_KR - 1, 0), nu - 1)] = 1.0
                if rs <= rk < rs + NA_KR:
                    valid[t, a, j] = col_ok
    rpb_p = jnp.pad(rpb, ((0, 0), (0, up - nu), (0, vp - nv)))
    tab = pl.pallas_call(
        _na_bias_kernel,
        out_shape=jax.ShapeDtypeStruct((NA_VARIANTS, nh, ajp, W * W), F32),
        grid=(nh,),
        in_specs=[pl.BlockSpec((1, up, vp), lambda h: (h, 0, 0)),
                  pl.BlockSpec(sel.shape, lambda h: (0, 0, 0)),
                  pl.BlockSpec(toe.shape, lambda h: (0, 0))],
        out_specs=pl.BlockSpec((NA_VARIANTS, 1, ajp, W * W), lambda h: (0, h, 0, 0)),
        compiler_params=_cp(("arbitrary",)),
        name="na_bias",
    )(rpb_p, jnp.asarray(sel), jnp.asarray(toe))
    tab = tab[:, :, :NA_QROWS * NA_KROWS].reshape(NA_VARIANTS, nh, NA_QROWS, NA_KROWS, W, W)
    tab = jnp.where(jnp.asarray(valid)[:, None], tab * math.log2(math.e), NEG_INF)
    tab = tab.reshape(NA_VARIANTS, nh // 2, 2, NA_QROWS, NA_KROWS, W, W)
    return tab.transpose(0, 1, 4, 6, 2, 3, 5).reshape(NA_VARIANTS, nh // 2, NA_KROWS * W, 2 * NA_QROWS * W)


def _na_kernel(*refs, rows, local):
    if local:
        qt_ref, kc_ref, vtc_ref, kl_ref, vtl_ref, bias_ref, o_ref, st_a, sc_a, st_b, sc_b = refs
    else:
        qt_ref, kc_ref, vtc_ref, o_ref, sc_a, sc_b = refs
        st_a = st_b = None
    nq = NA_QROWS * GRID_W
    nk = NA_KROWS * GRID_W
    kc, vtc = kc_ref[...], vtc_ref[...]
    row = lax.broadcasted_iota(I32, (LANES, nq), 0)
    pairs = qt_ref.shape[1] // nq
    qi = pl.program_id(2) if local else 0

    def ext(vt):
        return jnp.concatenate([vt, jnp.ones((NA_ONES, vt.shape[1]), BF16)], axis=0)

    def window(t):
        r = (qi * pairs + t) * NA_QROWS
        k0 = pl.multiple_of(_na_window_start(r, rows) * GRID_W, LANES)
        var = jnp.where(r == 0, 0, jnp.where(r == 2, 1, jnp.where(r == rows - 4, 3, jnp.where(r == rows - 2, 4, 2))))
        return k0, var

    def scores(t, st_ref, sc_ref):
        qt = qt_ref[:, t * nq:(t + 1) * nq]
        zero = jnp.zeros_like(qt)
        qbd = jnp.concatenate([jnp.where(row < NA_HD, qt, zero), jnp.where(row >= NA_HD, qt, zero)], axis=1)
        sc_ref[...] = _dot(kc, qbd)
        if local:
            k0, var = window(t)
            st_ref[...] = _dot(kl_ref[pl.ds(k0, nk), :], qbd) + bias_ref[var, 0]

    def finish(t, st_ref, sc_ref):
        sc = sc_ref[...]
        m = jnp.max(sc, axis=0, keepdims=True)
        if local:
            st = st_ref[...]
            m = jnp.maximum(m, jnp.max(st, axis=0, keepdims=True))
            pt = jnp.exp2(st - m).astype(BF16)
            vtw = vtl_ref[:, pl.ds(window(t)[0], nk)]
        pc = jnp.exp2(sc - m).astype(BF16)
        outs = []
        for hl in range(2):
            hs, qs = slice(hl * NA_HD, (hl + 1) * NA_HD), slice(hl * nq, (hl + 1) * nq)
            acc = _dot(ext(vtc[hs, :]), pc[:, qs])
            if local:
                acc = acc + _dot(ext(vtw[hs, :]), pt[:, qs])
            outs.append(acc[:NA_HD] / acc[NA_HD:NA_HD + 1])
        o_ref[t * nq:(t + 1) * nq, :] = jnp.concatenate(outs, axis=0).T.astype(o_ref.dtype)

    bufs = ((st_a, sc_a), (st_b, sc_b))
    scores(0, *bufs[0])
    for t in range(pairs):
        if t + 1 < pairs:
            scores(t + 1, *bufs[(t + 1) % 2])
        finish(t, *bufs[t % 2])


def _na_attention(lay, qt, k, vt, bias, *, tq=1024):
    B, S, C = lay.B, lay.S, lay.C
    rows = S // GRID_W
    HP = NA_HEADS // 2
    nq = S // tq
    cblk0 = lay.NL // C
    nk, nqp = NA_KROWS * GRID_W, NA_QROWS * GRID_W
    o_lat = pl.pallas_call(
        functools.partial(_na_kernel, rows=rows, local=True),
        out_shape=jax.ShapeDtypeStruct((lay.NL, NA_HEADS * NA_HD), BF16),
        grid=(B, HP, nq),
        in_specs=[pl.BlockSpec((LANES, tq), lambda b, h, i: (h, b * nq + i)),
                  pl.BlockSpec((C, LANES), lambda b, h, i: (cblk0 + b, h)),
                  pl.BlockSpec((LANES, C), lambda b, h, i: (h, cblk0 + b)),
                  pl.BlockSpec((S, LANES), lambda b, h, i: (b, h)),
                  pl.BlockSpec((LANES, S), lambda b, h, i: (h, b)),
                  pl.BlockSpec((NA_VARIANTS, 1) + bias.shape[2:], lambda b, h, i: (0, h, 0, 0))],
        out_specs=pl.BlockSpec((tq, LANES), lambda b, h, i: (b * nq + i, h)),
        scratch_shapes=[pltpu.VMEM((nk, 2 * nqp), F32), pltpu.VMEM((C, 2 * nqp), F32)] * 2,
        compiler_params=_cp(("arbitrary", "arbitrary", "arbitrary")),
        name="na_attn_latent",
    )(qt, k, vt, k, vt, bias)
    o_ctx = pl.pallas_call(
        functools.partial(_na_kernel, rows=rows, local=False),
        out_shape=jax.ShapeDtypeStruct((lay.NC, NA_HEADS * NA_HD), BF16),
        grid=(B, HP),
        in_specs=[pl.BlockSpec((LANES, C), lambda b, h: (h, cblk0 + b)),
                  pl.BlockSpec((C, LANES), lambda b, h: (cblk0 + b, h)),
                  pl.BlockSpec((LANES, C), lambda b, h: (h, cblk0 + b))],
        out_specs=pl.BlockSpec((C, LANES), lambda b, h: (b, h)),
        scratch_shapes=[pltpu.VMEM((C, 2 * nqp), F32)] * 2,
        compiler_params=_cp(("arbitrary", "arbitrary")),
        name="na_attn_ctx",
    )(qt, k, vt)
    return o_lat, o_ctx


SW_SUB = 128
SW_BAND = SW_SUB + 2 * SW_WINDOW
SW_ONES = 16


def _swa_kernel(*refs, S, local):
    if local:
        (sink_ref, qt_ref, kc_ref, vtc_ref, kp_ref, kcur_ref, kn_ref, vtp_ref, vtcur_ref, vtn_ref, o_ref,
         kbuf, vtbuf, st_a, sc_a, st_b, sc_b) = refs
        tq = qt_ref.shape[1]
        W = SW_WINDOW
        kbuf[0:W] = kp_ref[...]
        kbuf[W:W + tq] = kcur_ref[...]
        kbuf[W + tq:] = kn_ref[...]
        vtbuf[:, 0:W] = vtp_ref[...]
        vtbuf[:, W:W + tq] = vtcur_ref[...]
        vtbuf[:, W + tq:] = vtn_ref[...]
        i = pl.program_id(1)
    else:
        sink_ref, qt_ref, kc_ref, vtc_ref, o_ref, sc_a, sc_b = refs
        st_a = st_b = None
        tq = qt_ref.shape[1]
    G = SW_HEADS // SW_KV_HEADS
    kc, vtc = kc_ref[...], vtc_ref[...]
    zeros_q = jnp.zeros((SW_HD, G * SW_SUB), BF16)

    def ext(vt):
        return jnp.concatenate([vt, jnp.ones((SW_ONES, vt.shape[1]), BF16)], axis=0)

    def band_bias(sb):
        key = lax.broadcasted_iota(I32, (SW_BAND, SW_SUB), 0)
        qry = lax.broadcasted_iota(I32, (SW_BAND, SW_SUB), 1)
        rel = key - SW_WINDOW - qry
        kpos = i * tq + sb * SW_SUB - SW_WINDOW + key
        ok = (jnp.abs(rel) <= SW_WINDOW) & (kpos >= 0) & (kpos < S)
        bias = jnp.where(ok, 0.0, NEG_INF).astype(F32)
        return jnp.concatenate([bias] * G, axis=1)

    def scores(sb, g, bias, st_ref, sc_ref):
        r0 = sb * SW_SUB
        tile, half = g // 2, g % 2
        sl = slice(tile * LANES, (tile + 1) * LANES)
        qg = jnp.concatenate([qt_ref[(G * g + hl) * SW_HD:(G * g + hl + 1) * SW_HD, r0:r0 + SW_SUB]
                              for hl in range(G)], axis=1)
        qpad = jnp.concatenate([qg, zeros_q] if half == 0 else [zeros_q, qg], axis=0)
        sc_ref[...] = _dot(kc[:, sl], qpad)
        if local:
            st_ref[...] = _dot(kbuf[r0:r0 + SW_BAND, sl], qpad) + bias

    def finish(sb, g, st_ref, sc_ref):
        r0 = sb * SW_SUB
        sink = sink_ref[g]
        sc = sc_ref[...]
        m = jnp.maximum(jnp.max(sc, axis=0, keepdims=True), sink)
        if local:
            st = st_ref[...]
            m = jnp.maximum(m, jnp.max(st, axis=0, keepdims=True))
        acc = _dot(ext(vtc[g * SW_HD:(g + 1) * SW_HD, :]), jnp.exp2(sc - m).astype(BF16))
        if local:
            acc = acc + _dot(ext(vtbuf[g * SW_HD:(g + 1) * SW_HD, r0:r0 + SW_BAND]), jnp.exp2(st - m).astype(BF16))
        og = acc[:SW_HD] / (acc[SW_HD:SW_HD + 1] + jnp.exp2(sink - m))
        return [og[:, hl * SW_SUB:(hl + 1) * SW_SUB] for hl in range(G)]

    items = [(sb, g) for sb in range(tq // SW_SUB) for g in range(SW_KV_HEADS)]
    bufs = ((st_a, sc_a), (st_b, sc_b))
    bias = band_bias(0) if local else None
    scores(*items[0], bias, *bufs[0])
    outs = []
    for n, (sb, g) in enumerate(items):
        if n + 1 < len(items):
            nsb, ng = items[n + 1]
            if local and ng == 0:
                bias = band_bias(nsb)
            scores(nsb, ng, bias, *bufs[(n + 1) % 2])
        outs.extend(finish(sb, g, *bufs[n % 2]))
        if g == SW_KV_HEADS - 1:
            o_ref[sb * SW_SUB:(sb + 1) * SW_SUB, :] = jnp.concatenate(outs, axis=0).T.astype(o_ref.dtype)
            outs = []


def _swa_attention(lay, qt, k, vt, sinks, *, tq=512):
    B, S, C = lay.B, lay.S, lay.C
    nq_rows = SW_HEADS * SW_HD
    nkv = SW_KV_HEADS * SW_HD
    G = SW_HEADS // SW_KV_HEADS
    nq = S // tq
    per = tq // SW_WINDOW
    last = lay.NT // SW_WINDOW - 1
    cblk0 = lay.NL // C
    prev = lambda b, i: jnp.maximum((b * nq + i) * per - 1, 0)
    nxt = lambda b, i: jnp.minimum((b * nq + i + 1) * per, last)
    sink_l = jnp.repeat(sinks.reshape(SW_KV_HEADS, 1, G) * math.log2(math.e), SW_SUB, axis=2)
    sink_spec = pl.BlockSpec(sink_l.shape, lambda *a: (0, 0, 0))
    o_lat = pl.pallas_call(
        functools.partial(_swa_kernel, S=S, local=True),
        out_shape=jax.ShapeDtypeStruct((lay.NL, nq_rows), BF16),
        grid=(B, nq),
        in_specs=[sink_spec,
                  pl.BlockSpec((nq_rows, tq), lambda b, i: (0, b * nq + i)),
                  pl.BlockSpec((C, nkv), lambda b, i: (cblk0 + b, 0)),
                  pl.BlockSpec((nkv, C), lambda b, i: (0, cblk0 + b)),
                  pl.BlockSpec((SW_WINDOW, nkv), lambda b, i: (prev(b, i), 0)),
                  pl.BlockSpec((tq, nkv), lambda b, i: (b * nq + i, 0)),
                  pl.BlockSpec((SW_WINDOW, nkv), lambda b, i: (nxt(b, i), 0)),
                  pl.BlockSpec((nkv, SW_WINDOW), lambda b, i: (0, prev(b, i))),
                  pl.BlockSpec((nkv, tq), lambda b, i: (0, b * nq + i)),
                  pl.BlockSpec((nkv, SW_WINDOW), lambda b, i: (0, nxt(b, i)))],
        out_specs=pl.BlockSpec((tq, nq_rows), lambda b, i: (b * nq + i, 0)),
        scratch_shapes=([pltpu.VMEM((tq + 2 * SW_WINDOW, nkv), BF16), pltpu.VMEM((nkv, tq + 2 * SW_WINDOW), BF16)]
                        + [pltpu.VMEM((SW_BAND, G * SW_SUB), F32), pltpu.VMEM((C, G * SW_SUB), F32)] * 2),
        compiler_params=_cp(("arbitrary", "arbitrary")),
        name="swa_attn_latent",
    )(sink_l, qt, k, vt, k, k, k, vt, vt, vt)
    o_ctx = pl.pallas_call(
        functools.partial(_swa_kernel, S=S, local=False),
        out_shape=jax.ShapeDtypeStruct((lay.NC, nq_rows), BF16),
        grid=(B,),
        in_specs=[sink_spec,
                  pl.BlockSpec((nq_rows, C), lambda b: (0, cblk0 + b)),
                  pl.BlockSpec((C, nkv), lambda b: (cblk0 + b, 0)),
                  pl.BlockSpec((nkv, C), lambda b: (0, cblk0 + b))],
        out_specs=pl.BlockSpec((C, nq_rows), lambda b: (b, 0)),
        scratch_shapes=[pltpu.VMEM((C, G * SW_SUB), F32)] * 2,
        compiler_params=_cp(("arbitrary",)),
        name="swa_attn_ctx",
    )(sink_l, qt, k, vt)
    return o_lat, o_ctx


def _route(logits, tri, carry):
    lane = _lane_iota(logits.shape)
    lanef = lane.astype(F32)
    big = float(LANES)
    rowmax = lambda t: jnp.max(t, axis=-1, keepdims=True)
    rowmin = lambda t: jnp.min(t, axis=-1, keepdims=True)
    rowsum = lambda t: jnp.sum(t, axis=-1, keepdims=True)
    is_g = lane < MOE_GROUPS
    mg = rowmax(jnp.where(is_g, logits, -jnp.inf))
    w_g = 1.0 / rowsum(jnp.where(is_g, jnp.exp(logits - mg), 0.0))
    gidx = rowmin(jnp.where(is_g & (logits == mg), lanef, big))
    g0 = MOE_GROUPS + MOE_PER_GROUP * gidx
    in_grp = (lanef >= g0) & (lanef < g0 + MOE_PER_GROUP)
    le = jnp.where(in_grp, logits, -jnp.inf)
    m1 = rowmax(le)
    i1 = rowmin(jnp.where(in_grp & (le == m1), lanef, big))
    le2 = jnp.where(lanef == i1, -jnp.inf, le)
    m2 = rowmax(le2)
    i2 = rowmin(jnp.where(in_grp & (lanef != i1) & (le2 == m2), lanef, big))
    r = jnp.exp(m2 - m1)
    gate1 = w_g / (1.0 + r)
    gate2 = w_g * r / (1.0 + r)
    sel1, sel2 = lanef == i1, lanef == i2
    member = (sel1 | sel2)
    cum = _dot(tri, member.astype(BF16)) + carry
    rank1 = rowsum(jnp.where(sel1, cum, 0.0))
    rank2 = rowsum(jnp.where(sel2, cum, 0.0))
    new_carry = carry + jnp.sum(member.astype(F32), axis=0, keepdims=True)
    rec = jnp.zeros_like(logits)
    for ln, val in ((0, i1 - MOE_GROUPS), (1, i2 - MOE_GROUPS), (2, rank1), (3, rank2), (4, gate1), (5, gate2)):
        rec = jnp.where(lane == ln, val, rec)
    return rec, new_carry


def _out_proj_kernel(*refs, has_bias, n_lat_tiles):
    if has_bias:
        (al_ref, ac_ref, x_ref, w_ref, b_ref, g1_ref, g2n_ref, sc_ref, sh_ref, wr_ref, br_ref, tri_ref,
         xo_ref, h2_ref, rec_ref, rect_ref, cnt_ref, carry_ref) = refs
    else:
        (al_ref, ac_ref, x_ref, w_ref, g1_ref, g2n_ref, sc_ref, sh_ref, wr_ref, br_ref, tri_ref,
         xo_ref, h2_ref, rec_ref, rect_ref, cnt_ref, carry_ref) = refs

    @pl.when(pl.program_id(0) == 0)
    def _():
        carry_ref[...] = jnp.zeros_like(carry_ref)

    a = jnp.where(pl.program_id(0) < n_lat_tiles, al_ref[...], ac_ref[...])
    y = _dot(a.astype(BF16), w_ref[...])
    if has_bias:
        y = y + b_ref[...]
    xn = x_ref[...] + g1_ref[0] * y
    xo_ref[...] = xn
    h2 = _normmod(xn, g2n_ref[...], sc_ref[0], sh_ref[0])
    h2_ref[...] = h2
    h_hi = h2.astype(BF16)
    h_lo = (h2 - h_hi.astype(F32)).astype(BF16)
    hw = _dot(h_hi, wr_ref[...])
    logits = hw[:, :LANES] + hw[:, LANES:] + _dot(h_lo, wr_ref[:, :LANES]) + br_ref[...]
    rec, carry = _route(logits, tri_ref[...], carry_ref[...])
    rec_ref[...] = rec
    rect_ref[...] = rec.T[:SUBLANES, :]
    carry_ref[...] = carry
    cnt_ref[...] = jnp.broadcast_to(carry, cnt_ref.shape)


def _out_proj(lay, a, x, w, b, g2n, modv, wr, br, tri):
    D = lay.D
    a_lat, a_ctx = a
    nl = lay.nl_tiles
    full = lambda t: pl.BlockSpec(t.shape, lambda i: (0,) * t.ndim)
    row = lambda n: pl.BlockSpec((TM, n), lambda i: (i, 0))
    ins = [a_lat, a_ctx, x, w] + ([b] if b is not None else []) + [modv, g2n, modv, modv, wr, br, tri]
    specs = ([pl.BlockSpec((TM, a_lat.shape[1]), lambda i: (jnp.minimum(i, nl - 1), 0)),
              pl.BlockSpec((TM, a_ctx.shape[1]), lambda i: (jnp.maximum(i - nl, 0), 0)), row(D), full(w)]
             + ([full(b)] if b is not None else [])
             + [lay.mod_spec(2), full(g2n), lay.mod_spec(4), lay.mod_spec(3), full(wr), full(br), full(tri)])
    return pl.pallas_call(
        functools.partial(_out_proj_kernel, has_bias=b is not None, n_lat_tiles=nl),
        out_shape=(jax.ShapeDtypeStruct((lay.NT, D), F32), jax.ShapeDtypeStruct((lay.NT, D), F32),
                   jax.ShapeDtypeStruct((lay.NT, LANES), F32), jax.ShapeDtypeStruct((SUBLANES, lay.NT), F32),
                   jax.ShapeDtypeStruct((SUBLANES, LANES), F32)),
        grid=(lay.n_tiles,),
        in_specs=specs,
        out_specs=(row(D), row(D), row(LANES), pl.BlockSpec((SUBLANES, TM), lambda i: (0, i)),
                   pl.BlockSpec((SUBLANES, LANES), lambda i: (0, 0))),
        scratch_shapes=[pltpu.VMEM((1, LANES), F32)],
        input_output_aliases={2: 0},
        compiler_params=_cp(("arbitrary",)),
        name="out_proj_router",
    )(*ins)


def _row_copy(src, s, dst, d, sem):
    return pltpu.make_async_copy(src.at[pl.ds(s, 1), :], dst.at[pl.ds(d, 1), :], sem)


def _dispatch_kernel(pos_ref, pend_ref, nu_ref, h_ref, xb_ref, zbuf, sem, zsem):
    base = pl.program_id(0) * TM
    n_tok = pos_ref.shape[0] // 2
    nblk = xb_ref.shape[0] // MOE_BM

    @pl.when(pl.program_id(0) == 0)
    def _():
        zbuf[...] = jnp.zeros_like(zbuf)

        def zero_block(row0):
            return pltpu.make_async_copy(zbuf, xb_ref.at[pl.ds(pl.multiple_of(row0, MOE_BM), MOE_BM), :], zsem)

        def each(fn):
            def expert(e, c):
                end = pend_ref[e]

                @pl.when(end > jnp.where(e > 0, pend_ref[jnp.maximum(e - 1, 0)], 0))
                def _():
                    fn(zero_block(end - MOE_BM))
                return c

            def tail(j, c):
                @pl.when(j >= nu_ref[0])
                def _():
                    fn(zero_block(j * MOE_BM))
                return c

            lax.fori_loop(0, MOE_EXPERTS, expert, 0)
            lax.fori_loop(0, nblk, tail, 0)

        each(lambda cp: cp.start())
        each(lambda cp: cp.wait())

    def issue(r, c):
        for k in range(2):
            _row_copy(h_ref, r, xb_ref, pos_ref[k * n_tok + base + r], sem).start(priority=k)
        return c

    lax.fori_loop(0, TM, issue, 0, unroll=8)
    for k in range(2):
        pltpu.make_async_copy(h_ref, xb_ref.at[pl.ds(0, TM), :], sem).wait()


def _dispatch(lay, pos, pend, n_used, h2, cap):
    D = lay.D
    return pl.pallas_call(
        _dispatch_kernel,
        out_shape=jax.ShapeDtypeStruct((cap, D), F32),
        grid_spec=pltpu.PrefetchScalarGridSpec(
            num_scalar_prefetch=3, grid=(lay.n_tiles,),
            in_specs=[pl.BlockSpec((TM, D), lambda i, p, e, n: (i, 0))],
            out_specs=pl.BlockSpec(memory_space=pl.ANY),
            scratch_shapes=[pltpu.VMEM((MOE_BM, D), F32), pltpu.SemaphoreType.DMA, pltpu.SemaphoreType.DMA]),
        compiler_params=_cp(("arbitrary",)),
        name="moe_dispatch",
    )(pos, pend, n_used, h2)


def _expert_kernel(be_ref, nu_ref, xb_ref, wg_ref, wu_ref, wd_ref, yb_ref, wgb, wub, wdb):
    j = pl.program_id(0)
    prev = be_ref[jnp.maximum(j - 1, 0)]

    @pl.when((j == 0) | (be_ref[j] != prev))
    def _():
        wgb[...] = wg_ref[0, 0].astype(BF16)
        wub[...] = wu_ref[0, 0].astype(BF16)
        wdb[...] = wd_ref[0, 0].astype(BF16)

    @pl.when(j < nu_ref[0])
    def _():
        xe = xb_ref[...].astype(BF16)
        g = _dot(xe, wgb[...])
        u = _dot(xe, wub[...])
        act = (g * jax.nn.sigmoid(g) * u).astype(BF16)
        yb_ref[...] = _dot(act, wdb[...])

    @pl.when(j >= nu_ref[0])
    def _():
        yb_ref[...] = jnp.zeros_like(yb_ref)


def _experts(xb, blk_e, n_used, w_gate, w_up, w_down, layer):
    cap, D = xb.shape
    FF = w_gate.shape[-1]
    nblk = cap // MOE_BM
    blk = lambda j, be, nu: (jnp.maximum(jnp.minimum(j, nu[0] - 1), 0), 0)
    wblk = lambda j, be, nu: (layer, be[j], 0, 0)
    return pl.pallas_call(
        _expert_kernel,
        out_shape=jax.ShapeDtypeStruct((cap, D), F32),
        grid_spec=pltpu.PrefetchScalarGridSpec(
            num_scalar_prefetch=2, grid=(nblk,),
            in_specs=[pl.BlockSpec((MOE_BM, D), blk),
                      pl.BlockSpec((1, 1, D, FF), wblk), pl.BlockSpec((1, 1, D, FF), wblk),
                      pl.BlockSpec((1, 1, FF, D), wblk)],
            out_specs=pl.BlockSpec((MOE_BM, D), lambda j, be, nu: (j, 0)),
            scratch_shapes=[pltpu.VMEM((D, FF), BF16), pltpu.VMEM((D, FF), BF16), pltpu.VMEM((FF, D), BF16)]),
        compiler_params=_cp(("arbitrary",)),
        name="moe_experts",
    )(blk_e, n_used, xb, w_gate, w_up, w_down)


def _combine_kernel(*refs, final):
    if final:
        pos_ref, x_ref, rec_ref, g2_ref, yb_ref, fg_ref, o_ref, buf, sem = refs
    else:
        pos_ref, x_ref, rec_ref, g2_ref, yb_ref, o_ref, buf, sem = refs
    i = pl.program_id(0)
    n_tok = pos_ref.shape[0] // 2
    slot = i % 2

    def gather(tile, s):
        def issue(r, c):
            for k in range(2):
                _row_copy(yb_ref, pos_ref[k * n_tok + tile * TM + r], buf.at[s, k], r, sem.at[s]).start(priority=k)
            return c
        lax.fori_loop(0, TM, issue, 0, unroll=8)

    @pl.when(i == 0)
    def _():
        gather(0, 0)

    for k in range(2):
        pltpu.make_async_copy(yb_ref.at[pl.ds(0, TM), :], buf.at[slot, k], sem.at[slot]).wait()

    @pl.when(i + 1 < pl.num_programs(0))
    def _():
        gather(i + 1, 1 - slot)

    rec = rec_ref[...]
    f = (rec[:, ROUTE_LANE_GATE:ROUTE_LANE_GATE + 1] * buf[slot, 0]
         + rec[:, ROUTE_LANE_GATE + 1:ROUTE_LANE_GATE + 2] * buf[slot, 1])
    xn = x_ref[...] + g2_ref[0] * f
    if final:
        xn = _rms(xn, fg_ref[...])
    o_ref[...] = xn


def _combine(lay, pos, x, rec, modv, yb, final_g):
    D = lay.D
    final = final_g is not None
    row = lambda n: pl.BlockSpec((TM, n), lambda i, p: (i, 0))
    specs = [row(D), row(LANES),
             pl.BlockSpec((1, 1, D), lambda i, p: (5 * SUBLANES + lay.mod_row(i), 0, 0)),
             pl.BlockSpec(memory_space=pl.ANY)]
    ins = [x, rec, modv, yb]
    if final:
        specs.append(pl.BlockSpec(final_g.shape, lambda i, p: (0, 0)))
        ins.append(final_g)
    n_rows, n_tiles = (lay.NL, lay.nl_tiles) if final else (lay.NT, lay.n_tiles)
    return pl.pallas_call(
        functools.partial(_combine_kernel, final=final),
        out_shape=jax.ShapeDtypeStruct((n_rows, D), F32),
        grid_spec=pltpu.PrefetchScalarGridSpec(
            num_scalar_prefetch=1, grid=(n_tiles,),
            in_specs=specs,
            out_specs=row(D),
            scratch_shapes=[pltpu.VMEM((2, 2, TM, D), F32), pltpu.SemaphoreType.DMA((2,))]),
        input_output_aliases={} if final else {1: 0},
        compiler_params=_cp(("arbitrary",)),
        name="moe_combine",
    )(pos, *ins)


def _moe(lay, x, h2, rec, rect, counts, modv, w_gate, w_up, w_down, layer, final_g):
    T = 2 * lay.NT
    cap = -(-T // MOE_BM) * MOE_BM + MOE_EXPERTS * MOE_BM
    cnt = counts[0, MOE_GROUPS:MOE_GROUPS + MOE_EXPERTS].astype(I32)
    pcnt = (cnt + MOE_BM - 1) // MOE_BM * MOE_BM
    pend = jnp.cumsum(pcnt)
    start = pend - pcnt
    eid = rect[ROUTE_LANE_EID:ROUTE_LANE_EID + 2].astype(I32)
    rank = rect[ROUTE_LANE_RANK:ROUTE_LANE_RANK + 2].astype(I32)
    eid, rank = eid.reshape(1, -1), rank.reshape(-1)
    hit = eid == jnp.arange(MOE_EXPERTS, dtype=I32)[:, None]
    pos = jnp.sum(jnp.where(hit, start[:, None], 0), axis=0) + rank
    blk_row = jnp.arange(cap // MOE_BM, dtype=I32) * MOE_BM
    blk_e = jnp.minimum(jnp.sum((pend[None, :] <= blk_row[:, None]).astype(I32), axis=1), MOE_EXPERTS - 1)
    n_used = (pend[-1:] // MOE_BM).astype(I32)
    xb = _dispatch(lay, pos, pend.astype(I32), n_used, h2, cap)
    yb = _experts(xb, blk_e, n_used, w_gate, w_up, w_down, layer)
    return _combine(lay, pos, x, rec, modv, yb, final_g)


def _rope_tables(lay):
    S = lay.S
    t = jnp.arange(S)
    n = MLA_ROPE // 4
    inv = ROPE_THETA ** (-jnp.arange(n, dtype=F32) / n)
    ang = jnp.concatenate([(t // GRID_W).astype(F32)[:, None] * inv, (t % GRID_W).astype(F32)[:, None] * inv], axis=-1)
    cos, sin = jnp.cos(ang), jnp.sin(ang)
    cos64 = jnp.concatenate([cos, cos], axis=-1)
    sin64 = jnp.concatenate([-sin, sin], axis=-1)
    rows = lambda lat, ctx_val: jnp.concatenate([jnp.tile(lat, (lay.B, 1)), jnp.full((lay.NC, 64), ctx_val, F32)], axis=0)
    cos64, sin64 = rows(cos64, 1.0), rows(sin64, 0.0)
    zero = jnp.zeros_like(cos64)
    return ((jnp.concatenate([cos64, zero], axis=1), jnp.concatenate([sin64, zero], axis=1)),
            (jnp.concatenate([cos64, cos64], axis=1), jnp.concatenate([sin64, sin64], axis=1)))


def kernel(x, c, ctx, c_ctx, mod_w, mod_b, norm1_g, norm2_g, mla_w_dq, mla_g_q, mla_w_uq, mla_w_dkv, mla_g_kv, mla_w_ukv, mla_w_o, fnet_w_o, fnet_b_o, na_w_qkv, na_rpb, na_w_o, swa_w_qkv, swa_sinks, swa_w_o, moe_w_grp, moe_b_grp, moe_w_rt, moe_b_rt, moe_w_gate, moe_w_up, moe_w_down, final_g):
    B, S, D = x.shape
    C = ctx.shape[1]
    depth = mod_w.shape[0]
    lay = _Layout(B, S, C, D)
    X = jnp.concatenate([x.reshape(B * S, D), ctx.reshape(B * C, D)], axis=0)
    cond = jnp.concatenate([c, c_ctx[None], jnp.zeros((SUBLANES - B - 1, D), F32)], axis=0)
    mod = _modulation(cond, mod_w, mod_b)
    (mla_cos, mla_sin), (swa_cos, swa_sin) = _rope_tables(lay)
    tri = (jnp.arange(TM)[:, None] > jnp.arange(TM)[None, :]).astype(BF16)
    n_mix = 4
    for i in range(depth):
        m, j = i % n_mix, i // n_mix
        modv = mod[i].reshape(SUBLANES, 6, D).transpose(1, 0, 2).reshape(6 * SUBLANES, 1, D)
        g1n, g2n = norm1_g[i][None], norm2_g[i][None]
        bias = None
        if m == 0:
            w1 = jnp.concatenate([mla_w_dq[j], mla_w_dkv[j], jnp.zeros((D, LANES - MLA_ROPE), F32)], axis=1).astype(BF16)
            wq = mla_w_uq[j].reshape(MLA_Q_RANK, MLA_HEADS, MLA_NOPE + MLA_ROPE)
            wq = jnp.concatenate([wq, jnp.zeros((MLA_Q_RANK, MLA_HEADS, LANES - MLA_ROPE), F32)], axis=-1)
            wq = wq.reshape(MLA_Q_RANK, MLA_HEADS * 2 * LANES).astype(BF16)
            q, k, v = _mla_proj(lay, X, g1n, modv, w1, mla_g_q[j][None], mla_g_kv[j][None], wq,
                                mla_w_ukv[j].astype(BF16), mla_cos, mla_sin)
            a = _mla_attention(lay, q, k, v)
            w_o = mla_w_o[j]
        elif m == 1:
            a = _fnet_mix(lay, X, g1n, modv, _dft_tables(S, C, D // FNET_GROUPS))
            w_o, bias = fnet_w_o[j], fnet_b_o[j][None]
        elif m == 2:
            qt, k, vt = _qkv_proj(lay, X, g1n, modv, na_w_qkv[j].astype(BF16), swa_cos, swa_sin,
                                  n_q=NA_HEADS * NA_HD, n_k=NA_HEADS * NA_HD, rope=False,
                                  q_scale=NA_HD ** -0.5 * math.log2(math.e))
            a = _na_attention(lay, qt, k, vt, _na_bias_tables(na_rpb[j], S // GRID_W))
            w_o = na_w_o[j]
        else:
            qt, k, vt = _qkv_proj(lay, X, g1n, modv, swa_w_qkv[j].astype(BF16), swa_cos, swa_sin,
                                  n_q=SW_HEADS * SW_HD, n_k=SW_KV_HEADS * SW_HD, rope=True,
                                  q_scale=SW_HD ** -0.5 * math.log2(math.e))
            a = _swa_attention(lay, qt, k, vt, swa_sinks[j])
            w_o = swa_w_o[j]
        wr = jnp.concatenate([moe_w_grp[i], moe_w_rt[i], jnp.zeros((D, LANES - MOE_GROUPS - MOE_EXPERTS), F32)], axis=1)
        br = jnp.concatenate([moe_b_grp[i], moe_b_rt[i], jnp.zeros((LANES - MOE_GROUPS - MOE_EXPERTS,), F32)])[None]
        wr_hi = wr.astype(BF16)
        wr2 = jnp.concatenate([wr_hi, (wr - wr_hi.astype(F32)).astype(BF16)], axis=1)
        X, h2, rec, rect, counts = _out_proj(lay, a, X, w_o.astype(BF16), bias, g2n, modv, wr2, br, tri)
        X = _moe(lay, X, h2, rec, rect, counts, modv, moe_w_gate, moe_w_up, moe_w_down, i,
                 final_g[None] if i == depth - 1 else None)
    return X.reshape(B, S, D)
```

```python
import functools
import math

import jax
import jax.numpy as jnp
import numpy as np
from jax import lax
from jax.experimental import pallas as pl
from jax.experimental.pallas import tpu as pltpu

F32 = jnp.float32
BF16 = jnp.bfloat16
I32 = jnp.int32
HIGHEST = lax.Precision.HIGHEST

GRID_W = 64
EPS = 1e-6
ROPE_THETA = 10000.0
NEG_INF = -1e30
MLA_HEADS, MLA_Q_RANK, MLA_KV_RANK, MLA_NOPE, MLA_ROPE, MLA_V = 8, 512, 256, 128, 64, 128
MLA_VT_ROWS = MLA_V + 16
MLA_UNROLL = 8
FNET_GROUPS = 4
NA_HEADS, NA_HD, NA_KR, NA_KC = 16, 64, 8, 16
SW_HEADS, SW_KV_HEADS, SW_HD, SW_WINDOW = 16, 4, 64, 128
MOE_GROUPS, MOE_PER_GROUP, MOE_FF = 4, 8, 512
MOE_EXPERTS = MOE_GROUPS * MOE_PER_GROUP

LANES = 128
SUBLANES = 8
TM = 512
MOE_BM = 512
VMEM_LIMIT = 56 * 1024 * 1024
ROUTE_LANE_EID, ROUTE_LANE_RANK, ROUTE_LANE_GATE = 0, 2, 4


def _cp(sem, vmem=VMEM_LIMIT):
    return pltpu.CompilerParams(dimension_semantics=sem, vmem_limit_bytes=vmem)


def _lane_iota(shape):
    return lax.broadcasted_iota(I32, shape, len(shape) - 1)


def _normmod(x, g, sc, sh):
    ms = jnp.mean(x * x, axis=-1, keepdims=True)
    return (x * lax.rsqrt(ms + EPS) * g) * (1.0 + sc) + sh


def _rms(x, g):
    ms = jnp.mean(x * x, axis=-1, keepdims=True)
    return x * lax.rsqrt(ms + EPS) * g


def _swap_halves(t, period):
    n = t.shape[-1]
    half = period // 2
    lane = _lane_iota(t.shape)
    return jnp.where((lane % period) < half, pltpu.roll(t, n - half, 1), pltpu.roll(t, half, 1))


def _dot(a, b):
    return jnp.dot(a, b, preferred_element_type=F32)


def _mod_kernel(a_ref, w_ref, b_ref, o_ref):
    a = a_ref[...]
    a = a * jax.nn.sigmoid(a)
    o_ref[0] = jnp.dot(a, w_ref[0], precision=HIGHEST, preferred_element_type=F32) + b_ref[0]


def _modulation(cond, mod_w, mod_b):
    depth, d, n = mod_w.shape
    tn = n // 4
    return pl.pallas_call(
        _mod_kernel,
        out_shape=jax.ShapeDtypeStruct((depth, SUBLANES, n), F32),
        grid=(depth, n // tn),
        in_specs=[pl.BlockSpec((SUBLANES, d), lambda l, j: (0, 0)),
                  pl.BlockSpec((1, d, tn), lambda l, j: (l, 0, j)),
                  pl.BlockSpec((1, 1, tn), lambda l, j: (l, 0, j))],
        out_specs=pl.BlockSpec((1, SUBLANES, tn), lambda l, j: (l, 0, j)),
        compiler_params=_cp(("arbitrary", "arbitrary")),
        name="modulation",
    )(cond, mod_w, mod_b.reshape(depth, 1, n))


class _Layout:
    def __init__(self, B, S, C, D):
        self.B, self.S, self.C, self.D = B, S, C, D
        self.NL, self.NC = B * S, B * C
        self.NT = self.NL + self.NC
        assert S % TM == 0 and self.NC % TM == 0 and TM % C == 0
        self.nl_tiles = self.NL // TM
        self.n_tiles = self.NT // TM
        self.tiles_per_batch = S // TM

    def mod_row(self, i):
        return jnp.where(i < self.nl_tiles, i // self.tiles_per_batch, self.B)

    def mod_spec(self, chunk):
        return pl.BlockSpec((1, 1, self.D), lambda i: (chunk * SUBLANES + self.mod_row(i), 0, 0))


def _mla_proj_kernel(x_ref, g_ref, sc_ref, sh_ref, w1_ref, gq_ref, gkv_ref, wq_ref, wkv_ref, cos_ref, sin_ref,
                     qt_ref, k_ref, vt_ref):
    h = _normmod(x_ref[...], g_ref[...], sc_ref[0], sh_ref[0]).astype(BF16)
    a = _dot(h, w1_ref[...])
    qa = _rms(a[:, :MLA_Q_RANK], gq_ref[...]).astype(BF16)
    ckv = _rms(a[:, MLA_Q_RANK:MLA_Q_RANK + MLA_KV_RANK], gkv_ref[...]).astype(BF16)
    cos, sin = cos_ref[...], sin_ref[...]

    def rope(t):
        return t * cos + _swap_halves(t, MLA_ROPE) * sin

    kr = rope(a[:, MLA_Q_RANK + MLA_KV_RANK:]).astype(BF16)
    scale = (MLA_NOPE + MLA_ROPE) ** -0.5 * math.log2(math.e)
    q = _dot(qa, wq_ref[...])
    kv = _dot(ckv, wkv_ref[...])
    ones = jnp.ones((MLA_VT_ROWS - MLA_V, x_ref.shape[0]), BF16)
    for hd in range(MLA_HEADS):
        c = hd * 2 * LANES
        qh = jnp.concatenate([q[:, c:c + LANES], rope(q[:, c + LANES:c + 2 * LANES])], axis=1) * scale
        qt_ref[hd] = qh.T.astype(BF16)
        k_ref[:, c:c + LANES] = kv[:, c:c + LANES].astype(BF16)
        k_ref[:, c + LANES:c + 2 * LANES] = kr
        vt_ref[hd, :MLA_V, :] = kv[:, c + LANES:c + 2 * LANES].T.astype(BF16)
        vt_ref[hd, MLA_V:, :] = ones


def _mla_proj(lay, x, g, modv, w1, gq, gkv, wq, wkv, cos, sin):
    D = lay.D
    full = lambda a: pl.BlockSpec(a.shape, lambda i: (0,) * a.ndim)
    row = lambda n: pl.BlockSpec((TM, n), lambda i: (i, 0))
    col = lambda r: pl.BlockSpec((MLA_HEADS, r, TM), lambda i: (0, 0, i))
    hq = MLA_HEADS * 2 * LANES
    return pl.pallas_call(
        _mla_proj_kernel,
        out_shape=(jax.ShapeDtypeStruct((MLA_HEADS, 2 * LANES, lay.NT), BF16),
                   jax.ShapeDtypeStruct((lay.NT, hq), BF16),
                   jax.ShapeDtypeStruct((MLA_HEADS, MLA_VT_ROWS, lay.NT), BF16)),
        grid=(lay.n_tiles,),
        in_specs=[row(D), full(g), lay.mod_spec(1), lay.mod_spec(0), full(w1), full(gq), full(gkv), full(wq),
                  full(wkv), row(LANES), row(LANES)],
        out_specs=(col(2 * LANES), row(hq), col(MLA_VT_ROWS)),
        compiler_params=_cp(("arbitrary",)),
        name="mla_proj",
    )(x, g, modv, modv, w1, gq, gkv, wq, wkv, cos, sin)


def _qkv_proj_kernel(x_ref, g_ref, sc_ref, sh_ref, w_ref, cos_ref, sin_ref, qt_ref, k_ref, vt_ref, *,
                     n_q, n_k, rope, q_scale, chunk):
    h = _normmod(x_ref[...], g_ref[...], sc_ref[0], sh_ref[0]).astype(BF16)
    n = w_ref.shape[1]
    for c0 in range(0, n, chunk):
        a = _dot(h, w_ref[:, c0:c0 + chunk])
        if rope and c0 < n_q + n_k:
            reps = chunk // LANES
            cos = jnp.concatenate([cos_ref[...]] * reps, axis=1)
            sin = jnp.concatenate([sin_ref[...]] * reps, axis=1)
            a = a * cos + _swap_halves(a, SW_HD) * sin
        if c0 < n_q:
            qt_ref[c0:c0 + chunk, :] = (a * q_scale).T.astype(BF16)
        elif c0 < n_q + n_k:
            k_ref[:, c0 - n_q:c0 - n_q + chunk] = a.astype(BF16)
        else:
            c = c0 - n_q - n_k
            vt_ref[c:c + chunk, :] = a.T.astype(BF16)


def _qkv_proj(lay, x, g, modv, w, cos, sin, *, n_q, n_k, rope, q_scale, chunk=256):
    D, n = lay.D, w.shape[1]
    n_v = n - n_q - n_k
    assert n_q % chunk == 0 and n_k % chunk == 0 and n_v % chunk == 0
    full = lambda a: pl.BlockSpec(a.shape, lambda i: (0,) * a.ndim)
    row = lambda m: pl.BlockSpec((TM, m), lambda i: (i, 0))
    col = lambda m: pl.BlockSpec((m, TM), lambda i: (0, i))
    return pl.pallas_call(
        functools.partial(_qkv_proj_kernel, n_q=n_q, n_k=n_k, rope=rope, q_scale=q_scale, chunk=chunk),
        out_shape=(jax.ShapeDtypeStruct((n_q, lay.NT), BF16), jax.ShapeDtypeStruct((lay.NT, n_k), BF16),
                   jax.ShapeDtypeStruct((n_v, lay.NT), BF16)),
        grid=(lay.n_tiles,),
        in_specs=[row(D), full(g), lay.mod_spec(1), lay.mod_spec(0), full(w), row(LANES), row(LANES)],
        out_specs=(col(n_q), row(n_k), col(n_v)),
        compiler_params=_cp(("arbitrary",)),
        name="qkv_proj",
    )(x, g, modv, modv, w, cos, sin)


def _mla_attn_kernel(*refs, tk, n_lat):
    if n_lat:
        qt_ref, kc_ref, vtc_ref, kl_ref, vtl_ref, o_ref, acc_ref, sa_ref, sb_ref = refs
    else:
        qt_ref, kc_ref, vtc_ref, o_ref, acc_ref = refs
    qt = qt_ref[0]

    st = _dot(kc_ref[...], qt)
    if n_lat:
        sa_ref[...] = _dot(kl_ref[pl.ds(0, tk), :], qt)
    m = jnp.max(st, axis=0, keepdims=True)
    acc_ref[...] = _dot(vtc_ref[0], jnp.exp2(st - m).astype(BF16))

    if n_lat:
        nch = n_lat // tk

        def softmax_pv(st, vt, m):
            m_new = jnp.maximum(m, jnp.max(st, axis=0, keepdims=True))
            acc_ref[...] = jnp.exp2(m - m_new) * acc_ref[...] + _dot(vt, jnp.exp2(st - m_new).astype(BF16))
            return m_new

        def body(jj, m):
            r0 = pl.multiple_of(2 * jj * tk, tk)
            r1 = pl.multiple_of((2 * jj + 1) * tk, tk)
            r2 = pl.multiple_of(jnp.minimum(2 * jj + 2, nch - 1) * tk, tk)
            sb_ref[...] = _dot(kl_ref[pl.ds(r1, tk), :], qt)
            m = softmax_pv(sa_ref[...], vtl_ref[0, :, pl.ds(r0, tk)], m)
            sa_ref[...] = _dot(kl_ref[pl.ds(r2, tk), :], qt)
            return softmax_pv(sb_ref[...], vtl_ref[0, :, pl.ds(r1, tk)], m)

        lax.fori_loop(0, nch // 2, body, m, unroll=math.gcd(nch // 2, MLA_UNROLL))
    o_ref[...] = (acc_ref[:MLA_V, :] / acc_ref[MLA_V:MLA_V + 1, :]).T.astype(o_ref.dtype)


def _mla_attention(lay, qt, k, vt, *, tq=1024, tk=512):
    B, S, C = lay.B, lay.S, lay.C
    H = MLA_HEADS
    nq = S // tq
    cblk0 = lay.NL // C
    assert S % (2 * tk) == 0
    o_lat = pl.pallas_call(
        functools.partial(_mla_attn_kernel, tk=tk, n_lat=S),
        out_shape=jax.ShapeDtypeStruct((lay.NL, H * LANES), BF16),
        grid=(B, H, nq),
        in_specs=[pl.BlockSpec((1, 2 * LANES, tq), lambda b, h, i: (h, 0, b * nq + i)),
                  pl.BlockSpec((C, 2 * LANES), lambda b, h, i: (cblk0 + b, h)),
                  pl.BlockSpec((1, MLA_VT_ROWS, C), lambda b, h, i: (h, 0, cblk0 + b)),
                  pl.BlockSpec((S, 2 * LANES), lambda b, h, i: (b, h)),
                  pl.BlockSpec((1, MLA_VT_ROWS, S), lambda b, h, i: (h, 0, b))],
        out_specs=pl.BlockSpec((tq, LANES), lambda b, h, i: (b * nq + i, h)),
        scratch_shapes=[pltpu.VMEM((MLA_VT_ROWS, tq), F32), pltpu.VMEM((tk, tq), F32), pltpu.VMEM((tk, tq), F32)],
        compiler_params=_cp(("arbitrary", "arbitrary", "arbitrary")),
        name="mla_attn_latent",
    )(qt, k, vt, k, vt)
    o_ctx = pl.pallas_call(
        functools.partial(_mla_attn_kernel, tk=tk, n_lat=0),
        out_shape=jax.ShapeDtypeStruct((lay.NC, H * LANES), BF16),
        grid=(B, H),
        in_specs=[pl.BlockSpec((1, 2 * LANES, C), lambda b, h: (h, 0, cblk0 + b)),
                  pl.BlockSpec((C, 2 * LANES), lambda b, h: (cblk0 + b, h)),
                  pl.BlockSpec((1, MLA_VT_ROWS, C), lambda b, h: (h, 0, cblk0 + b))],
        out_specs=pl.BlockSpec((C, LANES), lambda b, h: (b, h)),
        scratch_shapes=[pltpu.VMEM((MLA_VT_ROWS, C), F32)],
        compiler_params=_cp(("arbitrary", "arbitrary")),
        name="mla_attn_ctx",
    )(qt, k, vt)
    return o_lat, o_ctx


def _dft_tables(S, C, gc):
    P = math.isqrt(S)
    assert P * P == S and (P & (P - 1)) == 0 and (gc & (gc - 1)) == 0 and (C & (C - 1)) == 0

    def cs(idx, n):
        ang = (idx % n).astype(F32) * (2.0 * math.pi / n)
        return jnp.cos(ang), jnp.sin(ang)

    k1 = jnp.arange(P, dtype=I32)
    idx = k1[None, :, None] * (P * k1[None, None, :] + k1[:, None, None])
    c, s = cs(idx, S)
    m1 = jnp.concatenate([c, -s], axis=1) * (1.0 / P)
    c, s = cs(k1[:, None] * k1[None, :], P)
    m2 = jnp.concatenate([jnp.concatenate([c, s], axis=1), jnp.concatenate([-s, c], axis=1)], axis=0)
    kc = jnp.arange(gc, dtype=I32)
    c, s = cs(kc[:, None] * kc[None, :], gc)
    mc = jnp.concatenate([c, s], axis=0) * (gc ** -0.5)
    kq = jnp.arange(C, dtype=I32)
    c, s = cs(kq[:, None] * kq[None, :], C)
    mctx = jnp.concatenate([c, s], axis=0) * (C ** -0.5)
    return m1.astype(BF16), m2.astype(BF16), mc.astype(BF16), mctx.astype(BF16)


def _fnet_stage1_kernel(x_ref, g_ref, sc_ref, sh_ref, m1_ref, z_ref, *, n2c, P):
    g, sc, sh = g_ref[...], sc_ref[0], sh_ref[0]
    for j in range(n2c):
        h = _normmod(x_ref[:, j, :], g, sc, sh).astype(BF16)
        z = _dot(m1_ref[j], h)
        z_ref[0, :, 0, j, :] = z[:P]
        z_ref[0, :, 1, j, :] = z[P:]


def _fnet_stage2_kernel(z_ref, m2_ref, mc_ref, f_ref, *, k1c, P, gc):
    D = f_ref.shape[-1]
    for j in range(k1c):
        z = z_ref[0, j].reshape(2 * P, D).astype(BF16)
        y = _dot(m2_ref[...], z)
        yr, yi = y[:P].astype(BF16), y[P:].astype(BF16)
        outs = []
        for gi in range(D // gc):
            sl = slice(gi * gc, (gi + 1) * gc)
            outs.append(_dot(yr[:, sl], mc_ref[:gc, :]) + _dot(yi[:, sl], mc_ref[gc:, :]))
        f_ref[:, j, :] = jnp.concatenate(outs, axis=1)


def _fnet_ctx_kernel(x_ref, g_ref, sc_ref, sh_ref, ml_ref, mc_ref, f_ref, *, C, gc):
    D = x_ref.shape[-1]
    h = _normmod(x_ref[...], g_ref[...], sc_ref[0], sh_ref[0]).astype(BF16)
    y = _dot(ml_ref[...], h)
    yc, ys = y[:C].astype(BF16), y[C:].astype(BF16)
    outs = []
    for gi in range(D // gc):
        sl = slice(gi * gc, (gi + 1) * gc)
        outs.append(_dot(yc[:, sl], mc_ref[:gc, :]) - _dot(ys[:, sl], mc_ref[gc:, :]))
    f_ref[...] = jnp.concatenate(outs, axis=1)


def _fnet_mix(lay, x, g, modv, tables):
    B, S, C, D = lay.B, lay.S, lay.C, lay.D
    m1, m2, mc, mctx = tables
    P = math.isqrt(S)
    gc = D // FNET_GROUPS
    n2c = SUBLANES
    k1c = SUBLANES
    full = lambda a: pl.BlockSpec(a.shape, lambda *i: (0,) * a.ndim)
    modspec = lambda chunk: pl.BlockSpec((1, 1, D), lambda b, j: (chunk * SUBLANES + b, 0, 0))
    assert C % P == 0 and P % n2c == 0 and P % k1c == 0
    x3 = x.reshape(lay.NT // P, P, D)
    z = pl.pallas_call(
        functools.partial(_fnet_stage1_kernel, n2c=n2c, P=P),
        out_shape=jax.ShapeDtypeStruct((B, P, 2, P, D), F32),
        grid=(B, P // n2c),
        in_specs=[pl.BlockSpec((P, n2c, D), lambda b, j: (b, j, 0)), full(g), modspec(1), modspec(0),
                  pl.BlockSpec((n2c, 2 * P, P), lambda b, j: (j, 0, 0))],
        out_specs=pl.BlockSpec((1, P, 2, n2c, D), lambda b, j: (b, 0, 0, j, 0)),
        compiler_params=_cp(("arbitrary", "arbitrary")),
        name="fnet_stage1",
    )(x3, g, modv, modv, m1)
    f_lat = pl.pallas_call(
        functools.partial(_fnet_stage2_kernel, k1c=k1c, P=P, gc=gc),
        out_shape=jax.ShapeDtypeStruct((lay.NL // P, P, D), F32),
        grid=(B, P // k1c),
        in_specs=[pl.BlockSpec((1, k1c, 2, P, D), lambda b, j: (b, j, 0, 0, 0)), full(m2), full(mc)],
        out_specs=pl.BlockSpec((P, k1c, D), lambda b, j: (b, j, 0)),
        compiler_params=_cp(("arbitrary", "arbitrary")),
        name="fnet_stage2",
    )(z, m2, mc)
    cblk0 = lay.NL // C
    ctx_mod = lambda chunk: pl.BlockSpec((1, 1, D), lambda b: (chunk * SUBLANES + B, 0, 0))
    f_ctx = pl.pallas_call(
        functools.partial(_fnet_ctx_kernel, C=C, gc=gc),
        out_shape=jax.ShapeDtypeStruct((lay.NC, D), F32),
        grid=(B,),
        in_specs=[pl.BlockSpec((C, D), lambda b: (cblk0 + b, 0)), full(g), ctx_mod(1), ctx_mod(0), full(mctx),
                  full(mc)],
        out_specs=pl.BlockSpec((C, D), lambda b: (b, 0)),
        compiler_params=_cp(("arbitrary",)),
        name="fnet_ctx",
    )(x, g, modv, modv, mctx, mc)
    return f_lat.reshape(lay.NL, D), f_ctx


NA_QROWS = 2
NA_KROWS = NA_KR + NA_QROWS
NA_VARIANTS = 5
NA_ONES = 16


def _na_bias_kernel(rpb_ref, sel_ref, toe_ref, o_ref):
    g = jnp.dot(rpb_ref[0], toe_ref[...], precision=HIGHEST, preferred_element_type=F32)
    for t in range(NA_VARIANTS):
        o_ref[t, 0] = jnp.dot(sel_ref[t], g, precision=HIGHEST, preferred_element_type=F32)


def _na_window_start(r, rows):
    return jnp.clip(r - NA_KR // 2, 0, rows - NA_KROWS)


def _na_bias_tables(rpb, rows):
    assert rows >= 16 and rows % NA_QROWS == 0
    W = GRID_W
    nh, nu, nv = rpb.shape
    up, vp, ajp = 2 * SUBLANES, LANES, 3 * SUBLANES
    assert nu <= up and nv <= vp and NA_QROWS * NA_KROWS <= ajp
    cq, ck = np.arange(W)[:, None], np.arange(W)[None, :]
    dc = np.clip(ck - cq + NA_KC - 1, 0, nv - 1).reshape(-1)
    toe = (np.arange(vp)[:, None] == dc[None, :]).astype(np.float32)
    cs = np.clip(cq - NA_KC // 2, 0, W - NA_KC)
    col_ok = (ck >= cs) & (ck < cs + NA_KC)
    sel = np.zeros((NA_VARIANTS, ajp, up), np.float32)
    valid = np.zeros((NA_VARIANTS, NA_QROWS, NA_KROWS, W, W), bool)
    for t, r in enumerate((0, 2, 6, rows - 4, rows - 2)):
        w0 = min(max(r - NA_KR // 2, 0), rows - NA_KROWS)
        for a in range(NA_QROWS):
            rs = min(max(r + a - NA_KR // 2, 0), rows - NA_KR)
            for j in range(NA_KROWS):
                rk = w0 + j
                sel[t, a * NA_KROWS + j, min(max(rk - (r + a) + NA_KR - 1, 0), nu - 1)] = 1.0
                if rs <= rk < rs + NA_KR:
                    valid[t, a, j] = col_ok
    rpb_p = jnp.pad(rpb, ((0, 0), (0, up - nu), (0, vp - nv)))
    tab = pl.pallas_call(
        _na_bias_kernel,
        out_shape=jax.ShapeDtypeStruct((NA_VARIANTS, nh, ajp, W * W), F32),
        grid=(nh,),
        in_specs=[pl.BlockSpec((1, up, vp), lambda h: (h, 0, 0)),
                  pl.BlockSpec(sel.shape, lambda h: (0, 0, 0)),
                  pl.BlockSpec(toe.shape, lambda h: (0, 0))],
        out_specs=pl.BlockSpec((NA_VARIANTS, 1, ajp, W * W), lambda h: (0, h, 0, 0)),
        compiler_params=_cp(("arbitrary",)),
        name="na_bias",
    )(rpb_p, jnp.asarray(sel), jnp.asarray(toe))
    tab = tab[:, :, :NA_QROWS * NA_KROWS].reshape(NA_VARIANTS, nh, NA_QROWS, NA_KROWS, W, W)
    tab = jnp.where(jnp.asarray(valid)[:, None], tab * math.log2(math.e), NEG_INF)
    tab = tab.reshape(NA_VARIANTS, nh // 2, 2, NA_QROWS, NA_KROWS, W, W)
    return tab.transpose(0, 1, 4, 6, 2, 3, 5).reshape(NA_VARIANTS, nh // 2, NA_KROWS * W, 2 * NA_QROWS * W)


def _na_kernel(*refs, rows, local):
    if local:
        qt_ref, kc_ref, vtc_ref, kl_ref, vtl_ref, bias_ref, o_ref, st_a, sc_a, st_b, sc_b = refs
    else:
        qt_ref, kc_ref, vtc_ref, o_ref, sc_a, sc_b = refs
        st_a = st_b = None
    nq = NA_QROWS * GRID_W
    nk = NA_KROWS * GRID_W
    kc, vtc = kc_ref[...], vtc_ref[...]
    row = lax.broadcasted_iota(I32, (LANES, nq), 0)
    pairs = qt_ref.shape[1] // nq
    qi = pl.program_id(2) if local else 0

    def ext(vt):
        return jnp.concatenate([vt, jnp.ones((NA_ONES, vt.shape[1]), BF16)], axis=0)

    def window(t):
        r = (qi * pairs + t) * NA_QROWS
        k0 = pl.multiple_of(_na_window_start(r, rows) * GRID_W, LANES)
        var = jnp.where(r == 0, 0, jnp.where(r == 2, 1, jnp.where(r == rows - 4, 3, jnp.where(r == rows - 2, 4, 2))))
        return k0, var

    def scores(t, st_ref, sc_ref):
        qt = qt_ref[:, t * nq:(t + 1) * nq]
        zero = jnp.zeros_like(qt)
        qbd = jnp.concatenate([jnp.where(row < NA_HD, qt, zero), jnp.where(row >= NA_HD, qt, zero)], axis=1)
        sc_ref[...] = _dot(kc, qbd)
        if local:
            k0, var = window(t)
            st_ref[...] = _dot(kl_ref[pl.ds(k0, nk), :], qbd) + bias_ref[var, 0]

    def finish(t, st_ref, sc_ref):
        sc = sc_ref[...]
        m = jnp.max(sc, axis=0, keepdims=True)
        if local:
            st = st_ref[...]
            m = jnp.maximum(m, jnp.max(st, axis=0, keepdims=True))
            pt = jnp.exp2(st - m).astype(BF16)
            vtw = vtl_ref[:, pl.ds(window(t)[0], nk)]
        pc = jnp.exp2(sc - m).astype(BF16)
        outs = []
        for hl in range(2):
            hs, qs = slice(hl * NA_HD, (hl + 1) * NA_HD), slice(hl * nq, (hl + 1) * nq)
            acc = _dot(ext(vtc[hs, :]), pc[:, qs])
            if local:
                acc = acc + _dot(ext(vtw[hs, :]), pt[:, qs])
            outs.append(acc[:NA_HD] / acc[NA_HD:NA_HD + 1])
        o_ref[t * nq:(t + 1) * nq, :] = jnp.concatenate(outs, axis=0).T.astype(o_ref.dtype)

    bufs = ((st_a, sc_a), (st_b, sc_b))
    scores(0, *bufs[0])
    for t in range(pairs):
        if t + 1 < pairs:
            scores(t + 1, *bufs[(t + 1) % 2])
        finish(t, *bufs[t % 2])


def _na_attention(lay, qt, k, vt, bias, *, tq=2048):
    B, S, C = lay.B, lay.S, lay.C
    tq = min(tq, S)
    rows = S // GRID_W
    HP = NA_HEADS // 2
    nq = S // tq
    cblk0 = lay.NL // C
    nk, nqp = NA_KROWS * GRID_W, NA_QROWS * GRID_W
    o_lat = pl.pallas_call(
        functools.partial(_na_kernel, rows=rows, local=True),
        out_shape=jax.ShapeDtypeStruct((lay.NL, NA_HEADS * NA_HD), BF16),
        grid=(B, HP, nq),
        in_specs=[pl.BlockSpec((LANES, tq), lambda b, h, i: (h, b * nq + i)),
                  pl.BlockSpec((C, LANES), lambda b, h, i: (cblk0 + b, h)),
                  pl.BlockSpec((LANES, C), lambda b, h, i: (h, cblk0 + b)),
                  pl.BlockSpec((S, LANES), lambda b, h, i: (b, h)),
                  pl.BlockSpec((LANES, S), lambda b, h, i: (h, b)),
                  pl.BlockSpec((NA_VARIANTS, 1) + bias.shape[2:], lambda b, h, i: (0, h, 0, 0))],
        out_specs=pl.BlockSpec((tq, LANES), lambda b, h, i: (b * nq + i, h)),
        scratch_shapes=[pltpu.VMEM((nk, 2 * nqp), F32), pltpu.VMEM((C, 2 * nqp), F32)] * 2,
        compiler_params=_cp(("arbitrary", "arbitrary", "arbitrary")),
        name="na_attn_latent",
    )(qt, k, vt, k, vt, bias)
    o_ctx = pl.pallas_call(
        functools.partial(_na_kernel, rows=rows, local=False),
        out_shape=jax.ShapeDtypeStruct((lay.NC, NA_HEADS * NA_HD), BF16),
        grid=(B, HP),
        in_specs=[pl.BlockSpec((LANES, C), lambda b, h: (h, cblk0 + b)),
                  pl.BlockSpec((C, LANES), lambda b, h: (cblk0 + b, h)),
                  pl.BlockSpec((LANES, C), lambda b, h: (h, cblk0 + b))],
        out_specs=pl.BlockSpec((C, LANES), lambda b, h: (b, h)),
        scratch_shapes=[pltpu.VMEM((C, 2 * nqp), F32)] * 2,
        compiler_params=_cp(("arbitrary", "arbitrary")),
        name="na_attn_ctx",
    )(qt, k, vt)
    return o_lat, o_ctx


SW_SUB = 128
SW_BAND = SW_SUB + 2 * SW_WINDOW
SW_ONES = 16


def _swa_kernel(*refs, S, local):
    if local:
        (sink_ref, qt_ref, kc_ref, vtc_ref, kp_ref, kcur_ref, kn_ref, vtp_ref, vtcur_ref, vtn_ref, o_ref,
         kbuf, vtbuf, st_a, sc_a, st_b, sc_b) = refs
        tq = qt_ref.shape[1]
        W = SW_WINDOW
        kbuf[0:W] = kp_ref[...]
        kbuf[W:W + tq] = kcur_ref[...]
        kbuf[W + tq:] = kn_ref[...]
        vtbuf[:, 0:W] = vtp_ref[...]
        vtbuf[:, W:W + tq] = vtcur_ref[...]
        vtbuf[:, W + tq:] = vtn_ref[...]
        i = pl.program_id(1)
    else:
        sink_ref, qt_ref, kc_ref, vtc_ref, o_ref, sc_a, sc_b = refs
        st_a = st_b = None
        tq = qt_ref.shape[1]
    G = SW_HEADS // SW_KV_HEADS
    kc, vtc = kc_ref[...], vtc_ref[...]
    zeros_q = jnp.zeros((SW_HD, G * SW_SUB), BF16)

    def ext(vt):
        return jnp.concatenate([vt, jnp.ones((SW_ONES, vt.shape[1]), BF16)], axis=0)

    def band_bias(sb):
        key = lax.broadcasted_iota(I32, (SW_BAND, SW_SUB), 0)
        qry = lax.broadcasted_iota(I32, (SW_BAND, SW_SUB), 1)
        rel = key - SW_WINDOW - qry
        kpos = i * tq + sb * SW_SUB - SW_WINDOW + key
        ok = (jnp.abs(rel) <= SW_WINDOW) & (kpos >= 0) & (kpos < S)
        bias = jnp.where(ok, 0.0, NEG_INF).astype(F32)
        return jnp.concatenate([bias] * G, axis=1)

    def scores(sb, g, bias, st_ref, sc_ref):
        r0 = sb * SW_SUB
        tile, half = g // 2, g % 2
        sl = slice(tile * LANES, (tile + 1) * LANES)
        qg = jnp.concatenate([qt_ref[(G * g + hl) * SW_HD:(G * g + hl + 1) * SW_HD, r0:r0 + SW_SUB]
                              for hl in range(G)], axis=1)
        qpad = jnp.concatenate([qg, zeros_q] if half == 0 else [zeros_q, qg], axis=0)
        sc_ref[...] = _dot(kc[:, sl], qpad)
        if local:
            st_ref[...] = _dot(kbuf[r0:r0 + SW_BAND, sl], qpad) + bias

    def finish(sb, g, st_ref, sc_ref):
        r0 = sb * SW_SUB
        sink = sink_ref[g]
        sc = sc_ref[...]
        m = jnp.maximum(jnp.max(sc, axis=0, keepdims=True), sink)
        if local:
            st = st_ref[...]
            m = jnp.maximum(m, jnp.max(st, axis=0, keepdims=True))
        acc = _dot(ext(vtc[g * SW_HD:(g + 1) * SW_HD, :]), jnp.exp2(sc - m).astype(BF16))
        if local:
            acc = acc + _dot(ext(vtbuf[g * SW_HD:(g + 1) * SW_HD, r0:r0 + SW_BAND]), jnp.exp2(st - m).astype(BF16))
        og = acc[:SW_HD] / (acc[SW_HD:SW_HD + 1] + jnp.exp2(sink - m))
        return [og[:, hl * SW_SUB:(hl + 1) * SW_SUB] for hl in range(G)]

    items = [(sb, g) for sb in range(tq // SW_SUB) for g in range(SW_KV_HEADS)]
    bufs = ((st_a, sc_a), (st_b, sc_b))
    bias = band_bias(0) if local else None
    scores(*items[0], bias, *bufs[0])
    outs = []
    for n, (sb, g) in enumerate(items):
        if n + 1 < len(items):
            nsb, ng = items[n + 1]
            if local and ng == 0:
                bias = band_bias(nsb)
            scores(nsb, ng, bias, *bufs[(n + 1) % 2])
        outs.extend(finish(sb, g, *bufs[n % 2]))
        if g == SW_KV_HEADS - 1:
            o_ref[sb * SW_SUB:(sb + 1) * SW_SUB, :] = jnp.concatenate(outs, axis=0).T.astype(o_ref.dtype)
            outs = []


def _swa_attention(lay, qt, k, vt, sinks, *, tq=1024):
    B, S, C = lay.B, lay.S, lay.C
    tq = min(tq, S)
    nq_rows = SW_HEADS * SW_HD
    nkv = SW_KV_HEADS * SW_HD
    G = SW_HEADS // SW_KV_HEADS
    nq = S // tq
    per = tq // SW_WINDOW
    last = lay.NT // SW_WINDOW - 1
    cblk0 = lay.NL // C
    prev = lambda b, i: jnp.maximum((b * nq + i) * per - 1, 0)
    nxt = lambda b, i: jnp.minimum((b * nq + i + 1) * per, last)
    sink_l = jnp.repeat(sinks.reshape(SW_KV_HEADS, 1, G) * math.log2(math.e), SW_SUB, axis=2)
    sink_spec = pl.BlockSpec(sink_l.shape, lambda *a: (0, 0, 0))
    o_lat = pl.pallas_call(
        functools.partial(_swa_kernel, S=S, local=True),
        out_shape=jax.ShapeDtypeStruct((lay.NL, nq_rows), BF16),
        grid=(B, nq),
        in_specs=[sink_spec,
                  pl.BlockSpec((nq_rows, tq), lambda b, i: (0, b * nq + i)),
                  pl.BlockSpec((C, nkv), lambda b, i: (cblk0 + b, 0)),
                  pl.BlockSpec((nkv, C), lambda b, i: (0, cblk0 + b)),
                  pl.BlockSpec((SW_WINDOW, nkv), lambda b, i: (prev(b, i), 0)),
                  pl.BlockSpec((tq, nkv), lambda b, i: (b * nq + i, 0)),
                  pl.BlockSpec((SW_WINDOW, nkv), lambda b, i: (nxt(b, i), 0)),
                  pl.BlockSpec((nkv, SW_WINDOW), lambda b, i: (0, prev(b, i))),
                  pl.BlockSpec((nkv, tq), lambda b, i: (0, b * nq + i)),
                  pl.BlockSpec((nkv, SW_WINDOW), lambda b, i: (0, nxt(b, i)))],
        out_specs=pl.BlockSpec((tq, nq_rows), lambda b, i: (b * nq + i, 0)),
        scratch_shapes=([pltpu.VMEM((tq + 2 * SW_WINDOW, nkv), BF16), pltpu.VMEM((nkv, tq + 2 * SW_WINDOW), BF16)]
                        + [pltpu.VMEM((SW_BAND, G * SW_SUB), F32), pltpu.VMEM((C, G * SW_SUB), F32)] * 2),
        compiler_params=_cp(("arbitrary", "arbitrary")),
        name="swa_attn_latent",
    )(sink_l, qt, k, vt, k, k, k, vt, vt, vt)
    o_ctx = pl.pallas_call(
        functools.partial(_swa_kernel, S=S, local=False),
        out_shape=jax.ShapeDtypeStruct((lay.NC, nq_rows), BF16),
        grid=(B,),
        in_specs=[sink_spec,
                  pl.BlockSpec((nq_rows, C), lambda b: (0, cblk0 + b)),
                  pl.BlockSpec((C, nkv), lambda b: (cblk0 + b, 0)),
                  pl.BlockSpec((nkv, C), lambda b: (0, cblk0 + b))],
        out_specs=pl.BlockSpec((C, nq_rows), lambda b: (b, 0)),
        scratch_shapes=[pltpu.VMEM((C, G * SW_SUB), F32)] * 2,
        compiler_params=_cp(("arbitrary",)),
        name="swa_attn_ctx",
    )(sink_l, qt, k, vt)
    return o_lat, o_ctx


def _route(logits, tri, carry):
    lane = _lane_iota(logits.shape)
    lanef = lane.astype(F32)
    big = float(LANES)
    rowmax = lambda t: jnp.max(t, axis=-1, keepdims=True)
    rowmin = lambda t: jnp.min(t, axis=-1, keepdims=True)
    rowsum = lambda t: jnp.sum(t, axis=-1, keepdims=True)
    is_g = lane < MOE_GROUPS
    mg = rowmax(jnp.where(is_g, logits, -jnp.inf))
    w_g = 1.0 / rowsum(jnp.where(is_g, jnp.exp(logits - mg), 0.0))
    gidx = rowmin(jnp.where(is_g & (logits == mg), lanef, big))
    g0 = MOE_GROUPS + MOE_PER_GROUP * gidx
    in_grp = (lanef >= g0) & (lanef < g0 + MOE_PER_GROUP)
    le = jnp.where(in_grp, logits, -jnp.inf)
    m1 = rowmax(le)
    i1 = rowmin(jnp.where(in_grp & (le == m1), lanef, big))
    le2 = jnp.where(lanef == i1, -jnp.inf, le)
    m2 = rowmax(le2)
    i2 = rowmin(jnp.where(in_grp & (lanef != i1) & (le2 == m2), lanef, big))
    r = jnp.exp(m2 - m1)
    gate1 = w_g / (1.0 + r)
    gate2 = w_g * r / (1.0 + r)
    sel1, sel2 = lanef == i1, lanef == i2
    member = (sel1 | sel2)
    cum = _dot(tri, member.astype(BF16)) + carry
    rank1 = rowsum(jnp.where(sel1, cum, 0.0))
    rank2 = rowsum(jnp.where(sel2, cum, 0.0))
    new_carry = carry + jnp.sum(member.astype(F32), axis=0, keepdims=True)
    rec = jnp.zeros_like(logits)
    for ln, val in ((0, i1 - MOE_GROUPS), (1, i2 - MOE_GROUPS), (2, rank1), (3, rank2), (4, gate1), (5, gate2)):
        rec = jnp.where(lane == ln, val, rec)
    return rec, new_carry


def _out_proj_kernel(*refs, has_bias, n_lat_tiles):
    if has_bias:
        (al_ref, ac_ref, x_ref, w_ref, b_ref, g1_ref, g2n_ref, sc_ref, sh_ref, wr_ref, br_ref, tri_ref,
         xo_ref, h2_ref, rec_ref, rect_ref, cnt_ref, carry_ref) = refs
    else:
        (al_ref, ac_ref, x_ref, w_ref, g1_ref, g2n_ref, sc_ref, sh_ref, wr_ref, br_ref, tri_ref,
         xo_ref, h2_ref, rec_ref, rect_ref, cnt_ref, carry_ref) = refs

    @pl.when(pl.program_id(0) == 0)
    def _():
        carry_ref[...] = jnp.zeros_like(carry_ref)

    a = jnp.where(pl.program_id(0) < n_lat_tiles, al_ref[...], ac_ref[...])
    y = _dot(a.astype(BF16), w_ref[...])
    if has_bias:
        y = y + b_ref[...]
    xn = x_ref[...] + g1_ref[0] * y
    xo_ref[...] = xn
    h2 = _normmod(xn, g2n_ref[...], sc_ref[0], sh_ref[0])
    h2_ref[...] = h2
    h_hi = h2.astype(BF16)
    h_lo = (h2 - h_hi.astype(F32)).astype(BF16)
    hw = _dot(h_hi, wr_ref[...])
    logits = hw[:, :LANES] + hw[:, LANES:] + _dot(h_lo, wr_ref[:, :LANES]) + br_ref[...]
    rec, carry = _route(logits, tri_ref[...], carry_ref[...])
    rec_ref[...] = rec
    rect_ref[...] = rec.T[:SUBLANES, :]
    carry_ref[...] = carry
    cnt_ref[...] = jnp.broadcast_to(carry, cnt_ref.shape)


def _out_proj(lay, a, x, w, b, g2n, modv, wr, br, tri):
    D = lay.D
    a_lat, a_ctx = a
    nl = lay.nl_tiles
    full = lambda t: pl.BlockSpec(t.shape, lambda i: (0,) * t.ndim)
    row = lambda n: pl.BlockSpec((TM, n), lambda i: (i, 0))
    ins = [a_lat, a_ctx, x, w] + ([b] if b is not None else []) + [modv, g2n, modv, modv, wr, br, tri]
    specs = ([pl.BlockSpec((TM, a_lat.shape[1]), lambda i: (jnp.minimum(i, nl - 1), 0)),
              pl.BlockSpec((TM, a_ctx.shape[1]), lambda i: (jnp.maximum(i - nl, 0), 0)), row(D), full(w)]
             + ([full(b)] if b is not None else [])
             + [lay.mod_spec(2), full(g2n), lay.mod_spec(4), lay.mod_spec(3), full(wr), full(br), full(tri)])
    return pl.pallas_call(
        functools.partial(_out_proj_kernel, has_bias=b is not None, n_lat_tiles=nl),
        out_shape=(jax.ShapeDtypeStruct((lay.NT, D), F32), jax.ShapeDtypeStruct((lay.NT, D), F32),
                   jax.ShapeDtypeStruct((lay.NT, LANES), F32), jax.ShapeDtypeStruct((SUBLANES, lay.NT), F32),
                   jax.ShapeDtypeStruct((SUBLANES, LANES), F32)),
        grid=(lay.n_tiles,),
        in_specs=specs,
        out_specs=(row(D), row(D), row(LANES), pl.BlockSpec((SUBLANES, TM), lambda i: (0, i)),
                   pl.BlockSpec((SUBLANES, LANES), lambda i: (0, 0))),
        scratch_shapes=[pltpu.VMEM((1, LANES), F32)],
        input_output_aliases={2: 0},
        compiler_params=_cp(("arbitrary",)),
        name="out_proj_router",
    )(*ins)


def _row_copy(src, s, dst, d, sem):
    return pltpu.make_async_copy(src.at[pl.ds(s, 1), :], dst.at[pl.ds(d, 1), :], sem)


def _dispatch_kernel(pos_ref, pend_ref, nu_ref, h_ref, xb_ref, zbuf, sem, zsem):
    base = pl.program_id(0) * TM
    n_tok = pos_ref.shape[0] // 2
    nblk = xb_ref.shape[0] // MOE_BM

    @pl.when(pl.program_id(0) == 0)
    def _():
        zbuf[...] = jnp.zeros_like(zbuf)

        def zero_block(row0):
            return pltpu.make_async_copy(zbuf, xb_ref.at[pl.ds(pl.multiple_of(row0, MOE_BM), MOE_BM), :], zsem)

        def each(fn):
            def expert(e, c):
                end = pend_ref[e]

                @pl.when(end > jnp.where(e > 0, pend_ref[jnp.maximum(e - 1, 0)], 0))
                def _():
                    fn(zero_block(end - MOE_BM))
                return c

            def tail(j, c):
                @pl.when(j >= nu_ref[0])
                def _():
                    fn(zero_block(j * MOE_BM))
                return c

            lax.fori_loop(0, MOE_EXPERTS, expert, 0)
            lax.fori_loop(0, nblk, tail, 0)

        each(lambda cp: cp.start())
        each(lambda cp: cp.wait())

    def issue(r, c):
        for k in range(2):
            _row_copy(h_ref, r, xb_ref, pos_ref[k * n_tok + base + r], sem).start(priority=k)
        return c

    lax.fori_loop(0, TM, issue, 0, unroll=8)
    for k in range(2):
        pltpu.make_async_copy(h_ref, xb_ref.at[pl.ds(0, TM), :], sem).wait()


def _dispatch(lay, pos, pend, n_used, h2, cap):
    D = lay.D
    return pl.pallas_call(
        _dispatch_kernel,
        out_shape=jax.ShapeDtypeStruct((cap, D), F32),
        grid_spec=pltpu.PrefetchScalarGridSpec(
            num_scalar_prefetch=3, grid=(lay.n_tiles,),
            in_specs=[pl.BlockSpec((TM, D), lambda i, p, e, n: (i, 0))],
            out_specs=pl.BlockSpec(memory_space=pl.ANY),
            scratch_shapes=[pltpu.VMEM((MOE_BM, D), F32), pltpu.SemaphoreType.DMA, pltpu.SemaphoreType.DMA]),
        compiler_params=_cp(("arbitrary",)),
        name="moe_dispatch",
    )(pos, pend, n_used, h2)


def _expert_kernel(be_ref, nu_ref, xb_ref, wg_ref, wu_ref, wd_ref, yb_ref, wgb, wub, wdb):
    j = pl.program_id(0)
    prev = be_ref[jnp.maximum(j - 1, 0)]

    @pl.when((j == 0) | (be_ref[j] != prev))
    def _():
        wgb[...] = wg_ref[0, 0].astype(BF16)
        wub[...] = wu_ref[0, 0].astype(BF16)
        wdb[...] = wd_ref[0, 0].astype(BF16)

    @pl.when(j < nu_ref[0])
    def _():
        xe = xb_ref[...].astype(BF16)
        g = _dot(xe, wgb[...])
        u = _dot(xe, wub[...])
        act = (g * jax.nn.sigmoid(g) * u).astype(BF16)
        yb_ref[...] = _dot(act, wdb[...])

    @pl.when(j >= nu_ref[0])
    def _():
        yb_ref[...] = jnp.zeros_like(yb_ref)


def _experts(xb, blk_e, n_used, w_gate, w_up, w_down, layer):
    cap, D = xb.shape
    FF = w_gate.shape[-1]
    nblk = cap // MOE_BM
    blk = lambda j, be, nu: (jnp.maximum(jnp.minimum(j, nu[0] - 1), 0), 0)
    wblk = lambda j, be, nu: (layer, be[j], 0, 0)
    return pl.pallas_call(
        _expert_kernel,
        out_shape=jax.ShapeDtypeStruct((cap, D), F32),
        grid_spec=pltpu.PrefetchScalarGridSpec(
            num_scalar_prefetch=2, grid=(nblk,),
            in_specs=[pl.BlockSpec((MOE_BM, D), blk),
                      pl.BlockSpec((1, 1, D, FF), wblk), pl.BlockSpec((1, 1, D, FF), wblk),
                      pl.BlockSpec((1, 1, FF, D), wblk)],
            out_specs=pl.BlockSpec((MOE_BM, D), lambda j, be, nu: (j, 0)),
            scratch_shapes=[pltpu.VMEM((D, FF), BF16), pltpu.VMEM((D, FF), BF16), pltpu.VMEM((FF, D), BF16)]),
        compiler_params=_cp(("arbitrary",)),
        name="moe_experts",
    )(blk_e, n_used, xb, w_gate, w_up, w_down)


def _combine_kernel(*refs, final):
    if final:
        pos_ref, x_ref, rec_ref, g2_ref, yb_ref, fg_ref, o_ref, buf, sem = refs
    else:
        pos_ref, x_ref, rec_ref, g2_ref, yb_ref, o_ref, buf, sem = refs
    i = pl.program_id(0)
    n_tok = pos_ref.shape[0] // 2
    slot = i % 2

    def gather(tile, s):
        def issue(r, c):
            for k in range(2):
                _row_copy(yb_ref, pos_ref[k * n_tok + tile * TM + r], buf.at[s, k], r, sem.at[s]).start(priority=k)
            return c
        lax.fori_loop(0, TM, issue, 0, unroll=8)

    @pl.when(i == 0)
    def _():
        gather(0, 0)

    for k in range(2):
        pltpu.make_async_copy(yb_ref.at[pl.ds(0, TM), :], buf.at[slot, k], sem.at[slot]).wait()

    @pl.when(i + 1 < pl.num_programs(0))
    def _():
        gather(i + 1, 1 - slot)

    rec = rec_ref[...]
    f = (rec[:, ROUTE_LANE_GATE:ROUTE_LANE_GATE + 1] * buf[slot, 0]
         + rec[:, ROUTE_LANE_GATE + 1:ROUTE_LANE_GATE + 2] * buf[slot, 1])
    xn = x_ref[...] + g2_ref[0] * f
    if final:
        xn = _rms(xn, fg_ref[...])
    o_ref[...] = xn


def _combine(lay, pos, x, rec, modv, yb, final_g):
    D = lay.D
    final = final_g is not None
    row = lambda n: pl.BlockSpec((TM, n), lambda i, p: (i, 0))
    specs = [row(D), row(LANES),
             pl.BlockSpec((1, 1, D), lambda i, p: (5 * SUBLANES + lay.mod_row(i), 0, 0)),
             pl.BlockSpec(memory_space=pl.ANY)]
    ins = [x, rec, modv, yb]
    if final:
        specs.append(pl.BlockSpec(final_g.shape, lambda i, p: (0, 0)))
        ins.append(final_g)
    n_rows, n_tiles = (lay.NL, lay.nl_tiles) if final else (lay.NT, lay.n_tiles)
    return pl.pallas_call(
        functools.partial(_combine_kernel, final=final),
        out_shape=jax.ShapeDtypeStruct((n_rows, D), F32),
        grid_spec=pltpu.PrefetchScalarGridSpec(
            num_scalar_prefetch=1, grid=(n_tiles,),
            in_specs=specs,
            out_specs=row(D),
            scratch_shapes=[pltpu.VMEM((2, 2, TM, D), F32), pltpu.SemaphoreType.DMA((2,))]),
        input_output_aliases={} if final else {1: 0},
        compiler_params=_cp(("arbitrary",)),
        name="moe_combine",
    )(pos, *ins)


def _moe(lay, x, h2, rec, rect, counts, modv, w_gate, w_up, w_down, layer, final_g):
    T = 2 * lay.NT
    cap = -(-T // MOE_BM) * MOE_BM + MOE_EXPERTS * MOE_BM
    cnt = counts[0, MOE_GROUPS:MOE_GROUPS + MOE_EXPERTS].astype(I32)
    pcnt = (cnt + MOE_BM - 1) // MOE_BM * MOE_BM
    pend = jnp.cumsum(pcnt)
    start = pend - pcnt
    eid = rect[ROUTE_LANE_EID:ROUTE_LANE_EID + 2].astype(I32)
    rank = rect[ROUTE_LANE_RANK:ROUTE_LANE_RANK + 2].astype(I32)
    eid, rank = eid.reshape(1, -1), rank.reshape(-1)
    hit = eid == jnp.arange(MOE_EXPERTS, dtype=I32)[:, None]
    pos = jnp.sum(jnp.where(hit, start[:, None], 0), axis=0) + rank
    blk_row = jnp.arange(cap // MOE_BM, dtype=I32) * MOE_BM
    blk_e = jnp.minimum(jnp.sum((pend[None, :] <= blk_row[:, None]).astype(I32), axis=1), MOE_EXPERTS - 1)
    n_used = (pend[-1:] // MOE_BM).astype(I32)
    xb = _dispatch(lay, pos, pend.astype(I32), n_used, h2, cap)
    yb = _experts(xb, blk_e, n_used, w_gate, w_up, w_down, layer)
    return _combine(lay, pos, x, rec, modv, yb, final_g)


def _rope_tables(lay):
    S = lay.S
    t = jnp.arange(S)
    n = MLA_ROPE // 4
    inv = ROPE_THETA ** (-jnp.arange(n, dtype=F32) / n)
    ang = jnp.concatenate([(t // GRID_W).astype(F32)[:, None] * inv, (t % GRID_W).astype(F32)[:, None] * inv], axis=-1)
    cos, sin = jnp.cos(ang), jnp.sin(ang)
    cos64 = jnp.concatenate([cos, cos], axis=-1)
    sin64 = jnp.concatenate([-sin, sin], axis=-1)
    rows = lambda lat, ctx_val: jnp.concatenate([jnp.tile(lat, (lay.B, 1)), jnp.full((lay.NC, 64), ctx_val, F32)], axis=0)
    cos64, sin64 = rows(cos64, 1.0), rows(sin64, 0.0)
    zero = jnp.zeros_like(cos64)
    return ((jnp.concatenate([cos64, zero], axis=1), jnp.concatenate([sin64, zero], axis=1)),
            (jnp.concatenate([cos64, cos64], axis=1), jnp.concatenate([sin64, sin64], axis=1)))


def kernel(x, c, ctx, c_ctx, mod_w, mod_b, norm1_g, norm2_g, mla_w_dq, mla_g_q, mla_w_uq, mla_w_dkv, mla_g_kv, mla_w_ukv, mla_w_o, fnet_w_o, fnet_b_o, na_w_qkv, na_rpb, na_w_o, swa_w_qkv, swa_sinks, swa_w_o, moe_w_grp, moe_b_grp, moe_w_rt, moe_b_rt, moe_w_gate, moe_w_up, moe_w_down, final_g):
    B, S, D = x.shape
    C = ctx.shape[1]
    depth = mod_w.shape[0]
    lay = _Layout(B, S, C, D)
    X = jnp.concatenate([x.reshape(B * S, D), ctx.reshape(B * C, D)], axis=0)
    cond = jnp.concatenate([c, c_ctx[None], jnp.zeros((SUBLANES - B - 1, D), F32)], axis=0)
    mod = _modulation(cond, mod_w, mod_b)
    (mla_cos, mla_sin), (swa_cos, swa_sin) = _rope_tables(lay)
    tri = (jnp.arange(TM)[:, None] > jnp.arange(TM)[None, :]).astype(BF16)
    n_mix = 4
    for i in range(depth):
        m, j = i % n_mix, i // n_mix
        modv = mod[i].reshape(SUBLANES, 6, D).transpose(1, 0, 2).reshape(6 * SUBLANES, 1, D)
        g1n, g2n = norm1_g[i][None], norm2_g[i][None]
        bias = None
        if m == 0:
            w1 = jnp.concatenate([mla_w_dq[j], mla_w_dkv[j], jnp.zeros((D, LANES - MLA_ROPE), F32)], axis=1).astype(BF16)
            wq = mla_w_uq[j].reshape(MLA_Q_RANK, MLA_HEADS, MLA_NOPE + MLA_ROPE)
            wq = jnp.concatenate([wq, jnp.zeros((MLA_Q_RANK, MLA_HEADS, LANES - MLA_ROPE), F32)], axis=-1)
            wq = wq.reshape(MLA_Q_RANK, MLA_HEADS * 2 * LANES).astype(BF16)
            q, k, v = _mla_proj(lay, X, g1n, modv, w1, mla_g_q[j][None], mla_g_kv[j][None], wq,
                                mla_w_ukv[j].astype(BF16), mla_cos, mla_sin)
            a = _mla_attention(lay, q, k, v)
            w_o = mla_w_o[j]
        elif m == 1:
            a = _fnet_mix(lay, X, g1n, modv, _dft_tables(S, C, D // FNET_GROUPS))
            w_o, bias = fnet_w_o[j], fnet_b_o[j][None]
        elif m == 2:
            qt, k, vt = _qkv_proj(lay, X, g1n, modv, na_w_qkv[j].astype(BF16), swa_cos, swa_sin,
                                  n_q=NA_HEADS * NA_HD, n_k=NA_HEADS * NA_HD, rope=False,
                                  q_scale=NA_HD ** -0.5 * math.log2(math.e))
            a = _na_attention(lay, qt, k, vt, _na_bias_tables(na_rpb[j], S // GRID_W))
            w_o = na_w_o[j]
        else:
            qt, k, vt = _qkv_proj(lay, X, g1n, modv, swa_w_qkv[j].astype(BF16), swa_cos, swa_sin,
                                  n_q=SW_HEADS * SW_HD, n_k=SW_KV_HEADS * SW_HD, rope=True,
                                  q_scale=SW_HD ** -0.5 * math.log2(math.e))
            a = _swa_attention(lay, qt, k, vt, swa_sinks[j])
            w_o = swa_w_o[j]
        wr = jnp.concatenate([moe_w_grp[i], moe_w_rt[i], jnp.zeros((D, LANES - MOE_GROUPS - MOE_EXPERTS), F32)], axis=1)
        br = jnp.concatenate([moe_b_grp[i], moe_b_rt[i], jnp.zeros((LANES - MOE_GROUPS - MOE_EXPERTS,), F32)])[None]
        wr_hi = wr.astype(BF16)
        wr2 = jnp.concatenate([wr_hi, (wr - wr_hi.astype(F32)).astype(BF16)], axis=1)
        X, h2, rec, rect, counts = _out_proj(lay, a, X, w_o.astype(BF16), bias, g2n, modv, wr2, br, tri)
        X = _moe(lay, X, h2, rec, rect, counts, modv, moe_w_gate, moe_w_up, moe_w_down, i,
                 final_g[None] if i == depth - 1 else None)
    return X.reshape(B, S, D)
```

```python
import functools
import math

import jax
import jax.numpy as jnp
import numpy as np
from jax import lax
from jax.experimental import pallas as pl
from jax.experimental.pallas import tpu as pltpu

F32 = jnp.float32
BF16 = jnp.bfloat16
I32 = jnp.int32
HIGHEST = lax.Precision.HIGHEST

GRID_W = 64
EPS = 1e-6
ROPE_THETA = 10000.0
NEG_INF = -1e30
MLA_HEADS, MLA_Q_RANK, MLA_KV_RANK, MLA_NOPE, MLA_ROPE, MLA_V = 8, 512, 256, 128, 64, 128
MLA_VT_ROWS = MLA_V + 16
MLA_UNROLL = 8
FNET_GROUPS = 4
NA_HEADS, NA_HD, NA_KR, NA_KC = 16, 64, 8, 16
SW_HEADS, SW_KV_HEADS, SW_HD, SW_WINDOW = 16, 4, 64, 128
MOE_GROUPS, MOE_PER_GROUP, MOE_FF = 4, 8, 512
MOE_EXPERTS = MOE_GROUPS * MOE_PER_GROUP

LANES = 128
SUBLANES = 8
TM = 512
MOE_BM = 512
VMEM_LIMIT = 56 * 1024 * 1024
ROUTE_LANE_EID, ROUTE_LANE_RANK, ROUTE_LANE_GATE = 0, 2, 4


def _cp(sem, vmem=VMEM_LIMIT):
    return pltpu.CompilerParams(dimension_semantics=sem, vmem_limit_bytes=vmem)


def _lane_iota(shape):
    return lax.broadcasted_iota(I32, shape, len(shape) - 1)


def _normmod(x, g, sc, sh):
    ms = jnp.mean(x * x, axis=-1, keepdims=True)
    return (x * lax.rsqrt(ms + EPS) * g) * (1.0 + sc) + sh


def _rms(x, g):
    ms = jnp.mean(x * x, axis=-1, keepdims=True)
    return x * lax.rsqrt(ms + EPS) * g


def _swap_halves(t, period):
    n = t.shape[-1]
    half = period // 2
    lane = _lane_iota(t.shape)
    return jnp.where((lane % period) < half, pltpu.roll(t, n - half, 1), pltpu.roll(t, half, 1))


def _dot(a, b):
    return jnp.dot(a, b, preferred_element_type=F32)


def _mod_kernel(a_ref, w_ref, b_ref, o_ref):
    a = a_ref[...]
    a = a * jax.nn.sigmoid(a)
    o_ref[0] = jnp.dot(a, w_ref[0], precision=HIGHEST, preferred_element_type=F32) + b_ref[0]


def _modulation(cond, mod_w, mod_b):
    depth, d, n = mod_w.shape
    tn = n // 4
    return pl.pallas_call(
        _mod_kernel,
        out_shape=jax.ShapeDtypeStruct((depth, SUBLANES, n), F32),
        grid=(depth, n // tn),
        in_specs=[pl.BlockSpec((SUBLANES, d), lambda l, j: (0, 0)),
                  pl.BlockSpec((1, d, tn), lambda l, j: (l, 0, j)),
                  pl.BlockSpec((1, 1, tn), lambda l, j: (l, 0, j))],
        out_specs=pl.BlockSpec((1, SUBLANES, tn), lambda l, j: (l, 0, j)),
        compiler_params=_cp(("arbitrary", "arbitrary")),
        name="modulation",
    )(cond, mod_w, mod_b.reshape(depth, 1, n))


class _Layout:
    def __init__(self, B, S, C, D):
        self.B, self.S, self.C, self.D = B, S, C, D
        self.NL, self.NC = B * S, B * C
        self.NT = self.NL + self.NC
        assert S % TM == 0 and self.NC % TM == 0 and TM % C == 0
        self.nl_tiles = self.NL // TM
        self.n_tiles = self.NT // TM
        self.tiles_per_batch = S // TM

    def mod_row(self, i):
        return jnp.where(i < self.nl_tiles, i // self.tiles_per_batch, self.B)

    def mod_spec(self, chunk):
        return pl.BlockSpec((1, 1, self.D), lambda i: (chunk * SUBLANES + self.mod_row(i), 0, 0))


def _mla_proj_kernel(x_ref, g_ref, sc_ref, sh_ref, w1_ref, gq_ref, gkv_ref, wq_ref, wkv_ref, cos_ref, sin_ref,
                     qt_ref, k_ref, vt_ref):
    h = _normmod(x_ref[...], g_ref[...], sc_ref[0], sh_ref[0]).astype(BF16)
    a = _dot(h, w1_ref[...])
    qa = _rms(a[:, :MLA_Q_RANK], gq_ref[...]).astype(BF16)
    ckv = _rms(a[:, MLA_Q_RANK:MLA_Q_RANK + MLA_KV_RANK], gkv_ref[...]).astype(BF16)
    cos, sin = cos_ref[...], sin_ref[...]

    def rope(t):
        return t * cos + _swap_halves(t, MLA_ROPE) * sin

    kr = rope(a[:, MLA_Q_RANK + MLA_KV_RANK:]).astype(BF16)
    scale = (MLA_NOPE + MLA_ROPE) ** -0.5 * math.log2(math.e)
    q = _dot(qa, wq_ref[...])
    kv = _dot(ckv, wkv_ref[...])
    ones = jnp.ones((MLA_VT_ROWS - MLA_V, x_ref.shape[0]), BF16)
    for hd in range(MLA_HEADS):
        c = hd * 2 * LANES
        qh = jnp.concatenate([q[:, c:c + LANES], rope(q[:, c + LANES:c + 2 * LANES])], axis=1) * scale
        qt_ref[hd] = qh.T.astype(BF16)
        k_ref[:, c:c + LANES] = kv[:, c:c + LANES].astype(BF16)
        k_ref[:, c + LANES:c + 2 * LANES] = kr
        vt_ref[hd, :MLA_V, :] = kv[:, c + LANES:c + 2 * LANES].T.astype(BF16)
        vt_ref[hd, MLA_V:, :] = ones


def _mla_proj(lay, x, g, modv, w1, gq, gkv, wq, wkv, cos, sin):
    D = lay.D
    full = lambda a: pl.BlockSpec(a.shape, lambda i: (0,) * a.ndim)
    row = lambda n: pl.BlockSpec((TM, n), lambda i: (i, 0))
    col = lambda r: pl.BlockSpec((MLA_HEADS, r, TM), lambda i: (0, 0, i))
    hq = MLA_HEADS * 2 * LANES
    return pl.pallas_call(
        _mla_proj_kernel,
        out_shape=(jax.ShapeDtypeStruct((MLA_HEADS, 2 * LANES, lay.NT), BF16),
                   jax.ShapeDtypeStruct((lay.NT, hq), BF16),
                   jax.ShapeDtypeStruct((MLA_HEADS, MLA_VT_ROWS, lay.NT), BF16)),
        grid=(lay.n_tiles,),
        in_specs=[row(D), full(g), lay.mod_spec(1), lay.mod_spec(0), full(w1), full(gq), full(gkv), full(wq),
                  full(wkv), row(LANES), row(LANES)],
        out_specs=(col(2 * LANES), row(hq), col(MLA_VT_ROWS)),
        compiler_params=_cp(("arbitrary",)),
        name="mla_proj",
    )(x, g, modv, modv, w1, gq, gkv, wq, wkv, cos, sin)


def _qkv_proj_kernel(x_ref, g_ref, sc_ref, sh_ref, w_ref, cos_ref, sin_ref, qt_ref, k_ref, vt_ref, *,
                     n_q, n_k, rope, q_scale, chunk):
    h = _normmod(x_ref[...], g_ref[...], sc_ref[0], sh_ref[0]).astype(BF16)
    n = w_ref.shape[1]
    for c0 in range(0, n, chunk):
        a = _dot(h, w_ref[:, c0:c0 + chunk])
        if rope and c0 < n_q + n_k:
            reps = chunk // LANES
            cos = jnp.concatenate([cos_ref[...]] * reps, axis=1)
            sin = jnp.concatenate([sin_ref[...]] * reps, axis=1)
            a = a * cos + _swap_halves(a, SW_HD) * sin
        if c0 < n_q:
            qt_ref[c0:c0 + chunk, :] = (a * q_scale).T.astype(BF16)
        elif c0 < n_q + n_k:
            k_ref[:, c0 - n_q:c0 - n_q + chunk] = a.astype(BF16)
        else:
            c = c0 - n_q - n_k
            vt_ref[c:c + chunk, :] = a.T.astype(BF16)


def _qkv_proj(lay, x, g, modv, w, cos, sin, *, n_q, n_k, rope, q_scale, chunk=256):
    D, n = lay.D, w.shape[1]
    n_v = n - n_q - n_k
    assert n_q % chunk == 0 and n_k % chunk == 0 and n_v % chunk == 0
    full = lambda a: pl.BlockSpec(a.shape, lambda i: (0,) * a.ndim)
    row = lambda m: pl.BlockSpec((TM, m), lambda i: (i, 0))
    col = lambda m: pl.BlockSpec((m, TM), lambda i: (0, i))
    return pl.pallas_call(
        functools.partial(_qkv_proj_kernel, n_q=n_q, n_k=n_k, rope=rope, q_scale=q_scale, chunk=chunk),
        out_shape=(jax.ShapeDtypeStruct((n_q, lay.NT), BF16), jax.ShapeDtypeStruct((lay.NT, n_k), BF16),
                   jax.ShapeDtypeStruct((n_v, lay.NT), BF16)),
        grid=(lay.n_tiles,),
        in_specs=[row(D), full(g), lay.mod_spec(1), lay.mod_spec(0), full(w), row(LANES), row(LANES)],
        out_specs=(col(n_q), row(n_k), col(n_v)),
        compiler_params=_cp(("arbitrary",)),
        name="qkv_proj",
    )(x, g, modv, modv, w, cos, sin)


def _mla_attn_kernel(*refs, tk, n_lat):
    if n_lat:
        qt_ref, kc_ref, vtc_ref, kl_ref, vtl_ref, o_ref, acc_ref, sa_ref, sb_ref = refs
    else:
        qt_ref, kc_ref, vtc_ref, o_ref, acc_ref = refs
    qt = qt_ref[0]

    st = _dot(kc_ref[...], qt)
    if n_lat:
        sa_ref[...] = _dot(kl_ref[pl.ds(0, tk), :], qt)
    m = jnp.max(st, axis=0, keepdims=True)
    acc_ref[...] = _dot(vtc_ref[0], jnp.exp2(st - m).astype(BF16))

    if n_lat:
        nch = n_lat // tk

        def softmax_pv(st, vt, m):
            m_new = jnp.maximum(m, jnp.max(st, axis=0, keepdims=True))
            acc_ref[...] = jnp.exp2(m - m_new) * acc_ref[...] + _dot(vt, jnp.exp2(st - m_new).astype(BF16))
            return m_new

        def body(jj, m):
            r0 = pl.multiple_of(2 * jj * tk, tk)
            r1 = pl.multiple_of((2 * jj + 1) * tk, tk)
            r2 = pl.multiple_of(jnp.minimum(2 * jj + 2, nch - 1) * tk, tk)
            sb_ref[...] = _dot(kl_ref[pl.ds(r1, tk), :], qt)
            m = softmax_pv(sa_ref[...], vtl_ref[0, :, pl.ds(r0, tk)], m)
            sa_ref[...] = _dot(kl_ref[pl.ds(r2, tk), :], qt)
            return softmax_pv(sb_ref[...], vtl_ref[0, :, pl.ds(r1, tk)], m)

        lax.fori_loop(0, nch // 2, body, m, unroll=math.gcd(nch // 2, MLA_UNROLL))
    o_ref[...] = (acc_ref[:MLA_V, :] / acc_ref[MLA_V:MLA_V + 1, :]).T.astype(o_ref.dtype)


def _mla_attention(lay, qt, k, vt, *, tq=2048, tk=512):
    B, S, C = lay.B, lay.S, lay.C
    tq = min(tq, S)
    H = MLA_HEADS
    nq = S // tq
    cblk0 = lay.NL // C
    assert S % (2 * tk) == 0
    o_lat = pl.pallas_call(
        functools.partial(_mla_attn_kernel, tk=tk, n_lat=S),
        out_shape=jax.ShapeDtypeStruct((lay.NL, H * LANES), BF16),
        grid=(B, H, nq),
        in_specs=[pl.BlockSpec((1, 2 * LANES, tq), lambda b, h, i: (h, 0, b * nq + i)),
                  pl.BlockSpec((C, 2 * LANES), lambda b, h, i: (cblk0 + b, h)),
                  pl.BlockSpec((1, MLA_VT_ROWS, C), lambda b, h, i: (h, 0, cblk0 + b)),
                  pl.BlockSpec((S, 2 * LANES), lambda b, h, i: (b, h)),
                  pl.BlockSpec((1, MLA_VT_ROWS, S), lambda b, h, i: (h, 0, b))],
        out_specs=pl.BlockSpec((tq, LANES), lambda b, h, i: (b * nq + i, h)),
        scratch_shapes=[pltpu.VMEM((MLA_VT_ROWS, tq), F32), pltpu.VMEM((tk, tq), F32), pltpu.VMEM((tk, tq), F32)],
        compiler_params=_cp(("arbitrary", "arbitrary", "arbitrary")),
        name="mla_attn_latent",
    )(qt, k, vt, k, vt)
    o_ctx = pl.pallas_call(
        functools.partial(_mla_attn_kernel, tk=tk, n_lat=0),
        out_shape=jax.ShapeDtypeStruct((lay.NC, H * LANES), BF16),
        grid=(B, H),
        in_specs=[pl.BlockSpec((1, 2 * LANES, C), lambda b, h: (h, 0, cblk0 + b)),
                  pl.BlockSpec((C, 2 * LANES), lambda b, h: (cblk0 + b, h)),
                  pl.BlockSpec((1, MLA_VT_ROWS, C), lambda b, h: (h, 0, cblk0 + b))],
        out_specs=pl.BlockSpec((C, LANES), lambda b, h: (b, h)),
        scratch_shapes=[pltpu.VMEM((MLA_VT_ROWS, C), F32)],
        compiler_params=_cp(("arbitrary", "arbitrary")),
        name="mla_attn_ctx",
    )(qt, k, vt)
    return o_lat, o_ctx


def _dft_tables(S, C, gc):
    P = math.isqrt(S)
    assert P * P == S and (P & (P - 1)) == 0 and (gc & (gc - 1)) == 0 and (C & (C - 1)) == 0

    def cs(idx, n):
        ang = (idx % n).astype(F32) * (2.0 * math.pi / n)
        return jnp.cos(ang), jnp.sin(ang)

    k1 = jnp.arange(P, dtype=I32)
    idx = k1[None, :, None] * (P * k1[None, None, :] + k1[:, None, None])
    c, s = cs(idx, S)
    m1 = jnp.concatenate([c, -s], axis=1) * (1.0 / P)
    c, s = cs(k1[:, None] * k1[None, :], P)
    m2 = jnp.concatenate([jnp.concatenate([c, s], axis=1), jnp.concatenate([-s, c], axis=1)], axis=0)
    kc = jnp.arange(gc, dtype=I32)
    c, s = cs(kc[:, None] * kc[None, :], gc)
    mc = jnp.concatenate([c, s], axis=0) * (gc ** -0.5)
    kq = jnp.arange(C, dtype=I32)
    c, s = cs(kq[:, None] * kq[None, :], C)
    mctx = jnp.concatenate([c, s], axis=0) * (C ** -0.5)
    return m1.astype(BF16), m2.astype(BF16), mc.astype(BF16), mctx.astype(BF16)


def _fnet_stage1_kernel(x_ref, g_ref, sc_ref, sh_ref, m1_ref, z_ref, *, n2c, P):
    g, sc, sh = g_ref[...], sc_ref[0], sh_ref[0]
    for j in range(n2c):
        h = _normmod(x_ref[:, j, :], g, sc, sh).astype(BF16)
        z = _dot(m1_ref[j], h)
        z_ref[0, :, 0, j, :] = z[:P]
        z_ref[0, :, 1, j, :] = z[P:]


def _fnet_stage2_kernel(z_ref, m2_ref, mc_ref, f_ref, *, k1c, P, gc):
    D = f_ref.shape[-1]
    for j in range(k1c):
        z = z_ref[0, j].reshape(2 * P, D).astype(BF16)
        y = _dot(m2_ref[...], z)
        yr, yi = y[:P].astype(BF16), y[P:].astype(BF16)
        outs = []
        for gi in range(D // gc):
            sl = slice(gi * gc, (gi + 1) * gc)
            outs.append(_dot(yr[:, sl], mc_ref[:gc, :]) + _dot(yi[:, sl], mc_ref[gc:, :]))
        f_ref[:, j, :] = jnp.concatenate(outs, axis=1)


def _fnet_ctx_kernel(x_ref, g_ref, sc_ref, sh_ref, ml_ref, mc_ref, f_ref, *, C, gc):
    D = x_ref.shape[-1]
    h = _normmod(x_ref[...], g_ref[...], sc_ref[0], sh_ref[0]).astype(BF16)
    y = _dot(ml_ref[...], h)
    yc, ys = y[:C].astype(BF16), y[C:].astype(BF16)
    outs = []
    for gi in range(D // gc):
        sl = slice(gi * gc, (gi + 1) * gc)
        outs.append(_dot(yc[:, sl], mc_ref[:gc, :]) - _dot(ys[:, sl], mc_ref[gc:, :]))
    f_ref[...] = jnp.concatenate(outs, axis=1)


def _fnet_mix(lay, x, g, modv, tables):
    B, S, C, D = lay.B, lay.S, lay.C, lay.D
    m1, m2, mc, mctx = tables
    P = math.isqrt(S)
    gc = D // FNET_GROUPS
    n2c = SUBLANES
    k1c = SUBLANES
    full = lambda a: pl.BlockSpec(a.shape, lambda *i: (0,) * a.ndim)
    modspec = lambda chunk: pl.BlockSpec((1, 1, D), lambda b, j: (chunk * SUBLANES + b, 0, 0))
    assert C % P == 0 and P % n2c == 0 and P % k1c == 0
    x3 = x.reshape(lay.NT // P, P, D)
    z = pl.pallas_call(
        functools.partial(_fnet_stage1_kernel, n2c=n2c, P=P),
        out_shape=jax.ShapeDtypeStruct((B, P, 2, P, D), F32),
        grid=(B, P // n2c),
        in_specs=[pl.BlockSpec((P, n2c, D), lambda b, j: (b, j, 0)), full(g), modspec(1), modspec(0),
                  pl.BlockSpec((n2c, 2 * P, P), lambda b, j: (j, 0, 0))],
        out_specs=pl.BlockSpec((1, P, 2, n2c, D), lambda b, j: (b, 0, 0, j, 0)),
        compiler_params=_cp(("arbitrary", "arbitrary")),
        name="fnet_stage1",
    )(x3, g, modv, modv, m1)
    f_lat = pl.pallas_call(
        functools.partial(_fnet_stage2_kernel, k1c=k1c, P=P, gc=gc),
        out_shape=jax.ShapeDtypeStruct((lay.NL // P, P, D), F32),
        grid=(B, P // k1c),
        in_specs=[pl.BlockSpec((1, k1c, 2, P, D), lambda b, j: (b, j, 0, 0, 0)), full(m2), full(mc)],
        out_specs=pl.BlockSpec((P, k1c, D), lambda b, j: (b, j, 0)),
        compiler_params=_cp(("arbitrary", "arbitrary")),
        name="fnet_stage2",
    )(z, m2, mc)
    cblk0 = lay.NL // C
    ctx_mod = lambda chunk: pl.BlockSpec((1, 1, D), lambda b: (chunk * SUBLANES + B, 0, 0))
    f_ctx = pl.pallas_call(
        functools.partial(_fnet_ctx_kernel, C=C, gc=gc),
        out_shape=jax.ShapeDtypeStruct((lay.NC, D), F32),
        grid=(B,),
        in_specs=[pl.BlockSpec((C, D), lambda b: (cblk0 + b, 0)), full(g), ctx_mod(1), ctx_mod(0), full(mctx),
                  full(mc)],
        out_specs=pl.BlockSpec((C, D), lambda b: (b, 0)),
        compiler_params=_cp(("arbitrary",)),
        name="fnet_ctx",
    )(x, g, modv, modv, mctx, mc)
    return f_lat.reshape(lay.NL, D), f_ctx


NA_QROWS = 2
NA_KROWS = NA_KR + NA_QROWS
NA_VARIANTS = 5
NA_ONES = 16


def _na_bias_kernel(rpb_ref, sel_ref, toe_ref, o_ref):
    g = jnp.dot(rpb_ref[0], toe_ref[...], precision=HIGHEST, preferred_element_type=F32)
    for t in range(NA_VARIANTS):
        o_ref[t, 0] = jnp.dot(sel_ref[t], g, precision=HIGHEST, preferred_element_type=F32)


def _na_window_start(r, rows):
    return jnp.clip(r - NA_KR // 2, 0, rows - NA_KROWS)


def _na_bias_tables(rpb, rows):
    assert rows >= 16 and rows % NA_QROWS == 0
    W = GRID_W
    nh, nu, nv = rpb.shape
    up, vp, ajp = 2 * SUBLANES, LANES, 3 * SUBLANES
    assert nu <= up and nv <= vp and NA_QROWS * NA_KROWS <= ajp
    cq, ck = np.arange(W)[:, None], np.arange(W)[None, :]
    dc = np.clip(ck - cq + NA_KC - 1, 0, nv - 1).reshape(-1)
    toe = (np.arange(vp)[:, None] == dc[None, :]).astype(np.float32)
    cs = np.clip(cq - NA_KC // 2, 0, W - NA_KC)
    col_ok = (ck >= cs) & (ck < cs + NA_KC)
    sel = np.zeros((NA_VARIANTS, ajp, up), np.float32)
    valid = np.zeros((NA_VARIANTS, NA_QROWS, NA_KROWS, W, W), bool)
    for t, r in enumerate((0, 2, 6, rows - 4, rows - 2)):
        w0 = min(max(r - NA_KR // 2, 0), rows - NA_KROWS)
        for a in range(NA_QROWS):
            rs = min(max(r + a - NA_KR // 2, 0), rows - NA_KR)
            for j in range(NA_KROWS):
                rk = w0 + j
                sel[t, a * NA_KROWS + j, min(max(rk - (r + a) + NA_KR - 1, 0), nu - 1)] = 1.0
                if rs <= rk < rs + NA_KR:
                    valid[t, a, j] = col_ok
    rpb_p = jnp.pad(rpb, ((0, 0), (0, up - nu), (0, vp - nv)))
    tab = pl.pallas_call(
        _na_bias_kernel,
        out_shape=jax.ShapeDtypeStruct((NA_VARIANTS, nh, ajp, W * W), F32),
        grid=(nh,),
        in_specs=[pl.BlockSpec((1, up, vp), lambda h: (h, 0, 0)),
                  pl.BlockSpec(sel.shape, lambda h: (0, 0, 0)),
                  pl.BlockSpec(toe.shape, lambda h: (0, 0))],
        out_specs=pl.BlockSpec((NA_VARIANTS, 1, ajp, W * W), lambda h: (0, h, 0, 0)),
        compiler_params=_cp(("arbitrary",)),
        name="na_bias",
    )(rpb_p, jnp.asarray(sel), jnp.asarray(toe))
    tab = tab[:, :, :NA_QROWS * NA_KROWS].reshape(NA_VARIANTS, nh, NA_QROWS, NA_KROWS, W, W)
    tab = jnp.where(jnp.asarray(valid)[:, None], tab * math.log2(math.e), NEG_INF)
    tab = tab.reshape(NA_VARIANTS, nh // 2, 2, NA_QROWS, NA_KROWS, W, W)
    return tab.transpose(0, 1, 4, 6, 2, 3, 5).reshape(NA_VARIANTS, nh // 2, NA_KROWS * W, 2 * NA_QROWS * W)


def _na_kernel(*refs, rows, local):
    if local:
        qt_ref, kc_ref, vtc_ref, kl_ref, vtl_ref, bias_ref, o_ref, st_a, sc_a, st_b, sc_b = refs
    else:
        qt_ref, kc_ref, vtc_ref, o_ref, sc_a, sc_b = refs
        st_a = st_b = None
    nq = NA_QROWS * GRID_W
    nk = NA_KROWS * GRID_W
    kc, vtc = kc_ref[...], vtc_ref[...]
    row = lax.broadcasted_iota(I32, (LANES, nq), 0)
    pairs = qt_ref.shape[1] // nq
    qi = pl.program_id(2) if local else 0

    def ext(vt):
        return jnp.concatenate([vt, jnp.ones((NA_ONES, vt.shape[1]), BF16)], axis=0)

    def window(t):
        r = (qi * pairs + t) * NA_QROWS
        k0 = pl.multiple_of(_na_window_start(r, rows) * GRID_W, LANES)
        var = jnp.where(r == 0, 0, jnp.where(r == 2, 1, jnp.where(r == rows - 4, 3, jnp.where(r == rows - 2, 4, 2))))
        return k0, var

    def scores(t, st_ref, sc_ref):
        qt = qt_ref[:, t * nq:(t + 1) * nq]
        zero = jnp.zeros_like(qt)
        qbd = jnp.concatenate([jnp.where(row < NA_HD, qt, zero), jnp.where(row >= NA_HD, qt, zero)], axis=1)
        sc_ref[...] = _dot(kc, qbd)
        if local:
            k0, var = window(t)
            st_ref[...] = _dot(kl_ref[pl.ds(k0, nk), :], qbd) + bias_ref[var, 0]

    def finish(t, st_ref, sc_ref):
        sc = sc_ref[...]
        m = jnp.max(sc, axis=0, keepdims=True)
        if local:
            st = st_ref[...]
            m = jnp.maximum(m, jnp.max(st, axis=0, keepdims=True))
            pt = jnp.exp2(st - m).astype(BF16)
            vtw = vtl_ref[:, pl.ds(window(t)[0], nk)]
        pc = jnp.exp2(sc - m).astype(BF16)
        outs = []
        for hl in range(2):
            hs, qs = slice(hl * NA_HD, (hl + 1) * NA_HD), slice(hl * nq, (hl + 1) * nq)
            acc = _dot(ext(vtc[hs, :]), pc[:, qs])
            if local:
                acc = acc + _dot(ext(vtw[hs, :]), pt[:, qs])
            outs.append(acc[:NA_HD] / acc[NA_HD:NA_HD + 1])
        o_ref[t * nq:(t + 1) * nq, :] = jnp.concatenate(outs, axis=0).T.astype(o_ref.dtype)

    bufs = ((st_a, sc_a), (st_b, sc_b))
    scores(0, *bufs[0])
    for t in range(pairs):
        if t + 1 < pairs:
            scores(t + 1, *bufs[(t + 1) % 2])
        finish(t, *bufs[t % 2])


def _na_attention(lay, qt, k, vt, bias, *, tq=2048):
    B, S, C = lay.B, lay.S, lay.C
    tq = min(tq, S)
    rows = S // GRID_W
    HP = NA_HEADS // 2
    nq = S // tq
    cblk0 = lay.NL // C
    nk, nqp = NA_KROWS * GRID_W, NA_QROWS * GRID_W
    o_lat = pl.pallas_call(
        functools.partial(_na_kernel, rows=rows, local=True),
        out_shape=jax.ShapeDtypeStruct((lay.NL, NA_HEADS * NA_HD), BF16),
        grid=(B, HP, nq),
        in_specs=[pl.BlockSpec((LANES, tq), lambda b, h, i: (h, b * nq + i)),
                  pl.BlockSpec((C, LANES), lambda b, h, i: (cblk0 + b, h)),
                  pl.BlockSpec((LANES, C), lambda b, h, i: (h, cblk0 + b)),
                  pl.BlockSpec((S, LANES), lambda b, h, i: (b, h)),
                  pl.BlockSpec((LANES, S), lambda b, h, i: (h, b)),
                  pl.BlockSpec((NA_VARIANTS, 1) + bias.shape[2:], lambda b, h, i: (0, h, 0, 0))],
        out_specs=pl.BlockSpec((tq, LANES), lambda b, h, i: (b * nq + i, h)),
        scratch_shapes=[pltpu.VMEM((nk, 2 * nqp), F32), pltpu.VMEM((C, 2 * nqp), F32)] * 2,
        compiler_params=_cp(("arbitrary", "arbitrary", "arbitrary")),
        name="na_attn_latent",
    )(qt, k, vt, k, vt, bias)
    o_ctx = pl.pallas_call(
        functools.partial(_na_kernel, rows=rows, local=False),
        out_shape=jax.ShapeDtypeStruct((lay.NC, NA_HEADS * NA_HD), BF16),
        grid=(B, HP),
        in_specs=[pl.BlockSpec((LANES, C), lambda b, h: (h, cblk0 + b)),
                  pl.BlockSpec((C, LANES), lambda b, h: (cblk0 + b, h)),
                  pl.BlockSpec((LANES, C), lambda b, h: (h, cblk0 + b))],
        out_specs=pl.BlockSpec((C, LANES), lambda b, h: (b, h)),
        scratch_shapes=[pltpu.VMEM((C, 2 * nqp), F32)] * 2,
        compiler_params=_cp(("arbitrary", "arbitrary")),
        name="na_attn_ctx",
    )(qt, k, vt)
    return o_lat, o_ctx


SW_SUB = 128
SW_BAND = SW_SUB + 2 * SW_WINDOW
SW_ONES = 16


def _swa_kernel(*refs, S, local):
    if local:
        (sink_ref, qt_ref, kc_ref, vtc_ref, kp_ref, kcur_ref, kn_ref, vtp_ref, vtcur_ref, vtn_ref, o_ref,
         kbuf, vtbuf, st_a, sc_a, st_b, sc_b) = refs
        tq = qt_ref.shape[1]
        W = SW_WINDOW
        kbuf[0:W] = kp_ref[...]
        kbuf[W:W + tq] = kcur_ref[...]
        kbuf[W + tq:] = kn_ref[...]
        vtbuf[:, 0:W] = vtp_ref[...]
        vtbuf[:, W:W + tq] = vtcur_ref[...]
        vtbuf[:, W + tq:] = vtn_ref[...]
        i = pl.program_id(1)
    else:
        sink_ref, qt_ref, kc_ref, vtc_ref, o_ref, sc_a, sc_b = refs
        st_a = st_b = None
        tq = qt_ref.shape[1]
    G = SW_HEADS // SW_KV_HEADS
    kc, vtc = kc_ref[...], vtc_ref[...]
    zeros_q = jnp.zeros((SW_HD, G * SW_SUB), BF16)

    def ext(vt):
        return jnp.concatenate([vt, jnp.ones((SW_ONES, vt.shape[1]), BF16)], axis=0)

    def band_bias(sb):
        key = lax.broadcasted_iota(I32, (SW_BAND, SW_SUB), 0)
        qry = lax.broadcasted_iota(I32, (SW_BAND, SW_SUB), 1)
        rel = key - SW_WINDOW - qry
        kpos = i * tq + sb * SW_SUB - SW_WINDOW + key
        ok = (jnp.abs(rel) <= SW_WINDOW) & (kpos >= 0) & (kpos < S)
        bias = jnp.where(ok, 0.0, NEG_INF).astype(F32)
        return jnp.concatenate([bias] * G, axis=1)

    def scores(sb, g, bias, st_ref, sc_ref):
        r0 = sb * SW_SUB
        tile, half = g // 2, g % 2
        sl = slice(tile * LANES, (tile + 1) * LANES)
        qg = jnp.concatenate([qt_ref[(G * g + hl) * SW_HD:(G * g + hl + 1) * SW_HD, r0:r0 + SW_SUB]
                              for hl in range(G)], axis=1)
        qpad = jnp.concatenate([qg, zeros_q] if half == 0 else [zeros_q, qg], axis=0)
        sc_ref[...] = _dot(kc[:, sl], qpad)
        if local:
            st_ref[...] = _dot(kbuf[r0:r0 + SW_BAND, sl], qpad) + bias

    def finish(sb, g, st_ref, sc_ref):
        r0 = sb * SW_SUB
        sink = sink_ref[g]
        sc = sc_ref[...]
        m = jnp.maximum(jnp.max(sc, axis=0, keepdims=True), sink)
        if local:
            st = st_ref[...]
            m = jnp.maximum(m, jnp.max(st, axis=0, keepdims=True))
        acc = _dot(ext(vtc[g * SW_HD:(g + 1) * SW_HD, :]), jnp.exp2(sc - m).astype(BF16))
        if local:
            acc = acc + _dot(ext(vtbuf[g * SW_HD:(g + 1) * SW_HD, r0:r0 + SW_BAND]), jnp.exp2(st - m).astype(BF16))
        og = acc[:SW_HD] / (acc[SW_HD:SW_HD + 1] + jnp.exp2(sink - m))
        return [og[:, hl * SW_SUB:(hl + 1) * SW_SUB] for hl in range(G)]

    items = [(sb, g) for sb in range(tq // SW_SUB) for g in range(SW_KV_HEADS)]
    bufs = ((st_a, sc_a), (st_b, sc_b))
    bias = band_bias(0) if local else None
    scores(*items[0], bias, *bufs[0])
    outs = []
    for n, (sb, g) in enumerate(items):
        if n + 1 < len(items):
            nsb, ng = items[n + 1]
            if local and ng == 0:
                bias = band_bias(nsb)
            scores(nsb, ng, bias, *bufs[(n + 1) % 2])
        outs.extend(finish(sb, g, *bufs[n % 2]))
        if g == SW_KV_HEADS - 1:
            o_ref[sb * SW_SUB:(sb + 1) * SW_SUB, :] = jnp.concatenate(outs, axis=0).T.astype(o_ref.dtype)
            outs = []


def _swa_attention(lay, qt, k, vt, sinks, *, tq=1024):
    B, S, C = lay.B, lay.S, lay.C
    tq = min(tq, S)
    nq_rows = SW_HEADS * SW_HD
    nkv = SW_KV_HEADS * SW_HD
    G = SW_HEADS // SW_KV_HEADS
    nq = S // tq
    per = tq // SW_WINDOW
    last = lay.NT // SW_WINDOW - 1
    cblk0 = lay.NL // C
    prev = lambda b, i: jnp.maximum((b * nq + i) * per - 1, 0)
    nxt = lambda b, i: jnp.minimum((b * nq + i + 1) * per, last)
    sink_l = jnp.repeat(sinks.reshape(SW_KV_HEADS, 1, G) * math.log2(math.e), SW_SUB, axis=2)
    sink_spec = pl.BlockSpec(sink_l.shape, lambda *a: (0, 0, 0))
    o_lat = pl.pallas_call(
        functools.partial(_swa_kernel, S=S, local=True),
        out_shape=jax.ShapeDtypeStruct((lay.NL, nq_rows), BF16),
        grid=(B, nq),
        in_specs=[sink_spec,
                  pl.BlockSpec((nq_rows, tq), lambda b, i: (0, b * nq + i)),
                  pl.BlockSpec((C, nkv), lambda b, i: (cblk0 + b, 0)),
                  pl.BlockSpec((nkv, C), lambda b, i: (0, cblk0 + b)),
                  pl.BlockSpec((SW_WINDOW, nkv), lambda b, i: (prev(b, i), 0)),
                  pl.BlockSpec((tq, nkv), lambda b, i: (b * nq + i, 0)),
                  pl.BlockSpec((SW_WINDOW, nkv), lambda b, i: (nxt(b, i), 0)),
                  pl.BlockSpec((nkv, SW_WINDOW), lambda b, i: (0, prev(b, i))),
                  pl.BlockSpec((nkv, tq), lambda b, i: (0, b * nq + i)),
                  pl.BlockSpec((nkv, SW_WINDOW), lambda b, i: (0, nxt(b, i)))],
        out_specs=pl.BlockSpec((tq, nq_rows), lambda b, i: (b * nq + i, 0)),
        scratch_shapes=([pltpu.VMEM((tq + 2 * SW_WINDOW, nkv), BF16), pltpu.VMEM((nkv, tq + 2 * SW_WINDOW), BF16)]
                        + [pltpu.VMEM((SW_BAND, G * SW_SUB), F32), pltpu.VMEM((C, G * SW_SUB), F32)] * 2),
        compiler_params=_cp(("arbitrary", "arbitrary")),
        name="swa_attn_latent",
    )(sink_l, qt, k, vt, k, k, k, vt, vt, vt)
    o_ctx = pl.pallas_call(
        functools.partial(_swa_kernel, S=S, local=False),
        out_shape=jax.ShapeDtypeStruct((lay.NC, nq_rows), BF16),
        grid=(B,),
        in_specs=[sink_spec,
                  pl.BlockSpec((nq_rows, C), lambda b: (0, cblk0 + b)),
                  pl.BlockSpec((C, nkv), lambda b: (cblk0 + b, 0)),
                  pl.BlockSpec((nkv, C), lambda b: (0, cblk0 + b))],
        out_specs=pl.BlockSpec((C, nq_rows), lambda b: (b, 0)),
        scratch_shapes=[pltpu.VMEM((C, G * SW_SUB), F32)] * 2,
        compiler_params=_cp(("arbitrary",)),
        name="swa_attn_ctx",
    )(sink_l, qt, k, vt)
    return o_lat, o_ctx


def _route(logits, tri, carry):
    lane = _lane_iota(logits.shape)
    lanef = lane.astype(F32)
    big = float(LANES)
    rowmax = lambda t: jnp.max(t, axis=-1, keepdims=True)
    rowmin = lambda t: jnp.min(t, axis=-1, keepdims=True)
    rowsum = lambda t: jnp.sum(t, axis=-1, keepdims=True)
    is_g = lane < MOE_GROUPS
    mg = rowmax(jnp.where(is_g, logits, -jnp.inf))
    w_g = 1.0 / rowsum(jnp.where(is_g, jnp.exp(logits - mg), 0.0))
    gidx = rowmin(jnp.where(is_g & (logits == mg), lanef, big))
    g0 = MOE_GROUPS + MOE_PER_GROUP * gidx
    in_grp = (lanef >= g0) & (lanef < g0 + MOE_PER_GROUP)
    le = jnp.where(in_grp, logits, -jnp.inf)
    m1 = rowmax(le)
    i1 = rowmin(jnp.where(in_grp & (le == m1), lanef, big))
    le2 = jnp.where(lanef == i1, -jnp.inf, le)
    m2 = rowmax(le2)
    i2 = rowmin(jnp.where(in_grp & (lanef != i1) & (le2 == m2), lanef, big))
    r = jnp.exp(m2 - m1)
    gate1 = w_g / (1.0 + r)
    gate2 = w_g * r / (1.0 + r)
    sel1, sel2 = lanef == i1, lanef == i2
    member = (sel1 | sel2)
    cum = _dot(tri, member.astype(BF16)) + carry
    rank1 = rowsum(jnp.where(sel1, cum, 0.0))
    rank2 = rowsum(jnp.where(sel2, cum, 0.0))
    new_carry = carry + jnp.sum(member.astype(F32), axis=0, keepdims=True)
    rec = jnp.zeros_like(logits)
    for ln, val in ((0, i1 - MOE_GROUPS), (1, i2 - MOE_GROUPS), (2, rank1), (3, rank2), (4, gate1), (5, gate2)):
        rec = jnp.where(lane == ln, val, rec)
    return rec, new_carry


def _out_proj_kernel(*refs, has_bias, n_lat_tiles):
    if has_bias:
        (al_ref, ac_ref, x_ref, w_ref, b_ref, g1_ref, g2n_ref, sc_ref, sh_ref, wr_ref, br_ref, tri_ref,
         xo_ref, h2_ref, rec_ref, rect_ref, cnt_ref, carry_ref) = refs
    else:
        (al_ref, ac_ref, x_ref, w_ref, g1_ref, g2n_ref, sc_ref, sh_ref, wr_ref, br_ref, tri_ref,
         xo_ref, h2_ref, rec_ref, rect_ref, cnt_ref, carry_ref) = refs

    @pl.when(pl.program_id(0) == 0)
    def _():
        carry_ref[...] = jnp.zeros_like(carry_ref)

    a = jnp.where(pl.program_id(0) < n_lat_tiles, al_ref[...], ac_ref[...])
    y = _dot(a.astype(BF16), w_ref[...])
    if has_bias:
        y = y + b_ref[...]
    xn = x_ref[...] + g1_ref[0] * y
    xo_ref[...] = xn
    h2 = _normmod(xn, g2n_ref[...], sc_ref[0], sh_ref[0])
    h2_ref[...] = h2
    h_hi = h2.astype(BF16)
    h_lo = (h2 - h_hi.astype(F32)).astype(BF16)
    hw = _dot(h_hi, wr_ref[...])
    logits = hw[:, :LANES] + hw[:, LANES:] + _dot(h_lo, wr_ref[:, :LANES]) + br_ref[...]
    rec, carry = _route(logits, tri_ref[...], carry_ref[...])
    rec_ref[...] = rec
    rect_ref[...] = rec.T[:SUBLANES, :]
    carry_ref[...] = carry
    cnt_ref[...] = jnp.broadcast_to(carry, cnt_ref.shape)


def _out_proj(lay, a, x, w, b, g2n, modv, wr, br, tri):
    D = lay.D
    a_lat, a_ctx = a
    nl = lay.nl_tiles
    full = lambda t: pl.BlockSpec(t.shape, lambda i: (0,) * t.ndim)
    row = lambda n: pl.BlockSpec((TM, n), lambda i: (i, 0))
    ins = [a_lat, a_ctx, x, w] + ([b] if b is not None else []) + [modv, g2n, modv, modv, wr, br, tri]
    specs = ([pl.BlockSpec((TM, a_lat.shape[1]), lambda i: (jnp.minimum(i, nl - 1), 0)),
              pl.BlockSpec((TM, a_ctx.shape[1]), lambda i: (jnp.maximum(i - nl, 0), 0)), row(D), full(w)]
             + ([full(b)] if b is not None else [])
             + [lay.mod_spec(2), full(g2n), lay.mod_spec(4), lay.mod_spec(3), full(wr), full(br), full(tri)])
    return pl.pallas_call(
        functools.partial(_out_proj_kernel, has_bias=b is not None, n_lat_tiles=nl),
        out_shape=(jax.ShapeDtypeStruct((lay.NT, D), F32), jax.ShapeDtypeStruct((lay.NT, D), F32),
                   jax.ShapeDtypeStruct((lay.NT, LANES), F32), jax.ShapeDtypeStruct((SUBLANES, lay.NT), F32),
                   jax.ShapeDtypeStruct((SUBLANES, LANES), F32)),
        grid=(lay.n_tiles,),
        in_specs=specs,
        out_specs=(row(D), row(D), row(LANES), pl.BlockSpec((SUBLANES, TM), lambda i: (0, i)),
                   pl.BlockSpec((SUBLANES, LANES), lambda i: (0, 0))),
        scratch_shapes=[pltpu.VMEM((1, LANES), F32)],
        input_output_aliases={2: 0},
        compiler_params=_cp(("arbitrary",)),
        name="out_proj_router",
    )(*ins)


def _row_copy(src, s, dst, d, sem):
    return pltpu.make_async_copy(src.at[pl.ds(s, 1), :], dst.at[pl.ds(d, 1), :], sem)


def _dispatch_kernel(pos_ref, pend_ref, nu_ref, h_ref, xb_ref, zbuf, sem, zsem):
    base = pl.program_id(0) * TM
    n_tok = pos_ref.shape[0] // 2
    nblk = xb_ref.shape[0] // MOE_BM

    @pl.when(pl.program_id(0) == 0)
    def _():
        zbuf[...] = jnp.zeros_like(zbuf)

        def zero_block(row0):
            return pltpu.make_async_copy(zbuf, xb_ref.at[pl.ds(pl.multiple_of(row0, MOE_BM), MOE_BM), :], zsem)

        def each(fn):
            def expert(e, c):
                end = pend_ref[e]

                @pl.when(end > jnp.where(e > 0, pend_ref[jnp.maximum(e - 1, 0)], 0))
                def _():
                    fn(zero_block(end - MOE_BM))
                return c

            def tail(j, c):
                @pl.when(j >= nu_ref[0])
                def _():
                    fn(zero_block(j * MOE_BM))
                return c

            lax.fori_loop(0, MOE_EXPERTS, expert, 0)
            lax.fori_loop(0, nblk, tail, 0)

        each(lambda cp: cp.start())
        each(lambda cp: cp.wait())

    def issue(r, c):
        for k in range(2):
            _row_copy(h_ref, r, xb_ref, pos_ref[k * n_tok + base + r], sem).start(priority=k)
        return c

    lax.fori_loop(0, TM, issue, 0, unroll=8)
    for k in range(2):
        pltpu.make_async_copy(h_ref, xb_ref.at[pl.ds(0, TM), :], sem).wait()


def _dispatch(lay, pos, pend, n_used, h2, cap):
    D = lay.D
    return pl.pallas_call(
        _dispatch_kernel,
        out_shape=jax.ShapeDtypeStruct((cap, D), F32),
        grid_spec=pltpu.PrefetchScalarGridSpec(
            num_scalar_prefetch=3, grid=(lay.n_tiles,),
            in_specs=[pl.BlockSpec((TM, D), lambda i, p, e, n: (i, 0))],
            out_specs=pl.BlockSpec(memory_space=pl.ANY),
            scratch_shapes=[pltpu.VMEM((MOE_BM, D), F32), pltpu.SemaphoreType.DMA, pltpu.SemaphoreType.DMA]),
        compiler_params=_cp(("arbitrary",)),
        name="moe_dispatch",
    )(pos, pend, n_used, h2)


def _expert_kernel(be_ref, nu_ref, xb_ref, wg_ref, wu_ref, wd_ref, yb_ref, wgb, wub, wdb):
    j = pl.program_id(0)
    prev = be_ref[jnp.maximum(j - 1, 0)]

    @pl.when((j == 0) | (be_ref[j] != prev))
    def _():
        wgb[...] = wg_ref[0, 0].astype(BF16)
        wub[...] = wu_ref[0, 0].astype(BF16)
        wdb[...] = wd_ref[0, 0].astype(BF16)

    @pl.when(j < nu_ref[0])
    def _():
        xe = xb_ref[...].astype(BF16)
        g = _dot(xe, wgb[...])
        u = _dot(xe, wub[...])
        act = (g * jax.nn.sigmoid(g) * u).astype(BF16)
        yb_ref[...] = _dot(act, wdb[...])

    @pl.when(j >= nu_ref[0])
    def _():
        yb_ref[...] = jnp.zeros_like(yb_ref)


def _experts(xb, blk_e, n_used, w_gate, w_up, w_down, layer):
    cap, D = xb.shape
    FF = w_gate.shape[-1]
    nblk = cap // MOE_BM
    blk = lambda j, be, nu: (jnp.maximum(jnp.minimum(j, nu[0] - 1), 0), 0)
    wblk = lambda j, be, nu: (layer, be[j], 0, 0)
    return pl.pallas_call(
        _expert_kernel,
        out_shape=jax.ShapeDtypeStruct((cap, D), F32),
        grid_spec=pltpu.PrefetchScalarGridSpec(
            num_scalar_prefetch=2, grid=(nblk,),
            in_specs=[pl.BlockSpec((MOE_BM, D), blk),
                      pl.BlockSpec((1, 1, D, FF), wblk), pl.BlockSpec((1, 1, D, FF), wblk),
                      pl.BlockSpec((1, 1, FF, D), wblk)],
            out_specs=pl.BlockSpec((MOE_BM, D), lambda j, be, nu: (j, 0)),
            scratch_shapes=[pltpu.VMEM((D, FF), BF16), pltpu.VMEM((D, FF), BF16), pltpu.VMEM((FF, D), BF16)]),
        compiler_params=_cp(("arbitrary",)),
        name="moe_experts",
    )(blk_e, n_used, xb, w_gate, w_up, w_down)


def _combine_kernel(*refs, final):
    if final:
        pos_ref, x_ref, rec_ref, g2_ref, yb_ref, fg_ref, o_ref, buf, sem = refs
    else:
        pos_ref, x_ref, rec_ref, g2_ref, yb_ref, o_ref, buf, sem = refs
    i = pl.program_id(0)
    n_tok = pos_ref.shape[0] // 2
    slot = i % 2

    def gather(tile, s):
        def issue(r, c):
            for k in range(2):
                _row_copy(yb_ref, pos_ref[k * n_tok + tile * TM + r], buf.at[s, k], r, sem.at[s]).start(priority=k)
            return c
        lax.fori_loop(0, TM, issue, 0, unroll=8)

    @pl.when(i == 0)
    def _():
        gather(0, 0)

    for k in range(2):
        pltpu.make_async_copy(yb_ref.at[pl.ds(0, TM), :], buf.at[slot, k], sem.at[slot]).wait()

    @pl.when(i + 1 < pl.num_programs(0))
    def _():
        gather(i + 1, 1 - slot)

    rec = rec_ref[...]
    f = (rec[:, ROUTE_LANE_GATE:ROUTE_LANE_GATE + 1] * buf[slot, 0]
         + rec[:, ROUTE_LANE_GATE + 1:ROUTE_LANE_GATE + 2] * buf[slot, 1])
    xn = x_ref[...] + g2_ref[0] * f
    if final:
        xn = _rms(xn, fg_ref[...])
    o_ref[...] = xn


def _combine(lay, pos, x, rec, modv, yb, final_g):
    D = lay.D
    final = final_g is not None
    row = lambda n: pl.BlockSpec((TM, n), lambda i, p: (i, 0))
    specs = [row(D), row(LANES),
             pl.BlockSpec((1, 1, D), lambda i, p: (5 * SUBLANES + lay.mod_row(i), 0, 0)),
             pl.BlockSpec(memory_space=pl.ANY)]
    ins = [x, rec, modv, yb]
    if final:
        specs.append(pl.BlockSpec(final_g.shape, lambda i, p: (0, 0)))
        ins.append(final_g)
    n_rows, n_tiles = (lay.NL, lay.nl_tiles) if final else (lay.NT, lay.n_tiles)
    return pl.pallas_call(
        functools.partial(_combine_kernel, final=final),
        out_shape=jax.ShapeDtypeStruct((n_rows, D), F32),
        grid_spec=pltpu.PrefetchScalarGridSpec(
            num_scalar_prefetch=1, grid=(n_tiles,),
            in_specs=specs,
            out_specs=row(D),
            scratch_shapes=[pltpu.VMEM((2, 2, TM, D), F32), pltpu.SemaphoreType.DMA((2,))]),
        input_output_aliases={} if final else {1: 0},
        compiler_params=_cp(("arbitrary",)),
        name="moe_combine",
    )(pos, *ins)


def _moe(lay, x, h2, rec, rect, counts, modv, w_gate, w_up, w_down, layer, final_g):
    T = 2 * lay.NT
    cap = -(-T // MOE_BM) * MOE_BM + MOE_EXPERTS * MOE_BM
    cnt = counts[0, MOE_GROUPS:MOE_GROUPS + MOE_EXPERTS].astype(I32)
    pcnt = (cnt + MOE_BM - 1) // MOE_BM * MOE_BM
    pend = jnp.cumsum(pcnt)
    start = pend - pcnt
    eid = rect[ROUTE_LANE_EID:ROUTE_LANE_EID + 2].astype(I32)
    rank = rect[ROUTE_LANE_RANK:ROUTE_LANE_RANK + 2].astype(I32)
    eid, rank = eid.reshape(1, -1), rank.reshape(-1)
    hit = eid == jnp.arange(MOE_EXPERTS, dtype=I32)[:, None]
    pos = jnp.sum(jnp.where(hit, start[:, None], 0), axis=0) + rank
    blk_row = jnp.arange(cap // MOE_BM, dtype=I32) * MOE_BM
    blk_e = jnp.minimum(jnp.sum((pend[None, :] <= blk_row[:, None]).astype(I32), axis=1), MOE_EXPERTS - 1)
    n_used = (pend[-1:] // MOE_BM).astype(I32)
    xb = _dispatch(lay, pos, pend.astype(I32), n_used, h2, cap)
    yb = _experts(xb, blk_e, n_used, w_gate, w_up, w_down, layer)
    return _combine(lay, pos, x, rec, modv, yb, final_g)


def _rope_tables(lay):
    S = lay.S
    t = jnp.arange(S)
    n = MLA_ROPE // 4
    inv = ROPE_THETA ** (-jnp.arange(n, dtype=F32) / n)
    ang = jnp.concatenate([(t // GRID_W).astype(F32)[:, None] * inv, (t % GRID_W).astype(F32)[:, None] * inv], axis=-1)
    cos, sin = jnp.cos(ang), jnp.sin(ang)
    cos64 = jnp.concatenate([cos, cos], axis=-1)
    sin64 = jnp.concatenate([-sin, sin], axis=-1)
    rows = lambda lat, ctx_val: jnp.concatenate([jnp.tile(lat, (lay.B, 1)), jnp.full((lay.NC, 64), ctx_val, F32)], axis=0)
    cos64, sin64 = rows(cos64, 1.0), rows(sin64, 0.0)
    zero = jnp.zeros_like(cos64)
    return ((jnp.concatenate([cos64, zero], axis=1), jnp.concatenate([sin64, zero], axis=1)),
            (jnp.concatenate([cos64, cos64], axis=1), jnp.concatenate([sin64, sin64], axis=1)))


def kernel(x, c, ctx, c_ctx, mod_w, mod_b, norm1_g, norm2_g, mla_w_dq, mla_g_q, mla_w_uq, mla_w_dkv, mla_g_kv, mla_w_ukv, mla_w_o, fnet_w_o, fnet_b_o, na_w_qkv, na_rpb, na_w_o, swa_w_qkv, swa_sinks, swa_w_o, moe_w_grp, moe_b_grp, moe_w_rt, moe_b_rt, moe_w_gate, moe_w_up, moe_w_down, final_g):
    B, S, D = x.shape
    C = ctx.shape[1]
    depth = mod_w.shape[0]
    lay = _Layout(B, S, C, D)
    X = jnp.concatenate([x.reshape(B * S, D), ctx.reshape(B * C, D)], axis=0)
    cond = jnp.concatenate([c, c_ctx[None], jnp.zeros((SUBLANES - B - 1, D), F32)], axis=0)
    mod = _modulation(cond, mod_w, mod_b)
    (mla_cos, mla_sin), (swa_cos, swa_sin) = _rope_tables(lay)
    tri = (jnp.arange(TM)[:, None] > jnp.arange(TM)[None, :]).astype(BF16)
    n_mix = 4
    for i in range(depth):
        m, j = i % n_mix, i // n_mix
        modv = mod[i].reshape(SUBLANES, 6, D).transpose(1, 0, 2).reshape(6 * SUBLANES, 1, D)
        g1n, g2n = norm1_g[i][None], norm2_g[i][None]
        bias = None
        if m == 0:
            w1 = jnp.concatenate([mla_w_dq[j], mla_w_dkv[j], jnp.zeros((D, LANES - MLA_ROPE), F32)], axis=1).astype(BF16)
            wq = mla_w_uq[j].reshape(MLA_Q_RANK, MLA_HEADS, MLA_NOPE + MLA_ROPE)
            wq = jnp.concatenate([wq, jnp.zeros((MLA_Q_RANK, MLA_HEADS, LANES - MLA_ROPE), F32)], axis=-1)
            wq = wq.reshape(MLA_Q_RANK, MLA_HEADS * 2 * LANES).astype(BF16)
            q, k, v = _mla_proj(lay, X, g1n, modv, w1, mla_g_q[j][None], mla_g_kv[j][None], wq,
                                mla_w_ukv[j].astype(BF16), mla_cos, mla_sin)
            a = _mla_attention(lay, q, k, v)
            w_o = mla_w_o[j]
        elif m == 1:
            a = _fnet_mix(lay, X, g1n, modv, _dft_tables(S, C, D // FNET_GROUPS))
            w_o, bias = fnet_w_o[j], fnet_b_o[j][None]
        elif m == 2:
            qt, k, vt = _qkv_proj(lay, X, g1n, modv, na_w_qkv[j].astype(BF16), swa_cos, swa_sin,
                                  n_q=NA_HEADS * NA_HD, n_k=NA_HEADS * NA_HD, rope=False,
                                  q_scale=NA_HD ** -0.5 * math.log2(math.e))
            a = _na_attention(lay, qt, k, vt, _na_bias_tables(na_rpb[j], S // GRID_W))
            w_o = na_w_o[j]
        else:
            qt, k, vt = _qkv_proj(lay, X, g1n, modv, swa_w_qkv[j].astype(BF16), swa_cos, swa_sin,
                                  n_q=SW_HEADS * SW_HD, n_k=SW_KV_HEADS * SW_HD, rope=True,
                                  q_scale=SW_HD ** -0.5 * math.log2(math.e))
            a = _swa_attention(lay, qt, k, vt, swa_sinks[j])
            w_o = swa_w_o[j]
        wr = jnp.concatenate([moe_w_grp[i], moe_w_rt[i], jnp.zeros((D, LANES - MOE_GROUPS - MOE_EXPERTS), F32)], axis=1)
        br = jnp.concatenate([moe_b_grp[i], moe_b_rt[i], jnp.zeros((LANES - MOE_GROUPS - MOE_EXPERTS,), F32)])[None]
        wr_hi = wr.astype(BF16)
        wr2 = jnp.concatenate([wr_hi, (wr - wr_hi.astype(F32)).astype(BF16)], axis=1)
        X, h2, rec, rect, counts = _out_proj(lay, a, X, w_o.astype(BF16), bias, g2n, modv, wr2, br, tri)
        X = _moe(lay, X, h2, rec, rect, counts, modv, moe_w_gate, moe_w_up, moe_w_down, i,
                 final_g[None] if i == depth - 1 else None)
    return X.reshape(B, S, D)
```

```python
import functools
import math

import jax
import jax.numpy as jnp
import numpy as np
from jax import lax
from jax.experimental import pallas as pl
from jax.experimental.pallas import tpu as pltpu

F32 = jnp.float32
BF16 = jnp.bfloat16
I32 = jnp.int32
HIGHEST = lax.Precision.HIGHEST

GRID_W = 64
EPS = 1e-6
ROPE_THETA = 10000.0
NEG_INF = -1e30
MLA_HEADS, MLA_Q_RANK, MLA_KV_RANK, MLA_NOPE, MLA_ROPE, MLA_V = 8, 512, 256, 128, 64, 128
MLA_VT_ROWS = MLA_V + 16
MLA_UNROLL = 8
FNET_GROUPS = 4
NA_HEADS, NA_HD, NA_KR, NA_KC = 16, 64, 8, 16
SW_HEADS, SW_KV_HEADS, SW_HD, SW_WINDOW = 16, 4, 64, 128
MOE_GROUPS, MOE_PER_GROUP, MOE_FF = 4, 8, 512
MOE_EXPERTS = MOE_GROUPS * MOE_PER_GROUP

LANES = 128
SUBLANES = 8
TM = 512
MOE_BM = 512
VMEM_LIMIT = 56 * 1024 * 1024
ROUTE_LANE_EID, ROUTE_LANE_RANK, ROUTE_LANE_GATE = 0, 2, 4


def _cp(sem, vmem=VMEM_LIMIT):
    return pltpu.CompilerParams(dimension_semantics=sem, vmem_limit_bytes=vmem)


def _lane_iota(shape):
    return lax.broadcasted_iota(I32, shape, len(shape) - 1)


def _normmod(x, g, sc, sh):
    ms = jnp.mean(x * x, axis=-1, keepdims=True)
    return (x * lax.rsqrt(ms + EPS) * g) * (1.0 + sc) + sh


def _rms(x, g):
    ms = jnp.mean(x * x, axis=-1, keepdims=True)
    return x * lax.rsqrt(ms + EPS) * g


def _swap_halves(t, period):
    n = t.shape[-1]
    half = period // 2
    lane = _lane_iota(t.shape)
    return jnp.where((lane % period) < half, pltpu.roll(t, n - half, 1), pltpu.roll(t, half, 1))


def _dot(a, b):
    return jnp.dot(a, b, preferred_element_type=F32)


def _mod_kernel(a_ref, w_ref, b_ref, o_ref):
    a = a_ref[...]
    a = a * jax.nn.sigmoid(a)
    o_ref[0] = jnp.dot(a, w_ref[0], precision=HIGHEST, preferred_element_type=F32) + b_ref[0]


def _modulation(cond, mod_w, mod_b):
    depth, d, n = mod_w.shape
    tn = n // 4
    return pl.pallas_call(
        _mod_kernel,
        out_shape=jax.ShapeDtypeStruct((depth, SUBLANES, n), F32),
        grid=(depth, n // tn),
        in_specs=[pl.BlockSpec((SUBLANES, d), lambda l, j: (0, 0)),
                  pl.BlockSpec((1, d, tn), lambda l, j: (l, 0, j)),
                  pl.BlockSpec((1, 1, tn), lambda l, j: (l, 0, j))],
        out_specs=pl.BlockSpec((1, SUBLANES, tn), lambda l, j: (l, 0, j)),
        compiler_params=_cp(("arbitrary", "arbitrary")),
        name="modulation",
    )(cond, mod_w, mod_b.reshape(depth, 1, n))


class _Layout:
    def __init__(self, B, S, C, D):
        self.B, self.S, self.C, self.D = B, S, C, D
        self.NL, self.NC = B * S, B * C
        self.NT = self.NL + self.NC
        assert S % TM == 0 and self.NC % TM == 0 and TM % C == 0
        self.nl_tiles = self.NL // TM
        self.n_tiles = self.NT // TM
        self.tiles_per_batch = S // TM

    def mod_row(self, i):
        return jnp.where(i < self.nl_tiles, i // self.tiles_per_batch, self.B)

    def mod_spec(self, chunk):
        return pl.BlockSpec((1, 1, self.D), lambda i: (chunk * SUBLANES + self.mod_row(i), 0, 0))


def _mla_proj_kernel(x_ref, g_ref, sc_ref, sh_ref, w1_ref, gq_ref, gkv_ref, wq_ref, wkv_ref, cos_ref, sin_ref,
                     qt_ref, k_ref, vt_ref):
    h = _normmod(x_ref[...], g_ref[...], sc_ref[0], sh_ref[0]).astype(BF16)
    a = _dot(h, w1_ref[...])
    qa = _rms(a[:, :MLA_Q_RANK], gq_ref[...]).astype(BF16)
    ckv = _rms(a[:, MLA_Q_RANK:MLA_Q_RANK + MLA_KV_RANK], gkv_ref[...]).astype(BF16)
    cos, sin = cos_ref[...], sin_ref[...]

    def rope(t):
        return t * cos + _swap_halves(t, MLA_ROPE) * sin

    kr = rope(a[:, MLA_Q_RANK + MLA_KV_RANK:]).astype(BF16)
    scale = (MLA_NOPE + MLA_ROPE) ** -0.5 * math.log2(math.e)
    q = _dot(qa, wq_ref[...])
    kv = _dot(ckv, wkv_ref[...])
    ones = jnp.ones((MLA_VT_ROWS - MLA_V, x_ref.shape[0]), BF16)
    for hd in range(MLA_HEADS):
        c = hd * 2 * LANES
        qh = jnp.concatenate([q[:, c:c + LANES], rope(q[:, c + LANES:c + 2 * LANES])], axis=1) * scale
        qt_ref[hd] = qh.T.astype(BF16)
        k_ref[:, c:c + LANES] = kv[:, c:c + LANES].astype(BF16)
        k_ref[:, c + LANES:c + 2 * LANES] = kr
        vt_ref[hd, :MLA_V, :] = kv[:, c + LANES:c + 2 * LANES].T.astype(BF16)
        vt_ref[hd, MLA_V:, :] = ones


def _mla_proj(lay, x, g, modv, w1, gq, gkv, wq, wkv, cos, sin):
    D = lay.D
    full = lambda a: pl.BlockSpec(a.shape, lambda i: (0,) * a.ndim)
    row = lambda n: pl.BlockSpec((TM, n), lambda i: (i, 0))
    col = lambda r: pl.BlockSpec((MLA_HEADS, r, TM), lambda i: (0, 0, i))
    hq = MLA_HEADS * 2 * LANES
    return pl.pallas_call(
        _mla_proj_kernel,
        out_shape=(jax.ShapeDtypeStruct((MLA_HEADS, 2 * LANES, lay.NT), BF16),
                   jax.ShapeDtypeStruct((lay.NT, hq), BF16),
                   jax.ShapeDtypeStruct((MLA_HEADS, MLA_VT_ROWS, lay.NT), BF16)),
        grid=(lay.n_tiles,),
        in_specs=[row(D), full(g), lay.mod_spec(1), lay.mod_spec(0), full(w1), full(gq), full(gkv), full(wq),
                  full(wkv), row(LANES), row(LANES)],
        out_specs=(col(2 * LANES), row(hq), col(MLA_VT_ROWS)),
        compiler_params=_cp(("arbitrary",)),
        name="mla_proj",
    )(x, g, modv, modv, w1, gq, gkv, wq, wkv, cos, sin)


def _qkv_proj_kernel(x_ref, g_ref, sc_ref, sh_ref, w_ref, cos_ref, sin_ref, qt_ref, k_ref, vt_ref, *,
                     n_q, n_k, rope, q_scale, chunk):
    h = _normmod(x_ref[...], g_ref[...], sc_ref[0], sh_ref[0]).astype(BF16)
    n = w_ref.shape[1]
    for c0 in range(0, n, chunk):
        a = _dot(h, w_ref[:, c0:c0 + chunk])
        if rope and c0 < n_q + n_k:
            reps = chunk // LANES
            cos = jnp.concatenate([cos_ref[...]] * reps, axis=1)
            sin = jnp.concatenate([sin_ref[...]] * reps, axis=1)
            a = a * cos + _swap_halves(a, SW_HD) * sin
        if c0 < n_q:
            qt_ref[c0:c0 + chunk, :] = (a * q_scale).T.astype(BF16)
        elif c0 < n_q + n_k:
            k_ref[:, c0 - n_q:c0 - n_q + chunk] = a.astype(BF16)
        else:
            c = c0 - n_q - n_k
            vt_ref[c:c + chunk, :] = a.T.astype(BF16)


def _qkv_proj(lay, x, g, modv, w, cos, sin, *, n_q, n_k, rope, q_scale, chunk=256):
    D, n = lay.D, w.shape[1]
    n_v = n - n_q - n_k
    assert n_q % chunk == 0 and n_k % chunk == 0 and n_v % chunk == 0
    full = lambda a: pl.BlockSpec(a.shape, lambda i: (0,) * a.ndim)
    row = lambda m: pl.BlockSpec((TM, m), lambda i: (i, 0))
    col = lambda m: pl.BlockSpec((m, TM), lambda i: (0, i))
    return pl.pallas_call(
        functools.partial(_qkv_proj_kernel, n_q=n_q, n_k=n_k, rope=rope, q_scale=q_scale, chunk=chunk),
        out_shape=(jax.ShapeDtypeStruct((n_q, lay.NT), BF16), jax.ShapeDtypeStruct((lay.NT, n_k), BF16),
                   jax.ShapeDtypeStruct((n_v, lay.NT), BF16)),
        grid=(lay.n_tiles,),
        in_specs=[row(D), full(g), lay.mod_spec(1), lay.mod_spec(0), full(w), row(LANES), row(LANES)],
        out_specs=(col(n_q), row(n_k), col(n_v)),
        compiler_params=_cp(("arbitrary",)),
        name="qkv_proj",
    )(x, g, modv, modv, w, cos, sin)


def _mla_attn_kernel(*refs, tk, n_lat):
    if n_lat:
        qt_ref, kc_ref, vtc_ref, kl_ref, vtl_ref, o_ref, acc_ref, sa_ref, sb_ref = refs
    else:
        qt_ref, kc_ref, vtc_ref, o_ref, acc_ref = refs
    qt = qt_ref[0]

    st = _dot(kc_ref[...], qt)
    if n_lat:
        sa_ref[...] = _dot(kl_ref[pl.ds(0, tk), :], qt)
    m = jnp.max(st, axis=0, keepdims=True)
    acc_ref[...] = _dot(vtc_ref[0], jnp.exp2(st - m).astype(BF16))

    if n_lat:
        nch = n_lat // tk

        def softmax_pv(st, vt, m):
            m_new = jnp.maximum(m, jnp.max(st, axis=0, keepdims=True))
            acc_ref[...] = jnp.exp2(m - m_new) * acc_ref[...] + _dot(vt, jnp.exp2(st - m_new).astype(BF16))
            return m_new

        def body(jj, m):
            r0 = pl.multiple_of(2 * jj * tk, tk)
            r1 = pl.multiple_of((2 * jj + 1) * tk, tk)
            r2 = pl.multiple_of(jnp.minimum(2 * jj + 2, nch - 1) * tk, tk)
            sb_ref[...] = _dot(kl_ref[pl.ds(r1, tk), :], qt)
            m = softmax_pv(sa_ref[...], vtl_ref[0, :, pl.ds(r0, tk)], m)
            sa_ref[...] = _dot(kl_ref[pl.ds(r2, tk), :], qt)
            return softmax_pv(sb_ref[...], vtl_ref[0, :, pl.ds(r1, tk)], m)

        lax.fori_loop(0, nch // 2, body, m, unroll=math.gcd(nch // 2, MLA_UNROLL))
    o_ref[...] = (acc_ref[:MLA_V, :] / acc_ref[MLA_V:MLA_V + 1, :]).T.astype(o_ref.dtype)


def _mla_attention(lay, qt, k, vt, *, tq=1024, tk=512):
    B, S, C = lay.B, lay.S, lay.C
    H = MLA_HEADS
    nq = S // tq
    cblk0 = lay.NL // C
    assert S % (2 * tk) == 0
    o_lat = pl.pallas_call(
        functools.partial(_mla_attn_kernel, tk=tk, n_lat=S),
        out_shape=jax.ShapeDtypeStruct((lay.NL, H * LANES), BF16),
        grid=(B, H, nq),
        in_specs=[pl.BlockSpec((1, 2 * LANES, tq), lambda b, h, i: (h, 0, b * nq + i)),
                  pl.BlockSpec((C, 2 * LANES), lambda b, h, i: (cblk0 + b, h)),
                  pl.BlockSpec((1, MLA_VT_ROWS, C), lambda b, h, i: (h, 0, cblk0 + b)),
                  pl.BlockSpec((S, 2 * LANES), lambda b, h, i: (b, h)),
                  pl.BlockSpec((1, MLA_VT_ROWS, S), lambda b, h, i: (h, 0, b))],
        out_specs=pl.BlockSpec((tq, LANES), lambda b, h, i: (b * nq + i, h)),
        scratch_shapes=[pltpu.VMEM((MLA_VT_ROWS, tq), F32), pltpu.VMEM((tk, tq), F32), pltpu.VMEM((tk, tq), F32)],
        compiler_params=_cp(("arbitrary", "arbitrary", "arbitrary")),
        name="mla_attn_latent",
    )(qt, k, vt, k, vt)
    o_ctx = pl.pallas_call(
        functools.partial(_mla_attn_kernel, tk=tk, n_lat=0),
        out_shape=jax.ShapeDtypeStruct((lay.NC, H * LANES), BF16),
        grid=(B, H),
        in_specs=[pl.BlockSpec((1, 2 * LANES, C), lambda b, h: (h, 0, cblk0 + b)),
                  pl.BlockSpec((C, 2 * LANES), lambda b, h: (cblk0 + b, h)),
                  pl.BlockSpec((1, MLA_VT_ROWS, C), lambda b, h: (h, 0, cblk0 + b))],
        out_specs=pl.BlockSpec((C, LANES), lambda b, h: (b, h)),
        scratch_shapes=[pltpu.VMEM((MLA_VT_ROWS, C), F32)],
        compiler_params=_cp(("arbitrary", "arbitrary")),
        name="mla_attn_ctx",
    )(qt, k, vt)
    return o_lat, o_ctx


def _dft_tables(S, C, gc):
    P = math.isqrt(S)
    assert P * P == S and (P & (P - 1)) == 0 and (gc & (gc - 1)) == 0 and (C & (C - 1)) == 0

    def cs(idx, n):
        ang = (idx % n).astype(F32) * (2.0 * math.pi / n)
        return jnp.cos(ang), jnp.sin(ang)

    k1 = jnp.arange(P, dtype=I32)
    idx = k1[None, :, None] * (P * k1[None, None, :] + k1[:, None, None])
    c, s = cs(idx, S)
    m1 = jnp.concatenate([c, -s], axis=1) * (1.0 / P)
    c, s = cs(k1[:, None] * k1[None, :], P)
    m2 = jnp.concatenate([jnp.concatenate([c, s], axis=1), jnp.concatenate([-s, c], axis=1)], axis=0)
    kc = jnp.arange(gc, dtype=I32)
    c, s = cs(kc[:, None] * kc[None, :], gc)
    mc = jnp.concatenate([c, s], axis=0) * (gc ** -0.5)
    kq = jnp.arange(C, dtype=I32)
    c, s = cs(kq[:, None] * kq[None, :], C)
    mctx = jnp.concatenate([c, s], axis=0) * (C ** -0.5)
    return m1.astype(BF16), m2.astype(BF16), mc.astype(BF16), mctx.astype(BF16)


def _fnet_stage1_kernel(x_ref, g_ref, sc_ref, sh_ref, m1_ref, z_ref, *, n2c, P):
    g, sc, sh = g_ref[...], sc_ref[0], sh_ref[0]
    for j in range(n2c):
        h = _normmod(x_ref[:, j, :], g, sc, sh).astype(BF16)
        z = _dot(m1_ref[j], h)
        z_ref[0, :, 0, j, :] = z[:P]
        z_ref[0, :, 1, j, :] = z[P:]


def _fnet_stage2_kernel(z_ref, m2_ref, mc_ref, f_ref, *, k1c, P, gc):
    D = f_ref.shape[-1]
    for j in range(k1c):
        z = z_ref[0, j].reshape(2 * P, D).astype(BF16)
        y = _dot(m2_ref[...], z)
        yr, yi = y[:P].astype(BF16), y[P:].astype(BF16)
        outs = []
        for gi in range(D // gc):
            sl = slice(gi * gc, (gi + 1) * gc)
            outs.append(_dot(yr[:, sl], mc_ref[:gc, :]) + _dot(yi[:, sl], mc_ref[gc:, :]))
        f_ref[:, j, :] = jnp.concatenate(outs, axis=1)


def _fnet_ctx_kernel(x_ref, g_ref, sc_ref, sh_ref, ml_ref, mc_ref, f_ref, *, C, gc):
    D = x_ref.shape[-1]
    h = _normmod(x_ref[...], g_ref[...], sc_ref[0], sh_ref[0]).astype(BF16)
    y = _dot(ml_ref[...], h)
    yc, ys = y[:C].astype(BF16), y[C:].astype(BF16)
    outs = []
    for gi in range(D // gc):
        sl = slice(gi * gc, (gi + 1) * gc)
        outs.append(_dot(yc[:, sl], mc_ref[:gc, :]) - _dot(ys[:, sl], mc_ref[gc:, :]))
    f_ref[...] = jnp.concatenate(outs, axis=1)


def _fnet_mix(lay, x, g, modv, tables):
    B, S, C, D = lay.B, lay.S, lay.C, lay.D
    m1, m2, mc, mctx = tables
    P = math.isqrt(S)
    gc = D // FNET_GROUPS
    n2c = SUBLANES
    k1c = SUBLANES
    full = lambda a: pl.BlockSpec(a.shape, lambda *i: (0,) * a.ndim)
    modspec = lambda chunk: pl.BlockSpec((1, 1, D), lambda b, j: (chunk * SUBLANES + b, 0, 0))
    assert C % P == 0 and P % n2c == 0 and P % k1c == 0
    x3 = x.reshape(lay.NT // P, P, D)
    z = pl.pallas_call(
        functools.partial(_fnet_stage1_kernel, n2c=n2c, P=P),
        out_shape=jax.ShapeDtypeStruct((B, P, 2, P, D), F32),
        grid=(B, P // n2c),
        in_specs=[pl.BlockSpec((P, n2c, D), lambda b, j: (b, j, 0)), full(g), modspec(1), modspec(0),
                  pl.BlockSpec((n2c, 2 * P, P), lambda b, j: (j, 0, 0))],
        out_specs=pl.BlockSpec((1, P, 2, n2c, D), lambda b, j: (b, 0, 0, j, 0)),
        compiler_params=_cp(("arbitrary", "arbitrary")),
        name="fnet_stage1",
    )(x3, g, modv, modv, m1)
    f_lat = pl.pallas_call(
        functools.partial(_fnet_stage2_kernel, k1c=k1c, P=P, gc=gc),
        out_shape=jax.ShapeDtypeStruct((lay.NL // P, P, D), F32),
        grid=(B, P // k1c),
        in_specs=[pl.BlockSpec((1, k1c, 2, P, D), lambda b, j: (b, j, 0, 0, 0)), full(m2), full(mc)],
        out_specs=pl.BlockSpec((P, k1c, D), lambda b, j: (b, j, 0)),
        compiler_params=_cp(("arbitrary", "arbitrary")),
        name="fnet_stage2",
    )(z, m2, mc)
    cblk0 = lay.NL // C
    ctx_mod = lambda chunk: pl.BlockSpec((1, 1, D), lambda b: (chunk * SUBLANES + B, 0, 0))
    f_ctx = pl.pallas_call(
        functools.partial(_fnet_ctx_kernel, C=C, gc=gc),
        out_shape=jax.ShapeDtypeStruct((lay.NC, D), F32),
        grid=(B,),
        in_specs=[pl.BlockSpec((C, D), lambda b: (cblk0 + b, 0)), full(g), ctx_mod(1), ctx_mod(0), full(mctx),
                  full(mc)],
        out_specs=pl.BlockSpec((C, D), lambda b: (b, 0)),
        compiler_params=_cp(("arbitrary",)),
        name="fnet_ctx",
    )(x, g, modv, modv, mctx, mc)
    return f_lat.reshape(lay.NL, D), f_ctx


NA_QROWS = 2
NA_KROWS = NA_KR + NA_QROWS
NA_VARIANTS = 5
NA_ONES = 16


def _na_bias_kernel(rpb_ref, sel_ref, toe_ref, o_ref):
    g = jnp.dot(rpb_ref[0], toe_ref[...], precision=HIGHEST, preferred_element_type=F32)
    for t in range(NA_VARIANTS):
        o_ref[t, 0] = jnp.dot(sel_ref[t], g, precision=HIGHEST, preferred_element_type=F32)


def _na_window_start(r, rows):
    return jnp.clip(r - NA_KR // 2, 0, rows - NA_KROWS)


def _na_bias_tables(rpb, rows):
    assert rows >= 16 and rows % NA_QROWS == 0
    W = GRID_W
    nh, nu, nv = rpb.shape
    up, vp, ajp = 2 * SUBLANES, LANES, 3 * SUBLANES
    assert nu <= up and nv <= vp and NA_QROWS * NA_KROWS <= ajp
    cq, ck = np.arange(W)[:, None], np.arange(W)[None, :]
    dc = np.clip(ck - cq + NA_KC - 1, 0, nv - 1).reshape(-1)
    toe = (np.arange(vp)[:, None] == dc[None, :]).astype(np.float32)
    cs = np.clip(cq - NA_KC // 2, 0, W - NA_KC)
    col_ok = (ck >= cs) & (ck < cs + NA_KC)
    sel = np.zeros((NA_VARIANTS, ajp, up), np.float32)
    valid = np.zeros((NA_VARIANTS, NA_QROWS, NA_KROWS, W, W), bool)
    for t, r in enumerate((0, 2, 6, rows - 4, rows - 2)):
        w0 = min(max(r - NA_KR // 2, 0), rows - NA_KROWS)
        for a in range(NA_QROWS):
            rs = min(max(r + a - NA_KR // 2, 0), rows - NA_KR)
            for j in range(NA_KROWS):
                rk = w0 + j
                sel[t, a * NA_KROWS + j, min(max(rk - (r + a) + NA_KR - 1, 0), nu - 1)] = 1.0
                if rs <= rk < rs + NA_KR:
                    valid[t, a, j] = col_ok
    rpb_p = jnp.pad(rpb, ((0, 0), (0, up - nu), (0, vp - nv)))
    tab = pl.pallas_call(
        _na_bias_kernel,
        out_shape=jax.ShapeDtypeStruct((NA_VARIANTS, nh, ajp, W * W), F32),
        grid=(nh,),
        in_specs=[pl.BlockSpec((1, up, vp), lambda h: (h, 0, 0)),
                  pl.BlockSpec(sel.shape, lambda h: (0, 0, 0)),
                  pl.BlockSpec(toe.shape, lambda h: (0, 0))],
        out_specs=pl.BlockSpec((NA_VARIANTS, 1, ajp, W * W), lambda h: (0, h, 0, 0)),
        compiler_params=_cp(("arbitrary",)),
        name="na_bias",
    )(rpb_p, jnp.asarray(sel), jnp.asarray(toe))
    tab = tab[:, :, :NA_QROWS * NA_KROWS].reshape(NA_VARIANTS, nh, NA_QROWS, NA_KROWS, W, W)
    tab = jnp.where(jnp.asarray(valid)[:, None], tab * math.log2(math.e), NEG_INF)
    tab = tab.reshape(NA_VARIANTS, nh // 2, 2, NA_QROWS, NA_KROWS, W, W)
    return tab.transpose(0, 1, 4, 6, 2, 3, 5).reshape(NA_VARIANTS, nh // 2, NA_KROWS * W, 2 * NA_QROWS * W)


def _na_kernel(*refs, rows, local):
    if local:
        qt_ref, kc_ref, vtc_ref, kl_ref, vtl_ref, bias_ref, o_ref, st_a, sc_a, st_b, sc_b = refs
    else:
        qt_ref, kc_ref, vtc_ref, o_ref, sc_a, sc_b = refs
        st_a = st_b = None
    nq = NA_QROWS * GRID_W
    nk = NA_KROWS * GRID_W
    kc, vtc = kc_ref[...], vtc_ref[...]
    row = lax.broadcasted_iota(I32, (LANES, nq), 0)
    pairs = qt_ref.shape[1] // nq
    qi = pl.program_id(2) if local else 0

    def ext(vt):
        return jnp.concatenate([vt, jnp.ones((NA_ONES, vt.shape[1]), BF16)], axis=0)

    def window(t):
        r = (qi * pairs + t) * NA_QROWS
        k0 = pl.multiple_of(_na_window_start(r, rows) * GRID_W, LANES)
        var = jnp.where(r == 0, 0, jnp.where(r == 2, 1, jnp.where(r == rows - 4, 3, jnp.where(r == rows - 2, 4, 2))))
        return k0, var

    def scores(t, st_ref, sc_ref):
        qt = qt_ref[:, t * nq:(t + 1) * nq]
        zero = jnp.zeros_like(qt)
        qbd = jnp.concatenate([jnp.where(row < NA_HD, qt, zero), jnp.where(row >= NA_HD, qt, zero)], axis=1)
        sc_ref[...] = _dot(kc, qbd)
        if local:
            k0, var = window(t)
            st_ref[...] = _dot(kl_ref[pl.ds(k0, nk), :], qbd) + bias_ref[var, 0]

    def finish(t, st_ref, sc_ref):
        sc = sc_ref[...]
        m = jnp.max(sc, axis=0, keepdims=True)
        if local:
            st = st_ref[...]
            m = jnp.maximum(m, jnp.max(st, axis=0, keepdims=True))
            pt = jnp.exp2(st - m).astype(BF16)
            vtw = vtl_ref[:, pl.ds(window(t)[0], nk)]
        pc = jnp.exp2(sc - m).astype(BF16)
        outs = []
        for hl in range(2):
            hs, qs = slice(hl * NA_HD, (hl + 1) * NA_HD), slice(hl * nq, (hl + 1) * nq)
            acc = _dot(ext(vtc[hs, :]), pc[:, qs])
            if local:
                acc = acc + _dot(ext(vtw[hs, :]), pt[:, qs])
            outs.append(acc[:NA_HD] / acc[NA_HD:NA_HD + 1])
        o_ref[t * nq:(t + 1) * nq, :] = jnp.concatenate(outs, axis=0).T.astype(o_ref.dtype)

    bufs = ((st_a, sc_a), (st_b, sc_b))
    scores(0, *bufs[0])
    for t in range(pairs):
        if t + 1 < pairs:
            scores(t + 1, *bufs[(t + 1) % 2])
        finish(t, *bufs[t % 2])


def _na_attention(lay, qt, k, vt, bias, *, tq=2048):
    B, S, C = lay.B, lay.S, lay.C
    tq = min(tq, S)
    rows = S // GRID_W
    HP = NA_HEADS // 2
    nq = S // tq
    cblk0 = lay.NL // C
    nk, nqp = NA_KROWS * GRID_W, NA_QROWS * GRID_W
    o_lat = pl.pallas_call(
        functools.partial(_na_kernel, rows=rows, local=True),
        out_shape=jax.ShapeDtypeStruct((lay.NL, NA_HEADS * NA_HD), BF16),
        grid=(B, HP, nq),
        in_specs=[pl.BlockSpec((LANES, tq), lambda b, h, i: (h, b * nq + i)),
                  pl.BlockSpec((C, LANES), lambda b, h, i: (cblk0 + b, h)),
                  pl.BlockSpec((LANES, C), lambda b, h, i: (h, cblk0 + b)),
                  pl.BlockSpec((S, LANES), lambda b, h, i: (b, h)),
                  pl.BlockSpec((LANES, S), lambda b, h, i: (h, b)),
                  pl.BlockSpec((NA_VARIANTS, 1) + bias.shape[2:], lambda b, h, i: (0, h, 0, 0))],
        out_specs=pl.BlockSpec((tq, LANES), lambda b, h, i: (b * nq + i, h)),
        scratch_shapes=[pltpu.VMEM((nk, 2 * nqp), F32), pltpu.VMEM((C, 2 * nqp), F32)] * 2,
        compiler_params=_cp(("arbitrary", "arbitrary", "arbitrary")),
        name="na_attn_latent",
    )(qt, k, vt, k, vt, bias)
    o_ctx = pl.pallas_call(
        functools.partial(_na_kernel, rows=rows, local=False),
        out_shape=jax.ShapeDtypeStruct((lay.NC, NA_HEADS * NA_HD), BF16),
        grid=(B, HP),
        in_specs=[pl.BlockSpec((LANES, C), lambda b, h: (h, cblk0 + b)),
                  pl.BlockSpec((C, LANES), lambda b, h: (cblk0 + b, h)),
                  pl.BlockSpec((LANES, C), lambda b, h: (h, cblk0 + b))],
        out_specs=pl.BlockSpec((C, LANES), lambda b, h: (b, h)),
        scratch_shapes=[pltpu.VMEM((C, 2 * nqp), F32)] * 2,
        compiler_params=_cp(("arbitrary", "arbitrary")),
        name="na_attn_ctx",
    )(qt, k, vt)
    return o_lat, o_ctx


SW_SUB = 128
SW_BAND = SW_SUB + 2 * SW_WINDOW
SW_ONES = 16


def _swa_kernel(*refs, S, local):
    if local:
        (sink_ref, qt_ref, kc_ref, vtc_ref, kp_ref, kcur_ref, kn_ref, vtp_ref, vtcur_ref, vtn_ref, o_ref,
         kbuf, vtbuf, st_a, sc_a, st_b, sc_b) = refs
        tq = qt_ref.shape[1]
        W = SW_WINDOW
        kbuf[0:W] = kp_ref[...]
        kbuf[W:W + tq] = kcur_ref[...]
        kbuf[W + tq:] = kn_ref[...]
        vtbuf[:, 0:W] = vtp_ref[...]
        vtbuf[:, W:W + tq] = vtcur_ref[...]
        vtbuf[:, W + tq:] = vtn_ref[...]
        i = pl.program_id(1)
    else:
        sink_ref, qt_ref, kc_ref, vtc_ref, o_ref, sc_a, sc_b = refs
        st_a = st_b = None
        tq = qt_ref.shape[1]
    G = SW_HEADS // SW_KV_HEADS
    kc, vtc = kc_ref[...], vtc_ref[...]
    zeros_q = jnp.zeros((SW_HD, G * SW_SUB), BF16)

    def ext(vt):
        return jnp.concatenate([vt, jnp.ones((SW_ONES, vt.shape[1]), BF16)], axis=0)

    def band_bias(sb):
        key = lax.broadcasted_iota(I32, (SW_BAND, SW_SUB), 0)
        qry = lax.broadcasted_iota(I32, (SW_BAND, SW_SUB), 1)
        rel = key - SW_WINDOW - qry
        kpos = i * tq + sb * SW_SUB - SW_WINDOW + key
        ok = (jnp.abs(rel) <= SW_WINDOW) & (kpos >= 0) & (kpos < S)
        bias = jnp.where(ok, 0.0, NEG_INF).astype(F32)
        return jnp.concatenate([bias] * G, axis=1)

    def scores(sb, g, bias, st_ref, sc_ref):
        r0 = sb * SW_SUB
        tile, half = g // 2, g % 2
        sl = slice(tile * LANES, (tile + 1) * LANES)
        qg = jnp.concatenate([qt_ref[(G * g + hl) * SW_HD:(G * g + hl + 1) * SW_HD, r0:r0 + SW_SUB]
                              for hl in range(G)], axis=1)
        qpad = jnp.concatenate([qg, zeros_q] if half == 0 else [zeros_q, qg], axis=0)
        sc_ref[...] = _dot(kc[:, sl], qpad)
        if local:
            st_ref[...] = _dot(kbuf[r0:r0 + SW_BAND, sl], qpad) + bias

    def finish(sb, g, st_ref, sc_ref):
        r0 = sb * SW_SUB
        sink = sink_ref[g]
        sc = sc_ref[...]
        m = jnp.maximum(jnp.max(sc, axis=0, keepdims=True), sink)
        if local:
            st = st_ref[...]
            m = jnp.maximum(m, jnp.max(st, axis=0, keepdims=True))
        acc = _dot(ext(vtc[g * SW_HD:(g + 1) * SW_HD, :]), jnp.exp2(sc - m).astype(BF16))
        if local:
            acc = acc + _dot(ext(vtbuf[g * SW_HD:(g + 1) * SW_HD, r0:r0 + SW_BAND]), jnp.exp2(st - m).astype(BF16))
        og = acc[:SW_HD] / (acc[SW_HD:SW_HD + 1] + jnp.exp2(sink - m))
        return [og[:, hl * SW_SUB:(hl + 1) * SW_SUB] for hl in range(G)]

    items = [(sb, g) for sb in range(tq // SW_SUB) for g in range(SW_KV_HEADS)]
    bufs = ((st_a, sc_a), (st_b, sc_b))
    bias = band_bias(0) if local else None
    scores(*items[0], bias, *bufs[0])
    outs = []
    for n, (sb, g) in enumerate(items):
        if n + 1 < len(items):
            nsb, ng = items[n + 1]
            if local and ng == 0:
                bias = band_bias(nsb)
            scores(nsb, ng, bias, *bufs[(n + 1) % 2])
        outs.extend(finish(sb, g, *bufs[n % 2]))
        if g == SW_KV_HEADS - 1:
            o_ref[sb * SW_SUB:(sb + 1) * SW_SUB, :] = jnp.concatenate(outs, axis=0).T.astype(o_ref.dtype)
            outs = []


def _swa_attention(lay, qt, k, vt, sinks, *, tq=1024):
    B, S, C = lay.B, lay.S, lay.C
    tq = min(tq, S)
    nq_rows = SW_HEADS * SW_HD
    nkv = SW_KV_HEADS * SW_HD
    G = SW_HEADS // SW_KV_HEADS
    nq = S // tq
    per = tq // SW_WINDOW
    last = lay.NT // SW_WINDOW - 1
    cblk0 = lay.NL // C
    prev = lambda b, i: jnp.maximum((b * nq + i) * per - 1, 0)
    nxt = lambda b, i: jnp.minimum((b * nq + i + 1) * per, last)
    sink_l = jnp.repeat(sinks.reshape(SW_KV_HEADS, 1, G) * math.log2(math.e), SW_SUB, axis=2)
    sink_spec = pl.BlockSpec(sink_l.shape, lambda *a: (0, 0, 0))
    o_lat = pl.pallas_call(
        functools.partial(_swa_kernel, S=S, local=True),
        out_shape=jax.ShapeDtypeStruct((lay.NL, nq_rows), BF16),
        grid=(B, nq),
        in_specs=[sink_spec,
                  pl.BlockSpec((nq_rows, tq), lambda b, i: (0, b * nq + i)),
                  pl.BlockSpec((C, nkv), lambda b, i: (cblk0 + b, 0)),
                  pl.BlockSpec((nkv, C), lambda b, i: (0, cblk0 + b)),
                  pl.BlockSpec((SW_WINDOW, nkv), lambda b, i: (prev(b, i), 0)),
                  pl.BlockSpec((tq, nkv), lambda b, i: (b * nq + i, 0)),
                  pl.BlockSpec((SW_WINDOW, nkv), lambda b, i: (nxt(b, i), 0)),
                  pl.BlockSpec((nkv, SW_WINDOW), lambda b, i: (0, prev(b, i))),
                  pl.BlockSpec((nkv, tq), lambda b, i: (0, b * nq + i)),
                  pl.BlockSpec((nkv, SW_WINDOW), lambda b, i: (0, nxt(b, i)))],
        out_specs=pl.BlockSpec((tq, nq_rows), lambda b, i: (b * nq + i, 0)),
        scratch_shapes=([pltpu.VMEM((tq + 2 * SW_WINDOW, nkv), BF16), pltpu.VMEM((nkv, tq + 2 * SW_WINDOW), BF16)]
                        + [pltpu.VMEM((SW_BAND, G * SW_SUB), F32), pltpu.VMEM((C, G * SW_SUB), F32)] * 2),
        compiler_params=_cp(("arbitrary", "arbitrary")),
        name="swa_attn_latent",
    )(sink_l, qt, k, vt, k, k, k, vt, vt, vt)
    o_ctx = pl.pallas_call(
        functools.partial(_swa_kernel, S=S, local=False),
        out_shape=jax.ShapeDtypeStruct((lay.NC, nq_rows), BF16),
        grid=(B,),
        in_specs=[sink_spec,
                  pl.BlockSpec((nq_rows, C), lambda b: (0, cblk0 + b)),
                  pl.BlockSpec((C, nkv), lambda b: (cblk0 + b, 0)),
                  pl.BlockSpec((nkv, C), lambda b: (0, cblk0 + b))],
        out_specs=pl.BlockSpec((C, nq_rows), lambda b: (b, 0)),
        scratch_shapes=[pltpu.VMEM((C, G * SW_SUB), F32)] * 2,
        compiler_params=_cp(("arbitrary",)),
        name="swa_attn_ctx",
    )(sink_l, qt, k, vt)
    return o_lat, o_ctx


def _route(logits, tri, carry):
    lane = _lane_iota(logits.shape)
    lanef = lane.astype(F32)
    big = float(LANES)
    rowmax = lambda t: jnp.max(t, axis=-1, keepdims=True)
    rowmin = lambda t: jnp.min(t, axis=-1, keepdims=True)
    rowsum = lambda t: jnp.sum(t, axis=-1, keepdims=True)
    is_g = lane < MOE_GROUPS
    mg = rowmax(jnp.where(is_g, logits, -jnp.inf))
    w_g = 1.0 / rowsum(jnp.where(is_g, jnp.exp(logits - mg), 0.0))
    gidx = rowmin(jnp.where(is_g & (logits == mg), lanef, big))
    g0 = MOE_GROUPS + MOE_PER_GROUP * gidx
    in_grp = (lanef >= g0) & (lanef < g0 + MOE_PER_GROUP)
    le = jnp.where(in_grp, logits, -jnp.inf)
    m1 = rowmax(le)
    i1 = rowmin(jnp.where(in_grp & (le == m1), lanef, big))
    le2 = jnp.where(lanef == i1, -jnp.inf, le)
    m2 = rowmax(le2)
    i2 = rowmin(jnp.where(in_grp & (lanef != i1) & (le2 == m2), lanef, big))
    r = jnp.exp(m2 - m1)
    gate1 = w_g / (1.0 + r)
    gate2 = w_g * r / (1.0 + r)
    sel1, sel2 = lanef == i1, lanef == i2
    member = (sel1 | sel2)
    cum = _dot(tri, member.astype(BF16)) + carry
    rank1 = rowsum(jnp.where(sel1, cum, 0.0))
    rank2 = rowsum(jnp.where(sel2, cum, 0.0))
    new_carry = carry + jnp.sum(member.astype(F32), axis=0, keepdims=True)
    rec = jnp.zeros_like(logits)
    for ln, val in ((0, i1 - MOE_GROUPS), (1, i2 - MOE_GROUPS), (2, rank1), (3, rank2), (4, gate1), (5, gate2)):
        rec = jnp.where(lane == ln, val, rec)
    return rec, new_carry


def _out_proj_kernel(*refs, has_bias, n_lat_tiles):
    if has_bias:
        (al_ref, ac_ref, x_ref, w_ref, b_ref, g1_ref, g2n_ref, sc_ref, sh_ref, wr_ref, br_ref, tri_ref,
         xo_ref, h2_ref, rec_ref, rect_ref, cnt_ref, carry_ref) = refs
    else:
        (al_ref, ac_ref, x_ref, w_ref, g1_ref, g2n_ref, sc_ref, sh_ref, wr_ref, br_ref, tri_ref,
         xo_ref, h2_ref, rec_ref, rect_ref, cnt_ref, carry_ref) = refs

    @pl.when(pl.program_id(0) == 0)
    def _():
        carry_ref[...] = jnp.zeros_like(carry_ref)

    a = jnp.where(pl.program_id(0) < n_lat_tiles, al_ref[...], ac_ref[...])
    y = _dot(a.astype(BF16), w_ref[...])
    if has_bias:
        y = y + b_ref[...]
    xn = x_ref[...] + g1_ref[0] * y
    xo_ref[...] = xn
    h2 = _normmod(xn, g2n_ref[...], sc_ref[0], sh_ref[0])
    h2_ref[...] = h2
    h_hi = h2.astype(BF16)
    h_lo = (h2 - h_hi.astype(F32)).astype(BF16)
    hw = _dot(h_hi, wr_ref[...])
    logits = hw[:, :LANES] + hw[:, LANES:] + _dot(h_lo, wr_ref[:, :LANES]) + br_ref[...]
    rec, carry = _route(logits, tri_ref[...], carry_ref[...])
    rec_ref[...] = rec
    rect_ref[...] = rec.T[:SUBLANES, :]
    carry_ref[...] = carry
    cnt_ref[...] = jnp.broadcast_to(carry, cnt_ref.shape)


def _out_proj(lay, a, x, w, b, g2n, modv, wr, br, tri):
    D = lay.D
    a_lat, a_ctx = a
    nl = lay.nl_tiles
    full = lambda t: pl.BlockSpec(t.shape, lambda i: (0,) * t.ndim)
    row = lambda n: pl.BlockSpec((TM, n), lambda i: (i, 0))
    ins = [a_lat, a_ctx, x, w] + ([b] if b is not None else []) + [modv, g2n, modv, modv, wr, br, tri]
    specs = ([pl.BlockSpec((TM, a_lat.shape[1]), lambda i: (jnp.minimum(i, nl - 1), 0)),
              pl.BlockSpec((TM, a_ctx.shape[1]), lambda i: (jnp.maximum(i - nl, 0), 0)), row(D), full(w)]
             + ([full(b)] if b is not None else [])
             + [lay.mod_spec(2), full(g2n), lay.mod_spec(4), lay.mod_spec(3), full(wr), full(br), full(tri)])
    return pl.pallas_call(
        functools.partial(_out_proj_kernel, has_bias=b is not None, n_lat_tiles=nl),
        out_shape=(jax.ShapeDtypeStruct((lay.NT, D), F32), jax.ShapeDtypeStruct((lay.NT, D), F32),
                   jax.ShapeDtypeStruct((lay.NT, LANES), F32), jax.ShapeDtypeStruct((SUBLANES, lay.NT), F32),
                   jax.ShapeDtypeStruct((SUBLANES, LANES), F32)),
        grid=(lay.n_tiles,),
        in_specs=specs,
        out_specs=(row(D), row(D), row(LANES), pl.BlockSpec((SUBLANES, TM), lambda i: (0, i)),
                   pl.BlockSpec((SUBLANES, LANES), lambda i: (0, 0))),
        scratch_shapes=[pltpu.VMEM((1, LANES), F32)],
        input_output_aliases={2: 0},
        compiler_params=_cp(("arbitrary",)),
        name="out_proj_router",
    )(*ins)


def _row_copy(src, s, dst, d, sem):
    return pltpu.make_async_copy(src.at[pl.ds(s, 1), :], dst.at[pl.ds(d, 1), :], sem)


def _dispatch_kernel(pos_ref, pend_ref, nu_ref, h_ref, xb_ref, zbuf, sem, zsem):
    base = pl.program_id(0) * TM
    n_tok = pos_ref.shape[0] // 2
    nblk = xb_ref.shape[0] // MOE_BM

    @pl.when(pl.program_id(0) == 0)
    def _():
        zbuf[...] = jnp.zeros_like(zbuf)

        def zero_block(row0):
            return pltpu.make_async_copy(zbuf, xb_ref.at[pl.ds(pl.multiple_of(row0, MOE_BM), MOE_BM), :], zsem)

        def each(fn):
            def expert(e, c):
                end = pend_ref[e]

                @pl.when(end > jnp.where(e > 0, pend_ref[jnp.maximum(e - 1, 0)], 0))
                def _():
                    fn(zero_block(end - MOE_BM))
                return c

            def tail(j, c):
                @pl.when(j >= nu_ref[0])
                def _():
                    fn(zero_block(j * MOE_BM))
                return c

            lax.fori_loop(0, MOE_EXPERTS, expert, 0)
            lax.fori_loop(0, nblk, tail, 0)

        each(lambda cp: cp.start())
        each(lambda cp: cp.wait())

    i = pl.program_id(0)
    slot = i % 2

    def drain(s):
        for k in range(2):
            pltpu.make_async_copy(h_ref.at[pl.ds(0, TM), :], xb_ref.at[pl.ds(0, TM), :], sem.at[s]).wait()

    def issue(r, c):
        for k in range(2):
            _row_copy(h_ref, base + r, xb_ref, pos_ref[k * n_tok + base + r], sem.at[slot]).start(priority=k)
        return c

    lax.fori_loop(0, TM, issue, 0, unroll=8)

    @pl.when(i > 0)
    def _():
        drain(1 - slot)

    @pl.when(i == pl.num_programs(0) - 1)
    def _():
        drain(slot)


def _dispatch(lay, pos, pend, n_used, h2, cap):
    D = lay.D
    return pl.pallas_call(
        _dispatch_kernel,
        out_shape=jax.ShapeDtypeStruct((cap, D), F32),
        grid_spec=pltpu.PrefetchScalarGridSpec(
            num_scalar_prefetch=3, grid=(lay.n_tiles,),
            in_specs=[pl.BlockSpec(memory_space=pl.ANY)],
            out_specs=pl.BlockSpec(memory_space=pl.ANY),
            scratch_shapes=[pltpu.VMEM((MOE_BM, D), F32), pltpu.SemaphoreType.DMA((2,)), pltpu.SemaphoreType.DMA]),
        compiler_params=_cp(("arbitrary",)),
        name="moe_dispatch",
    )(pos, pend, n_used, h2)


def _expert_kernel(be_ref, nu_ref, xb_ref, wg_ref, wu_ref, wd_ref, yb_ref, wgb, wub, wdb):
    j = pl.program_id(0)
    prev = be_ref[jnp.maximum(j - 1, 0)]

    @pl.when((j == 0) | (be_ref[j] != prev))
    def _():
        wgb[...] = wg_ref[0, 0].astype(BF16)
        wub[...] = wu_ref[0, 0].astype(BF16)
        wdb[...] = wd_ref[0, 0].astype(BF16)

    @pl.when(j < nu_ref[0])
    def _():
        xe = xb_ref[...].astype(BF16)
        g = _dot(xe, wgb[...])
        u = _dot(xe, wub[...])
        act = (g * jax.nn.sigmoid(g) * u).astype(BF16)
        yb_ref[...] = _dot(act, wdb[...])

    @pl.when(j >= nu_ref[0])
    def _():
        yb_ref[...] = jnp.zeros_like(yb_ref)


def _experts(xb, blk_e, n_used, w_gate, w_up, w_down, layer):
    cap, D = xb.shape
    FF = w_gate.shape[-1]
    nblk = cap // MOE_BM
    blk = lambda j, be, nu: (jnp.maximum(jnp.minimum(j, nu[0] - 1), 0), 0)
    wblk = lambda j, be, nu: (layer, be[j], 0, 0)
    return pl.pallas_call(
        _expert_kernel,
        out_shape=jax.ShapeDtypeStruct((cap, D), F32),
        grid_spec=pltpu.PrefetchScalarGridSpec(
            num_scalar_prefetch=2, grid=(nblk,),
            in_specs=[pl.BlockSpec((MOE_BM, D), blk),
                      pl.BlockSpec((1, 1, D, FF), wblk), pl.BlockSpec((1, 1, D, FF), wblk),
                      pl.BlockSpec((1, 1, FF, D), wblk)],
            out_specs=pl.BlockSpec((MOE_BM, D), lambda j, be, nu: (j, 0)),
            scratch_shapes=[pltpu.VMEM((D, FF), BF16), pltpu.VMEM((D, FF), BF16), pltpu.VMEM((FF, D), BF16)]),
        compiler_params=_cp(("arbitrary",)),
        name="moe_experts",
    )(blk_e, n_used, xb, w_gate, w_up, w_down)


def _combine_kernel(*refs, final):
    if final:
        pos_ref, x_ref, rec_ref, g2_ref, yb_ref, fg_ref, o_ref, buf, sem = refs
    else:
        pos_ref, x_ref, rec_ref, g2_ref, yb_ref, o_ref, buf, sem = refs
    i = pl.program_id(0)
    n_tok = pos_ref.shape[0] // 2
    slot = i % 2

    def gather(tile, s):
        def issue(r, c):
            for k in range(2):
                _row_copy(yb_ref, pos_ref[k * n_tok + tile * TM + r], buf.at[s, k], r, sem.at[s]).start(priority=k)
            return c
        lax.fori_loop(0, TM, issue, 0, unroll=8)

    @pl.when(i == 0)
    def _():
        gather(0, 0)

    for k in range(2):
        pltpu.make_async_copy(yb_ref.at[pl.ds(0, TM), :], buf.at[slot, k], sem.at[slot]).wait()

    @pl.when(i + 1 < pl.num_programs(0))
    def _():
        gather(i + 1, 1 - slot)

    rec = rec_ref[...]
    f = (rec[:, ROUTE_LANE_GATE:ROUTE_LANE_GATE + 1] * buf[slot, 0]
         + rec[:, ROUTE_LANE_GATE + 1:ROUTE_LANE_GATE + 2] * buf[slot, 1])
    xn = x_ref[...] + g2_ref[0] * f
    if final:
        xn = _rms(xn, fg_ref[...])
    o_ref[...] = xn


def _combine(lay, pos, x, rec, modv, yb, final_g):
    D = lay.D
    final = final_g is not None
    row = lambda n: pl.BlockSpec((TM, n), lambda i, p: (i, 0))
    specs = [row(D), row(LANES),
             pl.BlockSpec((1, 1, D), lambda i, p: (5 * SUBLANES + lay.mod_row(i), 0, 0)),
             pl.BlockSpec(memory_space=pl.ANY)]
    ins = [x, rec, modv, yb]
    if final:
        specs.append(pl.BlockSpec(final_g.shape, lambda i, p: (0, 0)))
        ins.append(final_g)
    n_rows, n_tiles = (lay.NL, lay.nl_tiles) if final else (lay.NT, lay.n_tiles)
    return pl.pallas_call(
        functools.partial(_combine_kernel, final=final),
        out_shape=jax.ShapeDtypeStruct((n_rows, D), F32),
        grid_spec=pltpu.PrefetchScalarGridSpec(
            num_scalar_prefetch=1, grid=(n_tiles,),
            in_specs=specs,
            out_specs=row(D),
            scratch_shapes=[pltpu.VMEM((2, 2, TM, D), F32), pltpu.SemaphoreType.DMA((2,))]),
        input_output_aliases={} if final else {1: 0},
        compiler_params=_cp(("arbitrary",)),
        name="moe_combine",
    )(pos, *ins)


def _moe(lay, x, h2, rec, rect, counts, modv, w_gate, w_up, w_down, layer, final_g):
    T = 2 * lay.NT
    cap = -(-T // MOE_BM) * MOE_BM + MOE_EXPERTS * MOE_BM
    cnt = counts[0, MOE_GROUPS:MOE_GROUPS + MOE_EXPERTS].astype(I32)
    pcnt = (cnt + MOE_BM - 1) // MOE_BM * MOE_BM
    pend = jnp.cumsum(pcnt)
    start = pend - pcnt
    eid = rect[ROUTE_LANE_EID:ROUTE_LANE_EID + 2].astype(I32)
    rank = rect[ROUTE_LANE_RANK:ROUTE_LANE_RANK + 2].astype(I32)
    eid, rank = eid.reshape(1, -1), rank.reshape(-1)
    hit = eid == jnp.arange(MOE_EXPERTS, dtype=I32)[:, None]
    pos = jnp.sum(jnp.where(hit, start[:, None], 0), axis=0) + rank
    blk_row = jnp.arange(cap // MOE_BM, dtype=I32) * MOE_BM
    blk_e = jnp.minimum(jnp.sum((pend[None, :] <= blk_row[:, None]).astype(I32), axis=1), MOE_EXPERTS - 1)
    n_used = (pend[-1:] // MOE_BM).astype(I32)
    xb = _dispatch(lay, pos, pend.astype(I32), n_used, h2, cap)
    yb = _experts(xb, blk_e, n_used, w_gate, w_up, w_down, layer)
    return _combine(lay, pos, x, rec, modv, yb, final_g)


def _rope_tables(lay):
    S = lay.S
    t = jnp.arange(S)
    n = MLA_ROPE // 4
    inv = ROPE_THETA ** (-jnp.arange(n, dtype=F32) / n)
    ang = jnp.concatenate([(t // GRID_W).astype(F32)[:, None] * inv, (t % GRID_W).astype(F32)[:, None] * inv], axis=-1)
    cos, sin = jnp.cos(ang), jnp.sin(ang)
    cos64 = jnp.concatenate([cos, cos], axis=-1)
    sin64 = jnp.concatenate([-sin, sin], axis=-1)
    rows = lambda lat, ctx_val: jnp.concatenate([jnp.tile(lat, (lay.B, 1)), jnp.full((lay.NC, 64), ctx_val, F32)], axis=0)
    cos64, sin64 = rows(cos64, 1.0), rows(sin64, 0.0)
    zero = jnp.zeros_like(cos64)
    return ((jnp.concatenate([cos64, zero], axis=1), jnp.concatenate([sin64, zero], axis=1)),
            (jnp.concatenate([cos64, cos64], axis=1), jnp.concatenate([sin64, sin64], axis=1)))


def kernel(x, c, ctx, c_ctx, mod_w, mod_b, norm1_g, norm2_g, mla_w_dq, mla_g_q, mla_w_uq, mla_w_dkv, mla_g_kv, mla_w_ukv, mla_w_o, fnet_w_o, fnet_b_o, na_w_qkv, na_rpb, na_w_o, swa_w_qkv, swa_sinks, swa_w_o, moe_w_grp, moe_b_grp, moe_w_rt, moe_b_rt, moe_w_gate, moe_w_up, moe_w_down, final_g):
    B, S, D = x.shape
    C = ctx.shape[1]
    depth = mod_w.shape[0]
    lay = _Layout(B, S, C, D)
    X = jnp.concatenate([x.reshape(B * S, D), ctx.reshape(B * C, D)], axis=0)
    cond = jnp.concatenate([c, c_ctx[None], jnp.zeros((SUBLANES - B - 1, D), F32)], axis=0)
    mod = _modulation(cond, mod_w, mod_b)
    (mla_cos, mla_sin), (swa_cos, swa_sin) = _rope_tables(lay)
    tri = (jnp.arange(TM)[:, None] > jnp.arange(TM)[None, :]).astype(BF16)
    n_mix = 4
    for i in range(depth):
        m, j = i % n_mix, i // n_mix
        modv = mod[i].reshape(SUBLANES, 6, D).transpose(1, 0, 2).reshape(6 * SUBLANES, 1, D)
        g1n, g2n = norm1_g[i][None], norm2_g[i][None]
        bias = None
        if m == 0:
            w1 = jnp.concatenate([mla_w_dq[j], mla_w_dkv[j], jnp.zeros((D, LANES - MLA_ROPE), F32)], axis=1).astype(BF16)
            wq = mla_w_uq[j].reshape(MLA_Q_RANK, MLA_HEADS, MLA_NOPE + MLA_ROPE)
            wq = jnp.concatenate([wq, jnp.zeros((MLA_Q_RANK, MLA_HEADS, LANES - MLA_ROPE), F32)], axis=-1)
            wq = wq.reshape(MLA_Q_RANK, MLA_HEADS * 2 * LANES).astype(BF16)
            q, k, v = _mla_proj(lay, X, g1n, modv, w1, mla_g_q[j][None], mla_g_kv[j][None], wq,
                                mla_w_ukv[j].astype(BF16), mla_cos, mla_sin)
            a = _mla_attention(lay, q, k, v)
            w_o = mla_w_o[j]
        elif m == 1:
            a = _fnet_mix(lay, X, g1n, modv, _dft_tables(S, C, D // FNET_GROUPS))
            w_o, bias = fnet_w_o[j], fnet_b_o[j][None]
        elif m == 2:
            qt, k, vt = _qkv_proj(lay, X, g1n, modv, na_w_qkv[j].astype(BF16), swa_cos, swa_sin,
                                  n_q=NA_HEADS * NA_HD, n_k=NA_HEADS * NA_HD, rope=False,
                                  q_scale=NA_HD ** -0.5 * math.log2(math.e))
            a = _na_attention(lay, qt, k, vt, _na_bias_tables(na_rpb[j], S // GRID_W))
            w_o = na_w_o[j]
        else:
            qt, k, vt = _qkv_proj(lay, X, g1n, modv, swa_w_qkv[j].astype(BF16), swa_cos, swa_sin,
                                  n_q=SW_HEADS * SW_HD, n_k=SW_KV_HEADS * SW_HD, rope=True,
                                  q_scale=SW_HD ** -0.5 * math.log2(math.e))
            a = _swa_attention(lay, qt, k, vt, swa_sinks[j])
            w_o = swa_w_o[j]
        wr = jnp.concatenate([moe_w_grp[i], moe_w_rt[i], jnp.zeros((D, LANES - MOE_GROUPS - MOE_EXPERTS), F32)], axis=1)
        br = jnp.concatenate([moe_b_grp[i], moe_b_rt[i], jnp.zeros((LANES - MOE_GROUPS - MOE_EXPERTS,), F32)])[None]
        wr_hi = wr.astype(BF16)
        wr2 = jnp.concatenate([wr_hi, (wr - wr_hi.astype(F32)).astype(BF16)], axis=1)
        X, h2, rec, rect, counts = _out_proj(lay, a, X, w_o.astype(BF16), bias, g2n, modv, wr2, br, tri)
        X = _moe(lay, X, h2, rec, rect, counts, modv, moe_w_gate, moe_w_up, moe_w_down, i,
                 final_g[None] if i == depth - 1 else None)
    return X.reshape(B, S, D)
```
